```python
import jax, jax.numpy as jnp
from jax import lax
import numpy as np

D_MODEL = 2048
BATCH = 2
SEQ = 8192
DEPTH = 1

GRID_W = 64
CTX_LEN = 256
D_MIX = D_MODEL
D_A = D_MIX // 2
D_B = D_MIX - D_A
LRU_HEADS = 8
LRU_BLK = D_A // LRU_HEADS
LRU_C = 8.0
CONV_A_W = 4
CONV_A_LEFT = 2
CONV_B_W = 3
CONV_B_LEFT = 1
N_EXPERTS = 32
TOP_K = 4
D_FF = D_MODEL
SWIGLU_LIMIT = 7.0
SWIGLU_ALPHA = 1.702
MOE_BLOCK = 256
EPS = 1e-6
N_IN = 2 * D_A + 3 * D_B

kernel_name = "hybrid_rglru_shortconv_moe_dit"


def rms_norm(x, g):
    xf = x.astype(jnp.float32)
    y = xf * lax.rsqrt(jnp.mean(xf * xf, axis=-1, keepdims=True) + EPS)
    return (y * g.astype(jnp.float32)).astype(x.dtype)


def modulate(x, shift, scale):
    return x * (1 + scale) + shift


def dwconv(x, w, axis, left):
    k_w = w.shape[0]
    n = x.shape[axis]
    pad = [(0, 0)] * x.ndim
    pad[axis] = (left, k_w - 1 - left)
    xp = jnp.pad(x, pad)
    out = lax.slice_in_dim(xp, 0, n, axis=axis) * w[0]
    for k in range(1, k_w):
        out = out + lax.slice_in_dim(xp, k, k + n, axis=axis) * w[k]
    return out


def conv_b_latent(z, w):
    bsz, n, ch = z.shape
    rows = n // GRID_W
    half = ch // 2
    zg = z.reshape(bsz, rows, GRID_W, ch)
    horiz = dwconv(zg[..., :half], w[:, :half], 2, CONV_B_LEFT)
    vert = dwconv(zg[..., half:], w[:, half:], 1, CONV_B_LEFT)
    return jnp.concatenate([horiz, vert], axis=-1).reshape(bsz, n, ch)


def split_in(u):
    return jnp.split(u, [D_A, 2 * D_A, 2 * D_A + D_B, 2 * D_A + 2 * D_B], axis=-1)


def rglru_coeffs(xa, conv_w, conv_b, w_r, b_r, w_i, b_i, lam):
    xc = dwconv(xa, conv_w, 1, CONV_A_LEFT) + conv_b
    bsz, n, _ = xc.shape
    xh = xc.reshape(bsz, n, LRU_HEADS, LRU_BLK)
    r = jax.nn.sigmoid(jnp.einsum('blhi,dhij->dblhj', xh, w_r).reshape(2, bsz, n, D_A) + b_r[:, None, None])
    i = jax.nn.sigmoid(jnp.einsum('blhi,dhij->dblhj', xh, w_i).reshape(2, bsz, n, D_A) + b_i[:, None, None])
    log_a = -LRU_C * r.astype(jnp.float32) * jax.nn.softplus(-lam.astype(jnp.float32))[:, None, None]
    a = jnp.exp(log_a)
    b = jnp.sqrt(-jnp.expm1(2.0 * log_a)) * (i.astype(jnp.float32) * xc.astype(jnp.float32)[None])
    return a, b


def linear_scan(a, b, h0, reverse):
    def combine(lhs, rhs):
        return lhs[0] * rhs[0], rhs[0] * lhs[1] + rhs[1]
    a_cum, b_cum = lax.associative_scan(combine, (a, b), axis=1, reverse=reverse)
    return a_cum * h0[:, None] + b_cum


def mixer_out(y_a, y_b, g_out_a, g_out_b, w_out):
    y = jnp.concatenate([rms_norm(y_a, g_out_a), rms_norm(y_b, g_out_b)], axis=-1)
    return y @ w_out


def moe_ffn(xt, w_router, b_router, w_gate, b_gate, w_up, b_up, w_down, b_down):
    n_tok, d = xt.shape
    logits = (xt @ w_router + b_router).astype(jnp.float32)
    top_v, top_e = lax.top_k(logits, TOP_K)
    top_w = jax.nn.softmax(top_v, axis=-1).astype(xt.dtype)
    n_slot = n_tok * TOP_K
    flat_e = top_e.reshape(n_slot)
    order = jnp.argsort(flat_e)
    sorted_e = flat_e[order]
    slot_tok = (order // TOP_K).astype(jnp.int32)
    slot_w = top_w.reshape(n_slot)[order]
    counts = jnp.bincount(flat_e, length=N_EXPERTS)
    padded = (counts + MOE_BLOCK - 1) // MOE_BLOCK * MOE_BLOCK
    pad_end = jnp.cumsum(padded)
    pad_start = pad_end - padded
    grp_start = jnp.cumsum(counts) - counts
    dest = pad_start[sorted_e] + jnp.arange(n_slot, dtype=jnp.int32) - grp_start[sorted_e]
    n_blocks = -(-n_slot // MOE_BLOCK) + N_EXPERTS
    n_buf = n_blocks * MOE_BLOCK
    buf_tok = jnp.zeros((n_buf,), jnp.int32).at[dest].set(slot_tok)
    buf_w = jnp.zeros((n_buf,), xt.dtype).at[dest].set(slot_w)
    block_e = jnp.minimum(
        jnp.searchsorted(pad_end, jnp.arange(n_blocks, dtype=jnp.int32) * MOE_BLOCK, side='right'),
        N_EXPERTS - 1)

    def expert_block(args):
        idx, wts, e = args
        h = jnp.take(xt, idx, axis=0)
        gate = jnp.minimum(h @ w_gate[e] + b_gate[e], SWIGLU_LIMIT)
        up = jnp.clip(h @ w_up[e] + b_up[e], -SWIGLU_LIMIT, SWIGLU_LIMIT)
        act = (up + 1) * gate * jax.nn.sigmoid(SWIGLU_ALPHA * gate)
        return (act @ w_down[e] + b_down[e]) * wts[:, None]

    y = lax.map(expert_block, (buf_tok.reshape(n_blocks, MOE_BLOCK),
                               buf_w.reshape(n_blocks, MOE_BLOCK), block_e))
    return jax.ops.segment_sum(y.reshape(n_buf, d), buf_tok, num_segments=n_tok)


def setup_inputs(seed: int = 0) -> dict:
    key = jax.random.key(seed)
    ks = jax.random.split(key, 32)
    f32 = jnp.float32

    def nrm(k, shape, scale):
        return jax.random.normal(k, shape, f32) * scale

    u = jax.random.uniform(ks[14], (DEPTH, 2, D_A), f32, minval=0.9, maxval=0.999)
    a0 = u ** (1.0 / LRU_C)
    lru_lam = jnp.log(a0) - jnp.log1p(-a0)
    return {
        "x": nrm(ks[0], (BATCH, SEQ, D_MODEL), 1.0),
        "c": nrm(ks[1], (BATCH, D_MODEL), 1.0),
        "ctx": nrm(ks[2], (BATCH, CTX_LEN, D_MODEL), 1.0),
        "c_ctx": nrm(ks[3], (D_MODEL,), 1.0),
        "w_mod": nrm(ks[4], (DEPTH, D_MODEL, 6 * D_MODEL), 0.5 * D_MODEL ** -0.5),
        "b_mod": nrm(ks[5], (DEPTH, 6 * D_MODEL), 0.02),
        "g_mix": 1.0 + nrm(ks[6], (DEPTH, D_MODEL), 0.02),
        "w_in": nrm(ks[7], (DEPTH, D_MODEL, N_IN), D_MODEL ** -0.5),
        "conv_a_w": nrm(ks[8], (DEPTH, CONV_A_W, D_A), CONV_A_W ** -0.5),
        "conv_a_b": nrm(ks[9], (DEPTH, D_A), 0.02),
        "lru_w_r": nrm(ks[10], (DEPTH, 2, LRU_HEADS, LRU_BLK, LRU_BLK), LRU_BLK ** -0.5),
        "lru_b_r": nrm(ks[11], (DEPTH, 2, D_A), 0.02),
        "lru_w_i": nrm(ks[12], (DEPTH, 2, LRU_HEADS, LRU_BLK, LRU_BLK), LRU_BLK ** -0.5),
        "lru_b_i": nrm(ks[13], (DEPTH, 2, D_A), 0.02),
        "lru_lam": lru_lam,
        "conv_b_w": nrm(ks[15], (DEPTH, CONV_B_W, D_B), CONV_B_W ** -0.5),
        "g_out_a": 1.0 + nrm(ks[16], (DEPTH, D_A), 0.02),
        "g_out_b": 1.0 + nrm(ks[17], (DEPTH, D_B), 0.02),
        "w_out": nrm(ks[18], (DEPTH, D_MIX, D_MODEL), D_MIX ** -0.5),
        "g_ffn": 1.0 + nrm(ks[19], (DEPTH, D_MODEL), 0.02),
        "w_router": nrm(ks[20], (DEPTH, D_MODEL, N_EXPERTS), D_MODEL ** -0.5),
        "b_router": nrm(ks[21], (DEPTH, N_EXPERTS), 0.01),
        "w_gate": nrm(ks[22], (DEPTH, N_EXPERTS, D_MODEL, D_FF), D_MODEL ** -0.5),
        "b_gate": nrm(ks[23], (DEPTH, N_EXPERTS, D_FF), 0.01),
        "w_up": nrm(ks[24], (DEPTH, N_EXPERTS, D_MODEL, D_FF), D_MODEL ** -0.5),
        "b_up": nrm(ks[25], (DEPTH, N_EXPERTS, D_FF), 0.01),
        "w_down": nrm(ks[26], (DEPTH, N_EXPERTS, D_FF, D_MODEL), D_FF ** -0.5),
        "b_down": nrm(ks[27], (DEPTH, N_EXPERTS, D_MODEL), 0.01),
        "g_final": 1.0 + nrm(ks[28], (D_MODEL,), 0.02),
    }


def reference(x, c, ctx, c_ctx, w_mod, b_mod, g_mix, w_in, conv_a_w, conv_a_b, lru_w_r, lru_b_r,
              lru_w_i, lru_b_i, lru_lam, conv_b_w, g_out_a, g_out_b, w_out, g_ffn, w_router,
              b_router, w_gate, b_gate, w_up, b_up, w_down, b_down, g_final):
    bsz, n_lat, d = x.shape
    n_ctx = ctx.shape[1]
    s = ctx
    for l in range(DEPTH):
        last = l == DEPTH - 1
        mod = jax.nn.silu(c) @ w_mod[l] + b_mod[l]
        mod_s = jax.nn.silu(c_ctx) @ w_mod[l] + b_mod[l]
        sh1, sc1, gt1, sh2, sc2, gt2 = [m[:, None] for m in jnp.split(mod, 6, axis=-1)]
        ssh1, ssc1, sgt1, ssh2, ssc2, sgt2 = jnp.split(mod_s, 6, axis=-1)
        lru_p = (conv_a_w[l], conv_a_b[l], lru_w_r[l], lru_b_r[l], lru_w_i[l], lru_b_i[l], lru_lam[l])
        zero_state = jnp.zeros((bsz, D_A), jnp.float32)

        sn = modulate(rms_norm(s, g_mix[l]), ssh1, ssc1)
        if last:
            s_ax = sn @ w_in[l][:, D_A:2 * D_A]
        else:
            s_ag, s_ax, s_bb, s_bc, s_bh = split_in(sn @ w_in[l])
        a_s, b_s = rglru_coeffs(s_ax, *lru_p)
        hf_s = linear_scan(a_s[0], b_s[0], zero_state, False)
        hb_s = linear_scan(a_s[1], b_s[1], zero_state, True)
        if not last:
            ys_a = jax.nn.gelu(s_ag) * (hf_s + hb_s).astype(s.dtype)
            ys_b = s_bb * dwconv(s_bc * s_bh, conv_b_w[l], 1, CONV_B_LEFT)
            s = s + sgt1 * mixer_out(ys_a, ys_b, g_out_a[l], g_out_b[l], w_out[l])

        xn = modulate(rms_norm(x, g_mix[l]), sh1, sc1)
        x_ag, x_ax, x_bb, x_bc, x_bh = split_in(xn @ w_in[l])
        a_x, b_x = rglru_coeffs(x_ax, *lru_p)
        hf = linear_scan(a_x[0], b_x[0], hf_s[:, -1], False)
        hb = linear_scan(a_x[1], b_x[1], hb_s[:, 0], True)
        y_a = jax.nn.gelu(x_ag) * (hf + hb).astype(x.dtype)
        y_b = x_bb * conv_b_latent(x_bc * x_bh, conv_b_w[l])
        x = x + gt1 * mixer_out(y_a, y_b, g_out_a[l], g_out_b[l], w_out[l])

        moe_p = (w_router[l], b_router[l], w_gate[l], b_gate[l], w_up[l], b_up[l], w_down[l], b_down[l])
        xn = modulate(rms_norm(x, g_ffn[l]), sh2, sc2).reshape(bsz * n_lat, d)
        if last:
            x = x + gt2 * moe_ffn(xn, *moe_p).reshape(bsz, n_lat, d)
        else:
            sn = modulate(rms_norm(s, g_ffn[l]), ssh2, ssc2).reshape(bsz * n_ctx, d)
            out = moe_ffn(jnp.concatenate([xn, sn], axis=0), *moe_p)
            x = x + gt2 * out[:bsz * n_lat].reshape(bsz, n_lat, d)
            s = s + sgt2 * out[bsz * n_lat:].reshape(bsz, n_ctx, d)
    return rms_norm(x, g_final)
```

```python
import functools

import jax
import jax.numpy as jnp
from jax import lax
from jax.experimental import pallas as pl
from jax.experimental.pallas import tpu as pltpu

F32 = jnp.float32
BF16 = jnp.bfloat16
I32 = jnp.int32
U32 = jnp.uint32

GRID_W = 64
TOP_K = 4
LRU_C = 8.0
CONV_A_LEFT = 2
SWIGLU_LIMIT = 7.0
SWIGLU_ALPHA = 1.702
EPS = 1e-6

LANES = 128
SUBLANES = 8
VMEM_LIMIT_BYTES = 56 * 1024 * 1024
NEG_BIG = -1e30


def _sds(shape, dtype):
    return jax.ShapeDtypeStruct(shape, dtype)


def _pick(n, pref, mult=LANES):
    if n <= pref:
        return n
    t = (pref // mult) * mult
    while t >= mult:
        if n % t == 0:
            return t
        t -= mult
    return n


def _params(sem):
    return pltpu.CompilerParams(dimension_semantics=sem, vmem_limit_bytes=VMEM_LIMIT_BYTES)


def _sigmoid(x):
    return 1.0 / (1.0 + jnp.exp(-x))


def _rms(x, g):
    ms = jnp.mean(x * x, axis=-1, keepdims=True)
    return (x * lax.rsqrt(ms + EPS)) * g


def _mod_kernel(c_ref, w_ref, b_ref, o_ref):
    c = c_ref[...]
    s = c * _sigmoid(c)
    o_ref[...] = jnp.dot(s.astype(BF16), w_ref[...].astype(BF16),
                         preferred_element_type=F32) + b_ref[...]


def _mod_call(cs, w, b):
    d, n6 = w.shape
    tn = _pick(n6, 1024)
    return pl.pallas_call(
        _mod_kernel,
        out_shape=_sds((cs.shape[0], n6), F32),
        grid=(n6 // tn,),
        in_specs=[pl.BlockSpec((cs.shape[0], d), lambda j: (0, 0)),
                  pl.BlockSpec((d, tn), lambda j: (0, j)),
                  pl.BlockSpec((1, tn), lambda j: (0, j))],
        out_specs=pl.BlockSpec((cs.shape[0], tn), lambda j: (0, j)),
        compiler_params=_params(("arbitrary",)),
        name="mod",
    )(cs, w, b)


def _inproj_kernel(x_ref, g_ref, sh_ref, sc_ref, w_ref, *out_refs, d_a, d_b, latent):
    x = x_ref[0]
    xn = _rms(x, g_ref[...]) * (1.0 + sc_ref[0]) + sh_ref[0]
    xb = xn.astype(BF16)

    def sec(lo, width):
        return jnp.dot(xb, w_ref[:, lo:lo + width], preferred_element_type=F32)

    if not latent:
        out_refs[0][0] = sec(0, d_a)
        return
    ag_ref, ax_ref, bb_ref, p_ref = out_refs
    ag_ref[0] = sec(0, d_a).astype(BF16)
    ax_ref[0] = sec(d_a, d_a)
    bb_ref[0] = sec(2 * d_a, d_b).astype(BF16)
    p_ref[0] = (sec(2 * d_a + d_b, d_b) * sec(2 * d_a + 2 * d_b, d_b)).astype(BF16)


def _inproj_call(x, g, sh, sc, w_bf, d_a, d_b, latent):
    bsz, n, d = x.shape
    tm = _pick(n, 512, SUBLANES)
    n_w = w_bf.shape[1]
    row = lambda b, i: (b, i, 0)
    if latent:
        out_shape = (_sds((bsz, n, d_a), BF16), _sds((bsz, n, d_a), F32),
                     _sds((bsz, n, d_b), BF16), _sds((bsz, n, d_b), BF16))
        out_specs = (pl.BlockSpec((1, tm, d_a), row), pl.BlockSpec((1, tm, d_a), row),
                     pl.BlockSpec((1, tm, d_b), row), pl.BlockSpec((1, tm, d_b), row))
    else:
        out_shape = (_sds((bsz, n, d_a), F32),)
        out_specs = (pl.BlockSpec((1, tm, d_a), row),)
    return pl.pallas_call(
        functools.partial(_inproj_kernel, d_a=d_a, d_b=d_b, latent=latent),
        out_shape=out_shape,
        grid=(bsz, n // tm),
        in_specs=[pl.BlockSpec((1, tm, d), row),
                  pl.BlockSpec((1, d), lambda b, i: (0, 0)),
                  pl.BlockSpec((1, 1, d), lambda b, i: (b, 0, 0)),
                  pl.BlockSpec((1, 1, d), lambda b, i: (b, 0, 0)),
                  pl.BlockSpec((d, n_w), lambda b, i: (0, 0), pipeline_mode=pl.Buffered(1))],
        out_specs=out_specs,
        compiler_params=_params(("arbitrary", "arbitrary")),
        name="inproj_lat" if latent else "inproj_ctx",
    )(x, g, sh, sc, w_bf)


def _gelu_tanh(x):
    c = 0.7978845608028654
    return x * (0.5 * (1.0 + jnp.tanh(c * (x + 0.044715 * (x * x * x)))))


def _lru_kernel(*refs, reverse, combine, nc, tl, heads, blk):
    if combine:
        (prev_ref, main_ref, next_ref, cw_ref, cb_ref, wri_ref, br_ref, bi_ref, lam_ref, h0_ref,
         hf_ref, ag_ref, out_ref, hlast_ref, ebuf, xc_s, a_s, b_s, carry) = refs
    else:
        (prev_ref, main_ref, next_ref, cw_ref, cb_ref, wri_ref, br_ref, bi_ref, lam_ref, h0_ref,
         out_ref, hlast_ref, ebuf, xc_s, a_s, b_s, carry) = refs
    da = heads * blk
    c = pl.program_id(1)
    cidx = (nc - 1 - c) if reverse else c

    @pl.when(c == 0)
    def _():
        carry[...] = jnp.broadcast_to(h0_ref[0], carry.shape)

    main = main_ref[0]
    zero8 = jnp.zeros((SUBLANES, da), F32)
    ebuf[0:SUBLANES, :] = jnp.where(cidx == 0, zero8, prev_ref[0])
    ebuf[SUBLANES:SUBLANES + tl, :] = main
    ebuf[SUBLANES + tl:2 * SUBLANES + tl, :] = jnp.where(cidx == nc - 1, zero8, next_ref[0])
    cw = cw_ref[...]
    off = SUBLANES - CONV_A_LEFT
    xc_s[...] = (cw[0:1] * ebuf[off:off + tl, :] + cw[1:2] * ebuf[off + 1:off + 1 + tl, :]
                 + cw[2:3] * main + cw[3:4] * ebuf[off + 3:off + 3 + tl, :] + cb_ref[...])

    z = -lam_ref[...]
    sp = jnp.maximum(z, 0.0) + jnp.log1p(jnp.exp(-jnp.abs(z)))
    rc = min(tl, 128)
    for r0 in range(0, tl, rc):
        for h in range(heads):
            cs = slice(h * blk, (h + 1) * blk)
            xh = xc_s[r0:r0 + rc, cs]
            zz = jnp.dot(xh.astype(BF16), wri_ref[h], preferred_element_type=F32)
            r = _sigmoid(zz[:, :blk] + br_ref[:, cs])
            i = _sigmoid(zz[:, blk:] + bi_ref[:, cs])
            log_a = (-LRU_C * r) * sp[:, cs]
            a = jnp.exp(log_a)
            a_s[r0:r0 + rc, cs] = a
            b_s[r0:r0 + rc, cs] = jnp.sqrt(jnp.tanh(-log_a) * (1.0 + a * a)) * (i * xh)

    row = lax.broadcasted_iota(I32, (SUBLANES, da), 0)
    ng = tl // SUBLANES
    shifts = (1, 2, 4)

    def body(gi, hc):
        g = (ng - 1 - gi) if reverse else gi
        r0 = pl.multiple_of(g * SUBLANES, SUBLANES)
        a = a_s[pl.ds(r0, SUBLANES), :]
        b = b_s[pl.ds(r0, SUBLANES), :]
        for s in shifts:
            if reverse:
                keep = row < (SUBLANES - s)
                sh = SUBLANES - s
            else:
                keep = row >= s
                sh = s
            a_sh = jnp.where(keep, pltpu.roll(a, sh, 0), 1.0)
            b_sh = jnp.where(keep, pltpu.roll(b, sh, 0), 0.0)
            b = a * b_sh + b
            a = a * a_sh
        hrows = a * hc + b
        b_s[pl.ds(r0, SUBLANES), :] = hrows
        edge = hrows[0:1, :] if reverse else hrows[SUBLANES - 1:SUBLANES, :]
        return jnp.broadcast_to(edge, (SUBLANES, da))

    hc = lax.fori_loop(0, ng, body, carry[...])
    carry[...] = hc
    hlast_ref[0] = hc[0:1, :]
    if combine:
        hsum = hf_ref[0] + b_s[...]
        out_ref[0] = (_gelu_tanh(ag_ref[0].astype(F32)) * hsum).astype(out_ref.dtype)
    else:
        out_ref[0] = b_s[...]


def _lru_call(ax, h0, cw, cb, wri, br, bi, lam, *, reverse, hf=None, ag=None):
    bsz, n, da = ax.shape
    heads, blk, _ = wri.shape
    tl = _pick(n, 512, SUBLANES)
    nc = n // tl
    nb8 = n // SUBLANES
    g8 = tl // SUBLANES
    combine = hf is not None

    def cidx(c):
        return (nc - 1 - c) if reverse else c

    main_map = lambda b, c: (b, cidx(c), 0)
    prev_map = lambda b, c: (b, jnp.maximum(cidx(c) * g8 - 1, 0), 0)
    next_map = lambda b, c: (b, jnp.minimum((cidx(c) + 1) * g8, nb8 - 1), 0)
    const2 = lambda b, c: (0, 0)
    in_specs = [pl.BlockSpec((1, SUBLANES, da), prev_map),
                pl.BlockSpec((1, tl, da), main_map),
                pl.BlockSpec((1, SUBLANES, da), next_map),
                pl.BlockSpec((4, da), const2),
                pl.BlockSpec((1, da), const2),
                pl.BlockSpec((heads, blk, 2 * blk), lambda b, c: (0, 0, 0)),
                pl.BlockSpec((1, da), const2),
                pl.BlockSpec((1, da), const2),
                pl.BlockSpec((1, da), const2),
                pl.BlockSpec((1, 1, da), lambda b, c: (b, 0, 0))]
    args = [ax, ax, ax, cw, cb, wri, br, bi, lam, h0]
    if combine:
        in_specs += [pl.BlockSpec((1, tl, da), main_map), pl.BlockSpec((1, tl, da), main_map)]
        args += [hf, ag]
    out_dtype = BF16 if combine else F32
    return pl.pallas_call(
        functools.partial(_lru_kernel, reverse=reverse, combine=combine, nc=nc, tl=tl,
                          heads=heads, blk=blk),
        out_shape=(_sds((bsz, n, da), out_dtype), _sds((bsz, 1, da), F32)),
        grid=(bsz, nc),
        in_specs=in_specs,
        out_specs=(pl.BlockSpec((1, tl, da), main_map),
                   pl.BlockSpec((1, 1, da), lambda b, c: (b, 0, 0))),
        scratch_shapes=[pltpu.VMEM((tl + 2 * SUBLANES, da), F32),
                        pltpu.VMEM((tl, da), F32),
                        pltpu.VMEM((tl, da), F32),
                        pltpu.VMEM((tl, da), F32),
                        pltpu.VMEM((SUBLANES, da), F32)],
        compiler_params=_params(("arbitrary", "arbitrary")),
        name=("lru_bwd" if reverse else "lru_fwd") + ("_mix" if combine else ""),
    )(*args)


def _mixout_kernel(ya_ref, bb_ref, p_ref, pu_ref, pd_ref, cbw_ref, ga_ref, gb_ref, wo_ref, x_ref,
                   gt_ref, gf_ref, sh_ref, sc_ref, wr_ref, brt_ref,
                   x1_ref, xp_ref, te_ref, tw_ref, *, tm, d_a, d_b, n_tiles):
    i = pl.program_id(1)
    half = d_b // 2
    z = p_ref[0].astype(F32)
    w = cbw_ref[...]
    zh = z[:, :half]
    col = lax.broadcasted_iota(I32, (tm, half), 0) % GRID_W
    left = jnp.where(col >= 1, pltpu.roll(zh, 1, 0), 0.0)
    right = jnp.where(col <= GRID_W - 2, pltpu.roll(zh, tm - 1, 0), 0.0)
    horiz = w[0:1, :half] * left + w[1:2, :half] * zh + w[2:3, :half] * right
    zv = z[:, half:]
    up_halo = jnp.where(i == 0, 0.0, pu_ref[0].astype(F32))
    dn_halo = jnp.where(i == n_tiles - 1, 0.0, pd_ref[0].astype(F32))
    if tm > GRID_W:
        up = jnp.concatenate([up_halo, zv[:tm - GRID_W]], axis=0)
        dn = jnp.concatenate([zv[GRID_W:], dn_halo], axis=0)
    else:
        up, dn = up_halo, dn_halo
    vert = w[0:1, half:] * up + w[1:2, half:] * zv + w[2:3, half:] * dn
    bb = bb_ref[0].astype(F32)
    yb = jnp.concatenate([bb[:, :half] * horiz, bb[:, half:] * vert], axis=1)
    ya = ya_ref[0].astype(F32)
    ya_n = _rms(ya, ga_ref[...]).astype(BF16)
    yb_n = _rms(yb, gb_ref[...]).astype(BF16)
    mix = (jnp.dot(ya_n, wo_ref[0:d_a, :], preferred_element_type=F32)
           + jnp.dot(yb_n, wo_ref[d_a:d_a + d_b, :], preferred_element_type=F32))
    x1 = x_ref[0] + gt_ref[0] * mix
    x1_ref[0] = x1
    xn = _rms(x1, gf_ref[...]) * (1.0 + sc_ref[0]) + sh_ref[0]
    xb = xn.astype(BF16)
    dh = xb.shape[1] // 2
    lo = lax.shift_right_logical(lax.bitcast_convert_type(xb[:, :dh].astype(F32), U32), jnp.uint32(16))
    hi = lax.bitcast_convert_type(xb[:, dh:].astype(F32), U32) & jnp.uint32(0xFFFF0000)
    xp_ref[0] = lo | hi
    logits = jnp.dot(xb, wr_ref[...], preferred_element_type=F32) + brt_ref[...]
    lane = lax.broadcasted_iota(I32, logits.shape, 1)
    lane_f = lane.astype(F32)
    vals = logits
    tv, te = [], []
    for _ in range(TOP_K):
        m = jnp.max(vals, axis=-1, keepdims=True)
        idx = jnp.min(jnp.where(vals == m, lane_f, float(LANES)), axis=-1, keepdims=True)
        tv.append(m)
        te.append(idx)
        vals = jnp.where(lane_f == idx, -jnp.inf, vals)
    ex = [jnp.exp(v - tv[0]) for v in tv]
    den = ex[0]
    for e in ex[1:]:
        den = den + e
    e_out = jnp.zeros(logits.shape, I32)
    w_out = jnp.zeros(logits.shape, F32)
    for k in range(TOP_K):
        e_out = jnp.where(lane == k, te[k].astype(I32), e_out)
        w_out = jnp.where(lane == k, ex[k] / den, w_out)
    te_ref[0] = e_out
    tw_ref[0] = w_out


def _mixout_call(ya, bb, p, cbw, ga, gb, wo_bf, x, gt1, gf, sh2, sc2, wr_bf, brt):
    bsz, n, d = x.shape
    d_a = ya.shape[-1]
    d_b = bb.shape[-1]
    half = d_b // 2
    tm = _pick(n, 256, GRID_W)
    n_tiles = n // tm
    rpt = tm // GRID_W
    n_rows = n // GRID_W
    row = lambda b, i: (b, i, 0)
    vec = lambda b, i: (b, 0, 0)
    const2 = lambda b, i: (0, 0)
    return pl.pallas_call(
        functools.partial(_mixout_kernel, tm=tm, d_a=d_a, d_b=d_b, n_tiles=n_tiles),
        out_shape=(_sds((bsz, n, d), F32), _sds((bsz, n, d // 2), U32),
                   _sds((bsz, n, LANES), I32), _sds((bsz, n, LANES), F32)),
        grid=(bsz, n_tiles),
        in_specs=[pl.BlockSpec((1, tm, d_a), row),
                  pl.BlockSpec((1, tm, d_b), row),
                  pl.BlockSpec((1, tm, d_b), row),
                  pl.BlockSpec((1, GRID_W, half), lambda b, i: (b, jnp.maximum(i * rpt - 1, 0), 1)),
                  pl.BlockSpec((1, GRID_W, half), lambda b, i: (b, jnp.minimum((i + 1) * rpt, n_rows - 1), 1)),
                  pl.BlockSpec((3, d_b), const2),
                  pl.BlockSpec((1, d_a), const2),
                  pl.BlockSpec((1, d_b), const2),
                  pl.BlockSpec((d_a + d_b, d), const2),
                  pl.BlockSpec((1, tm, d), row),
                  pl.BlockSpec((1, 1, d), vec),
                  pl.BlockSpec((1, d), const2),
                  pl.BlockSpec((1, 1, d), vec),
                  pl.BlockSpec((1, 1, d), vec),
                  pl.BlockSpec((d, LANES), const2),
                  pl.BlockSpec((1, LANES), const2)],
        out_specs=(pl.BlockSpec((1, tm, d), row), pl.BlockSpec((1, tm, d // 2), row),
                   pl.BlockSpec((1, tm, LANES), row), pl.BlockSpec((1, tm, LANES), row)),
        compiler_params=_params(("arbitrary", "arbitrary")),
        name="mixout",
    )(ya, bb, p, p, p, cbw, ga, gb, wo_bf, x, gt1, gf, sh2, sc2, wr_bf, brt)


def _rank_kernel(e_ref, rank_ref, cnt_ref, carry, *, tt):
    @pl.when(pl.program_id(0) == 0)
    def _():
        carry[...] = jnp.zeros(carry.shape, F32)

    e = e_ref[...]
    lane = lax.broadcasted_iota(I32, (tt, LANES), 1)
    ohs = []
    m = jnp.zeros((tt, LANES), F32)
    for k in range(TOP_K):
        oh = lane == e[:, k:k + 1]
        ohs.append(oh)
        m = m + jnp.where(oh, 1.0, 0.0)
    ri = lax.broadcasted_iota(I32, (tt, tt), 0)
    ci = lax.broadcasted_iota(I32, (tt, tt), 1)
    ltri = jnp.where(ri > ci, 1.0, 0.0).astype(BF16)
    pref = jnp.dot(ltri, m.astype(BF16), preferred_element_type=F32) + carry[0:1, :]
    out = jnp.zeros((tt, LANES), I32)
    for k in range(TOP_K):
        rk = jnp.sum(jnp.where(ohs[k], pref, 0.0), axis=-1, keepdims=True)
        out = jnp.where(lane == k, rk.astype(I32), out)
    rank_ref[...] = out
    tot = carry[0:1, :] + jnp.sum(m, axis=0, keepdims=True)
    carry[...] = jnp.broadcast_to(tot, carry.shape)
    cnt_ref[...] = jnp.broadcast_to(tot, cnt_ref.shape)


def _rank_call(top_e):
    t = top_e.shape[0]
    tt = _pick(t, 512, SUBLANES)
    return pl.pallas_call(
        functools.partial(_rank_kernel, tt=tt),
        out_shape=(_sds((t, LANES), I32), _sds((SUBLANES, LANES), F32)),
        grid=(t // tt,),
        in_specs=[pl.BlockSpec((tt, LANES), lambda i: (i, 0))],
        out_specs=(pl.BlockSpec((tt, LANES), lambda i: (i, 0)),
                   pl.BlockSpec((SUBLANES, LANES), lambda i: (0, 0))),
        scratch_shapes=[pltpu.VMEM((SUBLANES, LANES), F32)],
        compiler_params=_params(("arbitrary",)),
        name="rank",
    )(top_e)


def _dispatch_kernel(pos_ref, x_hbm, o_hbm, sem, *, td):
    base = pl.program_id(0) * td
    n_dma = td * TOP_K

    def body(j, c):
        t = base + j // TOP_K
        pltpu.make_async_copy(x_hbm.at[pl.ds(t, 1)], o_hbm.at[pl.ds(pos_ref[0, 0, j], 1)], sem).start()
        return c

    lax.fori_loop(0, n_dma, body, 0)
    pltpu.make_async_copy(x_hbm.at[pl.ds(0, n_dma)], o_hbm.at[pl.ds(0, n_dma)], sem).wait()


def _dispatch_call(xp, pos):
    t, dw = xp.shape
    td = _pick(t, 512, SUBLANES)
    pos3 = pos.reshape(t // td, 1, td * TOP_K)
    return pl.pallas_call(
        functools.partial(_dispatch_kernel, td=td),
        out_shape=_sds((t * TOP_K, dw), U32),
        grid=(t // td,),
        in_specs=[pl.BlockSpec((1, 1, td * TOP_K), lambda i: (i, 0, 0), memory_space=pltpu.SMEM),
                  pl.BlockSpec(memory_space=pl.ANY)],
        out_specs=pl.BlockSpec(memory_space=pl.ANY),
        scratch_shapes=[pltpu.SemaphoreType.DMA],
        compiler_params=_params(("arbitrary",)),
        name="dispatch",
    )(pos3, xp)


def _unpack_rows(xp):
    lo = lax.bitcast_convert_type(lax.shift_left(xp, jnp.uint32(16)), F32)
    hi = lax.bitcast_convert_type(xp & jnp.uint32(0xFFFF0000), F32)
    return jnp.concatenate([lo.astype(BF16), hi.astype(BF16)], axis=1)


def _store_rows(out_ref, new, lo, tm):
    @pl.when(lo == 0)
    def _():
        out_ref[...] = new

    @pl.when(lo > 0)
    def _():
        row = lax.broadcasted_iota(I32, new.shape, 0)
        out_ref[...] = jnp.where(row >= lo, new, out_ref[...])


def _gmm1_kernel(vt_ref, ve_ref, lo_ref, nv_ref, xs_ref, wg_ref, wu_ref, bg_ref, bu_ref, act_ref,
                 wbf, *, tm, tf):
    v = pl.program_id(1)

    @pl.when(v < nv_ref[0])
    def _():
        changed = jnp.logical_or(v == 0, ve_ref[v] != ve_ref[jnp.maximum(v - 1, 0)])

        @pl.when(changed)
        def _():
            wbf[:, 0:tf] = wg_ref[0].astype(BF16)
            wbf[:, tf:2 * tf] = wu_ref[0].astype(BF16)

        h = _unpack_rows(xs_ref[...])
        gu = jnp.dot(h, wbf[...], preferred_element_type=F32)
        gate = jnp.minimum(gu[:, :tf] + bg_ref[0], SWIGLU_LIMIT)
        up = jnp.clip(gu[:, tf:] + bu_ref[0], -SWIGLU_LIMIT, SWIGLU_LIMIT)
        act = (up + 1.0) * gate * _sigmoid(SWIGLU_ALPHA * gate)
        _store_rows(act_ref, act.astype(BF16), lo_ref[v], tm)


def _gmm1_call(meta, xs, wg, wu, bg, bu, tm):
    vt, ve, lo, nv = meta
    n_rows, dw = xs.shape
    n_e, d, dff = wg.shape
    tf = _pick(dff, 512)
    nf = dff // tf
    n_v = vt.shape[0]
    return pl.pallas_call(
        functools.partial(_gmm1_kernel, tm=tm, tf=tf),
        out_shape=_sds((n_rows, dff), BF16),
        grid_spec=pltpu.PrefetchScalarGridSpec(
            num_scalar_prefetch=4,
            grid=(nf, n_v),
            in_specs=[pl.BlockSpec((tm, dw), lambda f, v, vt, ve, lo, nv: (vt[v], 0)),
                      pl.BlockSpec((1, d, tf), lambda f, v, vt, ve, lo, nv: (ve[v], 0, f)),
                      pl.BlockSpec((1, d, tf), lambda f, v, vt, ve, lo, nv: (ve[v], 0, f)),
                      pl.BlockSpec((1, 1, tf), lambda f, v, vt, ve, lo, nv: (ve[v], 0, f)),
                      pl.BlockSpec((1, 1, tf), lambda f, v, vt, ve, lo, nv: (ve[v], 0, f))],
            out_specs=pl.BlockSpec((tm, tf), lambda f, v, vt, ve, lo, nv: (vt[v], f)),
            scratch_shapes=[pltpu.VMEM((d, 2 * tf), BF16)]),
        compiler_params=_params(("arbitrary", "arbitrary")),
        name="gmm1",
    )(vt, ve, lo, nv, xs, wg, wu, bg, bu)


def _gmm2_kernel(vt_ref, ve_ref, lo_ref, nv_ref, act_ref, wd_ref, bd_ref, y_ref, wbf, *, tm):
    v = pl.program_id(1)

    @pl.when(v < nv_ref[0])
    def _():
        changed = jnp.logical_or(v == 0, ve_ref[v] != ve_ref[jnp.maximum(v - 1, 0)])

        @pl.when(changed)
        def _():
            wbf[...] = wd_ref[0].astype(BF16)

        y = jnp.dot(act_ref[...], wbf[...], preferred_element_type=F32) + bd_ref[0]
        _store_rows(y_ref, y, lo_ref[v], tm)


def _gmm2_call(meta, act, wd, bd, tm):
    vt, ve, lo, nv = meta
    n_rows, dff = act.shape
    n_e, _, d = wd.shape
    tn = _pick(d, 1024)
    nn = d // tn
    n_v = vt.shape[0]
    return pl.pallas_call(
        functools.partial(_gmm2_kernel, tm=tm),
        out_shape=_sds((n_rows, d), F32),
        grid_spec=pltpu.PrefetchScalarGridSpec(
            num_scalar_prefetch=4,
            grid=(nn, n_v),
            in_specs=[pl.BlockSpec((tm, dff), lambda n, v, vt, ve, lo, nv: (vt[v], 0)),
                      pl.BlockSpec((1, dff, tn), lambda n, v, vt, ve, lo, nv: (ve[v], 0, n)),
                      pl.BlockSpec((1, 1, tn), lambda n, v, vt, ve, lo, nv: (ve[v], 0, n))],
            out_specs=pl.BlockSpec((tm, tn), lambda n, v, vt, ve, lo, nv: (vt[v], n)),
            scratch_shapes=[pltpu.VMEM((dff, tn), BF16)]),
        compiler_params=_params(("arbitrary", "arbitrary")),
        name="gmm2",
    )(vt, ve, lo, nv, act, wd, bd)


def _final_kernel(pos_ref, y_hbm, tw_ref, x1_ref, gt_ref, gfin_ref, o_ref, ybuf, sem, *, tc):
    n_dma = tc * TOP_K

    def body(j, c):
        t = j // TOP_K
        k = j % TOP_K
        pltpu.make_async_copy(y_hbm.at[pl.ds(pos_ref[0, 0, j], 1)], ybuf.at[k, pl.ds(t, 1)], sem).start()
        return c

    lax.fori_loop(0, n_dma, body, 0)
    for k in range(TOP_K):
        pltpu.make_async_copy(y_hbm.at[pl.ds(0, tc)], ybuf.at[k], sem).wait()
    tw = tw_ref[0]
    moe = tw[:, 0:1] * ybuf[0]
    for k in range(1, TOP_K):
        moe = moe + tw[:, k:k + 1] * ybuf[k]
    x2 = x1_ref[0] + gt_ref[0] * moe
    o_ref[0] = _rms(x2, gfin_ref[...])


def _final_call(y, pos, top_w, x1, gt2, gfin):
    bsz, n, d = x1.shape
    tc = _pick(n, 256, SUBLANES)
    nt = n // tc
    pos3 = pos.reshape(bsz * nt, 1, tc * TOP_K)
    row = lambda b, i: (b, i, 0)
    return pl.pallas_call(
        functools.partial(_final_kernel, tc=tc),
        out_shape=_sds((bsz, n, d), F32),
        grid=(bsz, nt),
        in_specs=[pl.BlockSpec((1, 1, tc * TOP_K), lambda b, i: (b * nt + i, 0, 0), memory_space=pltpu.SMEM),
                  pl.BlockSpec(memory_space=pl.ANY),
                  pl.BlockSpec((1, tc, LANES), row),
                  pl.BlockSpec((1, tc, d), row),
                  pl.BlockSpec((1, 1, d), lambda b, i: (b, 0, 0)),
                  pl.BlockSpec((1, d), lambda b, i: (0, 0))],
        out_specs=pl.BlockSpec((1, tc, d), row),
        scratch_shapes=[pltpu.VMEM((TOP_K, tc, d), F32), pltpu.SemaphoreType.DMA],
        compiler_params=_params(("arbitrary", "arbitrary")),
        name="final",
    )(pos3, y, top_w, x1, gt2, gfin)


def _visit_tables(counts, n_rows, tm):
    n_e = counts.shape[0]
    ends = jnp.cumsum(counts)
    starts = ends - counts
    first = starts // tm
    last = jnp.maximum(ends - 1, 0) // tm
    nvis = jnp.where(counts > 0, last - first + 1, 0)
    vend = jnp.cumsum(nvis)
    vstart = vend - nvis
    nv = vend[-1]
    n_v = n_rows // tm + n_e - 1
    v = jnp.arange(n_v, dtype=I32)
    ve = jnp.clip(jnp.searchsorted(vend, v, side="right"), 0, n_e - 1).astype(I32)
    vt = first[ve] + (v - vstart[ve])
    lo = jnp.maximum(starts[ve] - vt * tm, 0)
    valid = v < nv
    last_v = jnp.maximum(nv - 1, 0)
    ve = jnp.where(valid, ve, ve[last_v]).astype(I32)
    vt = jnp.where(valid, vt, vt[last_v]).astype(I32)
    lo = jnp.where(valid, lo, 0).astype(I32)
    return starts, (vt, ve, lo, nv.reshape(1).astype(I32))


def kernel(x, c, ctx, c_ctx, w_mod, b_mod, g_mix, w_in, conv_a_w, conv_a_b, lru_w_r, lru_b_r,
           lru_w_i, lru_b_i, lru_lam, conv_b_w, g_out_a, g_out_b, w_out, g_ffn, w_router,
           b_router, w_gate, b_gate, w_up, b_up, w_down, b_down, g_final):
    assert w_mod.shape[0] == 1, "single-layer block"
    bsz, n_lat, d = x.shape
    d_a = conv_a_w.shape[-1]
    d_b = conv_b_w.shape[-1]
    n_e = w_router.shape[-1]
    assert n_lat % GRID_W == 0 and n_e <= LANES and d % (2 * LANES) == 0
    l = 0

    cs = jnp.zeros((SUBLANES, d), F32).at[:bsz].set(c).at[bsz].set(c_ctx)
    mod = _mod_call(cs, w_mod[l], b_mod[l][None])
    sh1, sc1, gt1, sh2, sc2, gt2 = [m[:bsz, None, :] for m in jnp.split(mod, 6, axis=-1)]
    ssh1, ssc1 = [jnp.broadcast_to(m[bsz][None, None, :], (bsz, 1, d))
                  for m in jnp.split(mod, 6, axis=-1)[:2]]

    w_in_bf = w_in[l].astype(BF16)
    g_mix2 = g_mix[l][None]
    cw = conv_a_w[l]
    cb = conv_a_b[l][None]
    wri = jnp.concatenate([lru_w_r[l], lru_w_i[l]], axis=-1).astype(BF16)
    br, bi, lam = lru_b_r[l], lru_b_i[l], lru_lam[l]
    lru_p = lambda dr: (cw, cb, wri[dr], br[dr][None], bi[dr][None], lam[dr][None])

    (s_ax,) = _inproj_call(ctx, g_mix2, ssh1, ssc1, w_in_bf[:, d_a:2 * d_a], d_a, d_b, latent=False)
    zero_state = jnp.zeros((bsz, 1, d_a), F32)
    _, h0f = _lru_call(s_ax, zero_state, *lru_p(0), reverse=False)
    _, h0b = _lru_call(s_ax, zero_state, *lru_p(1), reverse=True)

    ag, ax, bb, p = _inproj_call(x, g_mix2, sh1, sc1, w_in_bf, d_a, d_b, latent=True)
    hf, _ = _lru_call(ax, h0f, *lru_p(0), reverse=False)
    ya, _ = _lru_call(ax, h0b, *lru_p(1), reverse=True, hf=hf, ag=ag)

    wr_bf = jnp.zeros((d, LANES), BF16).at[:, :n_e].set(w_router[l].astype(BF16))
    brt = jnp.full((1, LANES), NEG_BIG, F32).at[0, :n_e].set(b_router[l])
    x1, xp, top_e, top_w = _mixout_call(
        ya, bb, p, conv_b_w[l], g_out_a[l][None], g_out_b[l][None], w_out[l].astype(BF16), x, gt1,
        g_ffn[l][None], sh2, sc2, wr_bf, brt)

    n_tok = bsz * n_lat
    n_rows = n_tok * TOP_K
    tm = _pick(n_rows, 512, SUBLANES)
    rank, cnt = _rank_call(top_e.reshape(n_tok, LANES))
    counts = cnt[0, :n_e].astype(I32)
    starts, meta = _visit_tables(counts, n_rows, tm)
    te = top_e.reshape(n_tok, LANES)[:, :TOP_K]
    onehot = te[:, :, None] == jnp.arange(n_e, dtype=I32)[None, None, :]
    pos = (jnp.sum(jnp.where(onehot, starts[None, None, :], 0), axis=-1) + rank[:, :TOP_K]).astype(I32)
    xs = _dispatch_call(xp.reshape(n_tok, d // 2), pos)
    act = _gmm1_call(meta, xs, w_gate[l], w_up[l], b_gate[l][:, None, :], b_up[l][:, None, :], tm)
    y = _gmm2_call(meta, act, w_down[l], b_down[l][:, None, :], tm)
    return _final_call(y, pos, top_w, x1, gt2, g_final[None])
```

```python
import functools

import jax
import jax.numpy as jnp
from jax import lax
from jax.experimental import pallas as pl
from jax.experimental.pallas import tpu as pltpu

F32 = jnp.float32
BF16 = jnp.bfloat16
I32 = jnp.int32
U32 = jnp.uint32

GRID_W = 64
TOP_K = 4
LRU_C = 8.0
CONV_A_LEFT = 2
SWIGLU_LIMIT = 7.0
SWIGLU_ALPHA = 1.702
EPS = 1e-6

LANES = 128
SUBLANES = 8
VMEM_LIMIT_BYTES = 56 * 1024 * 1024
NEG_BIG = -1e30


def _sds(shape, dtype):
    return jax.ShapeDtypeStruct(shape, dtype)


def _pick(n, pref, mult=LANES):
    if n <= pref:
        return n
    t = (pref // mult) * mult
    while t >= mult:
        if n % t == 0:
            return t
        t -= mult
    return n


def _params(sem):
    return pltpu.CompilerParams(dimension_semantics=sem, vmem_limit_bytes=VMEM_LIMIT_BYTES)


def _sigmoid(x):
    return 1.0 / (1.0 + jnp.exp(-x))


def _rms(x, g):
    ms = jnp.mean(x * x, axis=-1, keepdims=True)
    return (x * lax.rsqrt(ms + EPS)) * g


def _mod_kernel(c_ref, w_ref, b_ref, o_ref):
    c = c_ref[...]
    s = c * _sigmoid(c)
    o_ref[...] = jnp.dot(s.astype(BF16), w_ref[...].astype(BF16),
                         preferred_element_type=F32) + b_ref[...]


def _mod_call(cs, w, b):
    d, n6 = w.shape
    tn = _pick(n6, 1024)
    return pl.pallas_call(
        _mod_kernel,
        out_shape=_sds((cs.shape[0], n6), F32),
        grid=(n6 // tn,),
        in_specs=[pl.BlockSpec((cs.shape[0], d), lambda j: (0, 0)),
                  pl.BlockSpec((d, tn), lambda j: (0, j)),
                  pl.BlockSpec((1, tn), lambda j: (0, j))],
        out_specs=pl.BlockSpec((cs.shape[0], tn), lambda j: (0, j)),
        compiler_params=_params(("arbitrary",)),
        name="mod",
    )(cs, w, b)


def _inproj_kernel(x_ref, g_ref, sh_ref, sc_ref, w_ref, *out_refs, d_a, d_b, latent):
    x = x_ref[0]
    xn = _rms(x, g_ref[...]) * (1.0 + sc_ref[0]) + sh_ref[0]
    xb = xn.astype(BF16)

    def sec(lo, width):
        return jnp.dot(xb, w_ref[:, lo:lo + width], preferred_element_type=F32)

    if not latent:
        out_refs[0][0] = sec(0, d_a)
        return
    ag_ref, ax_ref, bb_ref, p_ref = out_refs
    ag_ref[0] = sec(0, d_a).astype(BF16)
    ax_ref[0] = sec(d_a, d_a)
    bb_ref[0] = sec(2 * d_a, d_b).astype(BF16)
    p_ref[0] = (sec(2 * d_a + d_b, d_b) * sec(2 * d_a + 2 * d_b, d_b)).astype(BF16)


def _inproj_call(x, g, sh, sc, w_bf, d_a, d_b, latent):
    bsz, n, d = x.shape
    tm = _pick(n, 512, SUBLANES)
    n_w = w_bf.shape[1]
    row = lambda b, i: (b, i, 0)
    if latent:
        out_shape = (_sds((bsz, n, d_a), BF16), _sds((bsz, n, d_a), F32),
                     _sds((bsz, n, d_b), BF16), _sds((bsz, n, d_b), BF16))
        out_specs = (pl.BlockSpec((1, tm, d_a), row), pl.BlockSpec((1, tm, d_a), row),
                     pl.BlockSpec((1, tm, d_b), row), pl.BlockSpec((1, tm, d_b), row))
    else:
        out_shape = (_sds((bsz, n, d_a), F32),)
        out_specs = (pl.BlockSpec((1, tm, d_a), row),)
    return pl.pallas_call(
        functools.partial(_inproj_kernel, d_a=d_a, d_b=d_b, latent=latent),
        out_shape=out_shape,
        grid=(bsz, n // tm),
        in_specs=[pl.BlockSpec((1, tm, d), row),
                  pl.BlockSpec((1, d), lambda b, i: (0, 0)),
                  pl.BlockSpec((1, 1, d), lambda b, i: (b, 0, 0)),
                  pl.BlockSpec((1, 1, d), lambda b, i: (b, 0, 0)),
                  pl.BlockSpec((d, n_w), lambda b, i: (0, 0), pipeline_mode=pl.Buffered(1))],
        out_specs=out_specs,
        compiler_params=_params(("arbitrary", "arbitrary")),
        name="inproj_lat" if latent else "inproj_ctx",
    )(x, g, sh, sc, w_bf)


def _gelu_tanh(x):
    c = 0.7978845608028654
    return x * (0.5 * (1.0 + jnp.tanh(c * (x + 0.044715 * (x * x * x)))))


def _lru_kernel(*refs, reverse, combine, nc, tl, heads, blk):
    if combine:
        (prev_ref, main_ref, next_ref, cw_ref, cb_ref, wri_ref, br_ref, bi_ref, lam_ref, h0_ref,
         hf_ref, ag_ref, out_ref, hlast_ref, ebuf, xc_s, a_s, b_s, carry) = refs
    else:
        (prev_ref, main_ref, next_ref, cw_ref, cb_ref, wri_ref, br_ref, bi_ref, lam_ref, h0_ref,
         out_ref, hlast_ref, ebuf, xc_s, a_s, b_s, carry) = refs
    da = heads * blk
    c = pl.program_id(1)
    cidx = (nc - 1 - c) if reverse else c

    @pl.when(c == 0)
    def _():
        carry[...] = jnp.broadcast_to(h0_ref[0], carry.shape)

    main = main_ref[0]
    zero8 = jnp.zeros((SUBLANES, da), F32)
    ebuf[0:SUBLANES, :] = jnp.where(cidx == 0, zero8, prev_ref[0])
    ebuf[SUBLANES:SUBLANES + tl, :] = main
    ebuf[SUBLANES + tl:2 * SUBLANES + tl, :] = jnp.where(cidx == nc - 1, zero8, next_ref[0])
    cw = cw_ref[...]
    off = SUBLANES - CONV_A_LEFT
    xc_s[...] = (cw[0:1] * ebuf[off:off + tl, :] + cw[1:2] * ebuf[off + 1:off + 1 + tl, :]
                 + cw[2:3] * main + cw[3:4] * ebuf[off + 3:off + 3 + tl, :] + cb_ref[...])

    z = -lam_ref[...]
    sp = jnp.maximum(z, 0.0) + jnp.log1p(jnp.exp(-jnp.abs(z)))
    rc = min(tl, 128)
    for r0 in range(0, tl, rc):
        for h in range(heads):
            cs = slice(h * blk, (h + 1) * blk)
            xh = xc_s[r0:r0 + rc, cs]
            zz = jnp.dot(xh.astype(BF16), wri_ref[h], preferred_element_type=F32)
            r = _sigmoid(zz[:, :blk] + br_ref[:, cs])
            i = _sigmoid(zz[:, blk:] + bi_ref[:, cs])
            log_a = (-LRU_C * r) * sp[:, cs]
            a = jnp.exp(log_a)
            a_s[r0:r0 + rc, cs] = a
            b_s[r0:r0 + rc, cs] = jnp.sqrt(jnp.tanh(-log_a) * (1.0 + a * a)) * (i * xh)

    row = lax.broadcasted_iota(I32, (SUBLANES, da), 0)
    ng = tl // SUBLANES
    shifts = (1, 2, 4)

    def body(gi, hc):
        g = (ng - 1 - gi) if reverse else gi
        r0 = pl.multiple_of(g * SUBLANES, SUBLANES)
        a = a_s[pl.ds(r0, SUBLANES), :]
        b = b_s[pl.ds(r0, SUBLANES), :]
        for s in shifts:
            if reverse:
                keep = row < (SUBLANES - s)
                sh = SUBLANES - s
            else:
                keep = row >= s
                sh = s
            a_sh = jnp.where(keep, pltpu.roll(a, sh, 0), 1.0)
            b_sh = jnp.where(keep, pltpu.roll(b, sh, 0), 0.0)
            b = a * b_sh + b
            a = a * a_sh
        hrows = a * hc + b
        b_s[pl.ds(r0, SUBLANES), :] = hrows
        edge = hrows[0:1, :] if reverse else hrows[SUBLANES - 1:SUBLANES, :]
        return jnp.broadcast_to(edge, (SUBLANES, da))

    hc = lax.fori_loop(0, ng, body, carry[...])
    carry[...] = hc
    hlast_ref[0] = hc[0:1, :]
    if combine:
        hsum = hf_ref[0] + b_s[...]
        out_ref[0] = (_gelu_tanh(ag_ref[0].astype(F32)) * hsum).astype(out_ref.dtype)
    else:
        out_ref[0] = b_s[...]


def _lru_call(ax, h0, cw, cb, wri, br, bi, lam, *, reverse, hf=None, ag=None):
    bsz, n, da = ax.shape
    heads, blk, _ = wri.shape
    tl = _pick(n, 512, SUBLANES)
    nc = n // tl
    nb8 = n // SUBLANES
    g8 = tl // SUBLANES
    combine = hf is not None

    def cidx(c):
        return (nc - 1 - c) if reverse else c

    main_map = lambda b, c: (b, cidx(c), 0)
    prev_map = lambda b, c: (b, jnp.maximum(cidx(c) * g8 - 1, 0), 0)
    next_map = lambda b, c: (b, jnp.minimum((cidx(c) + 1) * g8, nb8 - 1), 0)
    const2 = lambda b, c: (0, 0)
    in_specs = [pl.BlockSpec((1, SUBLANES, da), prev_map),
                pl.BlockSpec((1, tl, da), main_map),
                pl.BlockSpec((1, SUBLANES, da), next_map),
                pl.BlockSpec((4, da), const2),
                pl.BlockSpec((1, da), const2),
                pl.BlockSpec((heads, blk, 2 * blk), lambda b, c: (0, 0, 0)),
                pl.BlockSpec((1, da), const2),
                pl.BlockSpec((1, da), const2),
                pl.BlockSpec((1, da), const2),
                pl.BlockSpec((1, 1, da), lambda b, c: (b, 0, 0))]
    args = [ax, ax, ax, cw, cb, wri, br, bi, lam, h0]
    if combine:
        in_specs += [pl.BlockSpec((1, tl, da), main_map), pl.BlockSpec((1, tl, da), main_map)]
        args += [hf, ag]
    out_dtype = BF16 if combine else F32
    return pl.pallas_call(
        functools.partial(_lru_kernel, reverse=reverse, combine=combine, nc=nc, tl=tl,
                          heads=heads, blk=blk),
        out_shape=(_sds((bsz, n, da), out_dtype), _sds((bsz, 1, da), F32)),
        grid=(bsz, nc),
        in_specs=in_specs,
        out_specs=(pl.BlockSpec((1, tl, da), main_map),
                   pl.BlockSpec((1, 1, da), lambda b, c: (b, 0, 0))),
        scratch_shapes=[pltpu.VMEM((tl + 2 * SUBLANES, da), F32),
                        pltpu.VMEM((tl, da), F32),
                        pltpu.VMEM((tl, da), F32),
                        pltpu.VMEM((tl, da), F32),
                        pltpu.VMEM((SUBLANES, da), F32)],
        compiler_params=_params(("arbitrary", "arbitrary")),
        name=("lru_bwd" if reverse else "lru_fwd") + ("_mix" if combine else ""),
    )(*args)


def _mixout_kernel(ya_ref, bb_ref, p_ref, pu_ref, pd_ref, cbw_ref, ga_ref, gb_ref, wo_ref, x_ref,
                   gt_ref, gf_ref, sh_ref, sc_ref, wr_ref, brt_ref,
                   x1_ref, xp_ref, te_ref, tw_ref, *, tm, d_a, d_b, n_tiles):
    i = pl.program_id(1)
    half = d_b // 2
    z = p_ref[0].astype(F32)
    w = cbw_ref[...]
    zh = z[:, :half]
    col = lax.broadcasted_iota(I32, (tm, half), 0) % GRID_W
    left = jnp.where(col >= 1, pltpu.roll(zh, 1, 0), 0.0)
    right = jnp.where(col <= GRID_W - 2, pltpu.roll(zh, tm - 1, 0), 0.0)
    horiz = w[0:1, :half] * left + w[1:2, :half] * zh + w[2:3, :half] * right
    zv = z[:, half:]
    up_halo = jnp.where(i == 0, 0.0, pu_ref[0].astype(F32))
    dn_halo = jnp.where(i == n_tiles - 1, 0.0, pd_ref[0].astype(F32))
    if tm > GRID_W:
        up = jnp.concatenate([up_halo, zv[:tm - GRID_W]], axis=0)
        dn = jnp.concatenate([zv[GRID_W:], dn_halo], axis=0)
    else:
        up, dn = up_halo, dn_halo
    vert = w[0:1, half:] * up + w[1:2, half:] * zv + w[2:3, half:] * dn
    bb = bb_ref[0].astype(F32)
    yb = jnp.concatenate([bb[:, :half] * horiz, bb[:, half:] * vert], axis=1)
    ya = ya_ref[0].astype(F32)
    ya_n = _rms(ya, ga_ref[...]).astype(BF16)
    yb_n = _rms(yb, gb_ref[...]).astype(BF16)
    mix = (jnp.dot(ya_n, wo_ref[0:d_a, :], preferred_element_type=F32)
           + jnp.dot(yb_n, wo_ref[d_a:d_a + d_b, :], preferred_element_type=F32))
    x1 = x_ref[0] + gt_ref[0] * mix
    x1_ref[0] = x1
    xn = _rms(x1, gf_ref[...]) * (1.0 + sc_ref[0]) + sh_ref[0]
    xb = xn.astype(BF16)
    dh = xb.shape[1] // 2
    lo = lax.shift_right_logical(lax.bitcast_convert_type(xb[:, :dh].astype(F32), U32), jnp.uint32(16))
    hi = lax.bitcast_convert_type(xb[:, dh:].astype(F32), U32) & jnp.uint32(0xFFFF0000)
    xp_ref[0] = lo | hi
    logits = jnp.dot(xb, wr_ref[...], preferred_element_type=F32) + brt_ref[...]
    lane = lax.broadcasted_iota(I32, logits.shape, 1)
    lane_f = lane.astype(F32)
    vals = logits
    tv, te = [], []
    for _ in range(TOP_K):
        m = jnp.max(vals, axis=-1, keepdims=True)
        idx = jnp.min(jnp.where(vals == m, lane_f, float(LANES)), axis=-1, keepdims=True)
        tv.append(m)
        te.append(idx)
        vals = jnp.where(lane_f == idx, -jnp.inf, vals)
    ex = [jnp.exp(v - tv[0]) for v in tv]
    den = ex[0]
    for e in ex[1:]:
        den = den + e
    e_out = jnp.zeros(logits.shape, I32)
    w_out = jnp.zeros(logits.shape, F32)
    for k in range(TOP_K):
        e_out = jnp.where(lane == k, te[k].astype(I32), e_out)
        w_out = jnp.where(lane == k, ex[k] / den, w_out)
    te_ref[0] = e_out
    tw_ref[0] = w_out


def _mixout_call(ya, bb, p, cbw, ga, gb, wo_bf, x, gt1, gf, sh2, sc2, wr_bf, brt):
    bsz, n, d = x.shape
    d_a = ya.shape[-1]
    d_b = bb.shape[-1]
    half = d_b // 2
    tm = _pick(n, 256, GRID_W)
    n_tiles = n // tm
    rpt = tm // GRID_W
    n_rows = n // GRID_W
    row = lambda b, i: (b, i, 0)
    vec = lambda b, i: (b, 0, 0)
    const2 = lambda b, i: (0, 0)
    return pl.pallas_call(
        functools.partial(_mixout_kernel, tm=tm, d_a=d_a, d_b=d_b, n_tiles=n_tiles),
        out_shape=(_sds((bsz, n, d), F32), _sds((bsz, n, d // 2), U32),
                   _sds((bsz, n, LANES), I32), _sds((bsz, n, LANES), F32)),
        grid=(bsz, n_tiles),
        in_specs=[pl.BlockSpec((1, tm, d_a), row),
                  pl.BlockSpec((1, tm, d_b), row),
                  pl.BlockSpec((1, tm, d_b), row),
                  pl.BlockSpec((1, GRID_W, half), lambda b, i: (b, jnp.maximum(i * rpt - 1, 0), 1)),
                  pl.BlockSpec((1, GRID_W, half), lambda b, i: (b, jnp.minimum((i + 1) * rpt, n_rows - 1), 1)),
                  pl.BlockSpec((3, d_b), const2),
                  pl.BlockSpec((1, d_a), const2),
                  pl.BlockSpec((1, d_b), const2),
                  pl.BlockSpec((d_a + d_b, d), const2),
                  pl.BlockSpec((1, tm, d), row),
                  pl.BlockSpec((1, 1, d), vec),
                  pl.BlockSpec((1, d), const2),
                  pl.BlockSpec((1, 1, d), vec),
                  pl.BlockSpec((1, 1, d), vec),
                  pl.BlockSpec((d, LANES), const2),
                  pl.BlockSpec((1, LANES), const2)],
        out_specs=(pl.BlockSpec((1, tm, d), row), pl.BlockSpec((1, tm, d // 2), row),
                   pl.BlockSpec((1, tm, LANES), row), pl.BlockSpec((1, tm, LANES), row)),
        compiler_params=_params(("arbitrary", "arbitrary")),
        name="mixout",
    )(ya, bb, p, p, p, cbw, ga, gb, wo_bf, x, gt1, gf, sh2, sc2, wr_bf, brt)


def _rank_kernel(e_ref, rank_ref, cnt_ref, carry, *, tt):
    @pl.when(pl.program_id(0) == 0)
    def _():
        carry[...] = jnp.zeros(carry.shape, F32)

    e = e_ref[...]
    lane = lax.broadcasted_iota(I32, (tt, LANES), 1)
    ohs = []
    m = jnp.zeros((tt, LANES), F32)
    for k in range(TOP_K):
        oh = lane == e[:, k:k + 1]
        ohs.append(oh)
        m = m + jnp.where(oh, 1.0, 0.0)
    ri = lax.broadcasted_iota(I32, (tt, tt), 0)
    ci = lax.broadcasted_iota(I32, (tt, tt), 1)
    ltri = jnp.where(ri > ci, 1.0, 0.0).astype(BF16)
    pref = jnp.dot(ltri, m.astype(BF16), preferred_element_type=F32) + carry[0:1, :]
    out = jnp.zeros((tt, LANES), I32)
    for k in range(TOP_K):
        rk = jnp.sum(jnp.where(ohs[k], pref, 0.0), axis=-1, keepdims=True)
        out = jnp.where(lane == k, rk.astype(I32), out)
    rank_ref[...] = out
    tot = carry[0:1, :] + jnp.sum(m, axis=0, keepdims=True)
    carry[...] = jnp.broadcast_to(tot, carry.shape)
    cnt_ref[...] = jnp.broadcast_to(tot, cnt_ref.shape)


def _rank_call(top_e):
    t = top_e.shape[0]
    tt = _pick(t, 512, SUBLANES)
    return pl.pallas_call(
        functools.partial(_rank_kernel, tt=tt),
        out_shape=(_sds((t, LANES), I32), _sds((SUBLANES, LANES), F32)),
        grid=(t // tt,),
        in_specs=[pl.BlockSpec((tt, LANES), lambda i: (i, 0))],
        out_specs=(pl.BlockSpec((tt, LANES), lambda i: (i, 0)),
                   pl.BlockSpec((SUBLANES, LANES), lambda i: (0, 0))),
        scratch_shapes=[pltpu.VMEM((SUBLANES, LANES), F32)],
        compiler_params=_params(("arbitrary",)),
        name="rank",
    )(top_e)


def _dispatch_kernel(pos_ref, x_ref, o_hbm, sem, *, td):
    def body(t, c):
        src = x_ref.at[pl.ds(t, 1)]
        for k in range(TOP_K):
            dst = o_hbm.at[pl.ds(pos_ref[0, 0, t * TOP_K + k], 1)]
            pltpu.make_async_copy(src, dst, sem).start()
        return c

    lax.fori_loop(0, td, body, 0, unroll=2)
    for _ in range(TOP_K):
        pltpu.make_async_copy(x_ref, o_hbm.at[pl.ds(0, td)], sem).wait()


def _dispatch_call(xp, pos):
    t, dw = xp.shape
    td = _pick(t, 512, SUBLANES)
    pos3 = pos.reshape(t // td, 1, td * TOP_K)
    return pl.pallas_call(
        functools.partial(_dispatch_kernel, td=td),
        out_shape=_sds((t * TOP_K, dw), U32),
        grid=(t // td,),
        in_specs=[pl.BlockSpec((1, 1, td * TOP_K), lambda i: (i, 0, 0), memory_space=pltpu.SMEM),
                  pl.BlockSpec((td, dw), lambda i: (i, 0))],
        out_specs=pl.BlockSpec(memory_space=pl.ANY),
        scratch_shapes=[pltpu.SemaphoreType.DMA],
        compiler_params=_params(("arbitrary",)),
        name="dispatch",
    )(pos3, xp)


def _unpack_rows(xp):
    lo = lax.bitcast_convert_type(lax.shift_left(xp, jnp.uint32(16)), F32)
    hi = lax.bitcast_convert_type(xp & jnp.uint32(0xFFFF0000), F32)
    return jnp.concatenate([lo.astype(BF16), hi.astype(BF16)], axis=1)


def _store_rows(out_ref, new, lo, tm):
    @pl.when(lo == 0)
    def _():
        out_ref[...] = new

    @pl.when(lo > 0)
    def _():
        row = lax.broadcasted_iota(I32, new.shape, 0)
        out_ref[...] = jnp.where(row >= lo, new, out_ref[...])


def _gmm1_kernel(vt_ref, ve_ref, lo_ref, nv_ref, xs_ref, wg_ref, wu_ref, bg_ref, bu_ref, act_ref,
                 wbf, *, tm, tf):
    v = pl.program_id(1)

    @pl.when(v < nv_ref[0])
    def _():
        changed = jnp.logical_or(v == 0, ve_ref[v] != ve_ref[jnp.maximum(v - 1, 0)])

        @pl.when(changed)
        def _():
            wbf[:, 0:tf] = wg_ref[0].astype(BF16)
            wbf[:, tf:2 * tf] = wu_ref[0].astype(BF16)

        h = _unpack_rows(xs_ref[...])
        gu = jnp.dot(h, wbf[...], preferred_element_type=F32)
        gate = jnp.minimum(gu[:, :tf] + bg_ref[0], SWIGLU_LIMIT)
        up = jnp.clip(gu[:, tf:] + bu_ref[0], -SWIGLU_LIMIT, SWIGLU_LIMIT)
        act = (up + 1.0) * gate * _sigmoid(SWIGLU_ALPHA * gate)
        _store_rows(act_ref, act.astype(BF16), lo_ref[v], tm)


def _gmm1_call(meta, xs, wg, wu, bg, bu, tm):
    vt, ve, lo, nv = meta
    n_rows, dw = xs.shape
    n_e, d, dff = wg.shape
    tf = _pick(dff, 512)
    nf = dff // tf
    n_v = vt.shape[0]
    return pl.pallas_call(
        functools.partial(_gmm1_kernel, tm=tm, tf=tf),
        out_shape=_sds((n_rows, dff), BF16),
        grid_spec=pltpu.PrefetchScalarGridSpec(
            num_scalar_prefetch=4,
            grid=(nf, n_v),
            in_specs=[pl.BlockSpec((tm, dw), lambda f, v, vt, ve, lo, nv: (vt[v], 0)),
                      pl.BlockSpec((1, d, tf), lambda f, v, vt, ve, lo, nv: (ve[v], 0, f)),
                      pl.BlockSpec((1, d, tf), lambda f, v, vt, ve, lo, nv: (ve[v], 0, f)),
                      pl.BlockSpec((1, 1, tf), lambda f, v, vt, ve, lo, nv: (ve[v], 0, f)),
                      pl.BlockSpec((1, 1, tf), lambda f, v, vt, ve, lo, nv: (ve[v], 0, f))],
            out_specs=pl.BlockSpec((tm, tf), lambda f, v, vt, ve, lo, nv: (vt[v], f)),
            scratch_shapes=[pltpu.VMEM((d, 2 * tf), BF16)]),
        compiler_params=_params(("arbitrary", "arbitrary")),
        name="gmm1",
    )(vt, ve, lo, nv, xs, wg, wu, bg, bu)


def _gmm2_kernel(vt_ref, ve_ref, lo_ref, nv_ref, act_ref, wd_ref, bd_ref, y_ref, wbf, *, tm):
    v = pl.program_id(1)

    @pl.when(v < nv_ref[0])
    def _():
        changed = jnp.logical_or(v == 0, ve_ref[v] != ve_ref[jnp.maximum(v - 1, 0)])

        @pl.when(changed)
        def _():
            wbf[...] = wd_ref[0].astype(BF16)

        y = jnp.dot(act_ref[...], wbf[...], preferred_element_type=F32) + bd_ref[0]
        _store_rows(y_ref, y, lo_ref[v], tm)


def _gmm2_call(meta, act, wd, bd, tm):
    vt, ve, lo, nv = meta
    n_rows, dff = act.shape
    n_e, _, d = wd.shape
    tn = _pick(d, 1024)
    nn = d // tn
    n_v = vt.shape[0]
    return pl.pallas_call(
        functools.partial(_gmm2_kernel, tm=tm),
        out_shape=_sds((n_rows, d), F32),
        grid_spec=pltpu.PrefetchScalarGridSpec(
            num_scalar_prefetch=4,
            grid=(nn, n_v),
            in_specs=[pl.BlockSpec((tm, dff), lambda n, v, vt, ve, lo, nv: (vt[v], 0)),
                      pl.BlockSpec((1, dff, tn), lambda n, v, vt, ve, lo, nv: (ve[v], 0, n)),
                      pl.BlockSpec((1, 1, tn), lambda n, v, vt, ve, lo, nv: (ve[v], 0, n))],
            out_specs=pl.BlockSpec((tm, tn), lambda n, v, vt, ve, lo, nv: (vt[v], n)),
            scratch_shapes=[pltpu.VMEM((dff, tn), BF16)]),
        compiler_params=_params(("arbitrary", "arbitrary")),
        name="gmm2",
    )(vt, ve, lo, nv, act, wd, bd)


def _final_kernel(pos_ref, posn_ref, y_hbm, tw_ref, x1_ref, gt_ref, gfin_ref, o_ref, ybuf, sem, *,
                  tc, n_steps):
    step = pl.program_id(0) * pl.num_programs(1) + pl.program_id(1)

    def issue(p_ref, slot):
        def body(t, c):
            for k in range(TOP_K):
                src = y_hbm.at[pl.ds(p_ref[0, 0, t * TOP_K + k], 1)]
                pltpu.make_async_copy(src, ybuf.at[slot, k, pl.ds(t, 1)], sem.at[slot]).start()
            return c

        lax.fori_loop(0, tc, body, 0, unroll=2)

    def run(cur):
        @pl.when(step + 1 < n_steps)
        def _():
            issue(posn_ref, 1 - cur)

        for k in range(TOP_K):
            pltpu.make_async_copy(y_hbm.at[pl.ds(0, tc)], ybuf.at[cur, k], sem.at[cur]).wait()
        tw = tw_ref[0]
        moe = tw[:, 0:1] * ybuf[cur, 0]
        for k in range(1, TOP_K):
            moe = moe + tw[:, k:k + 1] * ybuf[cur, k]
        x2 = x1_ref[0] + gt_ref[0] * moe
        o_ref[0] = _rms(x2, gfin_ref[...])

    @pl.when(step == 0)
    def _():
        issue(pos_ref, 0)

    parity = lax.rem(step, 2)

    @pl.when(parity == 0)
    def _():
        run(0)

    @pl.when(parity == 1)
    def _():
        run(1)


def _final_call(y, pos, top_w, x1, gt2, gfin):
    bsz, n, d = x1.shape
    tc = _pick(n, 256, SUBLANES)
    nt = n // tc
    n_steps = bsz * nt
    pos3 = pos.reshape(n_steps, 1, tc * TOP_K)
    row = lambda b, i: (b, i, 0)
    return pl.pallas_call(
        functools.partial(_final_kernel, tc=tc, n_steps=n_steps),
        out_shape=_sds((bsz, n, d), F32),
        grid=(bsz, nt),
        in_specs=[pl.BlockSpec((1, 1, tc * TOP_K), lambda b, i: (b * nt + i, 0, 0), memory_space=pltpu.SMEM),
                  pl.BlockSpec((1, 1, tc * TOP_K), lambda b, i: (jnp.minimum(b * nt + i + 1, n_steps - 1), 0, 0),
                               memory_space=pltpu.SMEM),
                  pl.BlockSpec(memory_space=pl.ANY),
                  pl.BlockSpec((1, tc, LANES), row),
                  pl.BlockSpec((1, tc, d), row),
                  pl.BlockSpec((1, 1, d), lambda b, i: (b, 0, 0)),
                  pl.BlockSpec((1, d), lambda b, i: (0, 0))],
        out_specs=pl.BlockSpec((1, tc, d), row),
        scratch_shapes=[pltpu.VMEM((2, TOP_K, tc, d), F32), pltpu.SemaphoreType.DMA((2,))],
        compiler_params=_params(("arbitrary", "arbitrary")),
        name="final",
    )(pos3, pos3, y, top_w, x1, gt2, gfin)


def _visit_tables(counts, n_rows, tm):
    n_e = counts.shape[0]
    ends = jnp.cumsum(counts)
    starts = ends - counts
    first = starts // tm
    last = jnp.maximum(ends - 1, 0) // tm
    nvis = jnp.where(counts > 0, last - first + 1, 0)
    vend = jnp.cumsum(nvis)
    vstart = vend - nvis
    nv = vend[-1]
    n_v = n_rows // tm + n_e - 1
    v = jnp.arange(n_v, dtype=I32)
    ve = jnp.clip(jnp.searchsorted(vend, v, side="right"), 0, n_e - 1).astype(I32)
    vt = first[ve] + (v - vstart[ve])
    lo = jnp.maximum(starts[ve] - vt * tm, 0)
    valid = v < nv
    last_v = jnp.maximum(nv - 1, 0)
    ve = jnp.where(valid, ve, ve[last_v]).astype(I32)
    vt = jnp.where(valid, vt, vt[last_v]).astype(I32)
    lo = jnp.where(valid, lo, 0).astype(I32)
    return starts, (vt, ve, lo, nv.reshape(1).astype(I32))


def kernel(x, c, ctx, c_ctx, w_mod, b_mod, g_mix, w_in, conv_a_w, conv_a_b, lru_w_r, lru_b_r,
           lru_w_i, lru_b_i, lru_lam, conv_b_w, g_out_a, g_out_b, w_out, g_ffn, w_router,
           b_router, w_gate, b_gate, w_up, b_up, w_down, b_down, g_final):
    assert w_mod.shape[0] == 1, "single-layer block"
    bsz, n_lat, d = x.shape
    d_a = conv_a_w.shape[-1]
    d_b = conv_b_w.shape[-1]
    n_e = w_router.shape[-1]
    assert n_lat % GRID_W == 0 and n_e <= LANES and d % (2 * LANES) == 0
    l = 0

    cs = jnp.zeros((SUBLANES, d), F32).at[:bsz].set(c).at[bsz].set(c_ctx)
    mod = _mod_call(cs, w_mod[l], b_mod[l][None])
    sh1, sc1, gt1, sh2, sc2, gt2 = [m[:bsz, None, :] for m in jnp.split(mod, 6, axis=-1)]
    ssh1, ssc1 = [jnp.broadcast_to(m[bsz][None, None, :], (bsz, 1, d))
                  for m in jnp.split(mod, 6, axis=-1)[:2]]

    w_in_bf = w_in[l].astype(BF16)
    g_mix2 = g_mix[l][None]
    cw = conv_a_w[l]
    cb = conv_a_b[l][None]
    wri = jnp.concatenate([lru_w_r[l], lru_w_i[l]], axis=-1).astype(BF16)
    br, bi, lam = lru_b_r[l], lru_b_i[l], lru_lam[l]
    lru_p = lambda dr: (cw, cb, wri[dr], br[dr][None], bi[dr][None], lam[dr][None])

    (s_ax,) = _inproj_call(ctx, g_mix2, ssh1, ssc1, w_in_bf[:, d_a:2 * d_a], d_a, d_b, latent=False)
    zero_state = jnp.zeros((bsz, 1, d_a), F32)
    _, h0f = _lru_call(s_ax, zero_state, *lru_p(0), reverse=False)
    _, h0b = _lru_call(s_ax, zero_state, *lru_p(1), reverse=True)

    ag, ax, bb, p = _inproj_call(x, g_mix2, sh1, sc1, w_in_bf, d_a, d_b, latent=True)
    hf, _ = _lru_call(ax, h0f, *lru_p(0), reverse=False)
    ya, _ = _lru_call(ax, h0b, *lru_p(1), reverse=True, hf=hf, ag=ag)

    wr_bf = jnp.zeros((d, LANES), BF16).at[:, :n_e].set(w_router[l].astype(BF16))
    brt = jnp.full((1, LANES), NEG_BIG, F32).at[0, :n_e].set(b_router[l])
    x1, xp, top_e, top_w = _mixout_call(
        ya, bb, p, conv_b_w[l], g_out_a[l][None], g_out_b[l][None], w_out[l].astype(BF16), x, gt1,
        g_ffn[l][None], sh2, sc2, wr_bf, brt)

    n_tok = bsz * n_lat
    n_rows = n_tok * TOP_K
    tm = _pick(n_rows, 512, SUBLANES)
    rank, cnt = _rank_call(top_e.reshape(n_tok, LANES))
    counts = cnt[0, :n_e].astype(I32)
    starts, meta = _visit_tables(counts, n_rows, tm)
    te = top_e.reshape(n_tok, LANES)[:, :TOP_K]
    onehot = te[:, :, None] == jnp.arange(n_e, dtype=I32)[None, None, :]
    pos = (jnp.sum(jnp.where(onehot, starts[None, None, :], 0), axis=-1) + rank[:, :TOP_K]).astype(I32)
    xs = _dispatch_call(xp.reshape(n_tok, d // 2), pos)
    act = _gmm1_call(meta, xs, w_gate[l], w_up[l], b_gate[l][:, None, :], b_up[l][:, None, :], tm)
    y = _gmm2_call(meta, act, w_down[l], b_down[l][:, None, :], tm)
    return _final_call(y, pos, top_w, x1, gt2, g_final[None])
```

```python
import functools

import jax
import jax.numpy as jnp
from jax import lax
from jax.experimental import pallas as pl
from jax.experimental.pallas import tpu as pltpu

F32 = jnp.float32
BF16 = jnp.bfloat16
I32 = jnp.int32
U32 = jnp.uint32

GRID_W = 64
TOP_K = 4
LRU_C = 8.0
CONV_A_LEFT = 2
SWIGLU_LIMIT = 7.0
SWIGLU_ALPHA = 1.702
EPS = 1e-6
MOE_ROW_BLOCK = 256

LANES = 128
SUBLANES = 8
VMEM_LIMIT_BYTES = 56 * 1024 * 1024
NEG_BIG = -1e30


def _sds(shape, dtype):
    return jax.ShapeDtypeStruct(shape, dtype)


def _pick(n, pref, mult=LANES):
    if n <= pref:
        return n
    t = (pref // mult) * mult
    while t >= mult:
        if n % t == 0:
            return t
        t -= mult
    return n


def _params(sem):
    return pltpu.CompilerParams(dimension_semantics=sem, vmem_limit_bytes=VMEM_LIMIT_BYTES)


def _sigmoid(x):
    return 1.0 / (1.0 + jnp.exp(-x))


def _rms(x, g):
    ms = jnp.mean(x * x, axis=-1, keepdims=True)
    return (x * lax.rsqrt(ms + EPS)) * g


def _mod_kernel(c_ref, w_ref, b_ref, o_ref):
    c = c_ref[...]
    s = c * _sigmoid(c)
    o_ref[...] = jnp.dot(s.astype(BF16), w_ref[...].astype(BF16),
                         preferred_element_type=F32) + b_ref[...]


def _mod_call(cs, w, b):
    d, n6 = w.shape
    tn = _pick(n6, 1024)
    return pl.pallas_call(
        _mod_kernel,
        out_shape=_sds((cs.shape[0], n6), F32),
        grid=(n6 // tn,),
        in_specs=[pl.BlockSpec((cs.shape[0], d), lambda j: (0, 0)),
                  pl.BlockSpec((d, tn), lambda j: (0, j)),
                  pl.BlockSpec((1, tn), lambda j: (0, j))],
        out_specs=pl.BlockSpec((cs.shape[0], tn), lambda j: (0, j)),
        compiler_params=_params(("arbitrary",)),
        name="mod",
    )(cs, w, b)


def _inproj_kernel(x_ref, g_ref, sh_ref, sc_ref, w_ref, *out_refs, d_a, d_b, latent):
    x = x_ref[0]
    xn = _rms(x, g_ref[...]) * (1.0 + sc_ref[0]) + sh_ref[0]
    xb = xn.astype(BF16)

    def sec(lo, width):
        return jnp.dot(xb, w_ref[:, lo:lo + width], preferred_element_type=F32)

    if not latent:
        out_refs[0][0] = sec(0, d_a)
        return
    ag_ref, ax_ref, bb_ref, p_ref = out_refs
    ag_ref[0] = sec(0, d_a).astype(BF16)
    ax_ref[0] = sec(d_a, d_a)
    bb_ref[0] = sec(2 * d_a, d_b).astype(BF16)
    p_ref[0] = (sec(2 * d_a + d_b, d_b) * sec(2 * d_a + 2 * d_b, d_b)).astype(BF16)


def _inproj_call(x, g, sh, sc, w_bf, d_a, d_b, latent):
    bsz, n, d = x.shape
    tm = _pick(n, 512, SUBLANES)
    n_w = w_bf.shape[1]
    row = lambda b, i: (b, i, 0)
    if latent:
        out_shape = (_sds((bsz, n, d_a), BF16), _sds((bsz, n, d_a), F32),
                     _sds((bsz, n, d_b), BF16), _sds((bsz, n, d_b), BF16))
        out_specs = (pl.BlockSpec((1, tm, d_a), row), pl.BlockSpec((1, tm, d_a), row),
                     pl.BlockSpec((1, tm, d_b), row), pl.BlockSpec((1, tm, d_b), row))
    else:
        out_shape = (_sds((bsz, n, d_a), F32),)
        out_specs = (pl.BlockSpec((1, tm, d_a), row),)
    return pl.pallas_call(
        functools.partial(_inproj_kernel, d_a=d_a, d_b=d_b, latent=latent),
        out_shape=out_shape,
        grid=(bsz, n // tm),
        in_specs=[pl.BlockSpec((1, tm, d), row),
                  pl.BlockSpec((1, d), lambda b, i: (0, 0)),
                  pl.BlockSpec((1, 1, d), lambda b, i: (b, 0, 0)),
                  pl.BlockSpec((1, 1, d), lambda b, i: (b, 0, 0)),
                  pl.BlockSpec((d, n_w), lambda b, i: (0, 0), pipeline_mode=pl.Buffered(1))],
        out_specs=out_specs,
        compiler_params=_params(("arbitrary", "arbitrary")),
        name="inproj_lat" if latent else "inproj_ctx",
    )(x, g, sh, sc, w_bf)


def _gelu_tanh(x):
    c = 0.7978845608028654
    return x * (0.5 * (1.0 + jnp.tanh(c * (x + 0.044715 * (x * x * x)))))


def _lru_kernel(*refs, reverse, combine, nc, tl, heads, blk):
    if combine:
        (prev_ref, main_ref, next_ref, cw_ref, cb_ref, wri_ref, br_ref, bi_ref, lam_ref, h0_ref,
         hf_ref, ag_ref, out_ref, hlast_ref, ebuf, xc_s, a_s, b_s, carry) = refs
    else:
        (prev_ref, main_ref, next_ref, cw_ref, cb_ref, wri_ref, br_ref, bi_ref, lam_ref, h0_ref,
         out_ref, hlast_ref, ebuf, xc_s, a_s, b_s, carry) = refs
    da = heads * blk
    c = pl.program_id(1)
    cidx = (nc - 1 - c) if reverse else c

    @pl.when(c == 0)
    def _():
        carry[...] = jnp.broadcast_to(h0_ref[0], carry.shape)

    main = main_ref[0]
    zero8 = jnp.zeros((SUBLANES, da), F32)
    ebuf[0:SUBLANES, :] = jnp.where(cidx == 0, zero8, prev_ref[0])
    ebuf[SUBLANES:SUBLANES + tl, :] = main
    ebuf[SUBLANES + tl:2 * SUBLANES + tl, :] = jnp.where(cidx == nc - 1, zero8, next_ref[0])
    cw = cw_ref[...]
    off = SUBLANES - CONV_A_LEFT
    xc_s[...] = (cw[0:1] * ebuf[off:off + tl, :] + cw[1:2] * ebuf[off + 1:off + 1 + tl, :]
                 + cw[2:3] * main + cw[3:4] * ebuf[off + 3:off + 3 + tl, :] + cb_ref[...])

    z = -lam_ref[...]
    sp = jnp.maximum(z, 0.0) + jnp.log1p(jnp.exp(-jnp.abs(z)))
    rc = min(tl, 128)
    for r0 in range(0, tl, rc):
        for h in range(heads):
            cs = slice(h * blk, (h + 1) * blk)
            xh = xc_s[r0:r0 + rc, cs]
            zz = jnp.dot(xh.astype(BF16), wri_ref[h], preferred_element_type=F32)
            r = _sigmoid(zz[:, :blk] + br_ref[:, cs])
            i = _sigmoid(zz[:, blk:] + bi_ref[:, cs])
            log_a = (-LRU_C * r) * sp[:, cs]
            a = jnp.exp(log_a)
            a_s[r0:r0 + rc, cs] = a
            b_s[r0:r0 + rc, cs] = jnp.sqrt(jnp.tanh(-log_a) * (1.0 + a * a)) * (i * xh)

    row = lax.broadcasted_iota(I32, (SUBLANES, da), 0)
    ng = tl // SUBLANES
    shifts = (1, 2, 4)

    def body(gi, hc):
        g = (ng - 1 - gi) if reverse else gi
        r0 = pl.multiple_of(g * SUBLANES, SUBLANES)
        a = a_s[pl.ds(r0, SUBLANES), :]
        b = b_s[pl.ds(r0, SUBLANES), :]
        for s in shifts:
            if reverse:
                keep = row < (SUBLANES - s)
                sh = SUBLANES - s
            else:
                keep = row >= s
                sh = s
            a_sh = jnp.where(keep, pltpu.roll(a, sh, 0), 1.0)
            b_sh = jnp.where(keep, pltpu.roll(b, sh, 0), 0.0)
            b = a * b_sh + b
            a = a * a_sh
        hrows = a * hc + b
        b_s[pl.ds(r0, SUBLANES), :] = hrows
        edge = hrows[0:1, :] if reverse else hrows[SUBLANES - 1:SUBLANES, :]
        return jnp.broadcast_to(edge, (SUBLANES, da))

    hc = lax.fori_loop(0, ng, body, carry[...])
    carry[...] = hc
    hlast_ref[0] = hc[0:1, :]
    if combine:
        hsum = hf_ref[0] + b_s[...]
        out_ref[0] = (_gelu_tanh(ag_ref[0].astype(F32)) * hsum).astype(out_ref.dtype)
    else:
        out_ref[0] = b_s[...]


def _lru_call(ax, h0, cw, cb, wri, br, bi, lam, *, reverse, hf=None, ag=None):
    bsz, n, da = ax.shape
    heads, blk, _ = wri.shape
    tl = _pick(n, 512, SUBLANES)
    nc = n // tl
    nb8 = n // SUBLANES
    g8 = tl // SUBLANES
    combine = hf is not None

    def cidx(c):
        return (nc - 1 - c) if reverse else c

    main_map = lambda b, c: (b, cidx(c), 0)
    prev_map = lambda b, c: (b, jnp.maximum(cidx(c) * g8 - 1, 0), 0)
    next_map = lambda b, c: (b, jnp.minimum((cidx(c) + 1) * g8, nb8 - 1), 0)
    const2 = lambda b, c: (0, 0)
    in_specs = [pl.BlockSpec((1, SUBLANES, da), prev_map),
                pl.BlockSpec((1, tl, da), main_map),
                pl.BlockSpec((1, SUBLANES, da), next_map),
                pl.BlockSpec((4, da), const2),
                pl.BlockSpec((1, da), const2),
                pl.BlockSpec((heads, blk, 2 * blk), lambda b, c: (0, 0, 0)),
                pl.BlockSpec((1, da), const2),
                pl.BlockSpec((1, da), const2),
                pl.BlockSpec((1, da), const2),
                pl.BlockSpec((1, 1, da), lambda b, c: (b, 0, 0))]
    args = [ax, ax, ax, cw, cb, wri, br, bi, lam, h0]
    if combine:
        in_specs += [pl.BlockSpec((1, tl, da), main_map), pl.BlockSpec((1, tl, da), main_map)]
        args += [hf, ag]
    out_dtype = BF16 if combine else F32
    return pl.pallas_call(
        functools.partial(_lru_kernel, reverse=reverse, combine=combine, nc=nc, tl=tl,
                          heads=heads, blk=blk),
        out_shape=(_sds((bsz, n, da), out_dtype), _sds((bsz, 1, da), F32)),
        grid=(bsz, nc),
        in_specs=in_specs,
        out_specs=(pl.BlockSpec((1, tl, da), main_map),
                   pl.BlockSpec((1, 1, da), lambda b, c: (b, 0, 0))),
        scratch_shapes=[pltpu.VMEM((tl + 2 * SUBLANES, da), F32),
                        pltpu.VMEM((tl, da), F32),
                        pltpu.VMEM((tl, da), F32),
                        pltpu.VMEM((tl, da), F32),
                        pltpu.VMEM((SUBLANES, da), F32)],
        compiler_params=_params(("arbitrary", "arbitrary")),
        name=("lru_bwd" if reverse else "lru_fwd") + ("_mix" if combine else ""),
    )(*args)


def _mixout_kernel(ya_ref, bb_ref, p_ref, pu_ref, pd_ref, cbw_ref, ga_ref, gb_ref, wo_ref, x_ref,
                   gt_ref, gf_ref, sh_ref, sc_ref, wr_ref, brt_ref,
                   x1_ref, xp_ref, te_ref, tw_ref, *, tm, d_a, d_b, n_tiles):
    i = pl.program_id(1)
    half = d_b // 2
    z = p_ref[0].astype(F32)
    w = cbw_ref[...]
    zh = z[:, :half]
    col = lax.broadcasted_iota(I32, (tm, half), 0) % GRID_W
    left = jnp.where(col >= 1, pltpu.roll(zh, 1, 0), 0.0)
    right = jnp.where(col <= GRID_W - 2, pltpu.roll(zh, tm - 1, 0), 0.0)
    horiz = w[0:1, :half] * left + w[1:2, :half] * zh + w[2:3, :half] * right
    zv = z[:, half:]
    up_halo = jnp.where(i == 0, 0.0, pu_ref[0].astype(F32))
    dn_halo = jnp.where(i == n_tiles - 1, 0.0, pd_ref[0].astype(F32))
    if tm > GRID_W:
        up = jnp.concatenate([up_halo, zv[:tm - GRID_W]], axis=0)
        dn = jnp.concatenate([zv[GRID_W:], dn_halo], axis=0)
    else:
        up, dn = up_halo, dn_halo
    vert = w[0:1, half:] * up + w[1:2, half:] * zv + w[2:3, half:] * dn
    bb = bb_ref[0].astype(F32)
    yb = jnp.concatenate([bb[:, :half] * horiz, bb[:, half:] * vert], axis=1)
    ya = ya_ref[0].astype(F32)
    ya_n = _rms(ya, ga_ref[...]).astype(BF16)
    yb_n = _rms(yb, gb_ref[...]).astype(BF16)
    mix = (jnp.dot(ya_n, wo_ref[0:d_a, :], preferred_element_type=F32)
           + jnp.dot(yb_n, wo_ref[d_a:d_a + d_b, :], preferred_element_type=F32))
    x1 = x_ref[0] + gt_ref[0] * mix
    x1_ref[0] = x1
    xn = _rms(x1, gf_ref[...]) * (1.0 + sc_ref[0]) + sh_ref[0]
    xb = xn.astype(BF16)
    dh = xb.shape[1] // 2
    lo = lax.shift_right_logical(lax.bitcast_convert_type(xb[:, :dh].astype(F32), U32), jnp.uint32(16))
    hi = lax.bitcast_convert_type(xb[:, dh:].astype(F32), U32) & jnp.uint32(0xFFFF0000)
    xp_ref[0] = lo | hi
    logits = jnp.dot(xb, wr_ref[...], preferred_element_type=F32) + brt_ref[...]
    lane = lax.broadcasted_iota(I32, logits.shape, 1)
    lane_f = lane.astype(F32)
    vals = logits
    tv, te = [], []
    for _ in range(TOP_K):
        m = jnp.max(vals, axis=-1, keepdims=True)
        idx = jnp.min(jnp.where(vals == m, lane_f, float(LANES)), axis=-1, keepdims=True)
        tv.append(m)
        te.append(idx)
        vals = jnp.where(lane_f == idx, -jnp.inf, vals)
    ex = [jnp.exp(v - tv[0]) for v in tv]
    den = ex[0]
    for e in ex[1:]:
        den = den + e
    e_out = jnp.zeros(logits.shape, I32)
    w_out = jnp.zeros(logits.shape, F32)
    for k in range(TOP_K):
        e_out = jnp.where(lane == k, te[k].astype(I32), e_out)
        w_out = jnp.where(lane == k, ex[k] / den, w_out)
    te_ref[0] = e_out
    tw_ref[0] = w_out


def _mixout_call(ya, bb, p, cbw, ga, gb, wo_bf, x, gt1, gf, sh2, sc2, wr_bf, brt):
    bsz, n, d = x.shape
    d_a = ya.shape[-1]
    d_b = bb.shape[-1]
    half = d_b // 2
    tm = _pick(n, 256, GRID_W)
    n_tiles = n // tm
    rpt = tm // GRID_W
    n_rows = n // GRID_W
    row = lambda b, i: (b, i, 0)
    vec = lambda b, i: (b, 0, 0)
    const2 = lambda b, i: (0, 0)
    return pl.pallas_call(
        functools.partial(_mixout_kernel, tm=tm, d_a=d_a, d_b=d_b, n_tiles=n_tiles),
        out_shape=(_sds((bsz, n, d), F32), _sds((bsz, n, d // 2), U32),
                   _sds((bsz, n, LANES), I32), _sds((bsz, n, LANES), F32)),
        grid=(bsz, n_tiles),
        in_specs=[pl.BlockSpec((1, tm, d_a), row),
                  pl.BlockSpec((1, tm, d_b), row),
                  pl.BlockSpec((1, tm, d_b), row),
                  pl.BlockSpec((1, GRID_W, half), lambda b, i: (b, jnp.maximum(i * rpt - 1, 0), 1)),
                  pl.BlockSpec((1, GRID_W, half), lambda b, i: (b, jnp.minimum((i + 1) * rpt, n_rows - 1), 1)),
                  pl.BlockSpec((3, d_b), const2),
                  pl.BlockSpec((1, d_a), const2),
                  pl.BlockSpec((1, d_b), const2),
                  pl.BlockSpec((d_a + d_b, d), const2),
                  pl.BlockSpec((1, tm, d), row),
                  pl.BlockSpec((1, 1, d), vec),
                  pl.BlockSpec((1, d), const2),
                  pl.BlockSpec((1, 1, d), vec),
                  pl.BlockSpec((1, 1, d), vec),
                  pl.BlockSpec((d, LANES), const2),
                  pl.BlockSpec((1, LANES), const2)],
        out_specs=(pl.BlockSpec((1, tm, d), row), pl.BlockSpec((1, tm, d // 2), row),
                   pl.BlockSpec((1, tm, LANES), row), pl.BlockSpec((1, tm, LANES), row)),
        compiler_params=_params(("arbitrary", "arbitrary")),
        name="mixout",
    )(ya, bb, p, p, p, cbw, ga, gb, wo_bf, x, gt1, gf, sh2, sc2, wr_bf, brt)


def _rank_kernel(e_ref, rank_ref, cnt_ref, carry, *, tt):
    @pl.when(pl.program_id(0) == 0)
    def _():
        carry[...] = jnp.zeros(carry.shape, F32)

    e = e_ref[...]
    lane = lax.broadcasted_iota(I32, (tt, LANES), 1)
    ohs = []
    m = jnp.zeros((tt, LANES), F32)
    for k in range(TOP_K):
        oh = lane == e[:, k:k + 1]
        ohs.append(oh)
        m = m + jnp.where(oh, 1.0, 0.0)
    ri = lax.broadcasted_iota(I32, (tt, tt), 0)
    ci = lax.broadcasted_iota(I32, (tt, tt), 1)
    ltri = jnp.where(ri > ci, 1.0, 0.0).astype(BF16)
    pref = jnp.dot(ltri, m.astype(BF16), preferred_element_type=F32) + carry[0:1, :]
    out = jnp.zeros((tt, LANES), I32)
    for k in range(TOP_K):
        rk = jnp.sum(jnp.where(ohs[k], pref, 0.0), axis=-1, keepdims=True)
        out = jnp.where(lane == k, rk.astype(I32), out)
    rank_ref[...] = out
    tot = carry[0:1, :] + jnp.sum(m, axis=0, keepdims=True)
    carry[...] = jnp.broadcast_to(tot, carry.shape)
    cnt_ref[...] = jnp.broadcast_to(tot, cnt_ref.shape)


def _rank_call(top_e):
    t = top_e.shape[0]
    tt = _pick(t, 512, SUBLANES)
    return pl.pallas_call(
        functools.partial(_rank_kernel, tt=tt),
        out_shape=(_sds((t, LANES), I32), _sds((SUBLANES, LANES), F32)),
        grid=(t // tt,),
        in_specs=[pl.BlockSpec((tt, LANES), lambda i: (i, 0))],
        out_specs=(pl.BlockSpec((tt, LANES), lambda i: (i, 0)),
                   pl.BlockSpec((SUBLANES, LANES), lambda i: (0, 0))),
        scratch_shapes=[pltpu.VMEM((SUBLANES, LANES), F32)],
        compiler_params=_params(("arbitrary",)),
        name="rank",
    )(top_e)


def _zero_tail(bs_ref, nb_ref, n_e, n_blocks, zbuf, dst_block, sem):
    zbuf[...] = jnp.zeros(zbuf.shape, zbuf.dtype)
    first = bs_ref[n_e - 1] + nb_ref[n_e - 1]

    def start(c, carry):
        pltpu.make_async_copy(zbuf, dst_block(c), sem).start()
        return carry

    def wait(c, carry):
        pltpu.make_async_copy(zbuf, dst_block(c), sem).wait()
        return carry

    lax.fori_loop(first, n_blocks, start, 0)
    lax.fori_loop(first, n_blocks, wait, 0)


def _dispatch_kernel(bs_ref, nb_ref, pos_ref, x_ref, o_hbm, zbuf, sem, zsem, *, td, rb, n_e, n_blocks):
    @pl.when(pl.program_id(0) == 0)
    def _():
        _zero_tail(bs_ref, nb_ref, n_e, n_blocks, zbuf,
                   lambda c: o_hbm.at[pl.ds(pl.multiple_of(c * rb, rb), rb)], zsem)

        def zero_copy(e):
            r = pl.multiple_of((bs_ref[e] + nb_ref[e] - 1) * rb, rb)
            return pltpu.make_async_copy(zbuf, o_hbm.at[pl.ds(r, rb)], zsem)

        for e in range(n_e):
            @pl.when(nb_ref[e] > 0)
            def _():
                zero_copy(e).start()

        for e in range(n_e):
            @pl.when(nb_ref[e] > 0)
            def _():
                zero_copy(e).wait()

    def body(t, c):
        src = x_ref.at[pl.ds(t, 1)]
        for k in range(TOP_K):
            dst = o_hbm.at[pl.ds(pos_ref[0, 0, t * TOP_K + k], 1)]
            pltpu.make_async_copy(src, dst, sem).start()
        return c

    lax.fori_loop(0, td, body, 0, unroll=2)
    for _ in range(TOP_K):
        pltpu.make_async_copy(x_ref, o_hbm.at[pl.ds(0, td)], sem).wait()


def _dispatch_call(groups, xp, pos, n_buf, rb):
    bstart, nblk = groups
    n_e = bstart.shape[0]
    t, dw = xp.shape
    td = _pick(t, 512, SUBLANES)
    pos3 = pos.reshape(t // td, 1, td * TOP_K)
    return pl.pallas_call(
        functools.partial(_dispatch_kernel, td=td, rb=rb, n_e=n_e, n_blocks=n_buf // rb),
        out_shape=_sds((n_buf, dw), U32),
        grid_spec=pltpu.PrefetchScalarGridSpec(
            num_scalar_prefetch=2,
            grid=(t // td,),
            in_specs=[pl.BlockSpec((1, 1, td * TOP_K), lambda i, bs, nb: (i, 0, 0),
                                   memory_space=pltpu.SMEM),
                      pl.BlockSpec((td, dw), lambda i, bs, nb: (i, 0))],
            out_specs=pl.BlockSpec(memory_space=pl.ANY),
            scratch_shapes=[pltpu.VMEM((rb, dw), U32), pltpu.SemaphoreType.DMA,
                            pltpu.SemaphoreType.DMA]),
        compiler_params=_params(("arbitrary",)),
        name="dispatch",
    )(bstart, nblk, pos3, xp)


def _unpack_rows(xp):
    lo = lax.bitcast_convert_type(lax.shift_left(xp, jnp.uint32(16)), F32)
    hi = lax.bitcast_convert_type(xp & jnp.uint32(0xFFFF0000), F32)
    return jnp.concatenate([lo.astype(BF16), hi.astype(BF16)], axis=1)


def _group_loop(nb, b0, rb, in_copy, out_copy, compute):
    @pl.when(nb > 0)
    def _():
        in_copy(0, 0).start()

        def body(c, carry):
            slot = lax.rem(c, 2)

            @pl.when(c + 1 < nb)
            def _():
                in_copy(c + 1, 1 - slot).start()

            in_copy(c, slot).wait()

            @pl.when(c >= 2)
            def _():
                out_copy(c - 2, slot).wait()

            compute(slot)
            out_copy(c, slot).start()
            return carry

        lax.fori_loop(0, nb, body, 0)

        @pl.when(nb >= 2)
        def _():
            out_copy(nb - 2, lax.rem(nb, 2)).wait()

        out_copy(nb - 1, lax.rem(nb - 1, 2)).wait()


def _gmm1_kernel(bs_ref, nb_ref, xs_hbm, wg_ref, wu_ref, bg_ref, bu_ref, act_hbm,
                 xbuf, obuf, wbf, zbuf, sin, sout, zsem, *, rb, tf, n_e, n_blocks):
    e = pl.program_id(0)
    nb = nb_ref[e]
    b0 = bs_ref[e]
    col = pl.multiple_of(pl.program_id(1) * tf, tf)

    @pl.when(e == 0)
    def _():
        _zero_tail(bs_ref, nb_ref, n_e, n_blocks, zbuf,
                   lambda c: act_hbm.at[pl.ds(pl.multiple_of(c * rb, rb), rb), pl.ds(col, tf)], zsem)

    def rows(c):
        return pl.ds(pl.multiple_of((b0 + c) * rb, rb), rb)

    def in_copy(c, slot):
        return pltpu.make_async_copy(xs_hbm.at[rows(c)], xbuf.at[slot], sin.at[slot])

    def out_copy(c, slot):
        return pltpu.make_async_copy(obuf.at[slot], act_hbm.at[rows(c), pl.ds(col, tf)], sout.at[slot])

    @pl.when(nb > 0)
    def _():
        wbf[:, 0:tf] = wg_ref[0].astype(BF16)
        wbf[:, tf:2 * tf] = wu_ref[0].astype(BF16)

    def compute(slot):
        h = _unpack_rows(xbuf[slot])
        gu = jnp.dot(h, wbf[...], preferred_element_type=F32)
        gate = jnp.minimum(gu[:, :tf] + bg_ref[0], SWIGLU_LIMIT)
        up = jnp.clip(gu[:, tf:] + bu_ref[0], -SWIGLU_LIMIT, SWIGLU_LIMIT)
        act = (up + 1.0) * gate * _sigmoid(SWIGLU_ALPHA * gate)
        obuf[slot] = act.astype(BF16)

    _group_loop(nb, b0, rb, in_copy, out_copy, compute)


def _gmm1_call(groups, xs, wg, wu, bg, bu, rb):
    bstart, nblk = groups
    n_buf, dw = xs.shape
    n_e, d, dff = wg.shape
    tf = _pick(dff, 512)
    wmap = lambda e, f, bs, nb: (e, 0, f)
    return pl.pallas_call(
        functools.partial(_gmm1_kernel, rb=rb, tf=tf, n_e=n_e, n_blocks=n_buf // rb),
        out_shape=_sds((n_buf, dff), BF16),
        grid_spec=pltpu.PrefetchScalarGridSpec(
            num_scalar_prefetch=2,
            grid=(n_e, dff // tf),
            in_specs=[pl.BlockSpec(memory_space=pl.ANY),
                      pl.BlockSpec((1, d, tf), wmap),
                      pl.BlockSpec((1, d, tf), wmap),
                      pl.BlockSpec((1, 1, tf), wmap),
                      pl.BlockSpec((1, 1, tf), wmap)],
            out_specs=pl.BlockSpec(memory_space=pl.ANY),
            scratch_shapes=[pltpu.VMEM((2, rb, dw), U32),
                            pltpu.VMEM((2, rb, tf), BF16),
                            pltpu.VMEM((d, 2 * tf), BF16),
                            pltpu.VMEM((rb, tf), BF16),
                            pltpu.SemaphoreType.DMA((2,)),
                            pltpu.SemaphoreType.DMA((2,)),
                            pltpu.SemaphoreType.DMA]),
        compiler_params=_params(("arbitrary", "arbitrary")),
        name="gmm1",
    )(bstart, nblk, xs, wg, wu, bg, bu)


def _gmm2_kernel(bs_ref, nb_ref, act_hbm, wd_ref, bd_ref, y_hbm, abuf, ybuf, wbf, zbuf, sin, sout, zsem,
                 *, rb, tn, n_e, n_blocks):
    e = pl.program_id(0)
    nb = nb_ref[e]
    b0 = bs_ref[e]
    col = pl.multiple_of(pl.program_id(1) * tn, tn)

    @pl.when(e == 0)
    def _():
        _zero_tail(bs_ref, nb_ref, n_e, n_blocks, zbuf,
                   lambda c: y_hbm.at[pl.ds(pl.multiple_of(c * rb, rb), rb), pl.ds(col, tn)], zsem)

    def rows(c):
        return pl.ds(pl.multiple_of((b0 + c) * rb, rb), rb)

    def in_copy(c, slot):
        return pltpu.make_async_copy(act_hbm.at[rows(c)], abuf.at[slot], sin.at[slot])

    def out_copy(c, slot):
        return pltpu.make_async_copy(ybuf.at[slot], y_hbm.at[rows(c), pl.ds(col, tn)], sout.at[slot])

    @pl.when(nb > 0)
    def _():
        wbf[...] = wd_ref[0].astype(BF16)

    def compute(slot):
        ybuf[slot] = jnp.dot(abuf[slot], wbf[...], preferred_element_type=F32) + bd_ref[0]

    _group_loop(nb, b0, rb, in_copy, out_copy, compute)


def _gmm2_call(groups, act, wd, bd, rb):
    bstart, nblk = groups
    n_buf, dff = act.shape
    n_e, _, d = wd.shape
    tn = _pick(d, 1024)
    wmap = lambda e, n, bs, nb: (e, 0, n)
    return pl.pallas_call(
        functools.partial(_gmm2_kernel, rb=rb, tn=tn, n_e=n_e, n_blocks=n_buf // rb),
        out_shape=_sds((n_buf, d), F32),
        grid_spec=pltpu.PrefetchScalarGridSpec(
            num_scalar_prefetch=2,
            grid=(n_e, d // tn),
            in_specs=[pl.BlockSpec(memory_space=pl.ANY),
                      pl.BlockSpec((1, dff, tn), wmap),
                      pl.BlockSpec((1, 1, tn), wmap)],
            out_specs=pl.BlockSpec(memory_space=pl.ANY),
            scratch_shapes=[pltpu.VMEM((2, rb, dff), BF16),
                            pltpu.VMEM((2, rb, tn), F32),
                            pltpu.VMEM((dff, tn), BF16),
                            pltpu.VMEM((rb, tn), F32),
                            pltpu.SemaphoreType.DMA((2,)),
                            pltpu.SemaphoreType.DMA((2,)),
                            pltpu.SemaphoreType.DMA]),
        compiler_params=_params(("arbitrary", "arbitrary")),
        name="gmm2",
    )(bstart, nblk, act, wd, bd)


def _final_kernel(pos_ref, posn_ref, y_hbm, tw_ref, x1_ref, gt_ref, gfin_ref, o_ref, ybuf, sem, *,
                  tc, n_steps):
    step = pl.program_id(0) * pl.num_programs(1) + pl.program_id(1)

    def issue(p_ref, slot):
        def body(t, c):
            for k in range(TOP_K):
                src = y_hbm.at[pl.ds(p_ref[0, 0, t * TOP_K + k], 1)]
                pltpu.make_async_copy(src, ybuf.at[slot, k, pl.ds(t, 1)], sem.at[slot]).start()
            return c

        lax.fori_loop(0, tc, body, 0, unroll=2)

    def run(cur):
        @pl.when(step + 1 < n_steps)
        def _():
            issue(posn_ref, 1 - cur)

        for k in range(TOP_K):
            pltpu.make_async_copy(y_hbm.at[pl.ds(0, tc)], ybuf.at[cur, k], sem.at[cur]).wait()
        tw = tw_ref[0]
        moe = tw[:, 0:1] * ybuf[cur, 0]
        for k in range(1, TOP_K):
            moe = moe + tw[:, k:k + 1] * ybuf[cur, k]
        x2 = x1_ref[0] + gt_ref[0] * moe
        o_ref[0] = _rms(x2, gfin_ref[...])

    @pl.when(step == 0)
    def _():
        issue(pos_ref, 0)

    parity = lax.rem(step, 2)

    @pl.when(parity == 0)
    def _():
        run(0)

    @pl.when(parity == 1)
    def _():
        run(1)


def _final_call(y, pos, top_w, x1, gt2, gfin):
    bsz, n, d = x1.shape
    tc = _pick(n, 256, SUBLANES)
    nt = n // tc
    n_steps = bsz * nt
    pos3 = pos.reshape(n_steps, 1, tc * TOP_K)
    row = lambda b, i: (b, i, 0)
    return pl.pallas_call(
        functools.partial(_final_kernel, tc=tc, n_steps=n_steps),
        out_shape=_sds((bsz, n, d), F32),
        grid=(bsz, nt),
        in_specs=[pl.BlockSpec((1, 1, tc * TOP_K), lambda b, i: (b * nt + i, 0, 0), memory_space=pltpu.SMEM),
                  pl.BlockSpec((1, 1, tc * TOP_K), lambda b, i: (jnp.minimum(b * nt + i + 1, n_steps - 1), 0, 0),
                               memory_space=pltpu.SMEM),
                  pl.BlockSpec(memory_space=pl.ANY),
                  pl.BlockSpec((1, tc, LANES), row),
                  pl.BlockSpec((1, tc, d), row),
                  pl.BlockSpec((1, 1, d), lambda b, i: (b, 0, 0)),
                  pl.BlockSpec((1, d), lambda b, i: (0, 0))],
        out_specs=pl.BlockSpec((1, tc, d), row),
        scratch_shapes=[pltpu.VMEM((2, TOP_K, tc, d), F32), pltpu.SemaphoreType.DMA((2,))],
        compiler_params=_params(("arbitrary", "arbitrary")),
        name="final",
    )(pos3, pos3, y, top_w, x1, gt2, gfin)


def _group_tables(counts, rb):
    nblk = (counts + rb - 1) // rb
    bstart = jnp.cumsum(nblk) - nblk
    return bstart * rb, (bstart.astype(I32), nblk.astype(I32))


def kernel(x, c, ctx, c_ctx, w_mod, b_mod, g_mix, w_in, conv_a_w, conv_a_b, lru_w_r, lru_b_r,
           lru_w_i, lru_b_i, lru_lam, conv_b_w, g_out_a, g_out_b, w_out, g_ffn, w_router,
           b_router, w_gate, b_gate, w_up, b_up, w_down, b_down, g_final):
    assert w_mod.shape[0] == 1, "single-layer block"
    bsz, n_lat, d = x.shape
    d_a = conv_a_w.shape[-1]
    d_b = conv_b_w.shape[-1]
    n_e = w_router.shape[-1]
    assert n_lat % GRID_W == 0 and n_e <= LANES and d % (2 * LANES) == 0
    l = 0

    cs = jnp.zeros((SUBLANES, d), F32).at[:bsz].set(c).at[bsz].set(c_ctx)
    mod = _mod_call(cs, w_mod[l], b_mod[l][None])
    sh1, sc1, gt1, sh2, sc2, gt2 = [m[:bsz, None, :] for m in jnp.split(mod, 6, axis=-1)]
    ssh1, ssc1 = [jnp.broadcast_to(m[bsz][None, None, :], (bsz, 1, d))
                  for m in jnp.split(mod, 6, axis=-1)[:2]]

    w_in_bf = w_in[l].astype(BF16)
    g_mix2 = g_mix[l][None]
    cw = conv_a_w[l]
    cb = conv_a_b[l][None]
    wri = jnp.concatenate([lru_w_r[l], lru_w_i[l]], axis=-1).astype(BF16)
    br, bi, lam = lru_b_r[l], lru_b_i[l], lru_lam[l]
    lru_p = lambda dr: (cw, cb, wri[dr], br[dr][None], bi[dr][None], lam[dr][None])

    (s_ax,) = _inproj_call(ctx, g_mix2, ssh1, ssc1, w_in_bf[:, d_a:2 * d_a], d_a, d_b, latent=False)
    zero_state = jnp.zeros((bsz, 1, d_a), F32)
    _, h0f = _lru_call(s_ax, zero_state, *lru_p(0), reverse=False)
    _, h0b = _lru_call(s_ax, zero_state, *lru_p(1), reverse=True)

    ag, ax, bb, p = _inproj_call(x, g_mix2, sh1, sc1, w_in_bf, d_a, d_b, latent=True)
    hf, _ = _lru_call(ax, h0f, *lru_p(0), reverse=False)
    ya, _ = _lru_call(ax, h0b, *lru_p(1), reverse=True, hf=hf, ag=ag)

    wr_bf = jnp.zeros((d, LANES), BF16).at[:, :n_e].set(w_router[l].astype(BF16))
    brt = jnp.full((1, LANES), NEG_BIG, F32).at[0, :n_e].set(b_router[l])
    x1, xp, top_e, top_w = _mixout_call(
        ya, bb, p, conv_b_w[l], g_out_a[l][None], g_out_b[l][None], w_out[l].astype(BF16), x, gt1,
        g_ffn[l][None], sh2, sc2, wr_bf, brt)

    n_tok = bsz * n_lat
    n_rows = n_tok * TOP_K
    rb = MOE_ROW_BLOCK
    n_buf = n_rows + n_e * rb
    rank, cnt = _rank_call(top_e.reshape(n_tok, LANES))
    counts = cnt[0, :n_e].astype(I32)
    starts, groups = _group_tables(counts, rb)
    te = top_e.reshape(n_tok, LANES)[:, :TOP_K]
    onehot = te[:, :, None] == jnp.arange(n_e, dtype=I32)[None, None, :]
    pos = (jnp.sum(jnp.where(onehot, starts[None, None, :], 0), axis=-1) + rank[:, :TOP_K]).astype(I32)
    xs = _dispatch_call(groups, xp.reshape(n_tok, d // 2), pos, n_buf, rb)
    act = _gmm1_call(groups, xs, w_gate[l], w_up[l], b_gate[l][:, None, :], b_up[l][:, None, :], rb)
    y = _gmm2_call(groups, act, w_down[l], b_down[l][:, None, :], rb)
    return _final_call(y, pos, top_w, x1, gt2, g_final[None])
```

```python
import functools

import jax
import jax.numpy as jnp
from jax import lax
from jax.experimental import pallas as pl
from jax.experimental.pallas import tpu as pltpu

F32 = jnp.float32
BF16 = jnp.bfloat16
I32 = jnp.int32
U32 = jnp.uint32

GRID_W = 64
TOP_K = 4
LRU_C = 8.0
CONV_A_LEFT = 2
SWIGLU_LIMIT = 7.0
SWIGLU_ALPHA = 1.702
EPS = 1e-6
MOE_ROW_BLOCK = 256

LANES = 128
SUBLANES = 8
VMEM_LIMIT_BYTES = 56 * 1024 * 1024
NEG_BIG = -1e30


def _sds(shape, dtype):
    return jax.ShapeDtypeStruct(shape, dtype)


def _pick(n, pref, mult=LANES):
    if n <= pref:
        return n
    t = (pref // mult) * mult
    while t >= mult:
        if n % t == 0:
            return t
        t -= mult
    return n


def _params(sem):
    return pltpu.CompilerParams(dimension_semantics=sem, vmem_limit_bytes=VMEM_LIMIT_BYTES)


def _sigmoid(x):
    return 1.0 / (1.0 + jnp.exp(-x))


def _rms(x, g):
    ms = jnp.mean(x * x, axis=-1, keepdims=True)
    return (x * lax.rsqrt(ms + EPS)) * g


def _mod_kernel(c_ref, w_ref, b_ref, o_ref):
    c = c_ref[...]
    s = c * _sigmoid(c)
    o_ref[...] = jnp.dot(s.astype(BF16), w_ref[...].astype(BF16),
                         preferred_element_type=F32) + b_ref[...]


def _mod_call(cs, w, b):
    d, n6 = w.shape
    tn = _pick(n6, 1024)
    return pl.pallas_call(
        _mod_kernel,
        out_shape=_sds((cs.shape[0], n6), F32),
        grid=(n6 // tn,),
        in_specs=[pl.BlockSpec((cs.shape[0], d), lambda j: (0, 0)),
                  pl.BlockSpec((d, tn), lambda j: (0, j)),
                  pl.BlockSpec((1, tn), lambda j: (0, j))],
        out_specs=pl.BlockSpec((cs.shape[0], tn), lambda j: (0, j)),
        compiler_params=_params(("arbitrary",)),
        name="mod",
    )(cs, w, b)


def _inproj_kernel(x_ref, g_ref, sh_ref, sc_ref, w_ref, *out_refs, d_a, d_b, latent):
    x = x_ref[0]
    xn = _rms(x, g_ref[...]) * (1.0 + sc_ref[0]) + sh_ref[0]
    xb = xn.astype(BF16)

    def sec(lo, width):
        return jnp.dot(xb, w_ref[:, lo:lo + width], preferred_element_type=F32)

    if not latent:
        out_refs[0][0] = sec(0, d_a)
        return
    ag_ref, ax_ref, bb_ref, p_ref = out_refs
    ag_ref[0] = sec(0, d_a).astype(BF16)
    ax_ref[0] = sec(d_a, d_a)
    bb_ref[0] = sec(2 * d_a, d_b).astype(BF16)
    p_ref[0] = (sec(2 * d_a + d_b, d_b) * sec(2 * d_a + 2 * d_b, d_b)).astype(BF16)


def _inproj_call(x, g, sh, sc, w_bf, d_a, d_b, latent):
    bsz, n, d = x.shape
    tm = _pick(n, 512, SUBLANES)
    n_w = w_bf.shape[1]
    row = lambda b, i: (b, i, 0)
    if latent:
        out_shape = (_sds((bsz, n, d_a), BF16), _sds((bsz, n, d_a), F32),
                     _sds((bsz, n, d_b), BF16), _sds((bsz, n, d_b), BF16))
        out_specs = (pl.BlockSpec((1, tm, d_a), row), pl.BlockSpec((1, tm, d_a), row),
                     pl.BlockSpec((1, tm, d_b), row), pl.BlockSpec((1, tm, d_b), row))
    else:
        out_shape = (_sds((bsz, n, d_a), F32),)
        out_specs = (pl.BlockSpec((1, tm, d_a), row),)
    return pl.pallas_call(
        functools.partial(_inproj_kernel, d_a=d_a, d_b=d_b, latent=latent),
        out_shape=out_shape,
        grid=(bsz, n // tm),
        in_specs=[pl.BlockSpec((1, tm, d), row),
                  pl.BlockSpec((1, d), lambda b, i: (0, 0)),
                  pl.BlockSpec((1, 1, d), lambda b, i: (b, 0, 0)),
                  pl.BlockSpec((1, 1, d), lambda b, i: (b, 0, 0)),
                  pl.BlockSpec((d, n_w), lambda b, i: (0, 0), pipeline_mode=pl.Buffered(1))],
        out_specs=out_specs,
        compiler_params=_params(("arbitrary", "arbitrary")),
        name="inproj_lat" if latent else "inproj_ctx",
    )(x, g, sh, sc, w_bf)


def _gelu_tanh(x):
    c = 0.7978845608028654
    return x * (0.5 * (1.0 + jnp.tanh(c * (x + 0.044715 * (x * x * x)))))


def _lru_kernel(*refs, reverse, combine, nc, tl, heads, blk):
    if combine:
        (prev_ref, main_ref, next_ref, cw_ref, cb_ref, wri_ref, br_ref, bi_ref, lam_ref, h0_ref,
         hf_ref, ag_ref, out_ref, hlast_ref, ebuf, xc_s, a_s, b_s, carry) = refs
    else:
        (prev_ref, main_ref, next_ref, cw_ref, cb_ref, wri_ref, br_ref, bi_ref, lam_ref, h0_ref,
         out_ref, hlast_ref, ebuf, xc_s, a_s, b_s, carry) = refs
    da = heads * blk
    c = pl.program_id(1)
    cidx = (nc - 1 - c) if reverse else c

    @pl.when(c == 0)
    def _():
        carry[...] = jnp.broadcast_to(h0_ref[0], carry.shape)

    main = main_ref[0]
    zero8 = jnp.zeros((SUBLANES, da), F32)
    ebuf[0:SUBLANES, :] = jnp.where(cidx == 0, zero8, prev_ref[0])
    ebuf[SUBLANES:SUBLANES + tl, :] = main
    ebuf[SUBLANES + tl:2 * SUBLANES + tl, :] = jnp.where(cidx == nc - 1, zero8, next_ref[0])
    cw = cw_ref[...]
    off = SUBLANES - CONV_A_LEFT
    xc_s[...] = (cw[0:1] * ebuf[off:off + tl, :] + cw[1:2] * ebuf[off + 1:off + 1 + tl, :]
                 + cw[2:3] * main + cw[3:4] * ebuf[off + 3:off + 3 + tl, :] + cb_ref[...])

    z = -lam_ref[...]
    sp = jnp.maximum(z, 0.0) + jnp.log1p(jnp.exp(-jnp.abs(z)))
    rc = min(tl, 128)
    for r0 in range(0, tl, rc):
        for h in range(heads):
            cs = slice(h * blk, (h + 1) * blk)
            xh = xc_s[r0:r0 + rc, cs]
            zz = jnp.dot(xh.astype(BF16), wri_ref[h], preferred_element_type=F32)
            r = _sigmoid(zz[:, :blk] + br_ref[:, cs])
            i = _sigmoid(zz[:, blk:] + bi_ref[:, cs])
            log_a = (-LRU_C * r) * sp[:, cs]
            a = jnp.exp(log_a)
            a_s[r0:r0 + rc, cs] = a
            b_s[r0:r0 + rc, cs] = jnp.sqrt(jnp.tanh(-log_a) * (1.0 + a * a)) * (i * xh)

    row = lax.broadcasted_iota(I32, (SUBLANES, da), 0)
    ng = tl // SUBLANES
    shifts = (1, 2, 4)

    def body(gi, hc):
        g = (ng - 1 - gi) if reverse else gi
        r0 = pl.multiple_of(g * SUBLANES, SUBLANES)
        a = a_s[pl.ds(r0, SUBLANES), :]
        b = b_s[pl.ds(r0, SUBLANES), :]
        for s in shifts:
            if reverse:
                keep = row < (SUBLANES - s)
                sh = SUBLANES - s
            else:
                keep = row >= s
                sh = s
            a_sh = jnp.where(keep, pltpu.roll(a, sh, 0), 1.0)
            b_sh = jnp.where(keep, pltpu.roll(b, sh, 0), 0.0)
            b = a * b_sh + b
            a = a * a_sh
        hrows = a * hc + b
        b_s[pl.ds(r0, SUBLANES), :] = hrows
        edge = hrows[0:1, :] if reverse else hrows[SUBLANES - 1:SUBLANES, :]
        return jnp.broadcast_to(edge, (SUBLANES, da))

    hc = lax.fori_loop(0, ng, body, carry[...])
    carry[...] = hc
    hlast_ref[0] = hc[0:1, :]
    if combine:
        hsum = hf_ref[0] + b_s[...]
        out_ref[0] = (_gelu_tanh(ag_ref[0].astype(F32)) * hsum).astype(out_ref.dtype)
    else:
        out_ref[0] = b_s[...]


def _lru_call(ax, h0, cw, cb, wri, br, bi, lam, *, reverse, hf=None, ag=None):
    bsz, n, da = ax.shape
    heads, blk, _ = wri.shape
    tl = _pick(n, 512, SUBLANES)
    nc = n // tl
    nb8 = n // SUBLANES
    g8 = tl // SUBLANES
    combine = hf is not None

    def cidx(c):
        return (nc - 1 - c) if reverse else c

    main_map = lambda b, c: (b, cidx(c), 0)
    prev_map = lambda b, c: (b, jnp.maximum(cidx(c) * g8 - 1, 0), 0)
    next_map = lambda b, c: (b, jnp.minimum((cidx(c) + 1) * g8, nb8 - 1), 0)
    const2 = lambda b, c: (0, 0)
    in_specs = [pl.BlockSpec((1, SUBLANES, da), prev_map),
                pl.BlockSpec((1, tl, da), main_map),
                pl.BlockSpec((1, SUBLANES, da), next_map),
                pl.BlockSpec((4, da), const2),
                pl.BlockSpec((1, da), const2),
                pl.BlockSpec((heads, blk, 2 * blk), lambda b, c: (0, 0, 0)),
                pl.BlockSpec((1, da), const2),
                pl.BlockSpec((1, da), const2),
                pl.BlockSpec((1, da), const2),
                pl.BlockSpec((1, 1, da), lambda b, c: (b, 0, 0))]
    args = [ax, ax, ax, cw, cb, wri, br, bi, lam, h0]
    if combine:
        in_specs += [pl.BlockSpec((1, tl, da), main_map), pl.BlockSpec((1, tl, da), main_map)]
        args += [hf, ag]
    out_dtype = BF16 if combine else F32
    return pl.pallas_call(
        functools.partial(_lru_kernel, reverse=reverse, combine=combine, nc=nc, tl=tl,
                          heads=heads, blk=blk),
        out_shape=(_sds((bsz, n, da), out_dtype), _sds((bsz, 1, da), F32)),
        grid=(bsz, nc),
        in_specs=in_specs,
        out_specs=(pl.BlockSpec((1, tl, da), main_map),
                   pl.BlockSpec((1, 1, da), lambda b, c: (b, 0, 0))),
        scratch_shapes=[pltpu.VMEM((tl + 2 * SUBLANES, da), F32),
                        pltpu.VMEM((tl, da), F32),
                        pltpu.VMEM((tl, da), F32),
                        pltpu.VMEM((tl, da), F32),
                        pltpu.VMEM((SUBLANES, da), F32)],
        compiler_params=_params(("arbitrary", "arbitrary")),
        name=("lru_bwd" if reverse else "lru_fwd") + ("_mix" if combine else ""),
    )(*args)


def _mixout_kernel(ya_ref, bb_ref, p_ref, pu_ref, pd_ref, cbw_ref, ga_ref, gb_ref, wo_ref, x_ref,
                   gt_ref, gf_ref, sh_ref, sc_ref, wr_ref, brt_ref,
                   x1_ref, xp_ref, te_ref, tw_ref, *, tm, d_a, d_b, n_tiles):
    i = pl.program_id(1)
    half = d_b // 2
    z = p_ref[0].astype(F32)
    w = cbw_ref[...]
    zh = z[:, :half]
    col = lax.broadcasted_iota(I32, (tm, half), 0) % GRID_W
    left = jnp.where(col >= 1, pltpu.roll(zh, 1, 0), 0.0)
    right = jnp.where(col <= GRID_W - 2, pltpu.roll(zh, tm - 1, 0), 0.0)
    horiz = w[0:1, :half] * left + w[1:2, :half] * zh + w[2:3, :half] * right
    zv = z[:, half:]
    up_halo = jnp.where(i == 0, 0.0, pu_ref[0].astype(F32))
    dn_halo = jnp.where(i == n_tiles - 1, 0.0, pd_ref[0].astype(F32))
    if tm > GRID_W:
        up = jnp.concatenate([up_halo, zv[:tm - GRID_W]], axis=0)
        dn = jnp.concatenate([zv[GRID_W:], dn_halo], axis=0)
    else:
        up, dn = up_halo, dn_halo
    vert = w[0:1, half:] * up + w[1:2, half:] * zv + w[2:3, half:] * dn
    bb = bb_ref[0].astype(F32)
    yb = jnp.concatenate([bb[:, :half] * horiz, bb[:, half:] * vert], axis=1)
    ya = ya_ref[0].astype(F32)
    ya_n = _rms(ya, ga_ref[...]).astype(BF16)
    yb_n = _rms(yb, gb_ref[...]).astype(BF16)
    mix = (jnp.dot(ya_n, wo_ref[0:d_a, :], preferred_element_type=F32)
           + jnp.dot(yb_n, wo_ref[d_a:d_a + d_b, :], preferred_element_type=F32))
    x1 = x_ref[0] + gt_ref[0] * mix
    x1_ref[0] = x1
    xn = _rms(x1, gf_ref[...]) * (1.0 + sc_ref[0]) + sh_ref[0]
    xb = xn.astype(BF16)
    dh = xb.shape[1] // 2
    lo = lax.shift_right_logical(lax.bitcast_convert_type(xb[:, :dh].astype(F32), U32), jnp.uint32(16))
    hi = lax.bitcast_convert_type(xb[:, dh:].astype(F32), U32) & jnp.uint32(0xFFFF0000)
    xp_ref[0] = lo | hi
    logits = jnp.dot(xb, wr_ref[...], preferred_element_type=F32) + brt_ref[...]
    lane = lax.broadcasted_iota(I32, logits.shape, 1)
    lane_f = lane.astype(F32)
    vals = logits
    tv, te = [], []
    for _ in range(TOP_K):
        m = jnp.max(vals, axis=-1, keepdims=True)
        idx = jnp.min(jnp.where(vals == m, lane_f, float(LANES)), axis=-1, keepdims=True)
        tv.append(m)
        te.append(idx)
        vals = jnp.where(lane_f == idx, -jnp.inf, vals)
    ex = [jnp.exp(v - tv[0]) for v in tv]
    den = ex[0]
    for e in ex[1:]:
        den = den + e
    e_out = jnp.zeros(logits.shape, I32)
    w_out = jnp.zeros(logits.shape, F32)
    for k in range(TOP_K):
        e_out = jnp.where(lane == k, te[k].astype(I32), e_out)
        w_out = jnp.where(lane == k, ex[k] / den, w_out)
    te_ref[0] = e_out
    tw_ref[0] = w_out


def _mixout_call(ya, bb, p, cbw, ga, gb, wo_bf, x, gt1, gf, sh2, sc2, wr_bf, brt):
    bsz, n, d = x.shape
    d_a = ya.shape[-1]
    d_b = bb.shape[-1]
    half = d_b // 2
    tm = _pick(n, 256, GRID_W)
    n_tiles = n // tm
    rpt = tm // GRID_W
    n_rows = n // GRID_W
    row = lambda b, i: (b, i, 0)
    vec = lambda b, i: (b, 0, 0)
    const2 = lambda b, i: (0, 0)
    return pl.pallas_call(
        functools.partial(_mixout_kernel, tm=tm, d_a=d_a, d_b=d_b, n_tiles=n_tiles),
        out_shape=(_sds((bsz, n, d), F32), _sds((bsz, n, d // 2), U32),
                   _sds((bsz, n, LANES), I32), _sds((bsz, n, LANES), F32)),
        grid=(bsz, n_tiles),
        in_specs=[pl.BlockSpec((1, tm, d_a), row),
                  pl.BlockSpec((1, tm, d_b), row),
                  pl.BlockSpec((1, tm, d_b), row),
                  pl.BlockSpec((1, GRID_W, half), lambda b, i: (b, jnp.maximum(i * rpt - 1, 0), 1)),
                  pl.BlockSpec((1, GRID_W, half), lambda b, i: (b, jnp.minimum((i + 1) * rpt, n_rows - 1), 1)),
                  pl.BlockSpec((3, d_b), const2),
                  pl.BlockSpec((1, d_a), const2),
                  pl.BlockSpec((1, d_b), const2),
                  pl.BlockSpec((d_a + d_b, d), const2),
                  pl.BlockSpec((1, tm, d), row),
                  pl.BlockSpec((1, 1, d), vec),
                  pl.BlockSpec((1, d), const2),
                  pl.BlockSpec((1, 1, d), vec),
                  pl.BlockSpec((1, 1, d), vec),
                  pl.BlockSpec((d, LANES), const2),
                  pl.BlockSpec((1, LANES), const2)],
        out_specs=(pl.BlockSpec((1, tm, d), row), pl.BlockSpec((1, tm, d // 2), row),
                   pl.BlockSpec((1, tm, LANES), row), pl.BlockSpec((1, tm, LANES), row)),
        compiler_params=_params(("arbitrary", "arbitrary")),
        name="mixout",
    )(ya, bb, p, p, p, cbw, ga, gb, wo_bf, x, gt1, gf, sh2, sc2, wr_bf, brt)


def _rank_kernel(e_ref, rank_ref, cnt_ref, carry, *, tt):
    @pl.when(pl.program_id(0) == 0)
    def _():
        carry[...] = jnp.zeros(carry.shape, F32)

    e = e_ref[...]
    lane = lax.broadcasted_iota(I32, (tt, LANES), 1)
    ohs = []
    m = jnp.zeros((tt, LANES), F32)
    for k in range(TOP_K):
        oh = lane == e[:, k:k + 1]
        ohs.append(oh)
        m = m + jnp.where(oh, 1.0, 0.0)
    ri = lax.broadcasted_iota(I32, (tt, tt), 0)
    ci = lax.broadcasted_iota(I32, (tt, tt), 1)
    ltri = jnp.where(ri > ci, 1.0, 0.0).astype(BF16)
    pref = jnp.dot(ltri, m.astype(BF16), preferred_element_type=F32) + carry[0:1, :]
    out = jnp.zeros((tt, LANES), I32)
    for k in range(TOP_K):
        rk = jnp.sum(jnp.where(ohs[k], pref, 0.0), axis=-1, keepdims=True)
        out = jnp.where(lane == k, rk.astype(I32), out)
    rank_ref[...] = out
    tot = carry[0:1, :] + jnp.sum(m, axis=0, keepdims=True)
    carry[...] = jnp.broadcast_to(tot, carry.shape)
    cnt_ref[...] = jnp.broadcast_to(tot, cnt_ref.shape)


def _rank_call(top_e):
    t = top_e.shape[0]
    tt = _pick(t, 512, SUBLANES)
    return pl.pallas_call(
        functools.partial(_rank_kernel, tt=tt),
        out_shape=(_sds((t, LANES), I32), _sds((SUBLANES, LANES), F32)),
        grid=(t // tt,),
        in_specs=[pl.BlockSpec((tt, LANES), lambda i: (i, 0))],
        out_specs=(pl.BlockSpec((tt, LANES), lambda i: (i, 0)),
                   pl.BlockSpec((SUBLANES, LANES), lambda i: (0, 0))),
        scratch_shapes=[pltpu.VMEM((SUBLANES, LANES), F32)],
        compiler_params=_params(("arbitrary",)),
        name="rank",
    )(top_e)


def _zero_tail(first, n_blocks, zbuf, dst_block, sem):
    zbuf[...] = jnp.zeros(zbuf.shape, zbuf.dtype)

    def start(c, carry):
        pltpu.make_async_copy(zbuf, dst_block(c), sem).start()
        return carry

    def wait(c, carry):
        pltpu.make_async_copy(zbuf, dst_block(c), sem).wait()
        return carry

    lax.fori_loop(first, n_blocks, start, 0)
    lax.fori_loop(first, n_blocks, wait, 0)


def _dispatch_kernel(bs_ref, nb_ref, pos_ref, x_ref, o_hbm, zbuf, sem, zsem, *, td, rb, n_e, n_blocks):
    @pl.when(pl.program_id(0) == 0)
    def _():
        _zero_tail(bs_ref[n_e - 1] + nb_ref[n_e - 1], n_blocks, zbuf,
                   lambda c: o_hbm.at[pl.ds(pl.multiple_of(c * rb, rb), rb)], zsem)

        def zero_copy(e):
            r = pl.multiple_of((bs_ref[e] + nb_ref[e] - 1) * rb, rb)
            return pltpu.make_async_copy(zbuf, o_hbm.at[pl.ds(r, rb)], zsem)

        for e in range(n_e):
            @pl.when(nb_ref[e] > 0)
            def _():
                zero_copy(e).start()

        for e in range(n_e):
            @pl.when(nb_ref[e] > 0)
            def _():
                zero_copy(e).wait()

    def body(t, c):
        src = x_ref.at[pl.ds(t, 1)]
        for k in range(TOP_K):
            dst = o_hbm.at[pl.ds(pos_ref[0, 0, t * TOP_K + k], 1)]
            pltpu.make_async_copy(src, dst, sem).start()
        return c

    lax.fori_loop(0, td, body, 0, unroll=2)
    for _ in range(TOP_K):
        pltpu.make_async_copy(x_ref, o_hbm.at[pl.ds(0, td)], sem).wait()


def _dispatch_call(groups, xp, pos, n_buf, rb):
    bstart, nblk = groups
    n_e = bstart.shape[0]
    t, dw = xp.shape
    td = _pick(t, 512, SUBLANES)
    pos3 = pos.reshape(t // td, 1, td * TOP_K)
    return pl.pallas_call(
        functools.partial(_dispatch_kernel, td=td, rb=rb, n_e=n_e, n_blocks=n_buf // rb),
        out_shape=_sds((n_buf, dw), U32),
        grid_spec=pltpu.PrefetchScalarGridSpec(
            num_scalar_prefetch=2,
            grid=(t // td,),
            in_specs=[pl.BlockSpec((1, 1, td * TOP_K), lambda i, bs, nb: (i, 0, 0),
                                   memory_space=pltpu.SMEM),
                      pl.BlockSpec((td, dw), lambda i, bs, nb: (i, 0))],
            out_specs=pl.BlockSpec(memory_space=pl.ANY),
            scratch_shapes=[pltpu.VMEM((rb, dw), U32), pltpu.SemaphoreType.DMA,
                            pltpu.SemaphoreType.DMA]),
        compiler_params=_params(("arbitrary",)),
        name="dispatch",
    )(bstart, nblk, pos3, xp)


def _unpack_rows(xp):
    lo = lax.bitcast_convert_type(lax.shift_left(xp, jnp.uint32(16)), F32)
    hi = lax.bitcast_convert_type(xp & jnp.uint32(0xFFFF0000), F32)
    return jnp.concatenate([lo.astype(BF16), hi.astype(BF16)], axis=1)


GROUP_IN_SLOTS = 4
GROUP_OUT_SLOTS = 3


def _chunk_loop(tabs, in_copy, out_copy, w_copies, load_weights, compute):
    ce, cs, cb, nge, ngs, nch = tabs
    n = nch[0]
    ahead = GROUP_IN_SLOTS - 1

    @pl.when(n > 0)
    def _():
        for cp in w_copies(ce[0], cs[0], 0):
            cp.start()
        for j in range(ahead):
            @pl.when(j < n)
            def _():
                in_copy(j, j).start()

        def body(j, groups_done):
            jp = jnp.maximum(j - 1, 0)
            first = jnp.logical_or(j == 0, jnp.logical_or(ce[j] != ce[jp], cs[j] != cs[jp]))

            @pl.when(first)
            def _():
                wslot = lax.rem(groups_done, 2)
                for cp in w_copies(ce[j], cs[j], wslot):
                    cp.wait()
                load_weights(wslot)

                @pl.when(nge[j] >= 0)
                def _():
                    for cp in w_copies(nge[j], ngs[j], 1 - wslot):
                        cp.start()

            islot = lax.rem(j, GROUP_IN_SLOTS)
            oslot = lax.rem(j, GROUP_OUT_SLOTS)

            @pl.when(j + ahead < n)
            def _():
                in_copy(j + ahead, lax.rem(j + ahead, GROUP_IN_SLOTS)).start()

            in_copy(j, islot).wait()

            @pl.when(j >= GROUP_OUT_SLOTS)
            def _():
                out_copy(j - GROUP_OUT_SLOTS, oslot).wait()

            compute(j, islot, oslot)
            out_copy(j, oslot).start()
            return groups_done + first.astype(I32)

        lax.fori_loop(0, n, body, jnp.int32(0))

        for k in range(GROUP_OUT_SLOTS):
            @pl.when(n > k)
            def _():
                out_copy(n - 1 - k, lax.rem(n - 1 - k, GROUP_OUT_SLOTS)).wait()


def _gmm1_kernel(ce, cs, cb, nge, ngs, nch, tail, xs_hbm, wg_hbm, wu_hbm, bg_ref, bu_ref, act_hbm,
                 xbuf, obuf, wraw, wbf, zbuf, sin, sout, wsem, zsem, *, rb, tf, n_blocks, n_split):
    def blk(b):
        return pl.ds(pl.multiple_of(b * rb, rb), rb)

    def cols(s_):
        return pl.ds(pl.multiple_of(s_ * tf, tf), tf)

    for s_ in range(n_split):
        _zero_tail(tail[0], n_blocks, zbuf, lambda c: act_hbm.at[blk(c), cols(s_)], zsem)

    def in_copy(j, slot):
        return pltpu.make_async_copy(xs_hbm.at[blk(cb[j])], xbuf.at[slot], sin.at[slot])

    def out_copy(j, slot):
        return pltpu.make_async_copy(obuf.at[slot], act_hbm.at[blk(cb[j]), cols(cs[j])], sout.at[slot])

    def w_copies(e, s_, slot):
        return (pltpu.make_async_copy(wg_hbm.at[e, :, cols(s_)], wraw.at[slot, 0], wsem.at[slot]),
                pltpu.make_async_copy(wu_hbm.at[e, :, cols(s_)], wraw.at[slot, 1], wsem.at[slot]))

    def load_weights(slot):
        wbf[:, 0:tf] = wraw[slot, 0].astype(BF16)
        wbf[:, tf:2 * tf] = wraw[slot, 1].astype(BF16)

    def compute(j, islot, oslot):
        g = ce[j] * n_split + cs[j]
        h = _unpack_rows(xbuf[islot])
        gu = jnp.dot(h, wbf[...], preferred_element_type=F32)
        gate = jnp.minimum(gu[:, :tf] + bg_ref[g], SWIGLU_LIMIT)
        up = jnp.clip(gu[:, tf:] + bu_ref[g], -SWIGLU_LIMIT, SWIGLU_LIMIT)
        act = (up + 1.0) * gate * _sigmoid(SWIGLU_ALPHA * gate)
        obuf[oslot] = act.astype(BF16)

    _chunk_loop((ce, cs, cb, nge, ngs, nch), in_copy, out_copy, w_copies, load_weights, compute)


def _gmm1_call(tabs, xs, wg, wu, bg, bu, rb, tf):
    n_buf, dw = xs.shape
    n_e, d, dff = wg.shape
    n_split = dff // tf
    vmem_full = lambda shape: pl.BlockSpec(shape, lambda i, *_: (0,) * len(shape))
    return pl.pallas_call(
        functools.partial(_gmm1_kernel, rb=rb, tf=tf, n_blocks=n_buf // rb, n_split=n_split),
        out_shape=_sds((n_buf, dff), BF16),
        grid_spec=pltpu.PrefetchScalarGridSpec(
            num_scalar_prefetch=len(tabs),
            grid=(1,),
            in_specs=[pl.BlockSpec(memory_space=pl.ANY),
                      pl.BlockSpec(memory_space=pl.ANY),
                      pl.BlockSpec(memory_space=pl.ANY),
                      vmem_full((n_e * n_split, 1, tf)),
                      vmem_full((n_e * n_split, 1, tf))],
            out_specs=pl.BlockSpec(memory_space=pl.ANY),
            scratch_shapes=[pltpu.VMEM((GROUP_IN_SLOTS, rb, dw), U32),
                            pltpu.VMEM((GROUP_OUT_SLOTS, rb, tf), BF16),
                            pltpu.VMEM((2, 2, d, tf), F32),
                            pltpu.VMEM((d, 2 * tf), BF16),
                            pltpu.VMEM((rb, tf), BF16),
                            pltpu.SemaphoreType.DMA((GROUP_IN_SLOTS,)),
                            pltpu.SemaphoreType.DMA((GROUP_OUT_SLOTS,)),
                            pltpu.SemaphoreType.DMA((2,)),
                            pltpu.SemaphoreType.DMA]),
        compiler_params=_params(("arbitrary",)),
        name="gmm1",
    )(*tabs, xs, wg, wu, bg.reshape(n_e * n_split, 1, tf), bu.reshape(n_e * n_split, 1, tf))


def _gmm2_kernel(ce, cs, cb, nge, ngs, nch, tail, act_hbm, wd_hbm, bd_ref, y_hbm,
                 abuf, ybuf, wraw, wbf, zbuf, sin, sout, wsem, zsem, *, rb, tn, n_blocks, n_split):
    def blk(b):
        return pl.ds(pl.multiple_of(b * rb, rb), rb)

    def cols(s_):
        return pl.ds(pl.multiple_of(s_ * tn, tn), tn)

    for s_ in range(n_split):
        _zero_tail(tail[0], n_blocks, zbuf, lambda c: y_hbm.at[blk(c), cols(s_)], zsem)

    def in_copy(j, slot):
        return pltpu.make_async_copy(act_hbm.at[blk(cb[j])], abuf.at[slot], sin.at[slot])

    def out_copy(j, slot):
        return pltpu.make_async_copy(ybuf.at[slot], y_hbm.at[blk(cb[j]), cols(cs[j])], sout.at[slot])

    def w_copies(e, s_, slot):
        return (pltpu.make_async_copy(wd_hbm.at[e, :, cols(s_)], wraw.at[slot], wsem.at[slot]),)

    def load_weights(slot):
        wbf[...] = wraw[slot].astype(BF16)

    def compute(j, islot, oslot):
        g = ce[j] * n_split + cs[j]
        ybuf[oslot] = jnp.dot(abuf[islot], wbf[...], preferred_element_type=F32) + bd_ref[g]

    _chunk_loop((ce, cs, cb, nge, ngs, nch), in_copy, out_copy, w_copies, load_weights, compute)


def _gmm2_call(tabs, act, wd, bd, rb, tn):
    n_buf, dff = act.shape
    n_e, _, d = wd.shape
    n_split = d // tn
    vmem_full = lambda shape: pl.BlockSpec(shape, lambda i, *_: (0,) * len(shape))
    return pl.pallas_call(
        functools.partial(_gmm2_kernel, rb=rb, tn=tn, n_blocks=n_buf // rb, n_split=n_split),
        out_shape=_sds((n_buf, d), F32),
        grid_spec=pltpu.PrefetchScalarGridSpec(
            num_scalar_prefetch=len(tabs),
            grid=(1,),
            in_specs=[pl.BlockSpec(memory_space=pl.ANY),
                      pl.BlockSpec(memory_space=pl.ANY),
                      vmem_full((n_e * n_split, 1, tn))],
            out_specs=pl.BlockSpec(memory_space=pl.ANY),
            scratch_shapes=[pltpu.VMEM((GROUP_IN_SLOTS, rb, dff), BF16),
                            pltpu.VMEM((GROUP_OUT_SLOTS, rb, tn), F32),
                            pltpu.VMEM((2, dff, tn), F32),
                            pltpu.VMEM((dff, tn), BF16),
                            pltpu.VMEM((rb, tn), F32),
                            pltpu.SemaphoreType.DMA((GROUP_IN_SLOTS,)),
                            pltpu.SemaphoreType.DMA((GROUP_OUT_SLOTS,)),
                            pltpu.SemaphoreType.DMA((2,)),
                            pltpu.SemaphoreType.DMA]),
        compiler_params=_params(("arbitrary",)),
        name="gmm2",
    )(*tabs, act, wd, bd.reshape(n_e * n_split, 1, tn))


def _final_kernel(pos_ref, posn_ref, y_hbm, tw_ref, x1_ref, gt_ref, gfin_ref, o_ref, ybuf, sem, *,
                  tc, n_steps):
    step = pl.program_id(0) * pl.num_programs(1) + pl.program_id(1)

    def issue(p_ref, slot):
        def body(t, c):
            for k in range(TOP_K):
                src = y_hbm.at[pl.ds(p_ref[0, 0, t * TOP_K + k], 1)]
                pltpu.make_async_copy(src, ybuf.at[slot, k, pl.ds(t, 1)], sem.at[slot]).start()
            return c

        lax.fori_loop(0, tc, body, 0, unroll=2)

    def run(cur):
        @pl.when(step + 1 < n_steps)
        def _():
            issue(posn_ref, 1 - cur)

        for k in range(TOP_K):
            pltpu.make_async_copy(y_hbm.at[pl.ds(0, tc)], ybuf.at[cur, k], sem.at[cur]).wait()
        tw = tw_ref[0]
        moe = tw[:, 0:1] * ybuf[cur, 0]
        for k in range(1, TOP_K):
            moe = moe + tw[:, k:k + 1] * ybuf[cur, k]
        x2 = x1_ref[0] + gt_ref[0] * moe
        o_ref[0] = _rms(x2, gfin_ref[...])

    @pl.when(step == 0)
    def _():
        issue(pos_ref, 0)

    parity = lax.rem(step, 2)

    @pl.when(parity == 0)
    def _():
        run(0)

    @pl.when(parity == 1)
    def _():
        run(1)


def _final_call(y, pos, top_w, x1, gt2, gfin):
    bsz, n, d = x1.shape
    tc = _pick(n, 256, SUBLANES)
    nt = n // tc
    n_steps = bsz * nt
    pos3 = pos.reshape(n_steps, 1, tc * TOP_K)
    row = lambda b, i: (b, i, 0)
    return pl.pallas_call(
        functools.partial(_final_kernel, tc=tc, n_steps=n_steps),
        out_shape=_sds((bsz, n, d), F32),
        grid=(bsz, nt),
        in_specs=[pl.BlockSpec((1, 1, tc * TOP_K), lambda b, i: (b * nt + i, 0, 0), memory_space=pltpu.SMEM),
                  pl.BlockSpec((1, 1, tc * TOP_K), lambda b, i: (jnp.minimum(b * nt + i + 1, n_steps - 1), 0, 0),
                               memory_space=pltpu.SMEM),
                  pl.BlockSpec(memory_space=pl.ANY),
                  pl.BlockSpec((1, tc, LANES), row),
                  pl.BlockSpec((1, tc, d), row),
                  pl.BlockSpec((1, 1, d), lambda b, i: (b, 0, 0)),
                  pl.BlockSpec((1, d), lambda b, i: (0, 0))],
        out_specs=pl.BlockSpec((1, tc, d), row),
        scratch_shapes=[pltpu.VMEM((2, TOP_K, tc, d), F32), pltpu.SemaphoreType.DMA((2,))],
        compiler_params=_params(("arbitrary", "arbitrary")),
        name="final",
    )(pos3, pos3, y, top_w, x1, gt2, gfin)


def _group_tables(counts, rb):
    nblk = (counts + rb - 1) // rb
    bstart = jnp.cumsum(nblk) - nblk
    return bstart * rb, (bstart.astype(I32), nblk.astype(I32))


def _chunk_tables(groups, n_split, max_blocks):
    bstart, nblk = groups
    n_e = nblk.shape[0]
    per_e = nblk * n_split
    cend = jnp.cumsum(per_e)
    cstart = cend - per_e
    j = jnp.arange(n_split * max_blocks, dtype=I32)
    ce = jnp.clip(jnp.searchsorted(cend, j, side="right"), 0, n_e - 1).astype(I32)
    r = j - cstart[ce]
    nb = jnp.maximum(nblk[ce], 1)
    cs = jnp.clip(r // nb, 0, n_split - 1)
    cb = bstart[ce] + r % nb
    idx = jnp.arange(n_e, dtype=I32)
    cand = jnp.where(nblk > 0, idx, n_e)
    nxt = lax.cummin(cand, axis=0, reverse=True)
    nxt_after = jnp.concatenate([nxt[1:], jnp.full((1,), n_e, I32)])
    last_slab = cs == n_split - 1
    nge = jnp.where(last_slab, nxt_after[ce], ce)
    nge = jnp.where(nge >= n_e, -1, nge)
    ngs = jnp.where(last_slab, 0, cs + 1)
    tail_first = (bstart[-1] + nblk[-1]).reshape(1)
    return tuple(a.astype(I32) for a in (ce, cs, cb, nge, ngs, cend[-1:], tail_first))


def kernel(x, c, ctx, c_ctx, w_mod, b_mod, g_mix, w_in, conv_a_w, conv_a_b, lru_w_r, lru_b_r,
           lru_w_i, lru_b_i, lru_lam, conv_b_w, g_out_a, g_out_b, w_out, g_ffn, w_router,
           b_router, w_gate, b_gate, w_up, b_up, w_down, b_down, g_final):
    assert w_mod.shape[0] == 1, "single-layer block"
    bsz, n_lat, d = x.shape
    d_a = conv_a_w.shape[-1]
    d_b = conv_b_w.shape[-1]
    n_e = w_router.shape[-1]
    assert n_lat % GRID_W == 0 and n_e <= LANES and d % (2 * LANES) == 0
    l = 0

    cs = jnp.zeros((SUBLANES, d), F32).at[:bsz].set(c).at[bsz].set(c_ctx)
    mod = _mod_call(cs, w_mod[l], b_mod[l][None])
    sh1, sc1, gt1, sh2, sc2, gt2 = [m[:bsz, None, :] for m in jnp.split(mod, 6, axis=-1)]
    ssh1, ssc1 = [jnp.broadcast_to(m[bsz][None, None, :], (bsz, 1, d))
                  for m in jnp.split(mod, 6, axis=-1)[:2]]

    w_in_bf = w_in[l].astype(BF16)
    g_mix2 = g_mix[l][None]
    cw = conv_a_w[l]
    cb = conv_a_b[l][None]
    wri = jnp.concatenate([lru_w_r[l], lru_w_i[l]], axis=-1).astype(BF16)
    br, bi, lam = lru_b_r[l], lru_b_i[l], lru_lam[l]
    lru_p = lambda dr: (cw, cb, wri[dr], br[dr][None], bi[dr][None], lam[dr][None])

    (s_ax,) = _inproj_call(ctx, g_mix2, ssh1, ssc1, w_in_bf[:, d_a:2 * d_a], d_a, d_b, latent=False)
    zero_state = jnp.zeros((bsz, 1, d_a), F32)
    _, h0f = _lru_call(s_ax, zero_state, *lru_p(0), reverse=False)
    _, h0b = _lru_call(s_ax, zero_state, *lru_p(1), reverse=True)

    ag, ax, bb, p = _inproj_call(x, g_mix2, sh1, sc1, w_in_bf, d_a, d_b, latent=True)
    hf, _ = _lru_call(ax, h0f, *lru_p(0), reverse=False)
    ya, _ = _lru_call(ax, h0b, *lru_p(1), reverse=True, hf=hf, ag=ag)

    wr_bf = jnp.zeros((d, LANES), BF16).at[:, :n_e].set(w_router[l].astype(BF16))
    brt = jnp.full((1, LANES), NEG_BIG, F32).at[0, :n_e].set(b_router[l])
    x1, xp, top_e, top_w = _mixout_call(
        ya, bb, p, conv_b_w[l], g_out_a[l][None], g_out_b[l][None], w_out[l].astype(BF16), x, gt1,
        g_ffn[l][None], sh2, sc2, wr_bf, brt)

    n_tok = bsz * n_lat
    n_rows = n_tok * TOP_K
    rb = MOE_ROW_BLOCK
    n_buf = n_rows + n_e * rb
    rank, cnt = _rank_call(top_e.reshape(n_tok, LANES))
    counts = cnt[0, :n_e].astype(I32)
    starts, groups = _group_tables(counts, rb)
    te = top_e.reshape(n_tok, LANES)[:, :TOP_K]
    onehot = te[:, :, None] == jnp.arange(n_e, dtype=I32)[None, None, :]
    pos = (jnp.sum(jnp.where(onehot, starts[None, None, :], 0), axis=-1) + rank[:, :TOP_K]).astype(I32)
    xs = _dispatch_call(groups, xp.reshape(n_tok, d // 2), pos, n_buf, rb)
    d_ff = w_gate.shape[-1]
    tf = _pick(d_ff, 512)
    tn = _pick(d, 1024)
    act = _gmm1_call(_chunk_tables(groups, d_ff // tf, n_buf // rb), xs, w_gate[l], w_up[l],
                     b_gate[l], b_up[l], rb, tf)
    y = _gmm2_call(_chunk_tables(groups, d // tn, n_buf // rb), act, w_down[l], b_down[l], rb, tn)
    return _final_call(y, pos, top_w, x1, gt2, g_final[None])
```

```python
import functools

import jax
import jax.numpy as jnp
from jax import lax
from jax.experimental import pallas as pl
from jax.experimental.pallas import tpu as pltpu

F32 = jnp.float32
BF16 = jnp.bfloat16
I32 = jnp.int32
U32 = jnp.uint32

GRID_W = 64
TOP_K = 4
LRU_C = 8.0
CONV_A_LEFT = 2
SWIGLU_LIMIT = 7.0
SWIGLU_ALPHA = 1.702
EPS = 1e-6
MOE_ROW_BLOCK = 256

LANES = 128
SUBLANES = 8
VMEM_LIMIT_BYTES = 56 * 1024 * 1024
NEG_BIG = -1e30


def _sds(shape, dtype):
    return jax.ShapeDtypeStruct(shape, dtype)


def _pick(n, pref, mult=LANES):
    if n <= pref:
        return n
    t = (pref // mult) * mult
    while t >= mult:
        if n % t == 0:
            return t
        t -= mult
    return n


def _params(sem):
    return pltpu.CompilerParams(dimension_semantics=sem, vmem_limit_bytes=VMEM_LIMIT_BYTES)


def _sigmoid(x):
    return 1.0 / (1.0 + jnp.exp(-x))


def _rms(x, g):
    ms = jnp.mean(x * x, axis=-1, keepdims=True)
    return (x * lax.rsqrt(ms + EPS)) * g


def _mod_kernel(c_ref, w_ref, b_ref, o_ref):
    c = c_ref[...]
    s = c * _sigmoid(c)
    o_ref[...] = jnp.dot(s.astype(BF16), w_ref[...].astype(BF16),
                         preferred_element_type=F32) + b_ref[...]


def _mod_call(cs, w, b):
    d, n6 = w.shape
    tn = _pick(n6, 1024)
    return pl.pallas_call(
        _mod_kernel,
        out_shape=_sds((cs.shape[0], n6), F32),
        grid=(n6 // tn,),
        in_specs=[pl.BlockSpec((cs.shape[0], d), lambda j: (0, 0)),
                  pl.BlockSpec((d, tn), lambda j: (0, j)),
                  pl.BlockSpec((1, tn), lambda j: (0, j))],
        out_specs=pl.BlockSpec((cs.shape[0], tn), lambda j: (0, j)),
        compiler_params=_params(("arbitrary",)),
        name="mod",
    )(cs, w, b)


def _inproj_kernel(x_ref, g_ref, sh_ref, sc_ref, w_ref, *out_refs, d_a, d_b, latent):
    x = x_ref[0]
    xn = _rms(x, g_ref[...]) * (1.0 + sc_ref[0]) + sh_ref[0]
    xb = xn.astype(BF16)

    def sec(lo, width):
        return jnp.dot(xb, w_ref[:, lo:lo + width], preferred_element_type=F32)

    if not latent:
        out_refs[0][0] = sec(0, d_a)
        return
    ag_ref, ax_ref, bb_ref, p_ref = out_refs
    ag_ref[0] = sec(0, d_a).astype(BF16)
    ax_ref[0] = sec(d_a, d_a)
    bb_ref[0] = sec(2 * d_a, d_b).astype(BF16)
    p_ref[0] = (sec(2 * d_a + d_b, d_b) * sec(2 * d_a + 2 * d_b, d_b)).astype(BF16)


def _inproj_call(x, g, sh, sc, w_bf, d_a, d_b, latent):
    bsz, n, d = x.shape
    tm = _pick(n, 512, SUBLANES)
    n_w = w_bf.shape[1]
    row = lambda b, i: (b, i, 0)
    if latent:
        out_shape = (_sds((bsz, n, d_a), BF16), _sds((bsz, n, d_a), F32),
                     _sds((bsz, n, d_b), BF16), _sds((bsz, n, d_b), BF16))
        out_specs = (pl.BlockSpec((1, tm, d_a), row), pl.BlockSpec((1, tm, d_a), row),
                     pl.BlockSpec((1, tm, d_b), row), pl.BlockSpec((1, tm, d_b), row))
    else:
        out_shape = (_sds((bsz, n, d_a), F32),)
        out_specs = (pl.BlockSpec((1, tm, d_a), row),)
    return pl.pallas_call(
        functools.partial(_inproj_kernel, d_a=d_a, d_b=d_b, latent=latent),
        out_shape=out_shape,
        grid=(bsz, n // tm),
        in_specs=[pl.BlockSpec((1, tm, d), row),
                  pl.BlockSpec((1, d), lambda b, i: (0, 0)),
                  pl.BlockSpec((1, 1, d), lambda b, i: (b, 0, 0)),
                  pl.BlockSpec((1, 1, d), lambda b, i: (b, 0, 0)),
                  pl.BlockSpec((d, n_w), lambda b, i: (0, 0), pipeline_mode=pl.Buffered(1))],
        out_specs=out_specs,
        compiler_params=_params(("arbitrary", "arbitrary")),
        name="inproj_lat" if latent else "inproj_ctx",
    )(x, g, sh, sc, w_bf)


def _gelu_tanh(x):
    c = 0.7978845608028654
    return x * (0.5 * (1.0 + jnp.tanh(c * (x + 0.044715 * (x * x * x)))))


def _lru_kernel(*refs, reverse, combine, nc, tl, heads, blk):
    if combine:
        (prev_ref, main_ref, next_ref, cw_ref, cb_ref, wri_ref, br_ref, bi_ref, lam_ref, h0_ref,
         hf_ref, ag_ref, out_ref, hlast_ref, ebuf, xc_s, a_s, b_s, carry) = refs
    else:
        (prev_ref, main_ref, next_ref, cw_ref, cb_ref, wri_ref, br_ref, bi_ref, lam_ref, h0_ref,
         out_ref, hlast_ref, ebuf, xc_s, a_s, b_s, carry) = refs
    da = heads * blk
    c = pl.program_id(1)
    cidx = (nc - 1 - c) if reverse else c

    @pl.when(c == 0)
    def _():
        carry[...] = jnp.broadcast_to(h0_ref[0], carry.shape)

    main = main_ref[0]
    zero8 = jnp.zeros((SUBLANES, da), F32)
    ebuf[0:SUBLANES, :] = jnp.where(cidx == 0, zero8, prev_ref[0])
    ebuf[SUBLANES:SUBLANES + tl, :] = main
    ebuf[SUBLANES + tl:2 * SUBLANES + tl, :] = jnp.where(cidx == nc - 1, zero8, next_ref[0])
    cw = cw_ref[...]
    off = SUBLANES - CONV_A_LEFT
    xc_s[...] = (cw[0:1] * ebuf[off:off + tl, :] + cw[1:2] * ebuf[off + 1:off + 1 + tl, :]
                 + cw[2:3] * main + cw[3:4] * ebuf[off + 3:off + 3 + tl, :] + cb_ref[...])

    z = -lam_ref[...]
    sp = jnp.maximum(z, 0.0) + jnp.log1p(jnp.exp(-jnp.abs(z)))
    rc = min(tl, 128)
    for r0 in range(0, tl, rc):
        for h in range(heads):
            cs = slice(h * blk, (h + 1) * blk)
            xh = xc_s[r0:r0 + rc, cs]
            zz = jnp.dot(xh.astype(BF16), wri_ref[h], preferred_element_type=F32)
            r = _sigmoid(zz[:, :blk] + br_ref[:, cs])
            i = _sigmoid(zz[:, blk:] + bi_ref[:, cs])
            log_a = (-LRU_C * r) * sp[:, cs]
            a = jnp.exp(log_a)
            a_s[r0:r0 + rc, cs] = a
            b_s[r0:r0 + rc, cs] = jnp.sqrt(jnp.tanh(-log_a) * (1.0 + a * a)) * (i * xh)

    row = lax.broadcasted_iota(I32, (SUBLANES, da), 0)
    ng = tl // SUBLANES
    shifts = (1, 2, 4)

    def body(gi, hc):
        g = (ng - 1 - gi) if reverse else gi
        r0 = pl.multiple_of(g * SUBLANES, SUBLANES)
        a = a_s[pl.ds(r0, SUBLANES), :]
        b = b_s[pl.ds(r0, SUBLANES), :]
        for s in shifts:
            if reverse:
                keep = row < (SUBLANES - s)
                sh = SUBLANES - s
            else:
                keep = row >= s
                sh = s
            a_sh = jnp.where(keep, pltpu.roll(a, sh, 0), 1.0)
            b_sh = jnp.where(keep, pltpu.roll(b, sh, 0), 0.0)
            b = a * b_sh + b
            a = a * a_sh
        hrows = a * hc + b
        b_s[pl.ds(r0, SUBLANES), :] = hrows
        edge = hrows[0:1, :] if reverse else hrows[SUBLANES - 1:SUBLANES, :]
        return jnp.broadcast_to(edge, (SUBLANES, da))

    hc = lax.fori_loop(0, ng, body, carry[...])
    carry[...] = hc
    hlast_ref[0] = hc[0:1, :]
    if combine:
        hsum = hf_ref[0] + b_s[...]
        out_ref[0] = (_gelu_tanh(ag_ref[0].astype(F32)) * hsum).astype(out_ref.dtype)
    else:
        out_ref[0] = b_s[...]


def _lru_call(ax, h0, cw, cb, wri, br, bi, lam, *, reverse, hf=None, ag=None):
    bsz, n, da = ax.shape
    heads, blk, _ = wri.shape
    tl = _pick(n, 512, SUBLANES)
    nc = n // tl
    nb8 = n // SUBLANES
    g8 = tl // SUBLANES
    combine = hf is not None

    def cidx(c):
        return (nc - 1 - c) if reverse else c

    main_map = lambda b, c: (b, cidx(c), 0)
    prev_map = lambda b, c: (b, jnp.maximum(cidx(c) * g8 - 1, 0), 0)
    next_map = lambda b, c: (b, jnp.minimum((cidx(c) + 1) * g8, nb8 - 1), 0)
    const2 = lambda b, c: (0, 0)
    in_specs = [pl.BlockSpec((1, SUBLANES, da), prev_map),
                pl.BlockSpec((1, tl, da), main_map),
                pl.BlockSpec((1, SUBLANES, da), next_map),
                pl.BlockSpec((4, da), const2),
                pl.BlockSpec((1, da), const2),
                pl.BlockSpec((heads, blk, 2 * blk), lambda b, c: (0, 0, 0)),
                pl.BlockSpec((1, da), const2),
                pl.BlockSpec((1, da), const2),
                pl.BlockSpec((1, da), const2),
                pl.BlockSpec((1, 1, da), lambda b, c: (b, 0, 0))]
    args = [ax, ax, ax, cw, cb, wri, br, bi, lam, h0]
    if combine:
        in_specs += [pl.BlockSpec((1, tl, da), main_map), pl.BlockSpec((1, tl, da), main_map)]
        args += [hf, ag]
    out_dtype = BF16 if combine else F32
    return pl.pallas_call(
        functools.partial(_lru_kernel, reverse=reverse, combine=combine, nc=nc, tl=tl,
                          heads=heads, blk=blk),
        out_shape=(_sds((bsz, n, da), out_dtype), _sds((bsz, 1, da), F32)),
        grid=(bsz, nc),
        in_specs=in_specs,
        out_specs=(pl.BlockSpec((1, tl, da), main_map),
                   pl.BlockSpec((1, 1, da), lambda b, c: (b, 0, 0))),
        scratch_shapes=[pltpu.VMEM((tl + 2 * SUBLANES, da), F32),
                        pltpu.VMEM((tl, da), F32),
                        pltpu.VMEM((tl, da), F32),
                        pltpu.VMEM((tl, da), F32),
                        pltpu.VMEM((SUBLANES, da), F32)],
        compiler_params=_params(("arbitrary", "arbitrary")),
        name=("lru_bwd" if reverse else "lru_fwd") + ("_mix" if combine else ""),
    )(*args)


def _mixout_kernel(ya_ref, bb_ref, p_ref, pu_ref, pd_ref, cbw_ref, ga_ref, gb_ref, wo_ref, x_ref,
                   gt_ref, gf_ref, sh_ref, sc_ref, wr_ref, brt_ref,
                   x1_ref, xp_ref, te_ref, tw_ref, *, tm, d_a, d_b, n_tiles):
    i = pl.program_id(1)
    half = d_b // 2
    z = p_ref[0].astype(F32)
    w = cbw_ref[...]
    zh = z[:, :half]
    col = lax.broadcasted_iota(I32, (tm, half), 0) % GRID_W
    left = jnp.where(col >= 1, pltpu.roll(zh, 1, 0), 0.0)
    right = jnp.where(col <= GRID_W - 2, pltpu.roll(zh, tm - 1, 0), 0.0)
    horiz = w[0:1, :half] * left + w[1:2, :half] * zh + w[2:3, :half] * right
    zv = z[:, half:]
    up_halo = jnp.where(i == 0, 0.0, pu_ref[0].astype(F32))
    dn_halo = jnp.where(i == n_tiles - 1, 0.0, pd_ref[0].astype(F32))
    if tm > GRID_W:
        up = jnp.concatenate([up_halo, zv[:tm - GRID_W]], axis=0)
        dn = jnp.concatenate([zv[GRID_W:], dn_halo], axis=0)
    else:
        up, dn = up_halo, dn_halo
    vert = w[0:1, half:] * up + w[1:2, half:] * zv + w[2:3, half:] * dn
    bb = bb_ref[0].astype(F32)
    yb = jnp.concatenate([bb[:, :half] * horiz, bb[:, half:] * vert], axis=1)
    ya = ya_ref[0].astype(F32)
    ya_n = _rms(ya, ga_ref[...]).astype(BF16)
    yb_n = _rms(yb, gb_ref[...]).astype(BF16)
    mix = (jnp.dot(ya_n, wo_ref[0:d_a, :], preferred_element_type=F32)
           + jnp.dot(yb_n, wo_ref[d_a:d_a + d_b, :], preferred_element_type=F32))
    x1 = x_ref[0] + gt_ref[0] * mix
    x1_ref[0] = x1
    xn = _rms(x1, gf_ref[...]) * (1.0 + sc_ref[0]) + sh_ref[0]
    xb = xn.astype(BF16)
    dh = xb.shape[1] // 2
    lo = lax.shift_right_logical(lax.bitcast_convert_type(xb[:, :dh].astype(F32), U32), jnp.uint32(16))
    hi = lax.bitcast_convert_type(xb[:, dh:].astype(F32), U32) & jnp.uint32(0xFFFF0000)
    xp_ref[0] = lo | hi
    logits = jnp.dot(xb, wr_ref[...], preferred_element_type=F32) + brt_ref[...]
    lane = lax.broadcasted_iota(I32, logits.shape, 1)
    lane_f = lane.astype(F32)
    vals = logits
    tv, te = [], []
    for _ in range(TOP_K):
        m = jnp.max(vals, axis=-1, keepdims=True)
        idx = jnp.min(jnp.where(vals == m, lane_f, float(LANES)), axis=-1, keepdims=True)
        tv.append(m)
        te.append(idx)
        vals = jnp.where(lane_f == idx, -jnp.inf, vals)
    ex = [jnp.exp(v - tv[0]) for v in tv]
    den = ex[0]
    for e in ex[1:]:
        den = den + e
    e_out = jnp.zeros(logits.shape, I32)
    w_out = jnp.zeros(logits.shape, F32)
    for k in range(TOP_K):
        e_out = jnp.where(lane == k, te[k].astype(I32), e_out)
        w_out = jnp.where(lane == k, ex[k] / den, w_out)
    te_ref[0] = e_out
    tw_ref[0] = w_out


def _mixout_call(ya, bb, p, cbw, ga, gb, wo_bf, x, gt1, gf, sh2, sc2, wr_bf, brt):
    bsz, n, d = x.shape
    d_a = ya.shape[-1]
    d_b = bb.shape[-1]
    half = d_b // 2
    tm = _pick(n, 256, GRID_W)
    n_tiles = n // tm
    rpt = tm // GRID_W
    n_rows = n // GRID_W
    row = lambda b, i: (b, i, 0)
    vec = lambda b, i: (b, 0, 0)
    const2 = lambda b, i: (0, 0)
    return pl.pallas_call(
        functools.partial(_mixout_kernel, tm=tm, d_a=d_a, d_b=d_b, n_tiles=n_tiles),
        out_shape=(_sds((bsz, n, d), F32), _sds((bsz, n, d // 2), U32),
                   _sds((bsz, n, LANES), I32), _sds((bsz, n, LANES), F32)),
        grid=(bsz, n_tiles),
        in_specs=[pl.BlockSpec((1, tm, d_a), row),
                  pl.BlockSpec((1, tm, d_b), row),
                  pl.BlockSpec((1, tm, d_b), row),
                  pl.BlockSpec((1, GRID_W, half), lambda b, i: (b, jnp.maximum(i * rpt - 1, 0), 1)),
                  pl.BlockSpec((1, GRID_W, half), lambda b, i: (b, jnp.minimum((i + 1) * rpt, n_rows - 1), 1)),
                  pl.BlockSpec((3, d_b), const2),
                  pl.BlockSpec((1, d_a), const2),
                  pl.BlockSpec((1, d_b), const2),
                  pl.BlockSpec((d_a + d_b, d), const2),
                  pl.BlockSpec((1, tm, d), row),
                  pl.BlockSpec((1, 1, d), vec),
                  pl.BlockSpec((1, d), const2),
                  pl.BlockSpec((1, 1, d), vec),
                  pl.BlockSpec((1, 1, d), vec),
                  pl.BlockSpec((d, LANES), const2),
                  pl.BlockSpec((1, LANES), const2)],
        out_specs=(pl.BlockSpec((1, tm, d), row), pl.BlockSpec((1, tm, d // 2), row),
                   pl.BlockSpec((1, tm, LANES), row), pl.BlockSpec((1, tm, LANES), row)),
        compiler_params=_params(("arbitrary", "arbitrary")),
        name="mixout",
    )(ya, bb, p, p, p, cbw, ga, gb, wo_bf, x, gt1, gf, sh2, sc2, wr_bf, brt)


def _rank_kernel(e_ref, rank_ref, cnt_ref, carry, *, tt):
    @pl.when(pl.program_id(0) == 0)
    def _():
        carry[...] = jnp.zeros(carry.shape, F32)

    e = e_ref[...]
    lane = lax.broadcasted_iota(I32, (tt, LANES), 1)
    ohs = []
    m = jnp.zeros((tt, LANES), F32)
    for k in range(TOP_K):
        oh = lane == e[:, k:k + 1]
        ohs.append(oh)
        m = m + jnp.where(oh, 1.0, 0.0)
    ri = lax.broadcasted_iota(I32, (tt, tt), 0)
    ci = lax.broadcasted_iota(I32, (tt, tt), 1)
    ltri = jnp.where(ri > ci, 1.0, 0.0).astype(BF16)
    pref = jnp.dot(ltri, m.astype(BF16), preferred_element_type=F32) + carry[0:1, :]
    out = jnp.zeros((tt, LANES), I32)
    for k in range(TOP_K):
        rk = jnp.sum(jnp.where(ohs[k], pref, 0.0), axis=-1, keepdims=True)
        out = jnp.where(lane == k, rk.astype(I32), out)
    rank_ref[...] = out
    tot = carry[0:1, :] + jnp.sum(m, axis=0, keepdims=True)
    carry[...] = jnp.broadcast_to(tot, carry.shape)
    cnt_ref[...] = jnp.broadcast_to(tot, cnt_ref.shape)


def _rank_call(top_e):
    t = top_e.shape[0]
    tt = _pick(t, 512, SUBLANES)
    return pl.pallas_call(
        functools.partial(_rank_kernel, tt=tt),
        out_shape=(_sds((t, LANES), I32), _sds((SUBLANES, LANES), F32)),
        grid=(t // tt,),
        in_specs=[pl.BlockSpec((tt, LANES), lambda i: (i, 0))],
        out_specs=(pl.BlockSpec((tt, LANES), lambda i: (i, 0)),
                   pl.BlockSpec((SUBLANES, LANES), lambda i: (0, 0))),
        scratch_shapes=[pltpu.VMEM((SUBLANES, LANES), F32)],
        compiler_params=_params(("arbitrary",)),
        name="rank",
    )(top_e)


def _zero_tail(first, n_blocks, zbuf, dst_block, sem):
    zbuf[...] = jnp.zeros(zbuf.shape, zbuf.dtype)

    def start(c, carry):
        pltpu.make_async_copy(zbuf, dst_block(c), sem).start()
        return carry

    def wait(c, carry):
        pltpu.make_async_copy(zbuf, dst_block(c), sem).wait()
        return carry

    lax.fori_loop(first, n_blocks, start, 0)
    lax.fori_loop(first, n_blocks, wait, 0)


def _dispatch_kernel(bs_ref, nb_ref, pos_ref, x_ref, o_hbm, zbuf, sem, zsem, *, td, rb, n_e, n_blocks):
    @pl.when(pl.program_id(0) == 0)
    def _():
        _zero_tail(bs_ref[n_e - 1] + nb_ref[n_e - 1], n_blocks, zbuf,
                   lambda c: o_hbm.at[pl.ds(pl.multiple_of(c * rb, rb), rb)], zsem)

        def zero_copy(e):
            r = pl.multiple_of((bs_ref[e] + nb_ref[e] - 1) * rb, rb)
            return pltpu.make_async_copy(zbuf, o_hbm.at[pl.ds(r, rb)], zsem)

        for e in range(n_e):
            @pl.when(nb_ref[e] > 0)
            def _():
                zero_copy(e).start()

        for e in range(n_e):
            @pl.when(nb_ref[e] > 0)
            def _():
                zero_copy(e).wait()

    def body(t, c):
        src = x_ref.at[pl.ds(t, 1)]
        for k in range(TOP_K):
            dst = o_hbm.at[pl.ds(pos_ref[0, 0, t * TOP_K + k], 1)]
            pltpu.make_async_copy(src, dst, sem).start()
        return c

    lax.fori_loop(0, td, body, 0, unroll=2)
    for _ in range(TOP_K):
        pltpu.make_async_copy(x_ref, o_hbm.at[pl.ds(0, td)], sem).wait()


def _dispatch_call(groups, xp, pos, n_buf, rb):
    bstart, nblk = groups
    n_e = bstart.shape[0]
    t, dw = xp.shape
    td = _pick(t, 512, SUBLANES)
    pos3 = pos.reshape(t // td, 1, td * TOP_K)
    return pl.pallas_call(
        functools.partial(_dispatch_kernel, td=td, rb=rb, n_e=n_e, n_blocks=n_buf // rb),
        out_shape=_sds((n_buf, dw), U32),
        grid_spec=pltpu.PrefetchScalarGridSpec(
            num_scalar_prefetch=2,
            grid=(t // td,),
            in_specs=[pl.BlockSpec((1, 1, td * TOP_K), lambda i, bs, nb: (i, 0, 0),
                                   memory_space=pltpu.SMEM),
                      pl.BlockSpec((td, dw), lambda i, bs, nb: (i, 0))],
            out_specs=pl.BlockSpec(memory_space=pl.ANY),
            scratch_shapes=[pltpu.VMEM((rb, dw), U32), pltpu.SemaphoreType.DMA,
                            pltpu.SemaphoreType.DMA]),
        compiler_params=_params(("arbitrary",)),
        name="dispatch",
    )(bstart, nblk, pos3, xp)


def _unpack_rows(xp):
    lo = lax.bitcast_convert_type(lax.shift_left(xp, jnp.uint32(16)), F32)
    hi = lax.bitcast_convert_type(xp & jnp.uint32(0xFFFF0000), F32)
    return jnp.concatenate([lo.astype(BF16), hi.astype(BF16)], axis=1)


GROUP_IN_SLOTS = 4
GROUP_OUT_SLOTS = 3


def _chunk_loop(tabs, in_copy, out_copy, w_copies, load_weights, compute):
    ce, cs, cb, nge, ngs, nch = tabs
    n = nch[0]
    ahead = GROUP_IN_SLOTS - 1

    @pl.when(n > 0)
    def _():
        for cp in w_copies(ce[0], cs[0]):
            cp.start()
        for j in range(ahead):
            @pl.when(j < n)
            def _():
                in_copy(j, j).start()

        def body(j, carry):
            jp = jnp.maximum(j - 1, 0)
            first = jnp.logical_or(j == 0, jnp.logical_or(ce[j] != ce[jp], cs[j] != cs[jp]))

            @pl.when(first)
            def _():
                for cp in w_copies(ce[j], cs[j]):
                    cp.wait()
                load_weights()

                @pl.when(nge[j] >= 0)
                def _():
                    for cp in w_copies(nge[j], ngs[j]):
                        cp.start()

            islot = lax.rem(j, GROUP_IN_SLOTS)
            oslot = lax.rem(j, GROUP_OUT_SLOTS)

            @pl.when(j + ahead < n)
            def _():
                in_copy(j + ahead, lax.rem(j + ahead, GROUP_IN_SLOTS)).start()

            in_copy(j, islot).wait()

            @pl.when(j >= GROUP_OUT_SLOTS)
            def _():
                out_copy(j - GROUP_OUT_SLOTS, oslot).wait()

            compute(j, islot, oslot)
            out_copy(j, oslot).start()
            return carry

        lax.fori_loop(0, n, body, 0)

        for k in range(GROUP_OUT_SLOTS):
            @pl.when(n > k)
            def _():
                out_copy(n - 1 - k, lax.rem(n - 1 - k, GROUP_OUT_SLOTS)).wait()


def _gmm1_kernel(ce, cs, cb, nge, ngs, nch, tail, xs_hbm, wg_hbm, wu_hbm, bg_ref, bu_ref, act_hbm,
                 xbuf, obuf, wraw, wbf, zbuf, sin, sout, wsem, zsem, *, rb, tf, n_blocks, n_split):
    def blk(b):
        return pl.ds(pl.multiple_of(b * rb, rb), rb)

    def cols(s_):
        return pl.ds(pl.multiple_of(s_ * tf, tf), tf)

    for s_ in range(n_split):
        _zero_tail(tail[0], n_blocks, zbuf, lambda c: act_hbm.at[blk(c), cols(s_)], zsem)

    def in_copy(j, slot):
        return pltpu.make_async_copy(xs_hbm.at[blk(cb[j])], xbuf.at[slot], sin.at[slot])

    def out_copy(j, slot):
        return pltpu.make_async_copy(obuf.at[slot], act_hbm.at[blk(cb[j]), cols(cs[j])], sout.at[slot])

    def w_copies(e, s_):
        return (pltpu.make_async_copy(wg_hbm.at[e, :, cols(s_)], wraw.at[0], wsem),
                pltpu.make_async_copy(wu_hbm.at[e, :, cols(s_)], wraw.at[1], wsem))

    def load_weights():
        wbf[:, 0:tf] = wraw[0].astype(BF16)
        wbf[:, tf:2 * tf] = wraw[1].astype(BF16)

    def compute(j, islot, oslot):
        g = ce[j] * n_split + cs[j]
        h = _unpack_rows(xbuf[islot])
        gu = jnp.dot(h, wbf[...], preferred_element_type=F32)
        gate = jnp.minimum(gu[:, :tf] + bg_ref[g], SWIGLU_LIMIT)
        up = jnp.clip(gu[:, tf:] + bu_ref[g], -SWIGLU_LIMIT, SWIGLU_LIMIT)
        act = (up + 1.0) * gate * _sigmoid(SWIGLU_ALPHA * gate)
        obuf[oslot] = act.astype(BF16)

    _chunk_loop((ce, cs, cb, nge, ngs, nch), in_copy, out_copy, w_copies, load_weights, compute)


def _gmm1_call(tabs, xs, wg, wu, bg, bu, rb, tf):
    n_buf, dw = xs.shape
    n_e, d, dff = wg.shape
    n_split = dff // tf
    vmem_full = lambda shape: pl.BlockSpec(shape, lambda i, *_: (0,) * len(shape))
    return pl.pallas_call(
        functools.partial(_gmm1_kernel, rb=rb, tf=tf, n_blocks=n_buf // rb, n_split=n_split),
        out_shape=_sds((n_buf, dff), BF16),
        grid_spec=pltpu.PrefetchScalarGridSpec(
            num_scalar_prefetch=len(tabs),
            grid=(1,),
            in_specs=[pl.BlockSpec(memory_space=pl.ANY),
                      pl.BlockSpec(memory_space=pl.ANY),
                      pl.BlockSpec(memory_space=pl.ANY),
                      vmem_full((n_e * n_split, 1, tf)),
                      vmem_full((n_e * n_split, 1, tf))],
            out_specs=pl.BlockSpec(memory_space=pl.ANY),
            scratch_shapes=[pltpu.VMEM((GROUP_IN_SLOTS, rb, dw), U32),
                            pltpu.VMEM((GROUP_OUT_SLOTS, rb, tf), BF16),
                            pltpu.VMEM((2, d, tf), F32),
                            pltpu.VMEM((d, 2 * tf), BF16),
                            pltpu.VMEM((rb, tf), BF16),
                            pltpu.SemaphoreType.DMA((GROUP_IN_SLOTS,)),
                            pltpu.SemaphoreType.DMA((GROUP_OUT_SLOTS,)),
                            pltpu.SemaphoreType.DMA,
                            pltpu.SemaphoreType.DMA]),
        compiler_params=_params(("arbitrary",)),
        name="gmm1",
    )(*tabs, xs, wg, wu, bg.reshape(n_e * n_split, 1, tf), bu.reshape(n_e * n_split, 1, tf))


def _gmm2_kernel(ce, cs, cb, nge, ngs, nch, tail, act_hbm, wd_hbm, bd_ref, y_hbm,
                 abuf, ybuf, wraw, wbf, zbuf, sin, sout, wsem, zsem, *, rb, tn, n_blocks, n_split):
    def blk(b):
        return pl.ds(pl.multiple_of(b * rb, rb), rb)

    def cols(s_):
        return pl.ds(pl.multiple_of(s_ * tn, tn), tn)

    for s_ in range(n_split):
        _zero_tail(tail[0], n_blocks, zbuf, lambda c: y_hbm.at[blk(c), cols(s_)], zsem)

    def in_copy(j, slot):
        return pltpu.make_async_copy(act_hbm.at[blk(cb[j])], abuf.at[slot], sin.at[slot])

    def out_copy(j, slot):
        return pltpu.make_async_copy(ybuf.at[slot], y_hbm.at[blk(cb[j]), cols(cs[j])], sout.at[slot])

    def w_copies(e, s_):
        return (pltpu.make_async_copy(wd_hbm.at[e, :, cols(s_)], wraw, wsem),)

    def load_weights():
        wbf[...] = wraw[...].astype(BF16)

    def compute(j, islot, oslot):
        g = ce[j] * n_split + cs[j]
        ybuf[oslot] = jnp.dot(abuf[islot], wbf[...], preferred_element_type=F32) + bd_ref[g]

    _chunk_loop((ce, cs, cb, nge, ngs, nch), in_copy, out_copy, w_copies, load_weights, compute)


def _gmm2_call(tabs, act, wd, bd, rb, tn):
    n_buf, dff = act.shape
    n_e, _, d = wd.shape
    n_split = d // tn
    vmem_full = lambda shape: pl.BlockSpec(shape, lambda i, *_: (0,) * len(shape))
    return pl.pallas_call(
        functools.partial(_gmm2_kernel, rb=rb, tn=tn, n_blocks=n_buf // rb, n_split=n_split),
        out_shape=_sds((n_buf, d), F32),
        grid_spec=pltpu.PrefetchScalarGridSpec(
            num_scalar_prefetch=len(tabs),
            grid=(1,),
            in_specs=[pl.BlockSpec(memory_space=pl.ANY),
                      pl.BlockSpec(memory_space=pl.ANY),
                      vmem_full((n_e * n_split, 1, tn))],
            out_specs=pl.BlockSpec(memory_space=pl.ANY),
            scratch_shapes=[pltpu.VMEM((GROUP_IN_SLOTS, rb, dff), BF16),
                            pltpu.VMEM((GROUP_OUT_SLOTS, rb, tn), F32),
                            pltpu.VMEM((dff, tn), F32),
                            pltpu.VMEM((dff, tn), BF16),
                            pltpu.VMEM((rb, tn), F32),
                            pltpu.SemaphoreType.DMA((GROUP_IN_SLOTS,)),
                            pltpu.SemaphoreType.DMA((GROUP_OUT_SLOTS,)),
                            pltpu.SemaphoreType.DMA,
                            pltpu.SemaphoreType.DMA]),
        compiler_params=_params(("arbitrary",)),
        name="gmm2",
    )(*tabs, act, wd, bd.reshape(n_e * n_split, 1, tn))


def _final_kernel(pos_ref, posn_ref, y_hbm, tw_ref, x1_ref, gt_ref, gfin_ref, o_ref, ybuf, sem, *,
                  tc, n_steps):
    step = pl.program_id(0) * pl.num_programs(1) + pl.program_id(1)

    def issue(p_ref, slot):
        def body(t, c):
            for k in range(TOP_K):
                src = y_hbm.at[pl.ds(p_ref[0, 0, t * TOP_K + k], 1)]
                pltpu.make_async_copy(src, ybuf.at[slot, k, pl.ds(t, 1)], sem.at[slot]).start()
            return c

        lax.fori_loop(0, tc, body, 0, unroll=2)

    def run(cur):
        @pl.when(step + 1 < n_steps)
        def _():
            issue(posn_ref, 1 - cur)

        for k in range(TOP_K):
            pltpu.make_async_copy(y_hbm.at[pl.ds(0, tc)], ybuf.at[cur, k], sem.at[cur]).wait()
        tw = tw_ref[0]
        moe = tw[:, 0:1] * ybuf[cur, 0]
        for k in range(1, TOP_K):
            moe = moe + tw[:, k:k + 1] * ybuf[cur, k]
        x2 = x1_ref[0] + gt_ref[0] * moe
        o_ref[0] = _rms(x2, gfin_ref[...])

    @pl.when(step == 0)
    def _():
        issue(pos_ref, 0)

    parity = lax.rem(step, 2)

    @pl.when(parity == 0)
    def _():
        run(0)

    @pl.when(parity == 1)
    def _():
        run(1)


def _final_call(y, pos, top_w, x1, gt2, gfin):
    bsz, n, d = x1.shape
    tc = _pick(n, 256, SUBLANES)
    nt = n // tc
    n_steps = bsz * nt
    pos3 = pos.reshape(n_steps, 1, tc * TOP_K)
    row = lambda b, i: (b, i, 0)
    return pl.pallas_call(
        functools.partial(_final_kernel, tc=tc, n_steps=n_steps),
        out_shape=_sds((bsz, n, d), F32),
        grid=(bsz, nt),
        in_specs=[pl.BlockSpec((1, 1, tc * TOP_K), lambda b, i: (b * nt + i, 0, 0), memory_space=pltpu.SMEM),
                  pl.BlockSpec((1, 1, tc * TOP_K), lambda b, i: (jnp.minimum(b * nt + i + 1, n_steps - 1), 0, 0),
                               memory_space=pltpu.SMEM),
                  pl.BlockSpec(memory_space=pl.ANY),
                  pl.BlockSpec((1, tc, LANES), row),
                  pl.BlockSpec((1, tc, d), row),
                  pl.BlockSpec((1, 1, d), lambda b, i: (b, 0, 0)),
                  pl.BlockSpec((1, d), lambda b, i: (0, 0))],
        out_specs=pl.BlockSpec((1, tc, d), row),
        scratch_shapes=[pltpu.VMEM((2, TOP_K, tc, d), F32), pltpu.SemaphoreType.DMA((2,))],
        compiler_params=_params(("arbitrary", "arbitrary")),
        name="final",
    )(pos3, pos3, y, top_w, x1, gt2, gfin)


def _group_tables(counts, rb):
    nblk = (counts + rb - 1) // rb
    bstart = jnp.cumsum(nblk) - nblk
    return bstart * rb, (bstart.astype(I32), nblk.astype(I32))


def _chunk_tables(groups, n_split, max_blocks):
    bstart, nblk = groups
    n_e = nblk.shape[0]
    per_e = nblk * n_split
    cend = jnp.cumsum(per_e)
    cstart = cend - per_e
    j = jnp.arange(n_split * max_blocks, dtype=I32)
    ce = jnp.minimum(jnp.sum((cend[None, :] <= j[:, None]).astype(I32), axis=1), n_e - 1)
    r = j - cstart[ce]
    nb = jnp.maximum(nblk[ce], 1)
    cs = jnp.clip(r // nb, 0, n_split - 1)
    cb = bstart[ce] + r % nb
    idx = jnp.arange(n_e, dtype=I32)
    cand = jnp.where(nblk > 0, idx, n_e)
    nxt = lax.cummin(cand, axis=0, reverse=True)
    nxt_after = jnp.concatenate([nxt[1:], jnp.full((1,), n_e, I32)])
    last_slab = cs == n_split - 1
    nge = jnp.where(last_slab, nxt_after[ce], ce)
    nge = jnp.where(nge >= n_e, -1, nge)
    ngs = jnp.where(last_slab, 0, cs + 1)
    tail_first = (bstart[-1] + nblk[-1]).reshape(1)
    return tuple(a.astype(I32) for a in (ce, cs, cb, nge, ngs, cend[-1:], tail_first))


def kernel(x, c, ctx, c_ctx, w_mod, b_mod, g_mix, w_in, conv_a_w, conv_a_b, lru_w_r, lru_b_r,
           lru_w_i, lru_b_i, lru_lam, conv_b_w, g_out_a, g_out_b, w_out, g_ffn, w_router,
           b_router, w_gate, b_gate, w_up, b_up, w_down, b_down, g_final):
    assert w_mod.shape[0] == 1, "single-layer block"
    bsz, n_lat, d = x.shape
    d_a = conv_a_w.shape[-1]
    d_b = conv_b_w.shape[-1]
    n_e = w_router.shape[-1]
    assert n_lat % GRID_W == 0 and n_e <= LANES and d % (2 * LANES) == 0
    l = 0

    cs = jnp.zeros((SUBLANES, d), F32).at[:bsz].set(c).at[bsz].set(c_ctx)
    mod = _mod_call(cs, w_mod[l], b_mod[l][None])
    sh1, sc1, gt1, sh2, sc2, gt2 = [m[:bsz, None, :] for m in jnp.split(mod, 6, axis=-1)]
    ssh1, ssc1 = [jnp.broadcast_to(m[bsz][None, None, :], (bsz, 1, d))
                  for m in jnp.split(mod, 6, axis=-1)[:2]]

    w_in_bf = w_in[l].astype(BF16)
    g_mix2 = g_mix[l][None]
    cw = conv_a_w[l]
    cb = conv_a_b[l][None]
    wri = jnp.concatenate([lru_w_r[l], lru_w_i[l]], axis=-1).astype(BF16)
    br, bi, lam = lru_b_r[l], lru_b_i[l], lru_lam[l]
    lru_p = lambda dr: (cw, cb, wri[dr], br[dr][None], bi[dr][None], lam[dr][None])

    (s_ax,) = _inproj_call(ctx, g_mix2, ssh1, ssc1, w_in_bf[:, d_a:2 * d_a], d_a, d_b, latent=False)
    zero_state = jnp.zeros((bsz, 1, d_a), F32)
    _, h0f = _lru_call(s_ax, zero_state, *lru_p(0), reverse=False)
    _, h0b = _lru_call(s_ax, zero_state, *lru_p(1), reverse=True)

    ag, ax, bb, p = _inproj_call(x, g_mix2, sh1, sc1, w_in_bf, d_a, d_b, latent=True)
    hf, _ = _lru_call(ax, h0f, *lru_p(0), reverse=False)
    ya, _ = _lru_call(ax, h0b, *lru_p(1), reverse=True, hf=hf, ag=ag)

    wr_bf = jnp.zeros((d, LANES), BF16).at[:, :n_e].set(w_router[l].astype(BF16))
    brt = jnp.full((1, LANES), NEG_BIG, F32).at[0, :n_e].set(b_router[l])
    x1, xp, top_e, top_w = _mixout_call(
        ya, bb, p, conv_b_w[l], g_out_a[l][None], g_out_b[l][None], w_out[l].astype(BF16), x, gt1,
        g_ffn[l][None], sh2, sc2, wr_bf, brt)

    n_tok = bsz * n_lat
    n_rows = n_tok * TOP_K
    rb = MOE_ROW_BLOCK
    n_buf = n_rows + n_e * rb
    rank, cnt = _rank_call(top_e.reshape(n_tok, LANES))
    counts = cnt[0, :n_e].astype(I32)
    starts, groups = _group_tables(counts, rb)
    te = top_e.reshape(n_tok, LANES)[:, :TOP_K]
    onehot = te[:, :, None] == jnp.arange(n_e, dtype=I32)[None, None, :]
    pos = (jnp.sum(jnp.where(onehot, starts[None, None, :], 0), axis=-1) + rank[:, :TOP_K]).astype(I32)
    xs = _dispatch_call(groups, xp.reshape(n_tok, d // 2), pos, n_buf, rb)
    d_ff = w_gate.shape[-1]
    tf = _pick(d_ff, 1024)
    tn = _pick(d, 2048)
    act = _gmm1_call(_chunk_tables(groups, d_ff // tf, n_buf // rb), xs, w_gate[l], w_up[l],
                     b_gate[l], b_up[l], rb, tf)
    y = _gmm2_call(_chunk_tables(groups, d // tn, n_buf // rb), act, w_down[l], b_down[l], rb, tn)
    return _final_call(y, pos, top_w, x1, gt2, g_final[None])
```

```python
import functools

import jax
import jax.numpy as jnp
from jax import lax
from jax.experimental import pallas as pl
from jax.experimental.pallas import tpu as pltpu

F32 = jnp.float32
BF16 = jnp.bfloat16
I32 = jnp.int32
U32 = jnp.uint32

GRID_W = 64
TOP_K = 4
LRU_C = 8.0
CONV_A_LEFT = 2
SWIGLU_LIMIT = 7.0
SWIGLU_ALPHA = 1.702
EPS = 1e-6
MOE_ROW_BLOCK = 256

LANES = 128
SUBLANES = 8
VMEM_LIMIT_BYTES = 56 * 1024 * 1024
NEG_BIG = -1e30


def _sds(shape, dtype):
    return jax.ShapeDtypeStruct(shape, dtype)


def _pick(n, pref, mult=LANES):
    if n <= pref:
        return n
    t = (pref // mult) * mult
    while t >= mult:
        if n % t == 0:
            return t
        t -= mult
    return n


def _params(sem):
    return pltpu.CompilerParams(dimension_semantics=sem, vmem_limit_bytes=VMEM_LIMIT_BYTES)


def _sigmoid(x):
    return 1.0 / (1.0 + jnp.exp(-x))


def _rms(x, g):
    ms = jnp.mean(x * x, axis=-1, keepdims=True)
    return (x * lax.rsqrt(ms + EPS)) * g


def _pack_pairs(x):
    w = x.shape[1] // 2
    lo = lax.bitcast_convert_type(x[:, :w].astype(BF16).astype(F32), U32)
    hi = lax.bitcast_convert_type(x[:, w:].astype(BF16).astype(F32), U32)
    return lax.shift_right_logical(lo, jnp.uint32(16)) | (hi & jnp.uint32(0xFFFF0000))


def _unpack_pairs(words):
    lo = lax.bitcast_convert_type(lax.shift_left(words, jnp.uint32(16)), F32)
    hi = lax.bitcast_convert_type(words & jnp.uint32(0xFFFF0000), F32)
    return lo, hi


def _store_token_tiles(ref, words, rows):
    nseg = words.shape[1] // LANES
    if nseg == 1:
        ref[...] = words
        return
    for s in range(nseg):
        ref[pl.ds(s, rows, stride=nseg), :] = words[:, s * LANES:(s + 1) * LANES]


def _load_token_tiles(ref, rows, nseg):
    if nseg == 1:
        return ref[...]
    return jnp.concatenate([ref[pl.ds(s, rows, stride=nseg), :] for s in range(nseg)], axis=1)


def _mod_kernel(c_ref, w_ref, b_ref, o_ref):
    c = c_ref[...]
    s = c * _sigmoid(c)
    o_ref[...] = jnp.dot(s.astype(BF16), w_ref[...].astype(BF16),
                         preferred_element_type=F32) + b_ref[...]


def _mod_call(cs, w, b):
    d, n6 = w.shape
    tn = _pick(n6, 1024)
    return pl.pallas_call(
        _mod_kernel,
        out_shape=_sds((cs.shape[0], n6), F32),
        grid=(n6 // tn,),
        in_specs=[pl.BlockSpec((cs.shape[0], d), lambda j: (0, 0)),
                  pl.BlockSpec((d, tn), lambda j: (0, j)),
                  pl.BlockSpec((1, tn), lambda j: (0, j))],
        out_specs=pl.BlockSpec((cs.shape[0], tn), lambda j: (0, j)),
        compiler_params=_params(("arbitrary",)),
        name="mod",
    )(cs, w, b)


def _inproj_kernel(x_ref, g_ref, sh_ref, sc_ref, w_ref, *out_refs, d_a, d_b, latent):
    x = x_ref[0]
    xn = _rms(x, g_ref[...]) * (1.0 + sc_ref[0]) + sh_ref[0]
    xb = xn.astype(BF16)

    def sec(lo, width):
        return jnp.dot(xb, w_ref[:, lo:lo + width], preferred_element_type=F32)

    if not latent:
        out_refs[0][0] = sec(0, d_a)
        return
    ag_ref, ax_ref, bb_ref, p_ref = out_refs
    ag_ref[0] = sec(0, d_a).astype(BF16)
    ax_ref[0] = sec(d_a, d_a)
    bb_ref[0] = sec(2 * d_a, d_b).astype(BF16)
    p_ref[0] = (sec(2 * d_a + d_b, d_b) * sec(2 * d_a + 2 * d_b, d_b)).astype(BF16)


def _inproj_call(x, g, sh, sc, w_bf, d_a, d_b, latent):
    bsz, n, d = x.shape
    tm = _pick(n, 512, SUBLANES)
    n_w = w_bf.shape[1]
    row = lambda b, i: (b, i, 0)
    if latent:
        out_shape = (_sds((bsz, n, d_a), BF16), _sds((bsz, n, d_a), F32),
                     _sds((bsz, n, d_b), BF16), _sds((bsz, n, d_b), BF16))
        out_specs = (pl.BlockSpec((1, tm, d_a), row), pl.BlockSpec((1, tm, d_a), row),
                     pl.BlockSpec((1, tm, d_b), row), pl.BlockSpec((1, tm, d_b), row))
    else:
        out_shape = (_sds((bsz, n, d_a), F32),)
        out_specs = (pl.BlockSpec((1, tm, d_a), row),)
    return pl.pallas_call(
        functools.partial(_inproj_kernel, d_a=d_a, d_b=d_b, latent=latent),
        out_shape=out_shape,
        grid=(bsz, n // tm),
        in_specs=[pl.BlockSpec((1, tm, d), row),
                  pl.BlockSpec((1, d), lambda b, i: (0, 0)),
                  pl.BlockSpec((1, 1, d), lambda b, i: (b, 0, 0)),
                  pl.BlockSpec((1, 1, d), lambda b, i: (b, 0, 0)),
                  pl.BlockSpec((d, n_w), lambda b, i: (0, 0), pipeline_mode=pl.Buffered(1))],
        out_specs=out_specs,
        compiler_params=_params(("arbitrary", "arbitrary")),
        name="inproj_lat" if latent else "inproj_ctx",
    )(x, g, sh, sc, w_bf)


def _gelu_tanh(x):
    c = 0.7978845608028654
    return x * (0.5 * (1.0 + jnp.tanh(c * (x + 0.044715 * (x * x * x)))))


def _lru_kernel(*refs, reverse, combine, nc, tl, heads, blk):
    if combine:
        (prev_ref, main_ref, next_ref, cw_ref, cb_ref, wri_ref, br_ref, bi_ref, lam_ref, h0_ref,
         hf_ref, ag_ref, out_ref, hlast_ref, ebuf, xc_s, a_s, b_s, carry) = refs
    else:
        (prev_ref, main_ref, next_ref, cw_ref, cb_ref, wri_ref, br_ref, bi_ref, lam_ref, h0_ref,
         out_ref, hlast_ref, ebuf, xc_s, a_s, b_s, carry) = refs
    da = heads * blk
    c = pl.program_id(1)
    cidx = (nc - 1 - c) if reverse else c

    @pl.when(c == 0)
    def _():
        carry[...] = jnp.broadcast_to(h0_ref[0], carry.shape)

    main = main_ref[0]
    zero8 = jnp.zeros((SUBLANES, da), F32)
    ebuf[0:SUBLANES, :] = jnp.where(cidx == 0, zero8, prev_ref[0])
    ebuf[SUBLANES:SUBLANES + tl, :] = main
    ebuf[SUBLANES + tl:2 * SUBLANES + tl, :] = jnp.where(cidx == nc - 1, zero8, next_ref[0])
    cw = cw_ref[...]
    off = SUBLANES - CONV_A_LEFT
    xc_s[...] = (cw[0:1] * ebuf[off:off + tl, :] + cw[1:2] * ebuf[off + 1:off + 1 + tl, :]
                 + cw[2:3] * main + cw[3:4] * ebuf[off + 3:off + 3 + tl, :] + cb_ref[...])

    z = -lam_ref[...]
    sp = jnp.maximum(z, 0.0) + jnp.log1p(jnp.exp(-jnp.abs(z)))
    rc = min(tl, 128)
    for r0 in range(0, tl, rc):
        for h in range(heads):
            cs = slice(h * blk, (h + 1) * blk)
            xh = xc_s[r0:r0 + rc, cs]
            zz = jnp.dot(xh.astype(BF16), wri_ref[h], preferred_element_type=F32)
            r = _sigmoid(zz[:, :blk] + br_ref[:, cs])
            i = _sigmoid(zz[:, blk:] + bi_ref[:, cs])
            log_a = (-LRU_C * r) * sp[:, cs]
            a = jnp.exp(log_a)
            a_s[r0:r0 + rc, cs] = a
            b_s[r0:r0 + rc, cs] = jnp.sqrt(jnp.tanh(-log_a) * (1.0 + a * a)) * (i * xh)

    row = lax.broadcasted_iota(I32, (SUBLANES, da), 0)
    ng = tl // SUBLANES
    shifts = (1, 2, 4)

    def body(gi, hc):
        g = (ng - 1 - gi) if reverse else gi
        r0 = pl.multiple_of(g * SUBLANES, SUBLANES)
        a = a_s[pl.ds(r0, SUBLANES), :]
        b = b_s[pl.ds(r0, SUBLANES), :]
        for s in shifts:
            if reverse:
                keep = row < (SUBLANES - s)
                sh = SUBLANES - s
            else:
                keep = row >= s
                sh = s
            a_sh = jnp.where(keep, pltpu.roll(a, sh, 0), 1.0)
            b_sh = jnp.where(keep, pltpu.roll(b, sh, 0), 0.0)
            b = a * b_sh + b
            a = a * a_sh
        hrows = a * hc + b
        b_s[pl.ds(r0, SUBLANES), :] = hrows
        edge = hrows[0:1, :] if reverse else hrows[SUBLANES - 1:SUBLANES, :]
        return jnp.broadcast_to(edge, (SUBLANES, da))

    hc = lax.fori_loop(0, ng, body, carry[...])
    carry[...] = hc
    hlast_ref[0] = hc[0:1, :]
    if combine:
        hsum = hf_ref[0] + b_s[...]
        out_ref[0] = (_gelu_tanh(ag_ref[0].astype(F32)) * hsum).astype(out_ref.dtype)
    else:
        out_ref[0] = b_s[...]


def _lru_call(ax, h0, cw, cb, wri, br, bi, lam, *, reverse, hf=None, ag=None):
    bsz, n, da = ax.shape
    heads, blk, _ = wri.shape
    tl = _pick(n, 512, SUBLANES)
    nc = n // tl
    nb8 = n // SUBLANES
    g8 = tl // SUBLANES
    combine = hf is not None

    def cidx(c):
        return (nc - 1 - c) if reverse else c

    main_map = lambda b, c: (b, cidx(c), 0)
    prev_map = lambda b, c: (b, jnp.maximum(cidx(c) * g8 - 1, 0), 0)
    next_map = lambda b, c: (b, jnp.minimum((cidx(c) + 1) * g8, nb8 - 1), 0)
    const2 = lambda b, c: (0, 0)
    in_specs = [pl.BlockSpec((1, SUBLANES, da), prev_map),
                pl.BlockSpec((1, tl, da), main_map),
                pl.BlockSpec((1, SUBLANES, da), next_map),
                pl.BlockSpec((4, da), const2),
                pl.BlockSpec((1, da), const2),
                pl.BlockSpec((heads, blk, 2 * blk), lambda b, c: (0, 0, 0)),
                pl.BlockSpec((1, da), const2),
                pl.BlockSpec((1, da), const2),
                pl.BlockSpec((1, da), const2),
                pl.BlockSpec((1, 1, da), lambda b, c: (b, 0, 0))]
    args = [ax, ax, ax, cw, cb, wri, br, bi, lam, h0]
    if combine:
        in_specs += [pl.BlockSpec((1, tl, da), main_map), pl.BlockSpec((1, tl, da), main_map)]
        args += [hf, ag]
    out_dtype = BF16 if combine else F32
    return pl.pallas_call(
        functools.partial(_lru_kernel, reverse=reverse, combine=combine, nc=nc, tl=tl,
                          heads=heads, blk=blk),
        out_shape=(_sds((bsz, n, da), out_dtype), _sds((bsz, 1, da), F32)),
        grid=(bsz, nc),
        in_specs=in_specs,
        out_specs=(pl.BlockSpec((1, tl, da), main_map),
                   pl.BlockSpec((1, 1, da), lambda b, c: (b, 0, 0))),
        scratch_shapes=[pltpu.VMEM((tl + 2 * SUBLANES, da), F32),
                        pltpu.VMEM((tl, da), F32),
                        pltpu.VMEM((tl, da), F32),
                        pltpu.VMEM((tl, da), F32),
                        pltpu.VMEM((SUBLANES, da), F32)],
        compiler_params=_params(("arbitrary", "arbitrary")),
        name=("lru_bwd" if reverse else "lru_fwd") + ("_mix" if combine else ""),
    )(*args)


def _mixout_kernel(ya_ref, bb_ref, p_ref, pu_ref, pd_ref, cbw_ref, ga_ref, gb_ref, wo_ref, x_ref,
                   gt_ref, gf_ref, sh_ref, sc_ref, wr_ref, brt_ref,
                   x1_ref, xp_ref, te_ref, tw_ref, *, tm, d_a, d_b, n_tiles):
    i = pl.program_id(1)
    half = d_b // 2
    z = p_ref[0].astype(F32)
    w = cbw_ref[...]
    zh = z[:, :half]
    col = lax.broadcasted_iota(I32, (tm, half), 0) % GRID_W
    left = jnp.where(col >= 1, pltpu.roll(zh, 1, 0), 0.0)
    right = jnp.where(col <= GRID_W - 2, pltpu.roll(zh, tm - 1, 0), 0.0)
    horiz = w[0:1, :half] * left + w[1:2, :half] * zh + w[2:3, :half] * right
    zv = z[:, half:]
    up_halo = jnp.where(i == 0, 0.0, pu_ref[0].astype(F32))
    dn_halo = jnp.where(i == n_tiles - 1, 0.0, pd_ref[0].astype(F32))
    if tm > GRID_W:
        up = jnp.concatenate([up_halo, zv[:tm - GRID_W]], axis=0)
        dn = jnp.concatenate([zv[GRID_W:], dn_halo], axis=0)
    else:
        up, dn = up_halo, dn_halo
    vert = w[0:1, half:] * up + w[1:2, half:] * zv + w[2:3, half:] * dn
    bb = bb_ref[0].astype(F32)
    yb = jnp.concatenate([bb[:, :half] * horiz, bb[:, half:] * vert], axis=1)
    ya = ya_ref[0].astype(F32)
    ya_n = _rms(ya, ga_ref[...]).astype(BF16)
    yb_n = _rms(yb, gb_ref[...]).astype(BF16)
    mix = (jnp.dot(ya_n, wo_ref[0:d_a, :], preferred_element_type=F32)
           + jnp.dot(yb_n, wo_ref[d_a:d_a + d_b, :], preferred_element_type=F32))
    x1 = x_ref[0] + gt_ref[0] * mix
    x1_ref[0] = x1
    xn = _rms(x1, gf_ref[...]) * (1.0 + sc_ref[0]) + sh_ref[0]
    xb = xn.astype(BF16)
    _store_token_tiles(xp_ref.at[0], _pack_pairs(xn), tm)
    logits = jnp.dot(xb, wr_ref[...], preferred_element_type=F32) + brt_ref[...]
    lane = lax.broadcasted_iota(I32, logits.shape, 1)
    lane_f = lane.astype(F32)
    vals = logits
    tv, te = [], []
    for _ in range(TOP_K):
        m = jnp.max(vals, axis=-1, keepdims=True)
        idx = jnp.min(jnp.where(vals == m, lane_f, float(LANES)), axis=-1, keepdims=True)
        tv.append(m)
        te.append(idx)
        vals = jnp.where(lane_f == idx, -jnp.inf, vals)
    ex = [jnp.exp(v - tv[0]) for v in tv]
    den = ex[0]
    for e in ex[1:]:
        den = den + e
    e_out = jnp.zeros(logits.shape, I32)
    w_out = jnp.zeros(logits.shape, F32)
    for k in range(TOP_K):
        e_out = jnp.where(lane == k, te[k].astype(I32), e_out)
        w_out = jnp.where(lane == k, ex[k] / den, w_out)
    te_ref[0] = e_out
    tw_ref[0] = w_out


def _mixout_call(ya, bb, p, cbw, ga, gb, wo_bf, x, gt1, gf, sh2, sc2, wr_bf, brt):
    bsz, n, d = x.shape
    d_a = ya.shape[-1]
    d_b = bb.shape[-1]
    half = d_b // 2
    tm = _pick(n, 256, GRID_W)
    n_tiles = n // tm
    nseg = (d // 2) // LANES
    rpt = tm // GRID_W
    n_rows = n // GRID_W
    row = lambda b, i: (b, i, 0)
    vec = lambda b, i: (b, 0, 0)
    const2 = lambda b, i: (0, 0)
    return pl.pallas_call(
        functools.partial(_mixout_kernel, tm=tm, d_a=d_a, d_b=d_b, n_tiles=n_tiles),
        out_shape=(_sds((bsz, n, d), F32), _sds((bsz, n * nseg, LANES), U32),
                   _sds((bsz, n, LANES), I32), _sds((bsz, n, LANES), F32)),
        grid=(bsz, n_tiles),
        in_specs=[pl.BlockSpec((1, tm, d_a), row),
                  pl.BlockSpec((1, tm, d_b), row),
                  pl.BlockSpec((1, tm, d_b), row),
                  pl.BlockSpec((1, GRID_W, half), lambda b, i: (b, jnp.maximum(i * rpt - 1, 0), 1)),
                  pl.BlockSpec((1, GRID_W, half), lambda b, i: (b, jnp.minimum((i + 1) * rpt, n_rows - 1), 1)),
                  pl.BlockSpec((3, d_b), const2),
                  pl.BlockSpec((1, d_a), const2),
                  pl.BlockSpec((1, d_b), const2),
                  pl.BlockSpec((d_a + d_b, d), const2),
                  pl.BlockSpec((1, tm, d), row),
                  pl.BlockSpec((1, 1, d), vec),
                  pl.BlockSpec((1, d), const2),
                  pl.BlockSpec((1, 1, d), vec),
                  pl.BlockSpec((1, 1, d), vec),
                  pl.BlockSpec((d, LANES), const2),
                  pl.BlockSpec((1, LANES), const2)],
        out_specs=(pl.BlockSpec((1, tm, d), row), pl.BlockSpec((1, tm * nseg, LANES), row),
                   pl.BlockSpec((1, tm, LANES), row), pl.BlockSpec((1, tm, LANES), row)),
        compiler_params=_params(("arbitrary", "arbitrary")),
        name="mixout",
    )(ya, bb, p, p, p, cbw, ga, gb, wo_bf, x, gt1, gf, sh2, sc2, wr_bf, brt)


def _rank_kernel(e_ref, rank_ref, cnt_ref, carry, *, tt):
    @pl.when(pl.program_id(0) == 0)
    def _():
        carry[...] = jnp.zeros(carry.shape, F32)

    e = e_ref[...]
    lane = lax.broadcasted_iota(I32, (tt, LANES), 1)
    ohs = []
    m = jnp.zeros((tt, LANES), F32)
    for k in range(TOP_K):
        oh = lane == e[:, k:k + 1]
        ohs.append(oh)
        m = m + jnp.where(oh, 1.0, 0.0)
    ri = lax.broadcasted_iota(I32, (tt, tt), 0)
    ci = lax.broadcasted_iota(I32, (tt, tt), 1)
    ltri = jnp.where(ri > ci, 1.0, 0.0).astype(BF16)
    pref = jnp.dot(ltri, m.astype(BF16), preferred_element_type=F32) + carry[0:1, :]
    out = jnp.zeros((tt, LANES), I32)
    for k in range(TOP_K):
        rk = jnp.sum(jnp.where(ohs[k], pref, 0.0), axis=-1, keepdims=True)
        out = jnp.where(lane == k, rk.astype(I32), out)
    rank_ref[...] = out
    tot = carry[0:1, :] + jnp.sum(m, axis=0, keepdims=True)
    carry[...] = jnp.broadcast_to(tot, carry.shape)
    cnt_ref[...] = jnp.broadcast_to(tot, cnt_ref.shape)


def _rank_call(top_e):
    t = top_e.shape[0]
    tt = _pick(t, 512, SUBLANES)
    return pl.pallas_call(
        functools.partial(_rank_kernel, tt=tt),
        out_shape=(_sds((t, LANES), I32), _sds((SUBLANES, LANES), F32)),
        grid=(t // tt,),
        in_specs=[pl.BlockSpec((tt, LANES), lambda i: (i, 0))],
        out_specs=(pl.BlockSpec((tt, LANES), lambda i: (i, 0)),
                   pl.BlockSpec((SUBLANES, LANES), lambda i: (0, 0))),
        scratch_shapes=[pltpu.VMEM((SUBLANES, LANES), F32)],
        compiler_params=_params(("arbitrary",)),
        name="rank",
    )(top_e)


def _zero_tail(first, n_blocks, zbuf, dst_block, sem):
    zbuf[...] = jnp.zeros(zbuf.shape, zbuf.dtype)

    def start(c, carry):
        pltpu.make_async_copy(zbuf, dst_block(c), sem).start()
        return carry

    def wait(c, carry):
        pltpu.make_async_copy(zbuf, dst_block(c), sem).wait()
        return carry

    lax.fori_loop(first, n_blocks, start, 0)
    lax.fori_loop(first, n_blocks, wait, 0)


def _dispatch_kernel(bs_ref, nb_ref, pos_ref, x_ref, o_hbm, zbuf, sem, zsem, *, td, rb, n_e, n_blocks, nseg):
    rbr = rb * nseg

    def blk(b):
        return pl.ds(pl.multiple_of(b * rbr, rbr), rbr)

    def tok(t):
        return pl.ds(pl.multiple_of(t * nseg, nseg), nseg)

    @pl.when(pl.program_id(0) == 0)
    def _():
        _zero_tail(bs_ref[n_e - 1] + nb_ref[n_e - 1], n_blocks, zbuf, lambda c: o_hbm.at[blk(c)], zsem)

        def zero_copy(e):
            return pltpu.make_async_copy(zbuf, o_hbm.at[blk(bs_ref[e] + nb_ref[e] - 1)], zsem)

        for e in range(n_e):
            @pl.when(nb_ref[e] > 0)
            def _():
                zero_copy(e).start()

        for e in range(n_e):
            @pl.when(nb_ref[e] > 0)
            def _():
                zero_copy(e).wait()

    def body(t, c):
        src = x_ref.at[tok(t)]
        for k in range(TOP_K):
            pltpu.make_async_copy(src, o_hbm.at[tok(pos_ref[0, 0, t * TOP_K + k])], sem).start()
        return c

    lax.fori_loop(0, td, body, 0, unroll=2)
    for _ in range(TOP_K):
        pltpu.make_async_copy(x_ref, o_hbm.at[pl.ds(0, td * nseg)], sem).wait()


def _dispatch_call(groups, xp, pos, n_buf, rb, nseg):
    bstart, nblk = groups
    n_e = bstart.shape[0]
    t = xp.shape[0] // nseg
    td = _pick(t, 512, SUBLANES)
    pos3 = pos.reshape(t // td, 1, td * TOP_K)
    return pl.pallas_call(
        functools.partial(_dispatch_kernel, td=td, rb=rb, n_e=n_e, n_blocks=n_buf // rb, nseg=nseg),
        out_shape=_sds((n_buf * nseg, LANES), U32),
        grid_spec=pltpu.PrefetchScalarGridSpec(
            num_scalar_prefetch=2,
            grid=(t // td,),
            in_specs=[pl.BlockSpec((1, 1, td * TOP_K), lambda i, bs, nb: (i, 0, 0),
                                   memory_space=pltpu.SMEM),
                      pl.BlockSpec((td * nseg, LANES), lambda i, bs, nb: (i, 0))],
            out_specs=pl.BlockSpec(memory_space=pl.ANY),
            scratch_shapes=[pltpu.VMEM((rb * nseg, LANES), U32), pltpu.SemaphoreType.DMA,
                            pltpu.SemaphoreType.DMA]),
        compiler_params=_params(("arbitrary",)),
        name="dispatch",
    )(bstart, nblk, pos3, xp)


GROUP_IN_SLOTS = 4
GROUP_OUT_SLOTS = 3


def _chunk_loop(tabs, in_copy, out_copy, w_copies, load_weights, compute):
    ce, cs, cb, nge, ngs, nch = tabs
    n = nch[0]
    ahead = GROUP_IN_SLOTS - 1

    @pl.when(n > 0)
    def _():
        for cp in w_copies(ce[0], cs[0]):
            cp.start()
        for j in range(ahead):
            @pl.when(j < n)
            def _():
                in_copy(j, j).start()

        def body(j, carry):
            jp = jnp.maximum(j - 1, 0)
            first = jnp.logical_or(j == 0, jnp.logical_or(ce[j] != ce[jp], cs[j] != cs[jp]))

            @pl.when(first)
            def _():
                for cp in w_copies(ce[j], cs[j]):
                    cp.wait()
                load_weights()

                @pl.when(nge[j] >= 0)
                def _():
                    for cp in w_copies(nge[j], ngs[j]):
                        cp.start()

            islot = lax.rem(j, GROUP_IN_SLOTS)
            oslot = lax.rem(j, GROUP_OUT_SLOTS)

            @pl.when(j + ahead < n)
            def _():
                in_copy(j + ahead, lax.rem(j + ahead, GROUP_IN_SLOTS)).start()

            in_copy(j, islot).wait()

            @pl.when(j >= GROUP_OUT_SLOTS)
            def _():
                out_copy(j - GROUP_OUT_SLOTS, oslot).wait()

            compute(j, islot, oslot)
            out_copy(j, oslot).start()
            return carry

        lax.fori_loop(0, n, body, 0)

        for k in range(GROUP_OUT_SLOTS):
            @pl.when(n > k)
            def _():
                out_copy(n - 1 - k, lax.rem(n - 1 - k, GROUP_OUT_SLOTS)).wait()


def _gmm1_kernel(ce, cs, cb, nge, ngs, nch, tail, xs_hbm, wg_hbm, wu_hbm, bg_ref, bu_ref, act_hbm,
                 xbuf, obuf, wraw, wbf, zbuf, sin, sout, wsem, zsem, *, rb, tf, n_blocks, n_split, nseg):
    def blk(b):
        return pl.ds(pl.multiple_of(b * rb, rb), rb)

    def tok_blk(b):
        return pl.ds(pl.multiple_of(b * (rb * nseg), rb * nseg), rb * nseg)

    def cols(s_):
        return pl.ds(pl.multiple_of(s_ * tf, tf), tf)

    for s_ in range(n_split):
        _zero_tail(tail[0], n_blocks, zbuf, lambda c: act_hbm.at[blk(c), cols(s_)], zsem)

    def in_copy(j, slot):
        return pltpu.make_async_copy(xs_hbm.at[tok_blk(cb[j])], xbuf.at[slot], sin.at[slot])

    def out_copy(j, slot):
        return pltpu.make_async_copy(obuf.at[slot], act_hbm.at[blk(cb[j]), cols(cs[j])], sout.at[slot])

    def w_copies(e, s_):
        return (pltpu.make_async_copy(wg_hbm.at[e, :, cols(s_)], wraw.at[0], wsem),
                pltpu.make_async_copy(wu_hbm.at[e, :, cols(s_)], wraw.at[1], wsem))

    def load_weights():
        wbf[:, 0:tf] = wraw[0].astype(BF16)
        wbf[:, tf:2 * tf] = wraw[1].astype(BF16)

    def compute(j, islot, oslot):
        g = ce[j] * n_split + cs[j]
        lo, hi = _unpack_pairs(_load_token_tiles(xbuf.at[islot], rb, nseg))
        h = jnp.concatenate([lo.astype(BF16), hi.astype(BF16)], axis=1)
        gu = jnp.dot(h, wbf[...], preferred_element_type=F32)
        gate = jnp.minimum(gu[:, :tf] + bg_ref[g], SWIGLU_LIMIT)
        up = jnp.clip(gu[:, tf:] + bu_ref[g], -SWIGLU_LIMIT, SWIGLU_LIMIT)
        act = (up + 1.0) * gate * _sigmoid(SWIGLU_ALPHA * gate)
        obuf[oslot] = act.astype(BF16)

    _chunk_loop((ce, cs, cb, nge, ngs, nch), in_copy, out_copy, w_copies, load_weights, compute)


def _gmm1_call(tabs, xs, wg, wu, bg, bu, rb, tf, nseg):
    n_buf = xs.shape[0] // nseg
    n_e, d, dff = wg.shape
    n_split = dff // tf
    vmem_full = lambda shape: pl.BlockSpec(shape, lambda i, *_: (0,) * len(shape))
    return pl.pallas_call(
        functools.partial(_gmm1_kernel, rb=rb, tf=tf, n_blocks=n_buf // rb, n_split=n_split, nseg=nseg),
        out_shape=_sds((n_buf, dff), BF16),
        grid_spec=pltpu.PrefetchScalarGridSpec(
            num_scalar_prefetch=len(tabs),
            grid=(1,),
            in_specs=[pl.BlockSpec(memory_space=pl.ANY),
                      pl.BlockSpec(memory_space=pl.ANY),
                      pl.BlockSpec(memory_space=pl.ANY),
                      vmem_full((n_e * n_split, 1, tf)),
                      vmem_full((n_e * n_split, 1, tf))],
            out_specs=pl.BlockSpec(memory_space=pl.ANY),
            scratch_shapes=[pltpu.VMEM((GROUP_IN_SLOTS, rb * nseg, LANES), U32),
                            pltpu.VMEM((GROUP_OUT_SLOTS, rb, tf), BF16),
                            pltpu.VMEM((2, d, tf), F32),
                            pltpu.VMEM((d, 2 * tf), BF16),
                            pltpu.VMEM((rb, tf), BF16),
                            pltpu.SemaphoreType.DMA((GROUP_IN_SLOTS,)),
                            pltpu.SemaphoreType.DMA((GROUP_OUT_SLOTS,)),
                            pltpu.SemaphoreType.DMA,
                            pltpu.SemaphoreType.DMA]),
        compiler_params=_params(("arbitrary",)),
        name="gmm1",
    )(*tabs, xs, wg, wu, bg.reshape(n_e * n_split, 1, tf), bu.reshape(n_e * n_split, 1, tf))


def _gmm2_kernel(ce, cs, cb, nge, ngs, nch, tail, act_hbm, wd_hbm, bd_ref, y_hbm,
                 abuf, ybuf, wraw, wbf, zbuf, sin, sout, wsem, zsem, *, rb, n_blocks, nseg):
    def blk(b):
        return pl.ds(pl.multiple_of(b * rb, rb), rb)

    def tok_blk(b):
        return pl.ds(pl.multiple_of(b * (rb * nseg), rb * nseg), rb * nseg)

    _zero_tail(tail[0], n_blocks, zbuf, lambda c: y_hbm.at[tok_blk(c)], zsem)

    def in_copy(j, slot):
        return pltpu.make_async_copy(act_hbm.at[blk(cb[j])], abuf.at[slot], sin.at[slot])

    def out_copy(j, slot):
        return pltpu.make_async_copy(ybuf.at[slot], y_hbm.at[tok_blk(cb[j])], sout.at[slot])

    def w_copies(e, s_):
        return (pltpu.make_async_copy(wd_hbm.at[e], wraw, wsem),)

    def load_weights():
        wbf[...] = wraw[...].astype(BF16)

    def compute(j, islot, oslot):
        y = jnp.dot(abuf[islot], wbf[...], preferred_element_type=F32) + bd_ref[ce[j]]
        _store_token_tiles(ybuf.at[oslot], _pack_pairs(y), rb)

    _chunk_loop((ce, cs, cb, nge, ngs, nch), in_copy, out_copy, w_copies, load_weights, compute)


def _gmm2_call(tabs, act, wd, bd, rb, nseg):
    n_buf, dff = act.shape
    n_e, _, d = wd.shape
    assert d == 2 * nseg * LANES
    vmem_full = lambda shape: pl.BlockSpec(shape, lambda i, *_: (0,) * len(shape))
    return pl.pallas_call(
        functools.partial(_gmm2_kernel, rb=rb, n_blocks=n_buf // rb, nseg=nseg),
        out_shape=_sds((n_buf * nseg, LANES), U32),
        grid_spec=pltpu.PrefetchScalarGridSpec(
            num_scalar_prefetch=len(tabs),
            grid=(1,),
            in_specs=[pl.BlockSpec(memory_space=pl.ANY),
                      pl.BlockSpec(memory_space=pl.ANY),
                      vmem_full((n_e, 1, d))],
            out_specs=pl.BlockSpec(memory_space=pl.ANY),
            scratch_shapes=[pltpu.VMEM((GROUP_IN_SLOTS, rb, dff), BF16),
                            pltpu.VMEM((GROUP_OUT_SLOTS, rb * nseg, LANES), U32),
                            pltpu.VMEM((dff, d), F32),
                            pltpu.VMEM((dff, d), BF16),
                            pltpu.VMEM((rb * nseg, LANES), U32),
                            pltpu.SemaphoreType.DMA((GROUP_IN_SLOTS,)),
                            pltpu.SemaphoreType.DMA((GROUP_OUT_SLOTS,)),
                            pltpu.SemaphoreType.DMA,
                            pltpu.SemaphoreType.DMA]),
        compiler_params=_params(("arbitrary",)),
        name="gmm2",
    )(*tabs, act, wd, bd.reshape(n_e, 1, d))


def _final_kernel(pos_ref, posn_ref, y_hbm, tw_ref, x1_ref, gt_ref, gfin_ref, o_ref, ybuf, sem, *,
                  tc, n_steps, nseg):
    step = pl.program_id(0) * pl.num_programs(1) + pl.program_id(1)

    def tok(t):
        return pl.ds(pl.multiple_of(t * nseg, nseg), nseg)

    def issue(p_ref, slot):
        def body(t, c):
            for k in range(TOP_K):
                src = y_hbm.at[tok(p_ref[0, 0, t * TOP_K + k])]
                pltpu.make_async_copy(src, ybuf.at[slot, k, tok(t)], sem.at[slot]).start()
            return c

        lax.fori_loop(0, tc, body, 0, unroll=2)

    def run(cur):
        @pl.when(step + 1 < n_steps)
        def _():
            issue(posn_ref, 1 - cur)

        for k in range(TOP_K):
            pltpu.make_async_copy(y_hbm.at[pl.ds(0, tc * nseg)], ybuf.at[cur, k], sem.at[cur]).wait()
        tw = tw_ref[0]
        lo, hi = None, None
        for k in range(TOP_K):
            lo_k, hi_k = _unpack_pairs(_load_token_tiles(ybuf.at[cur, k], tc, nseg))
            wk = tw[:, k:k + 1]
            lo = wk * lo_k if lo is None else lo + wk * lo_k
            hi = wk * hi_k if hi is None else hi + wk * hi_k
        moe = jnp.concatenate([lo, hi], axis=1)
        x2 = x1_ref[0] + gt_ref[0] * moe
        o_ref[0] = _rms(x2, gfin_ref[...])

    @pl.when(step == 0)
    def _():
        issue(pos_ref, 0)

    parity = lax.rem(step, 2)

    @pl.when(parity == 0)
    def _():
        run(0)

    @pl.when(parity == 1)
    def _():
        run(1)


def _final_call(y, pos, top_w, x1, gt2, gfin, nseg):
    bsz, n, d = x1.shape
    tc = _pick(n, 256, SUBLANES)
    nt = n // tc
    n_steps = bsz * nt
    pos3 = pos.reshape(n_steps, 1, tc * TOP_K)
    row = lambda b, i: (b, i, 0)
    return pl.pallas_call(
        functools.partial(_final_kernel, tc=tc, n_steps=n_steps, nseg=nseg),
        out_shape=_sds((bsz, n, d), F32),
        grid=(bsz, nt),
        in_specs=[pl.BlockSpec((1, 1, tc * TOP_K), lambda b, i: (b * nt + i, 0, 0), memory_space=pltpu.SMEM),
                  pl.BlockSpec((1, 1, tc * TOP_K), lambda b, i: (jnp.minimum(b * nt + i + 1, n_steps - 1), 0, 0),
                               memory_space=pltpu.SMEM),
                  pl.BlockSpec(memory_space=pl.ANY),
                  pl.BlockSpec((1, tc, LANES), row),
                  pl.BlockSpec((1, tc, d), row),
                  pl.BlockSpec((1, 1, d), lambda b, i: (b, 0, 0)),
                  pl.BlockSpec((1, d), lambda b, i: (0, 0))],
        out_specs=pl.BlockSpec((1, tc, d), row),
        scratch_shapes=[pltpu.VMEM((2, TOP_K, tc * nseg, LANES), U32), pltpu.SemaphoreType.DMA((2,))],
        compiler_params=_params(("arbitrary", "arbitrary")),
        name="final",
    )(pos3, pos3, y, top_w, x1, gt2, gfin)


def _group_tables(counts, rb):
    nblk = (counts + rb - 1) // rb
    bstart = jnp.cumsum(nblk) - nblk
    return bstart * rb, (bstart.astype(I32), nblk.astype(I32))


def _chunk_tables(groups, n_split, max_blocks):
    bstart, nblk = groups
    n_e = nblk.shape[0]
    per_e = nblk * n_split
    cend = jnp.cumsum(per_e)
    cstart = cend - per_e
    j = jnp.arange(n_split * max_blocks, dtype=I32)
    ce = jnp.minimum(jnp.sum((cend[None, :] <= j[:, None]).astype(I32), axis=1), n_e - 1)
    r = j - cstart[ce]
    nb = jnp.maximum(nblk[ce], 1)
    cs = jnp.clip(r // nb, 0, n_split - 1)
    cb = bstart[ce] + r % nb
    idx = jnp.arange(n_e, dtype=I32)
    cand = jnp.where(nblk > 0, idx, n_e)
    nxt = lax.cummin(cand, axis=0, reverse=True)
    nxt_after = jnp.concatenate([nxt[1:], jnp.full((1,), n_e, I32)])
    last_slab = cs == n_split - 1
    nge = jnp.where(last_slab, nxt_after[ce], ce)
    nge = jnp.where(nge >= n_e, -1, nge)
    ngs = jnp.where(last_slab, 0, cs + 1)
    tail_first = (bstart[-1] + nblk[-1]).reshape(1)
    return tuple(a.astype(I32) for a in (ce, cs, cb, nge, ngs, cend[-1:], tail_first))


def kernel(x, c, ctx, c_ctx, w_mod, b_mod, g_mix, w_in, conv_a_w, conv_a_b, lru_w_r, lru_b_r,
           lru_w_i, lru_b_i, lru_lam, conv_b_w, g_out_a, g_out_b, w_out, g_ffn, w_router,
           b_router, w_gate, b_gate, w_up, b_up, w_down, b_down, g_final):
    assert w_mod.shape[0] == 1, "single-layer block"
    bsz, n_lat, d = x.shape
    d_a = conv_a_w.shape[-1]
    d_b = conv_b_w.shape[-1]
    n_e = w_router.shape[-1]
    assert n_lat % GRID_W == 0 and n_e <= LANES and d % (2 * LANES) == 0
    l = 0

    cs = jnp.zeros((SUBLANES, d), F32).at[:bsz].set(c).at[bsz].set(c_ctx)
    mod = _mod_call(cs, w_mod[l], b_mod[l][None])
    sh1, sc1, gt1, sh2, sc2, gt2 = [m[:bsz, None, :] for m in jnp.split(mod, 6, axis=-1)]
    ssh1, ssc1 = [jnp.broadcast_to(m[bsz][None, None, :], (bsz, 1, d))
                  for m in jnp.split(mod, 6, axis=-1)[:2]]

    w_in_bf = w_in[l].astype(BF16)
    g_mix2 = g_mix[l][None]
    cw = conv_a_w[l]
    cb = conv_a_b[l][None]
    wri = jnp.concatenate([lru_w_r[l], lru_w_i[l]], axis=-1).astype(BF16)
    br, bi, lam = lru_b_r[l], lru_b_i[l], lru_lam[l]
    lru_p = lambda dr: (cw, cb, wri[dr], br[dr][None], bi[dr][None], lam[dr][None])

    (s_ax,) = _inproj_call(ctx, g_mix2, ssh1, ssc1, w_in_bf[:, d_a:2 * d_a], d_a, d_b, latent=False)
    zero_state = jnp.zeros((bsz, 1, d_a), F32)
    _, h0f = _lru_call(s_ax, zero_state, *lru_p(0), reverse=False)
    _, h0b = _lru_call(s_ax, zero_state, *lru_p(1), reverse=True)

    ag, ax, bb, p = _inproj_call(x, g_mix2, sh1, sc1, w_in_bf, d_a, d_b, latent=True)
    hf, _ = _lru_call(ax, h0f, *lru_p(0), reverse=False)
    ya, _ = _lru_call(ax, h0b, *lru_p(1), reverse=True, hf=hf, ag=ag)

    wr_bf = jnp.zeros((d, LANES), BF16).at[:, :n_e].set(w_router[l].astype(BF16))
    brt = jnp.full((1, LANES), NEG_BIG, F32).at[0, :n_e].set(b_router[l])
    x1, xp, top_e, top_w = _mixout_call(
        ya, bb, p, conv_b_w[l], g_out_a[l][None], g_out_b[l][None], w_out[l].astype(BF16), x, gt1,
        g_ffn[l][None], sh2, sc2, wr_bf, brt)

    n_tok = bsz * n_lat
    n_rows = n_tok * TOP_K
    rb = MOE_ROW_BLOCK
    n_buf = n_rows + n_e * rb
    rank, cnt = _rank_call(top_e.reshape(n_tok, LANES))
    counts = cnt[0, :n_e].astype(I32)
    starts, groups = _group_tables(counts, rb)
    te = top_e.reshape(n_tok, LANES)[:, :TOP_K]
    onehot = te[:, :, None] == jnp.arange(n_e, dtype=I32)[None, None, :]
    pos = (jnp.sum(jnp.where(onehot, starts[None, None, :], 0), axis=-1) + rank[:, :TOP_K]).astype(I32)
    nseg = (d // 2) // LANES
    xs = _dispatch_call(groups, xp.reshape(n_tok * nseg, LANES), pos, n_buf, rb, nseg)
    d_ff = w_gate.shape[-1]
    tf = _pick(d_ff, 1024)
    act = _gmm1_call(_chunk_tables(groups, d_ff // tf, n_buf // rb), xs, w_gate[l], w_up[l],
                     b_gate[l], b_up[l], rb, tf, nseg)
    y = _gmm2_call(_chunk_tables(groups, 1, n_buf // rb), act, w_down[l], b_down[l], rb, nseg)
    return _final_call(y, pos, top_w, x1, gt2, g_final[None], nseg)
```

```python
import functools

import jax
import jax.numpy as jnp
from jax import lax
from jax.experimental import pallas as pl
from jax.experimental.pallas import tpu as pltpu

F32 = jnp.float32
BF16 = jnp.bfloat16
I32 = jnp.int32
U32 = jnp.uint32

GRID_W = 64
TOP_K = 4
LRU_C = 8.0
CONV_A_LEFT = 2
SWIGLU_LIMIT = 7.0
SWIGLU_ALPHA = 1.702
EPS = 1e-6
MOE_ROW_BLOCK = 256

LANES = 128
SUBLANES = 8
VMEM_LIMIT_BYTES = 56 * 1024 * 1024
NEG_BIG = -1e30


def _sds(shape, dtype):
    return jax.ShapeDtypeStruct(shape, dtype)


def _pick(n, pref, mult=LANES):
    if n <= pref:
        return n
    t = (pref // mult) * mult
    while t >= mult:
        if n % t == 0:
            return t
        t -= mult
    return n


def _params(sem):
    return pltpu.CompilerParams(dimension_semantics=sem, vmem_limit_bytes=VMEM_LIMIT_BYTES)


def _sigmoid(x):
    return 1.0 / (1.0 + jnp.exp(-x))


def _rms(x, g):
    ms = jnp.mean(x * x, axis=-1, keepdims=True)
    return (x * lax.rsqrt(ms + EPS)) * g


def _pack_pairs(x):
    w = x.shape[1] // 2
    lo = lax.bitcast_convert_type(x[:, :w].astype(BF16).astype(F32), U32)
    hi = lax.bitcast_convert_type(x[:, w:].astype(BF16).astype(F32), U32)
    return lax.shift_right_logical(lo, jnp.uint32(16)) | (hi & jnp.uint32(0xFFFF0000))


def _unpack_pairs(words):
    lo = lax.bitcast_convert_type(lax.shift_left(words, jnp.uint32(16)), F32)
    hi = lax.bitcast_convert_type(words & jnp.uint32(0xFFFF0000), F32)
    return lo, hi


def _store_token_tiles(ref, words, rows):
    nseg = words.shape[1] // LANES
    if nseg == 1:
        ref[...] = words
        return
    for s in range(nseg):
        ref[pl.ds(s, rows, stride=nseg), :] = words[:, s * LANES:(s + 1) * LANES]


def _load_token_tiles(ref, rows, nseg):
    if nseg == 1:
        return ref[...]
    return jnp.concatenate([ref[pl.ds(s, rows, stride=nseg), :] for s in range(nseg)], axis=1)


def _mod_kernel(c_ref, w_ref, b_ref, o_ref):
    c = c_ref[...]
    s = c * _sigmoid(c)
    o_ref[...] = jnp.dot(s.astype(BF16), w_ref[...].astype(BF16),
                         preferred_element_type=F32) + b_ref[...]


def _mod_call(cs, w, b):
    d, n6 = w.shape
    tn = _pick(n6, 1024)
    return pl.pallas_call(
        _mod_kernel,
        out_shape=_sds((cs.shape[0], n6), F32),
        grid=(n6 // tn,),
        in_specs=[pl.BlockSpec((cs.shape[0], d), lambda j: (0, 0)),
                  pl.BlockSpec((d, tn), lambda j: (0, j)),
                  pl.BlockSpec((1, tn), lambda j: (0, j))],
        out_specs=pl.BlockSpec((cs.shape[0], tn), lambda j: (0, j)),
        compiler_params=_params(("arbitrary",)),
        name="mod",
    )(cs, w, b)


def _inproj_kernel(x_ref, g_ref, sh_ref, sc_ref, w_ref, *out_refs, d_a, d_b, latent):
    x = x_ref[0]
    xn = _rms(x, g_ref[...]) * (1.0 + sc_ref[0]) + sh_ref[0]
    xb = xn.astype(BF16)

    def sec(lo, width):
        return jnp.dot(xb, w_ref[:, lo:lo + width], preferred_element_type=F32)

    if not latent:
        out_refs[0][0] = sec(0, d_a)
        return
    ag_ref, ax_ref, bb_ref, p_ref = out_refs
    ag_ref[0] = sec(0, d_a).astype(BF16)
    ax_ref[0] = sec(d_a, d_a)
    bb_ref[0] = sec(2 * d_a, d_b).astype(BF16)
    p_ref[0] = (sec(2 * d_a + d_b, d_b) * sec(2 * d_a + 2 * d_b, d_b)).astype(BF16)


def _inproj_call(x, g, sh, sc, w_bf, d_a, d_b, latent):
    bsz, n, d = x.shape
    tm = _pick(n, 512, SUBLANES)
    n_w = w_bf.shape[1]
    row = lambda b, i: (b, i, 0)
    if latent:
        out_shape = (_sds((bsz, n, d_a), BF16), _sds((bsz, n, d_a), F32),
                     _sds((bsz, n, d_b), BF16), _sds((bsz, n, d_b), BF16))
        out_specs = (pl.BlockSpec((1, tm, d_a), row), pl.BlockSpec((1, tm, d_a), row),
                     pl.BlockSpec((1, tm, d_b), row), pl.BlockSpec((1, tm, d_b), row))
    else:
        out_shape = (_sds((bsz, n, d_a), F32),)
        out_specs = (pl.BlockSpec((1, tm, d_a), row),)
    return pl.pallas_call(
        functools.partial(_inproj_kernel, d_a=d_a, d_b=d_b, latent=latent),
        out_shape=out_shape,
        grid=(bsz, n // tm),
        in_specs=[pl.BlockSpec((1, tm, d), row),
                  pl.BlockSpec((1, d), lambda b, i: (0, 0)),
                  pl.BlockSpec((1, 1, d), lambda b, i: (b, 0, 0)),
                  pl.BlockSpec((1, 1, d), lambda b, i: (b, 0, 0)),
                  pl.BlockSpec((d, n_w), lambda b, i: (0, 0), pipeline_mode=pl.Buffered(1))],
        out_specs=out_specs,
        compiler_params=_params(("arbitrary", "arbitrary")),
        name="inproj_lat" if latent else "inproj_ctx",
    )(x, g, sh, sc, w_bf)


def _gelu_tanh(x):
    c = 0.7978845608028654
    return x * (0.5 * (1.0 + jnp.tanh(c * (x + 0.044715 * (x * x * x)))))


def _lru_kernel(*refs, reverse, combine, nc, tl, heads, blk):
    if combine:
        (prev_ref, main_ref, next_ref, cw_ref, cb_ref, wri_ref, br_ref, bi_ref, lam_ref, h0_ref,
         hf_ref, ag_ref, out_ref, hlast_ref, ebuf, xc_s, a_s, b_s, carry) = refs
    else:
        (prev_ref, main_ref, next_ref, cw_ref, cb_ref, wri_ref, br_ref, bi_ref, lam_ref, h0_ref,
         out_ref, hlast_ref, ebuf, xc_s, a_s, b_s, carry) = refs
    da = heads * blk
    c = pl.program_id(1)
    cidx = (nc - 1 - c) if reverse else c

    @pl.when(c == 0)
    def _():
        carry[...] = jnp.broadcast_to(h0_ref[0], carry.shape)

    main = main_ref[0]
    zero8 = jnp.zeros((SUBLANES, da), F32)
    ebuf[0:SUBLANES, :] = jnp.where(cidx == 0, zero8, prev_ref[0])
    ebuf[SUBLANES:SUBLANES + tl, :] = main
    ebuf[SUBLANES + tl:2 * SUBLANES + tl, :] = jnp.where(cidx == nc - 1, zero8, next_ref[0])
    cw = cw_ref[...]
    off = SUBLANES - CONV_A_LEFT
    xc_s[...] = (cw[0:1] * ebuf[off:off + tl, :] + cw[1:2] * ebuf[off + 1:off + 1 + tl, :]
                 + cw[2:3] * main + cw[3:4] * ebuf[off + 3:off + 3 + tl, :] + cb_ref[...])

    z = -lam_ref[...]
    sp = jnp.maximum(z, 0.0) + jnp.log1p(jnp.exp(-jnp.abs(z)))
    rc = min(tl, 128)
    for r0 in range(0, tl, rc):
        for h in range(heads):
            cs = slice(h * blk, (h + 1) * blk)
            xh = xc_s[r0:r0 + rc, cs]
            zz = jnp.dot(xh.astype(BF16), wri_ref[h], preferred_element_type=F32)
            r = _sigmoid(zz[:, :blk] + br_ref[:, cs])
            i = _sigmoid(zz[:, blk:] + bi_ref[:, cs])
            log_a = (-LRU_C * r) * sp[:, cs]
            a = jnp.exp(log_a)
            a_s[r0:r0 + rc, cs] = a
            b_s[r0:r0 + rc, cs] = jnp.sqrt(jnp.tanh(-log_a) * (1.0 + a * a)) * (i * xh)

    row = lax.broadcasted_iota(I32, (SUBLANES, da), 0)
    ng = tl // SUBLANES
    shifts = (1, 2, 4)

    def body(gi, hc):
        g = (ng - 1 - gi) if reverse else gi
        r0 = pl.multiple_of(g * SUBLANES, SUBLANES)
        a = a_s[pl.ds(r0, SUBLANES), :]
        b = b_s[pl.ds(r0, SUBLANES), :]
        for s in shifts:
            if reverse:
                keep = row < (SUBLANES - s)
                sh = SUBLANES - s
            else:
                keep = row >= s
                sh = s
            a_sh = jnp.where(keep, pltpu.roll(a, sh, 0), 1.0)
            b_sh = jnp.where(keep, pltpu.roll(b, sh, 0), 0.0)
            b = a * b_sh + b
            a = a * a_sh
        hrows = a * hc + b
        b_s[pl.ds(r0, SUBLANES), :] = hrows
        edge = hrows[0:1, :] if reverse else hrows[SUBLANES - 1:SUBLANES, :]
        return jnp.broadcast_to(edge, (SUBLANES, da))

    hc = lax.fori_loop(0, ng, body, carry[...])
    carry[...] = hc
    hlast_ref[0] = hc[0:1, :]
    if combine:
        hsum = hf_ref[0] + b_s[...]
        out_ref[0] = (_gelu_tanh(ag_ref[0].astype(F32)) * hsum).astype(out_ref.dtype)
    else:
        out_ref[0] = b_s[...]


def _lru_call(ax, h0, cw, cb, wri, br, bi, lam, *, reverse, hf=None, ag=None):
    bsz, n, da = ax.shape
    heads, blk, _ = wri.shape
    tl = _pick(n, 512, SUBLANES)
    nc = n // tl
    nb8 = n // SUBLANES
    g8 = tl // SUBLANES
    combine = hf is not None

    def cidx(c):
        return (nc - 1 - c) if reverse else c

    main_map = lambda b, c: (b, cidx(c), 0)
    prev_map = lambda b, c: (b, jnp.maximum(cidx(c) * g8 - 1, 0), 0)
    next_map = lambda b, c: (b, jnp.minimum((cidx(c) + 1) * g8, nb8 - 1), 0)
    const2 = lambda b, c: (0, 0)
    in_specs = [pl.BlockSpec((1, SUBLANES, da), prev_map),
                pl.BlockSpec((1, tl, da), main_map),
                pl.BlockSpec((1, SUBLANES, da), next_map),
                pl.BlockSpec((4, da), const2),
                pl.BlockSpec((1, da), const2),
                pl.BlockSpec((heads, blk, 2 * blk), lambda b, c: (0, 0, 0)),
                pl.BlockSpec((1, da), const2),
                pl.BlockSpec((1, da), const2),
                pl.BlockSpec((1, da), const2),
                pl.BlockSpec((1, 1, da), lambda b, c: (b, 0, 0))]
    args = [ax, ax, ax, cw, cb, wri, br, bi, lam, h0]
    if combine:
        in_specs += [pl.BlockSpec((1, tl, da), main_map), pl.BlockSpec((1, tl, da), main_map)]
        args += [hf, ag]
    out_dtype = BF16 if combine else F32
    return pl.pallas_call(
        functools.partial(_lru_kernel, reverse=reverse, combine=combine, nc=nc, tl=tl,
                          heads=heads, blk=blk),
        out_shape=(_sds((bsz, n, da), out_dtype), _sds((bsz, 1, da), F32)),
        grid=(bsz, nc),
        in_specs=in_specs,
        out_specs=(pl.BlockSpec((1, tl, da), main_map),
                   pl.BlockSpec((1, 1, da), lambda b, c: (b, 0, 0))),
        scratch_shapes=[pltpu.VMEM((tl + 2 * SUBLANES, da), F32),
                        pltpu.VMEM((tl, da), F32),
                        pltpu.VMEM((tl, da), F32),
                        pltpu.VMEM((tl, da), F32),
                        pltpu.VMEM((SUBLANES, da), F32)],
        compiler_params=_params(("arbitrary", "arbitrary")),
        name=("lru_bwd" if reverse else "lru_fwd") + ("_mix" if combine else ""),
    )(*args)


def _mixout_kernel(ya_ref, bb_ref, p_ref, pu_ref, pd_ref, cbw_ref, ga_ref, gb_ref, wo_ref, x_ref,
                   gt_ref, gf_ref, sh_ref, sc_ref, wr_ref, brt_ref,
                   x1_ref, xp_ref, te_ref, tw_ref, *, tm, d_a, d_b, n_tiles):
    i = pl.program_id(1)
    half = d_b // 2
    z = p_ref[0].astype(F32)
    w = cbw_ref[...]
    zh = z[:, :half]
    col = lax.broadcasted_iota(I32, (tm, half), 0) % GRID_W
    left = jnp.where(col >= 1, pltpu.roll(zh, 1, 0), 0.0)
    right = jnp.where(col <= GRID_W - 2, pltpu.roll(zh, tm - 1, 0), 0.0)
    horiz = w[0:1, :half] * left + w[1:2, :half] * zh + w[2:3, :half] * right
    zv = z[:, half:]
    up_halo = jnp.where(i == 0, 0.0, pu_ref[0].astype(F32))
    dn_halo = jnp.where(i == n_tiles - 1, 0.0, pd_ref[0].astype(F32))
    if tm > GRID_W:
        up = jnp.concatenate([up_halo, zv[:tm - GRID_W]], axis=0)
        dn = jnp.concatenate([zv[GRID_W:], dn_halo], axis=0)
    else:
        up, dn = up_halo, dn_halo
    vert = w[0:1, half:] * up + w[1:2, half:] * zv + w[2:3, half:] * dn
    bb = bb_ref[0].astype(F32)
    yb = jnp.concatenate([bb[:, :half] * horiz, bb[:, half:] * vert], axis=1)
    ya = ya_ref[0].astype(F32)
    ya_n = _rms(ya, ga_ref[...]).astype(BF16)
    yb_n = _rms(yb, gb_ref[...]).astype(BF16)
    mix = (jnp.dot(ya_n, wo_ref[0:d_a, :], preferred_element_type=F32)
           + jnp.dot(yb_n, wo_ref[d_a:d_a + d_b, :], preferred_element_type=F32))
    x1 = x_ref[0] + gt_ref[0] * mix
    x1_ref[0] = x1
    xn = _rms(x1, gf_ref[...]) * (1.0 + sc_ref[0]) + sh_ref[0]
    xb = xn.astype(BF16)
    _store_token_tiles(xp_ref.at[0], _pack_pairs(xn), tm)
    logits = jnp.dot(xb, wr_ref[...], preferred_element_type=F32) + brt_ref[...]
    lane = lax.broadcasted_iota(I32, logits.shape, 1)
    lane_f = lane.astype(F32)
    vals = logits
    tv, te = [], []
    for _ in range(TOP_K):
        m = jnp.max(vals, axis=-1, keepdims=True)
        idx = jnp.min(jnp.where(vals == m, lane_f, float(LANES)), axis=-1, keepdims=True)
        tv.append(m)
        te.append(idx)
        vals = jnp.where(lane_f == idx, -jnp.inf, vals)
    ex = [jnp.exp(v - tv[0]) for v in tv]
    den = ex[0]
    for e in ex[1:]:
        den = den + e
    e_out = jnp.zeros(logits.shape, I32)
    w_out = jnp.zeros(logits.shape, F32)
    for k in range(TOP_K):
        e_out = jnp.where(lane == k, te[k].astype(I32), e_out)
        w_out = jnp.where(lane == k, ex[k] / den, w_out)
    te_ref[0] = e_out
    tw_ref[0] = w_out


def _mixout_call(ya, bb, p, cbw, ga, gb, wo_bf, x, gt1, gf, sh2, sc2, wr_bf, brt):
    bsz, n, d = x.shape
    d_a = ya.shape[-1]
    d_b = bb.shape[-1]
    half = d_b // 2
    tm = _pick(n, 512, GRID_W)
    n_tiles = n // tm
    nseg = (d // 2) // LANES
    rpt = tm // GRID_W
    n_rows = n // GRID_W
    row = lambda b, i: (b, i, 0)
    vec = lambda b, i: (b, 0, 0)
    const2 = lambda b, i: (0, 0)
    return pl.pallas_call(
        functools.partial(_mixout_kernel, tm=tm, d_a=d_a, d_b=d_b, n_tiles=n_tiles),
        out_shape=(_sds((bsz, n, d), F32), _sds((bsz, n * nseg, LANES), U32),
                   _sds((bsz, n, LANES), I32), _sds((bsz, n, LANES), F32)),
        grid=(bsz, n_tiles),
        in_specs=[pl.BlockSpec((1, tm, d_a), row),
                  pl.BlockSpec((1, tm, d_b), row),
                  pl.BlockSpec((1, tm, d_b), row),
                  pl.BlockSpec((1, GRID_W, half), lambda b, i: (b, jnp.maximum(i * rpt - 1, 0), 1)),
                  pl.BlockSpec((1, GRID_W, half), lambda b, i: (b, jnp.minimum((i + 1) * rpt, n_rows - 1), 1)),
                  pl.BlockSpec((3, d_b), const2),
                  pl.BlockSpec((1, d_a), const2),
                  pl.BlockSpec((1, d_b), const2),
                  pl.BlockSpec((d_a + d_b, d), const2),
                  pl.BlockSpec((1, tm, d), row),
                  pl.BlockSpec((1, 1, d), vec),
                  pl.BlockSpec((1, d), const2),
                  pl.BlockSpec((1, 1, d), vec),
                  pl.BlockSpec((1, 1, d), vec),
                  pl.BlockSpec((d, LANES), const2),
                  pl.BlockSpec((1, LANES), const2)],
        out_specs=(pl.BlockSpec((1, tm, d), row), pl.BlockSpec((1, tm * nseg, LANES), row),
                   pl.BlockSpec((1, tm, LANES), row), pl.BlockSpec((1, tm, LANES), row)),
        compiler_params=_params(("arbitrary", "arbitrary")),
        name="mixout",
    )(ya, bb, p, p, p, cbw, ga, gb, wo_bf, x, gt1, gf, sh2, sc2, wr_bf, brt)


def _rank_kernel(e_ref, rank_ref, cnt_ref, carry, *, tt):
    @pl.when(pl.program_id(0) == 0)
    def _():
        carry[...] = jnp.zeros(carry.shape, F32)

    e = e_ref[...]
    lane = lax.broadcasted_iota(I32, (tt, LANES), 1)
    ohs = []
    m = jnp.zeros((tt, LANES), F32)
    for k in range(TOP_K):
        oh = lane == e[:, k:k + 1]
        ohs.append(oh)
        m = m + jnp.where(oh, 1.0, 0.0)
    ri = lax.broadcasted_iota(I32, (tt, tt), 0)
    ci = lax.broadcasted_iota(I32, (tt, tt), 1)
    ltri = jnp.where(ri > ci, 1.0, 0.0).astype(BF16)
    pref = jnp.dot(ltri, m.astype(BF16), preferred_element_type=F32) + carry[0:1, :]
    out = jnp.zeros((tt, LANES), I32)
    for k in range(TOP_K):
        rk = jnp.sum(jnp.where(ohs[k], pref, 0.0), axis=-1, keepdims=True)
        out = jnp.where(lane == k, rk.astype(I32), out)
    rank_ref[...] = out
    tot = carry[0:1, :] + jnp.sum(m, axis=0, keepdims=True)
    carry[...] = jnp.broadcast_to(tot, carry.shape)
    cnt_ref[...] = jnp.broadcast_to(tot, cnt_ref.shape)


def _rank_call(top_e):
    t = top_e.shape[0]
    tt = _pick(t, 512, SUBLANES)
    return pl.pallas_call(
        functools.partial(_rank_kernel, tt=tt),
        out_shape=(_sds((t, LANES), I32), _sds((SUBLANES, LANES), F32)),
        grid=(t // tt,),
        in_specs=[pl.BlockSpec((tt, LANES), lambda i: (i, 0))],
        out_specs=(pl.BlockSpec((tt, LANES), lambda i: (i, 0)),
                   pl.BlockSpec((SUBLANES, LANES), lambda i: (0, 0))),
        scratch_shapes=[pltpu.VMEM((SUBLANES, LANES), F32)],
        compiler_params=_params(("arbitrary",)),
        name="rank",
    )(top_e)


def _zero_tail(first, n_blocks, zbuf, dst_block, sem):
    zbuf[...] = jnp.zeros(zbuf.shape, zbuf.dtype)

    def start(c, carry):
        pltpu.make_async_copy(zbuf, dst_block(c), sem).start()
        return carry

    def wait(c, carry):
        pltpu.make_async_copy(zbuf, dst_block(c), sem).wait()
        return carry

    lax.fori_loop(first, n_blocks, start, 0)
    lax.fori_loop(first, n_blocks, wait, 0)


def _dispatch_kernel(bs_ref, nb_ref, pos_ref, x_ref, o_hbm, zbuf, sem, zsem, *, td, rb, n_e, n_blocks, nseg):
    rbr = rb * nseg

    def blk(b):
        return pl.ds(pl.multiple_of(b * rbr, rbr), rbr)

    def tok(t):
        return pl.ds(pl.multiple_of(t * nseg, nseg), nseg)

    @pl.when(pl.program_id(0) == 0)
    def _():
        _zero_tail(bs_ref[n_e - 1] + nb_ref[n_e - 1], n_blocks, zbuf, lambda c: o_hbm.at[blk(c)], zsem)

        def zero_copy(e):
            return pltpu.make_async_copy(zbuf, o_hbm.at[blk(bs_ref[e] + nb_ref[e] - 1)], zsem)

        for e in range(n_e):
            @pl.when(nb_ref[e] > 0)
            def _():
                zero_copy(e).start()

        for e in range(n_e):
            @pl.when(nb_ref[e] > 0)
            def _():
                zero_copy(e).wait()

    def body(t, c):
        src = x_ref.at[tok(t)]
        for k in range(TOP_K):
            pltpu.make_async_copy(src, o_hbm.at[tok(pos_ref[0, 0, t * TOP_K + k])], sem).start(priority=k % 2)
        return c

    lax.fori_loop(0, td, body, 0, unroll=2)
    for _ in range(TOP_K):
        pltpu.make_async_copy(x_ref, o_hbm.at[pl.ds(0, td * nseg)], sem).wait()


def _dispatch_call(groups, xp, pos, n_buf, rb, nseg):
    bstart, nblk = groups
    n_e = bstart.shape[0]
    t = xp.shape[0] // nseg
    td = _pick(t, 512, SUBLANES)
    pos3 = pos.reshape(t // td, 1, td * TOP_K)
    return pl.pallas_call(
        functools.partial(_dispatch_kernel, td=td, rb=rb, n_e=n_e, n_blocks=n_buf // rb, nseg=nseg),
        out_shape=_sds((n_buf * nseg, LANES), U32),
        grid_spec=pltpu.PrefetchScalarGridSpec(
            num_scalar_prefetch=2,
            grid=(t // td,),
            in_specs=[pl.BlockSpec((1, 1, td * TOP_K), lambda i, bs, nb: (i, 0, 0),
                                   memory_space=pltpu.SMEM),
                      pl.BlockSpec((td * nseg, LANES), lambda i, bs, nb: (i, 0))],
            out_specs=pl.BlockSpec(memory_space=pl.ANY),
            scratch_shapes=[pltpu.VMEM((rb * nseg, LANES), U32), pltpu.SemaphoreType.DMA,
                            pltpu.SemaphoreType.DMA]),
        compiler_params=_params(("arbitrary",)),
        name="dispatch",
    )(bstart, nblk, pos3, xp)


GROUP_IN_SLOTS = 4
GROUP_OUT_SLOTS = 3


def _build_chunks(bs_ref, nb_ref, ce, cs, cb, n_split):
    def per_expert(e, j):
        nb = nb_ref[e]
        b0 = bs_ref[e]

        def per_slab(s_, j):
            def per_block(c, j):
                ce[j] = e
                cs[j] = s_
                cb[j] = b0 + c
                return j + 1

            return lax.fori_loop(0, nb, per_block, j)

        return lax.fori_loop(0, n_split, per_slab, j)

    return lax.fori_loop(0, bs_ref.shape[0], per_expert, jnp.int32(0))


def _chunk_loop(n, nb_ref, ce, cs, in_copy, out_copy, w_copies, load_weights, compute):
    ahead = GROUP_IN_SLOTS - 1

    @pl.when(n > 0)
    def _():
        for cp in w_copies(ce[0], cs[0]):
            cp.start()
        for j in range(ahead):
            @pl.when(j < n)
            def _():
                in_copy(j, j).start()

        def body(j, carry):
            jp = jnp.maximum(j - 1, 0)
            first = jnp.logical_or(j == 0, jnp.logical_or(ce[j] != ce[jp], cs[j] != cs[jp]))

            @pl.when(first)
            def _():
                for cp in w_copies(ce[j], cs[j]):
                    cp.wait()
                load_weights()
                jn = j + nb_ref[ce[j]]

                @pl.when(jn < n)
                def _():
                    for cp in w_copies(ce[jn], cs[jn]):
                        cp.start()

            islot = lax.rem(j, GROUP_IN_SLOTS)
            oslot = lax.rem(j, GROUP_OUT_SLOTS)

            @pl.when(j + ahead < n)
            def _():
                in_copy(j + ahead, lax.rem(j + ahead, GROUP_IN_SLOTS)).start()

            in_copy(j, islot).wait()

            @pl.when(j >= GROUP_OUT_SLOTS)
            def _():
                out_copy(j - GROUP_OUT_SLOTS, oslot).wait()

            compute(j, islot, oslot)
            out_copy(j, oslot).start()
            return carry

        lax.fori_loop(0, n, body, 0)

        for k in range(GROUP_OUT_SLOTS):
            @pl.when(n > k)
            def _():
                out_copy(n - 1 - k, lax.rem(n - 1 - k, GROUP_OUT_SLOTS)).wait()


def _gmm1_kernel(bs_ref, nb_ref, xs_hbm, wg_hbm, wu_hbm, bg_ref, bu_ref, act_hbm,
                 xbuf, obuf, wraw, wbf, zbuf, ce, cs, cb, sin, sout, wsem, zsem,
                 *, rb, tf, n_blocks, n_split, nseg):
    def blk(b):
        return pl.ds(pl.multiple_of(b * rb, rb), rb)

    def tok_blk(b):
        return pl.ds(pl.multiple_of(b * (rb * nseg), rb * nseg), rb * nseg)

    def cols(s_):
        return pl.ds(pl.multiple_of(s_ * tf, tf), tf)

    n_e = bs_ref.shape[0]
    for s_ in range(n_split):
        _zero_tail(bs_ref[n_e - 1] + nb_ref[n_e - 1], n_blocks, zbuf,
                   lambda c: act_hbm.at[blk(c), cols(s_)], zsem)
    n = _build_chunks(bs_ref, nb_ref, ce, cs, cb, n_split)

    def in_copy(j, slot):
        return pltpu.make_async_copy(xs_hbm.at[tok_blk(cb[j])], xbuf.at[slot], sin.at[slot])

    def out_copy(j, slot):
        return pltpu.make_async_copy(obuf.at[slot], act_hbm.at[blk(cb[j]), cols(cs[j])], sout.at[slot])

    def w_copies(e, s_):
        return (pltpu.make_async_copy(wg_hbm.at[e, :, cols(s_)], wraw.at[0], wsem),
                pltpu.make_async_copy(wu_hbm.at[e, :, cols(s_)], wraw.at[1], wsem))

    def load_weights():
        wbf[:, 0:tf] = wraw[0].astype(BF16)
        wbf[:, tf:2 * tf] = wraw[1].astype(BF16)

    def compute(j, islot, oslot):
        g = ce[j] * n_split + cs[j]
        lo, hi = _unpack_pairs(_load_token_tiles(xbuf.at[islot], rb, nseg))
        h = jnp.concatenate([lo.astype(BF16), hi.astype(BF16)], axis=1)
        gu = jnp.dot(h, wbf[...], preferred_element_type=F32)
        gate = jnp.minimum(gu[:, :tf] + bg_ref[g], SWIGLU_LIMIT)
        up = jnp.clip(gu[:, tf:] + bu_ref[g], -SWIGLU_LIMIT, SWIGLU_LIMIT)
        act = (up + 1.0) * gate * _sigmoid(SWIGLU_ALPHA * gate)
        obuf[oslot] = act.astype(BF16)

    _chunk_loop(n, nb_ref, ce, cs, in_copy, out_copy, w_copies, load_weights, compute)


def _gmm1_call(groups, xs, wg, wu, bg, bu, rb, tf, nseg):
    n_buf = xs.shape[0] // nseg
    n_e, d, dff = wg.shape
    n_split = dff // tf
    max_chunks = n_split * (n_buf // rb)
    vmem_full = lambda shape: pl.BlockSpec(shape, lambda i, *_: (0,) * len(shape))
    return pl.pallas_call(
        functools.partial(_gmm1_kernel, rb=rb, tf=tf, n_blocks=n_buf // rb, n_split=n_split, nseg=nseg),
        out_shape=_sds((n_buf, dff), BF16),
        grid_spec=pltpu.PrefetchScalarGridSpec(
            num_scalar_prefetch=2,
            grid=(1,),
            in_specs=[pl.BlockSpec(memory_space=pl.ANY),
                      pl.BlockSpec(memory_space=pl.ANY),
                      pl.BlockSpec(memory_space=pl.ANY),
                      vmem_full((n_e * n_split, 1, tf)),
                      vmem_full((n_e * n_split, 1, tf))],
            out_specs=pl.BlockSpec(memory_space=pl.ANY),
            scratch_shapes=[pltpu.VMEM((GROUP_IN_SLOTS, rb * nseg, LANES), U32),
                            pltpu.VMEM((GROUP_OUT_SLOTS, rb, tf), BF16),
                            pltpu.VMEM((2, d, tf), F32),
                            pltpu.VMEM((d, 2 * tf), BF16),
                            pltpu.VMEM((rb, tf), BF16),
                            pltpu.SMEM((max_chunks,), I32),
                            pltpu.SMEM((max_chunks,), I32),
                            pltpu.SMEM((max_chunks,), I32),
                            pltpu.SemaphoreType.DMA((GROUP_IN_SLOTS,)),
                            pltpu.SemaphoreType.DMA((GROUP_OUT_SLOTS,)),
                            pltpu.SemaphoreType.DMA,
                            pltpu.SemaphoreType.DMA]),
        compiler_params=_params(("arbitrary",)),
        name="gmm1",
    )(*groups, xs, wg, wu, bg.reshape(n_e * n_split, 1, tf), bu.reshape(n_e * n_split, 1, tf))


def _gmm2_kernel(bs_ref, nb_ref, act_hbm, wd_hbm, bd_ref, y_hbm,
                 abuf, ybuf, wraw, wbf, zbuf, ce, cs, cb, sin, sout, wsem, zsem, *, rb, n_blocks, nseg):
    def blk(b):
        return pl.ds(pl.multiple_of(b * rb, rb), rb)

    def tok_blk(b):
        return pl.ds(pl.multiple_of(b * (rb * nseg), rb * nseg), rb * nseg)

    n_e = bs_ref.shape[0]
    _zero_tail(bs_ref[n_e - 1] + nb_ref[n_e - 1], n_blocks, zbuf, lambda c: y_hbm.at[tok_blk(c)], zsem)
    n = _build_chunks(bs_ref, nb_ref, ce, cs, cb, 1)

    def in_copy(j, slot):
        return pltpu.make_async_copy(act_hbm.at[blk(cb[j])], abuf.at[slot], sin.at[slot])

    def out_copy(j, slot):
        return pltpu.make_async_copy(ybuf.at[slot], y_hbm.at[tok_blk(cb[j])], sout.at[slot])

    def w_copies(e, s_):
        return (pltpu.make_async_copy(wd_hbm.at[e], wraw, wsem),)

    def load_weights():
        wbf[...] = wraw[...].astype(BF16)

    def compute(j, islot, oslot):
        y = jnp.dot(abuf[islot], wbf[...], preferred_element_type=F32) + bd_ref[ce[j]]
        _store_token_tiles(ybuf.at[oslot], _pack_pairs(y), rb)

    _chunk_loop(n, nb_ref, ce, cs, in_copy, out_copy, w_copies, load_weights, compute)


def _gmm2_call(groups, act, wd, bd, rb, nseg):
    n_buf, dff = act.shape
    n_e, _, d = wd.shape
    assert d == 2 * nseg * LANES
    max_chunks = n_buf // rb
    vmem_full = lambda shape: pl.BlockSpec(shape, lambda i, *_: (0,) * len(shape))
    return pl.pallas_call(
        functools.partial(_gmm2_kernel, rb=rb, n_blocks=n_buf // rb, nseg=nseg),
        out_shape=_sds((n_buf * nseg, LANES), U32),
        grid_spec=pltpu.PrefetchScalarGridSpec(
            num_scalar_prefetch=2,
            grid=(1,),
            in_specs=[pl.BlockSpec(memory_space=pl.ANY),
                      pl.BlockSpec(memory_space=pl.ANY),
                      vmem_full((n_e, 1, d))],
            out_specs=pl.BlockSpec(memory_space=pl.ANY),
            scratch_shapes=[pltpu.VMEM((GROUP_IN_SLOTS, rb, dff), BF16),
                            pltpu.VMEM((GROUP_OUT_SLOTS, rb * nseg, LANES), U32),
                            pltpu.VMEM((dff, d), F32),
                            pltpu.VMEM((dff, d), BF16),
                            pltpu.VMEM((rb * nseg, LANES), U32),
                            pltpu.SMEM((max_chunks,), I32),
                            pltpu.SMEM((max_chunks,), I32),
                            pltpu.SMEM((max_chunks,), I32),
                            pltpu.SemaphoreType.DMA((GROUP_IN_SLOTS,)),
                            pltpu.SemaphoreType.DMA((GROUP_OUT_SLOTS,)),
                            pltpu.SemaphoreType.DMA,
                            pltpu.SemaphoreType.DMA]),
        compiler_params=_params(("arbitrary",)),
        name="gmm2",
    )(*groups, act, wd, bd.reshape(n_e, 1, d))


def _final_kernel(pos_ref, posn_ref, y_hbm, tw_ref, x1_ref, gt_ref, gfin_ref, o_ref, ybuf, sem, *,
                  tc, n_steps, nseg):
    step = pl.program_id(0) * pl.num_programs(1) + pl.program_id(1)

    def tok(t):
        return pl.ds(pl.multiple_of(t * nseg, nseg), nseg)

    def issue(p_ref, slot):
        def body(t, c):
            for k in range(TOP_K):
                src = y_hbm.at[tok(p_ref[0, 0, t * TOP_K + k])]
                pltpu.make_async_copy(src, ybuf.at[slot, k, tok(t)], sem.at[slot]).start(priority=k % 2)
            return c

        lax.fori_loop(0, tc, body, 0, unroll=2)

    def run(cur):
        @pl.when(step + 1 < n_steps)
        def _():
            issue(posn_ref, 1 - cur)

        for k in range(TOP_K):
            pltpu.make_async_copy(y_hbm.at[pl.ds(0, tc * nseg)], ybuf.at[cur, k], sem.at[cur]).wait()
        tw = tw_ref[0]
        lo, hi = None, None
        for k in range(TOP_K):
            lo_k, hi_k = _unpack_pairs(_load_token_tiles(ybuf.at[cur, k], tc, nseg))
            wk = tw[:, k:k + 1]
            lo = wk * lo_k if lo is None else lo + wk * lo_k
            hi = wk * hi_k if hi is None else hi + wk * hi_k
        moe = jnp.concatenate([lo, hi], axis=1)
        x2 = x1_ref[0] + gt_ref[0] * moe
        o_ref[0] = _rms(x2, gfin_ref[...])

    @pl.when(step == 0)
    def _():
        issue(pos_ref, 0)

    parity = lax.rem(step, 2)

    @pl.when(parity == 0)
    def _():
        run(0)

    @pl.when(parity == 1)
    def _():
        run(1)


def _final_call(y, pos, top_w, x1, gt2, gfin, nseg):
    bsz, n, d = x1.shape
    tc = _pick(n, 256, SUBLANES)
    nt = n // tc
    n_steps = bsz * nt
    pos3 = pos.reshape(n_steps, 1, tc * TOP_K)
    row = lambda b, i: (b, i, 0)
    return pl.pallas_call(
        functools.partial(_final_kernel, tc=tc, n_steps=n_steps, nseg=nseg),
        out_shape=_sds((bsz, n, d), F32),
        grid=(bsz, nt),
        in_specs=[pl.BlockSpec((1, 1, tc * TOP_K), lambda b, i: (b * nt + i, 0, 0), memory_space=pltpu.SMEM),
                  pl.BlockSpec((1, 1, tc * TOP_K), lambda b, i: (jnp.minimum(b * nt + i + 1, n_steps - 1), 0, 0),
                               memory_space=pltpu.SMEM),
                  pl.BlockSpec(memory_space=pl.ANY),
                  pl.BlockSpec((1, tc, LANES), row),
                  pl.BlockSpec((1, tc, d), row),
                  pl.BlockSpec((1, 1, d), lambda b, i: (b, 0, 0)),
                  pl.BlockSpec((1, d), lambda b, i: (0, 0))],
        out_specs=pl.BlockSpec((1, tc, d), row),
        scratch_shapes=[pltpu.VMEM((2, TOP_K, tc * nseg, LANES), U32), pltpu.SemaphoreType.DMA((2,))],
        compiler_params=_params(("arbitrary", "arbitrary")),
        name="final",
    )(pos3, pos3, y, top_w, x1, gt2, gfin)


def _group_tables(counts, rb):
    nblk = (counts + rb - 1) // rb
    bstart = jnp.cumsum(nblk) - nblk
    return bstart * rb, (bstart.astype(I32), nblk.astype(I32))


def kernel(x, c, ctx, c_ctx, w_mod, b_mod, g_mix, w_in, conv_a_w, conv_a_b, lru_w_r, lru_b_r,
           lru_w_i, lru_b_i, lru_lam, conv_b_w, g_out_a, g_out_b, w_out, g_ffn, w_router,
           b_router, w_gate, b_gate, w_up, b_up, w_down, b_down, g_final):
    assert w_mod.shape[0] == 1, "single-layer block"
    bsz, n_lat, d = x.shape
    d_a = conv_a_w.shape[-1]
    d_b = conv_b_w.shape[-1]
    n_e = w_router.shape[-1]
    assert n_lat % GRID_W == 0 and n_e <= LANES and d % (2 * LANES) == 0
    l = 0

    cs = jnp.zeros((SUBLANES, d), F32).at[:bsz].set(c).at[bsz].set(c_ctx)
    mod = _mod_call(cs, w_mod[l], b_mod[l][None])
    sh1, sc1, gt1, sh2, sc2, gt2 = [m[:bsz, None, :] for m in jnp.split(mod, 6, axis=-1)]
    ssh1, ssc1 = [jnp.broadcast_to(m[bsz][None, None, :], (bsz, 1, d))
                  for m in jnp.split(mod, 6, axis=-1)[:2]]

    w_in_bf = w_in[l].astype(BF16)
    g_mix2 = g_mix[l][None]
    cw = conv_a_w[l]
    cb = conv_a_b[l][None]
    wri = jnp.concatenate([lru_w_r[l], lru_w_i[l]], axis=-1).astype(BF16)
    br, bi, lam = lru_b_r[l], lru_b_i[l], lru_lam[l]
    lru_p = lambda dr: (cw, cb, wri[dr], br[dr][None], bi[dr][None], lam[dr][None])

    (s_ax,) = _inproj_call(ctx, g_mix2, ssh1, ssc1, w_in_bf[:, d_a:2 * d_a], d_a, d_b, latent=False)
    zero_state = jnp.zeros((bsz, 1, d_a), F32)
    _, h0f = _lru_call(s_ax, zero_state, *lru_p(0), reverse=False)
    _, h0b = _lru_call(s_ax, zero_state, *lru_p(1), reverse=True)

    ag, ax, bb, p = _inproj_call(x, g_mix2, sh1, sc1, w_in_bf, d_a, d_b, latent=True)
    hf, _ = _lru_call(ax, h0f, *lru_p(0), reverse=False)
    ya, _ = _lru_call(ax, h0b, *lru_p(1), reverse=True, hf=hf, ag=ag)

    wr_bf = jnp.zeros((d, LANES), BF16).at[:, :n_e].set(w_router[l].astype(BF16))
    brt = jnp.full((1, LANES), NEG_BIG, F32).at[0, :n_e].set(b_router[l])
    x1, xp, top_e, top_w = _mixout_call(
        ya, bb, p, conv_b_w[l], g_out_a[l][None], g_out_b[l][None], w_out[l].astype(BF16), x, gt1,
        g_ffn[l][None], sh2, sc2, wr_bf, brt)

    n_tok = bsz * n_lat
    n_rows = n_tok * TOP_K
    rb = MOE_ROW_BLOCK
    n_buf = n_rows + n_e * rb
    rank, cnt = _rank_call(top_e.reshape(n_tok, LANES))
    counts = cnt[0, :n_e].astype(I32)
    starts, groups = _group_tables(counts, rb)
    te = top_e.reshape(n_tok, LANES)[:, :TOP_K]
    onehot = te[:, :, None] == jnp.arange(n_e, dtype=I32)[None, None, :]
    pos = (jnp.sum(jnp.where(onehot, starts[None, None, :], 0), axis=-1) + rank[:, :TOP_K]).astype(I32)
    nseg = (d // 2) // LANES
    xs = _dispatch_call(groups, xp.reshape(n_tok * nseg, LANES), pos, n_buf, rb, nseg)
    d_ff = w_gate.shape[-1]
    tf = _pick(d_ff, 1024)
    act = _gmm1_call(groups, xs, w_gate[l], w_up[l], b_gate[l], b_up[l], rb, tf, nseg)
    y = _gmm2_call(groups, act, w_down[l], b_down[l], rb, nseg)
    return _final_call(y, pos, top_w, x1, gt2, g_final[None], nseg)
```

```python
import functools

import jax
import jax.numpy as jnp
from jax import lax
from jax.experimental import pallas as pl
from jax.experimental.pallas import tpu as pltpu

F32 = jnp.float32
BF16 = jnp.bfloat16
I32 = jnp.int32
U32 = jnp.uint32

GRID_W = 64
TOP_K = 4
LRU_C = 8.0
CONV_A_LEFT = 2
SCAN_SEG = 4
SWIGLU_LIMIT = 7.0
SWIGLU_ALPHA = 1.702
EPS = 1e-6
MOE_ROW_BLOCK = 256

LANES = 128
SUBLANES = 8
VMEM_LIMIT_BYTES = 56 * 1024 * 1024
NEG_BIG = -1e30


def _sds(shape, dtype):
    return jax.ShapeDtypeStruct(shape, dtype)


def _pick(n, pref, mult=LANES):
    if n <= pref:
        return n
    t = (pref // mult) * mult
    while t >= mult:
        if n % t == 0:
            return t
        t -= mult
    return n


def _params(sem):
    return pltpu.CompilerParams(dimension_semantics=sem, vmem_limit_bytes=VMEM_LIMIT_BYTES)


def _sigmoid(x):
    return 1.0 / (1.0 + jnp.exp(-x))


def _rms(x, g):
    ms = jnp.mean(x * x, axis=-1, keepdims=True)
    return (x * lax.rsqrt(ms + EPS)) * g


def _pack_pairs(x):
    w = x.shape[1] // 2
    lo = lax.bitcast_convert_type(x[:, :w].astype(BF16).astype(F32), U32)
    hi = lax.bitcast_convert_type(x[:, w:].astype(BF16).astype(F32), U32)
    return lax.shift_right_logical(lo, jnp.uint32(16)) | (hi & jnp.uint32(0xFFFF0000))


def _unpack_pairs(words):
    lo = lax.bitcast_convert_type(lax.shift_left(words, jnp.uint32(16)), F32)
    hi = lax.bitcast_convert_type(words & jnp.uint32(0xFFFF0000), F32)
    return lo, hi


def _store_token_tiles(ref, words, rows):
    nseg = words.shape[1] // LANES
    if nseg == 1:
        ref[...] = words
        return
    for s in range(nseg):
        ref[pl.ds(s, rows, stride=nseg), :] = words[:, s * LANES:(s + 1) * LANES]


def _load_token_tiles(ref, rows, nseg):
    if nseg == 1:
        return ref[...]
    return jnp.concatenate([ref[pl.ds(s, rows, stride=nseg), :] for s in range(nseg)], axis=1)


def _mod_kernel(c_ref, w_ref, b_ref, o_ref):
    c = c_ref[...]
    s = c * _sigmoid(c)
    o_ref[...] = jnp.dot(s.astype(BF16), w_ref[...].astype(BF16),
                         preferred_element_type=F32) + b_ref[...]


def _mod_call(cs, w, b):
    d, n6 = w.shape
    tn = _pick(n6, 1024)
    return pl.pallas_call(
        _mod_kernel,
        out_shape=_sds((cs.shape[0], n6), F32),
        grid=(n6 // tn,),
        in_specs=[pl.BlockSpec((cs.shape[0], d), lambda j: (0, 0)),
                  pl.BlockSpec((d, tn), lambda j: (0, j)),
                  pl.BlockSpec((1, tn), lambda j: (0, j))],
        out_specs=pl.BlockSpec((cs.shape[0], tn), lambda j: (0, j)),
        compiler_params=_params(("arbitrary",)),
        name="mod",
    )(cs, w, b)


def _inproj_kernel(x_ref, g_ref, sh_ref, sc_ref, w_ref, *out_refs, d_a, d_b, latent):
    x = x_ref[0]
    xn = _rms(x, g_ref[...]) * (1.0 + sc_ref[0]) + sh_ref[0]
    xb = xn.astype(BF16)

    def sec(lo, width):
        return jnp.dot(xb, w_ref[:, lo:lo + width], preferred_element_type=F32)

    if not latent:
        out_refs[0][0] = sec(0, d_a)
        return
    ag_ref, ax_ref, bb_ref, p_ref = out_refs
    ag_ref[0] = sec(0, d_a).astype(BF16)
    ax_ref[0] = sec(d_a, d_a)
    bb_ref[0] = sec(2 * d_a, d_b).astype(BF16)
    p_ref[0] = (sec(2 * d_a + d_b, d_b) * sec(2 * d_a + 2 * d_b, d_b)).astype(BF16)


def _inproj_call(x, g, sh, sc, w_bf, d_a, d_b, latent):
    bsz, n, d = x.shape
    tm = _pick(n, 512, SUBLANES)
    n_w = w_bf.shape[1]
    row = lambda b, i: (b, i, 0)
    if latent:
        out_shape = (_sds((bsz, n, d_a), BF16), _sds((bsz, n, d_a), F32),
                     _sds((bsz, n, d_b), BF16), _sds((bsz, n, d_b), BF16))
        out_specs = (pl.BlockSpec((1, tm, d_a), row), pl.BlockSpec((1, tm, d_a), row),
                     pl.BlockSpec((1, tm, d_b), row), pl.BlockSpec((1, tm, d_b), row))
    else:
        out_shape = (_sds((bsz, n, d_a), F32),)
        out_specs = (pl.BlockSpec((1, tm, d_a), row),)
    return pl.pallas_call(
        functools.partial(_inproj_kernel, d_a=d_a, d_b=d_b, latent=latent),
        out_shape=out_shape,
        grid=(bsz, n // tm),
        in_specs=[pl.BlockSpec((1, tm, d), row),
                  pl.BlockSpec((1, d), lambda b, i: (0, 0)),
                  pl.BlockSpec((1, 1, d), lambda b, i: (b, 0, 0)),
                  pl.BlockSpec((1, 1, d), lambda b, i: (b, 0, 0)),
                  pl.BlockSpec((d, n_w), lambda b, i: (0, 0), pipeline_mode=pl.Buffered(1))],
        out_specs=out_specs,
        compiler_params=_params(("arbitrary", "arbitrary")),
        name="inproj_lat" if latent else "inproj_ctx",
    )(x, g, sh, sc, w_bf)


def _gelu_tanh(x):
    c = 0.7978845608028654
    return x * (0.5 * (1.0 + jnp.tanh(c * (x + 0.044715 * (x * x * x)))))


def _lru_kernel(*refs, reverse, combine, nc, tl, heads, blk):
    if combine:
        (prev_ref, main_ref, next_ref, cw_ref, cb_ref, wri_ref, br_ref, bi_ref, lam_ref, h0_ref,
         hf_ref, ag_ref, out_ref, hlast_ref, ebuf, xc_s, a_s, b_s, carry) = refs
    else:
        (prev_ref, main_ref, next_ref, cw_ref, cb_ref, wri_ref, br_ref, bi_ref, lam_ref, h0_ref,
         out_ref, hlast_ref, ebuf, xc_s, a_s, b_s, carry) = refs
    da = heads * blk
    c = pl.program_id(1)
    cidx = (nc - 1 - c) if reverse else c

    @pl.when(c == 0)
    def _():
        carry[...] = jnp.broadcast_to(h0_ref[0], carry.shape)

    main = main_ref[0]
    zero8 = jnp.zeros((SUBLANES, da), F32)
    ebuf[0:SUBLANES, :] = jnp.where(cidx == 0, zero8, prev_ref[0])
    ebuf[SUBLANES:SUBLANES + tl, :] = main
    ebuf[SUBLANES + tl:2 * SUBLANES + tl, :] = jnp.where(cidx == nc - 1, zero8, next_ref[0])
    cw = cw_ref[...]
    off = SUBLANES - CONV_A_LEFT
    xc_s[...] = (cw[0:1] * ebuf[off:off + tl, :] + cw[1:2] * ebuf[off + 1:off + 1 + tl, :]
                 + cw[2:3] * main + cw[3:4] * ebuf[off + 3:off + 3 + tl, :] + cb_ref[...])

    z = -lam_ref[...]
    sp = jnp.maximum(z, 0.0) + jnp.log1p(jnp.exp(-jnp.abs(z)))
    spb = blk // LANES
    rc = min(tl, 128)
    for r0 in range(0, tl, rc):
        for h in range(heads):
            cs = slice(h * blk, (h + 1) * blk)
            xh = xc_s[r0:r0 + rc, cs]
            zz = jnp.dot(xh.astype(BF16), wri_ref[h], preferred_element_type=F32)
            r = _sigmoid(zz[:, :blk] + br_ref[:, cs])
            i = _sigmoid(zz[:, blk:] + bi_ref[:, cs])
            log_a = (-LRU_C * r) * sp[:, cs]
            a = jnp.exp(log_a)
            v = jnp.tanh(-log_a) * (1.0 + a * a)
            b = jnp.where(v > 0.0, v * lax.rsqrt(v), 0.0) * (i * xh)
            for q in range(spb):
                a_s[h * spb + q, r0:r0 + rc, :] = a[:, q * LANES:(q + 1) * LANES]
                b_s[h * spb + q, r0:r0 + rc, :] = b[:, q * LANES:(q + 1) * LANES]

    n_slab = da // LANES
    sub_rows = SUBLANES * SCAN_SEG
    n_sub = tl // sub_rows
    row = lax.broadcasted_iota(I32, (SUBLANES, LANES), 0)
    ks = list(range(SCAN_SEG - 1, -1, -1)) if reverse else list(range(SCAN_SEG))

    def seg_scan(at, bt):
        for s in (1, 2, 4):
            if reverse:
                keep = row < (SUBLANES - s)
                sh = SUBLANES - s
            else:
                keep = row >= s
                sh = s
            a_sh = jnp.where(keep, pltpu.roll(at, sh, 0), 1.0)
            b_sh = jnp.where(keep, pltpu.roll(bt, sh, 0), 0.0)
            bt = at * b_sh + bt
            at = at * a_sh
        return at, bt

    def body(si, hcs):
        sub = (n_sub - 1 - si) if reverse else si
        base = sub * sub_rows
        out = []
        for slab in range(n_slab):
            hc = hcs[slab]
            rows = [pl.ds(base + k, SUBLANES, stride=SCAN_SEG) for k in range(SCAN_SEG)]
            acc_a, acc_b = {}, {}
            a_run = b_run = None
            for k in ks:
                ak = a_s[slab, rows[k], :]
                bk = b_s[slab, rows[k], :]
                if a_run is None:
                    a_run, b_run = ak, bk
                else:
                    b_run = ak * b_run + bk
                    a_run = ak * a_run
                acc_a[k], acc_b[k] = a_run, b_run
            at, bt = seg_scan(a_run, b_run)
            h_out = at * hc + bt
            if reverse:
                h_in = jnp.where(row < SUBLANES - 1, pltpu.roll(h_out, SUBLANES - 1, 0), hc)
                edge = h_out[0:1, :]
            else:
                h_in = jnp.where(row >= 1, pltpu.roll(h_out, 1, 0), hc)
                edge = h_out[SUBLANES - 1:SUBLANES, :]
            for k in ks:
                b_s[slab, rows[k], :] = acc_a[k] * h_in + acc_b[k]
            out.append(jnp.broadcast_to(edge, (SUBLANES, LANES)))
        return tuple(out)

    hcs = tuple(carry[:, slab * LANES:(slab + 1) * LANES] for slab in range(n_slab))
    hcs = lax.fori_loop(0, n_sub, body, hcs)
    for slab in range(n_slab):
        ls = slice(slab * LANES, (slab + 1) * LANES)
        carry[:, ls] = hcs[slab]
        hlast_ref[0, :, ls] = hcs[slab][0:1, :]
        if combine:
            hsum = hf_ref[0, :, ls] + b_s[slab]
            out_ref[0, :, ls] = (_gelu_tanh(ag_ref[0, :, ls].astype(F32)) * hsum).astype(out_ref.dtype)
        else:
            out_ref[0, :, ls] = b_s[slab]


def _lru_call(ax, h0, cw, cb, wri, br, bi, lam, *, reverse, hf=None, ag=None):
    bsz, n, da = ax.shape
    heads, blk, _ = wri.shape
    tl = _pick(n, 512, SUBLANES * SCAN_SEG)
    assert blk % LANES == 0 and tl % (SUBLANES * SCAN_SEG) == 0
    nc = n // tl
    nb8 = n // SUBLANES
    g8 = tl // SUBLANES
    combine = hf is not None

    def cidx(c):
        return (nc - 1 - c) if reverse else c

    main_map = lambda b, c: (b, cidx(c), 0)
    prev_map = lambda b, c: (b, jnp.maximum(cidx(c) * g8 - 1, 0), 0)
    next_map = lambda b, c: (b, jnp.minimum((cidx(c) + 1) * g8, nb8 - 1), 0)
    const2 = lambda b, c: (0, 0)
    in_specs = [pl.BlockSpec((1, SUBLANES, da), prev_map),
                pl.BlockSpec((1, tl, da), main_map),
                pl.BlockSpec((1, SUBLANES, da), next_map),
                pl.BlockSpec((4, da), const2),
                pl.BlockSpec((1, da), const2),
                pl.BlockSpec((heads, blk, 2 * blk), lambda b, c: (0, 0, 0)),
                pl.BlockSpec((1, da), const2),
                pl.BlockSpec((1, da), const2),
                pl.BlockSpec((1, da), const2),
                pl.BlockSpec((1, 1, da), lambda b, c: (b, 0, 0))]
    args = [ax, ax, ax, cw, cb, wri, br, bi, lam, h0]
    if combine:
        in_specs += [pl.BlockSpec((1, tl, da), main_map), pl.BlockSpec((1, tl, da), main_map)]
        args += [hf, ag]
    out_dtype = BF16 if combine else F32
    return pl.pallas_call(
        functools.partial(_lru_kernel, reverse=reverse, combine=combine, nc=nc, tl=tl,
                          heads=heads, blk=blk),
        out_shape=(_sds((bsz, n, da), out_dtype), _sds((bsz, 1, da), F32)),
        grid=(bsz, nc),
        in_specs=in_specs,
        out_specs=(pl.BlockSpec((1, tl, da), main_map),
                   pl.BlockSpec((1, 1, da), lambda b, c: (b, 0, 0))),
        scratch_shapes=[pltpu.VMEM((tl + 2 * SUBLANES, da), F32),
                        pltpu.VMEM((tl, da), F32),
                        pltpu.VMEM((da // LANES, tl, LANES), F32),
                        pltpu.VMEM((da // LANES, tl, LANES), F32),
                        pltpu.VMEM((SUBLANES, da), F32)],
        compiler_params=_params(("arbitrary", "arbitrary")),
        name=("lru_bwd" if reverse else "lru_fwd") + ("_mix" if combine else ""),
    )(*args)


def _mixout_kernel(ya_ref, bb_ref, p_ref, pu_ref, pd_ref, cbw_ref, ga_ref, gb_ref, wo_ref, x_ref,
                   gt_ref, gf_ref, sh_ref, sc_ref, wr_ref, brt_ref,
                   x1_ref, xp_ref, te_ref, tw_ref, *, tm, d_a, d_b, n_tiles):
    i = pl.program_id(1)
    half = d_b // 2
    z = p_ref[0].astype(F32)
    w = cbw_ref[...]
    zh = z[:, :half]
    col = lax.broadcasted_iota(I32, (tm, half), 0) % GRID_W
    left = jnp.where(col >= 1, pltpu.roll(zh, 1, 0), 0.0)
    right = jnp.where(col <= GRID_W - 2, pltpu.roll(zh, tm - 1, 0), 0.0)
    horiz = w[0:1, :half] * left + w[1:2, :half] * zh + w[2:3, :half] * right
    zv = z[:, half:]
    up_halo = jnp.where(i == 0, 0.0, pu_ref[0].astype(F32))
    dn_halo = jnp.where(i == n_tiles - 1, 0.0, pd_ref[0].astype(F32))
    if tm > GRID_W:
        up = jnp.concatenate([up_halo, zv[:tm - GRID_W]], axis=0)
        dn = jnp.concatenate([zv[GRID_W:], dn_halo], axis=0)
    else:
        up, dn = up_halo, dn_halo
    vert = w[0:1, half:] * up + w[1:2, half:] * zv + w[2:3, half:] * dn
    bb = bb_ref[0].astype(F32)
    yb = jnp.concatenate([bb[:, :half] * horiz, bb[:, half:] * vert], axis=1)
    ya = ya_ref[0].astype(F32)
    ya_n = _rms(ya, ga_ref[...]).astype(BF16)
    yb_n = _rms(yb, gb_ref[...]).astype(BF16)
    mix = (jnp.dot(ya_n, wo_ref[0:d_a, :], preferred_element_type=F32)
           + jnp.dot(yb_n, wo_ref[d_a:d_a + d_b, :], preferred_element_type=F32))
    x1 = x_ref[0] + gt_ref[0] * mix
    x1_ref[0] = x1
    xn = _rms(x1, gf_ref[...]) * (1.0 + sc_ref[0]) + sh_ref[0]
    xb = xn.astype(BF16)
    _store_token_tiles(xp_ref.at[0], _pack_pairs(xn), tm)
    logits = jnp.dot(xb, wr_ref[...], preferred_element_type=F32) + brt_ref[...]
    lane = lax.broadcasted_iota(I32, logits.shape, 1)
    lane_f = lane.astype(F32)
    vals = logits
    tv, te = [], []
    for _ in range(TOP_K):
        m = jnp.max(vals, axis=-1, keepdims=True)
        idx = jnp.min(jnp.where(vals == m, lane_f, float(LANES)), axis=-1, keepdims=True)
        tv.append(m)
        te.append(idx)
        vals = jnp.where(lane_f == idx, -jnp.inf, vals)
    ex = [jnp.exp(v - tv[0]) for v in tv]
    den = ex[0]
    for e in ex[1:]:
        den = den + e
    e_out = jnp.zeros(logits.shape, I32)
    w_out = jnp.zeros(logits.shape, F32)
    for k in range(TOP_K):
        e_out = jnp.where(lane == k, te[k].astype(I32), e_out)
        w_out = jnp.where(lane == k, ex[k] / den, w_out)
    te_ref[0] = e_out
    tw_ref[0] = w_out


def _mixout_call(ya, bb, p, cbw, ga, gb, wo_bf, x, gt1, gf, sh2, sc2, wr_bf, brt):
    bsz, n, d = x.shape
    d_a = ya.shape[-1]
    d_b = bb.shape[-1]
    half = d_b // 2
    tm = _pick(n, 512, GRID_W)
    n_tiles = n // tm
    nseg = (d // 2) // LANES
    rpt = tm // GRID_W
    n_rows = n // GRID_W
    row = lambda b, i: (b, i, 0)
    vec = lambda b, i: (b, 0, 0)
    const2 = lambda b, i: (0, 0)
    return pl.pallas_call(
        functools.partial(_mixout_kernel, tm=tm, d_a=d_a, d_b=d_b, n_tiles=n_tiles),
        out_shape=(_sds((bsz, n, d), F32), _sds((bsz, n * nseg, LANES), U32),
                   _sds((bsz, n, LANES), I32), _sds((bsz, n, LANES), F32)),
        grid=(bsz, n_tiles),
        in_specs=[pl.BlockSpec((1, tm, d_a), row),
                  pl.BlockSpec((1, tm, d_b), row),
                  pl.BlockSpec((1, tm, d_b), row),
                  pl.BlockSpec((1, GRID_W, half), lambda b, i: (b, jnp.maximum(i * rpt - 1, 0), 1)),
                  pl.BlockSpec((1, GRID_W, half), lambda b, i: (b, jnp.minimum((i + 1) * rpt, n_rows - 1), 1)),
                  pl.BlockSpec((3, d_b), const2),
                  pl.BlockSpec((1, d_a), const2),
                  pl.BlockSpec((1, d_b), const2),
                  pl.BlockSpec((d_a + d_b, d), const2),
                  pl.BlockSpec((1, tm, d), row),
                  pl.BlockSpec((1, 1, d), vec),
                  pl.BlockSpec((1, d), const2),
                  pl.BlockSpec((1, 1, d), vec),
                  pl.BlockSpec((1, 1, d), vec),
                  pl.BlockSpec((d, LANES), const2),
                  pl.BlockSpec((1, LANES), const2)],
        out_specs=(pl.BlockSpec((1, tm, d), row), pl.BlockSpec((1, tm * nseg, LANES), row),
                   pl.BlockSpec((1, tm, LANES), row), pl.BlockSpec((1, tm, LANES), row)),
        compiler_params=_params(("arbitrary", "arbitrary")),
        name="mixout",
    )(ya, bb, p, p, p, cbw, ga, gb, wo_bf, x, gt1, gf, sh2, sc2, wr_bf, brt)


def _rank_kernel(e_ref, rank_ref, cnt_ref, carry, *, tt):
    @pl.when(pl.program_id(0) == 0)
    def _():
        carry[...] = jnp.zeros(carry.shape, F32)

    e = e_ref[...]
    lane = lax.broadcasted_iota(I32, (tt, LANES), 1)
    ohs = []
    m = jnp.zeros((tt, LANES), F32)
    for k in range(TOP_K):
        oh = lane == e[:, k:k + 1]
        ohs.append(oh)
        m = m + jnp.where(oh, 1.0, 0.0)
    ri = lax.broadcasted_iota(I32, (tt, tt), 0)
    ci = lax.broadcasted_iota(I32, (tt, tt), 1)
    ltri = jnp.where(ri > ci, 1.0, 0.0).astype(BF16)
    pref = jnp.dot(ltri, m.astype(BF16), preferred_element_type=F32) + carry[0:1, :]
    out = jnp.zeros((tt, LANES), I32)
    for k in range(TOP_K):
        rk = jnp.sum(jnp.where(ohs[k], pref, 0.0), axis=-1, keepdims=True)
        out = jnp.where(lane == k, rk.astype(I32), out)
    rank_ref[...] = out
    tot = carry[0:1, :] + jnp.sum(m, axis=0, keepdims=True)
    carry[...] = jnp.broadcast_to(tot, carry.shape)
    cnt_ref[...] = jnp.broadcast_to(tot, cnt_ref.shape)


def _rank_call(top_e):
    t = top_e.shape[0]
    tt = _pick(t, 512, SUBLANES)
    return pl.pallas_call(
        functools.partial(_rank_kernel, tt=tt),
        out_shape=(_sds((t, LANES), I32), _sds((SUBLANES, LANES), F32)),
        grid=(t // tt,),
        in_specs=[pl.BlockSpec((tt, LANES), lambda i: (i, 0))],
        out_specs=(pl.BlockSpec((tt, LANES), lambda i: (i, 0)),
                   pl.BlockSpec((SUBLANES, LANES), lambda i: (0, 0))),
        scratch_shapes=[pltpu.VMEM((SUBLANES, LANES), F32)],
        compiler_params=_params(("arbitrary",)),
        name="rank",
    )(top_e)


def _zero_tail(first, n_blocks, zbuf, dst_block, sem):
    zbuf[...] = jnp.zeros(zbuf.shape, zbuf.dtype)

    def start(c, carry):
        pltpu.make_async_copy(zbuf, dst_block(c), sem).start()
        return carry

    def wait(c, carry):
        pltpu.make_async_copy(zbuf, dst_block(c), sem).wait()
        return carry

    lax.fori_loop(first, n_blocks, start, 0)
    lax.fori_loop(first, n_blocks, wait, 0)


def _dispatch_kernel(bs_ref, nb_ref, pos_ref, x_ref, o_hbm, zbuf, sem, zsem, *, td, rb, n_e, n_blocks, nseg):
    rbr = rb * nseg

    def blk(b):
        return pl.ds(pl.multiple_of(b * rbr, rbr), rbr)

    def tok(t):
        return pl.ds(pl.multiple_of(t * nseg, nseg), nseg)

    @pl.when(pl.program_id(0) == 0)
    def _():
        _zero_tail(bs_ref[n_e - 1] + nb_ref[n_e - 1], n_blocks, zbuf, lambda c: o_hbm.at[blk(c)], zsem)

        def zero_copy(e):
            return pltpu.make_async_copy(zbuf, o_hbm.at[blk(bs_ref[e] + nb_ref[e] - 1)], zsem)

        for e in range(n_e):
            @pl.when(nb_ref[e] > 0)
            def _():
                zero_copy(e).start()

        for e in range(n_e):
            @pl.when(nb_ref[e] > 0)
            def _():
                zero_copy(e).wait()

    def body(t, c):
        src = x_ref.at[tok(t)]
        for k in range(TOP_K):
            pltpu.make_async_copy(src, o_hbm.at[tok(pos_ref[0, 0, t * TOP_K + k])], sem).start(priority=k % 2)
        return c

    lax.fori_loop(0, td, body, 0, unroll=2)
    for _ in range(TOP_K):
        pltpu.make_async_copy(x_ref, o_hbm.at[pl.ds(0, td * nseg)], sem).wait()


def _dispatch_call(groups, xp, pos, n_buf, rb, nseg):
    bstart, nblk = groups
    n_e = bstart.shape[0]
    t = xp.shape[0] // nseg
    td = _pick(t, 512, SUBLANES)
    pos3 = pos.reshape(t // td, 1, td * TOP_K)
    return pl.pallas_call(
        functools.partial(_dispatch_kernel, td=td, rb=rb, n_e=n_e, n_blocks=n_buf // rb, nseg=nseg),
        out_shape=_sds((n_buf * nseg, LANES), U32),
        grid_spec=pltpu.PrefetchScalarGridSpec(
            num_scalar_prefetch=2,
            grid=(t // td,),
            in_specs=[pl.BlockSpec((1, 1, td * TOP_K), lambda i, bs, nb: (i, 0, 0),
                                   memory_space=pltpu.SMEM),
                      pl.BlockSpec((td * nseg, LANES), lambda i, bs, nb: (i, 0))],
            out_specs=pl.BlockSpec(memory_space=pl.ANY),
            scratch_shapes=[pltpu.VMEM((rb * nseg, LANES), U32), pltpu.SemaphoreType.DMA,
                            pltpu.SemaphoreType.DMA]),
        compiler_params=_params(("arbitrary",)),
        name="dispatch",
    )(bstart, nblk, pos3, xp)


GROUP_IN_SLOTS = 4
GROUP_OUT_SLOTS = 3


def _build_chunks(bs_ref, nb_ref, ce, cs, cb, cn, n_split):
    def per_expert(e, j):
        nb = nb_ref[e]
        b0 = bs_ref[e]

        def per_slab(s_, j):
            def per_pair(c, j):
                ce[j] = e
                cs[j] = s_
                cb[j] = b0 + 2 * c
                cn[j] = jnp.minimum(nb - 2 * c, 2)
                return j + 1

            return lax.fori_loop(0, (nb + 1) // 2, per_pair, j)

        return lax.fori_loop(0, n_split, per_slab, j)

    return lax.fori_loop(0, bs_ref.shape[0], per_expert, jnp.int32(0))


def _chunk_loop(n, nb_ref, ce, cs, cn, in_copy, out_copy, w_copies, load_weights, compute):
    ahead = GROUP_IN_SLOTS - 1

    def by_size(j, fn):
        @pl.when(cn[j] == 2)
        def _():
            fn(2)

        @pl.when(cn[j] == 1)
        def _():
            fn(1)

    @pl.when(n > 0)
    def _():
        for cp in w_copies(ce[0], cs[0]):
            cp.start()
        for j in range(ahead):
            @pl.when(j < n)
            def _():
                by_size(j, lambda m: in_copy(j, j, m).start())

        def body(j, carry):
            jp = jnp.maximum(j - 1, 0)
            first = jnp.logical_or(j == 0, jnp.logical_or(ce[j] != ce[jp], cs[j] != cs[jp]))

            @pl.when(first)
            def _():
                for cp in w_copies(ce[j], cs[j]):
                    cp.wait()
                load_weights()
                jn = j + (nb_ref[ce[j]] + 1) // 2

                @pl.when(jn < n)
                def _():
                    for cp in w_copies(ce[jn], cs[jn]):
                        cp.start()

            islot = lax.rem(j, GROUP_IN_SLOTS)
            oslot = lax.rem(j, GROUP_OUT_SLOTS)

            @pl.when(j + ahead < n)
            def _():
                ja = j + ahead
                by_size(ja, lambda m: in_copy(ja, lax.rem(ja, GROUP_IN_SLOTS), m).start())

            by_size(j, lambda m: in_copy(j, islot, m).wait())

            @pl.when(j >= GROUP_OUT_SLOTS)
            def _():
                jo = j - GROUP_OUT_SLOTS
                by_size(jo, lambda m: out_copy(jo, oslot, m).wait())

            def run(m):
                compute(j, islot, oslot, m)
                out_copy(j, oslot, m).start()

            by_size(j, run)
            return carry

        lax.fori_loop(0, n, body, 0)

        for k in range(GROUP_OUT_SLOTS):
            @pl.when(n > k)
            def _():
                jl = n - 1 - k
                by_size(jl, lambda m: out_copy(jl, lax.rem(jl, GROUP_OUT_SLOTS), m).wait())


def _gmm1_kernel(bs_ref, nb_ref, xs_hbm, wg_hbm, wu_hbm, bg_ref, bu_ref, act_hbm,
                 xbuf, obuf, wraw, wbf, zbuf, ce, cs, cb, cn, sin, sout, wsem, zsem,
                 *, rb, tf, n_blocks, n_split, nseg):
    def blk(b, m=1):
        return pl.ds(pl.multiple_of(b * rb, rb), m * rb)

    def tok_blk(b, m=1):
        return pl.ds(pl.multiple_of(b * (rb * nseg), rb * nseg), m * rb * nseg)

    def cols(s_):
        return pl.ds(pl.multiple_of(s_ * tf, tf), tf)

    n_e = bs_ref.shape[0]
    for s_ in range(n_split):
        _zero_tail(bs_ref[n_e - 1] + nb_ref[n_e - 1], n_blocks, zbuf,
                   lambda c: act_hbm.at[blk(c), cols(s_)], zsem)
    n = _build_chunks(bs_ref, nb_ref, ce, cs, cb, cn, n_split)

    def in_copy(j, slot, m):
        return pltpu.make_async_copy(xs_hbm.at[tok_blk(cb[j], m)],
                                     xbuf.at[slot, pl.ds(0, m * rb * nseg)], sin.at[slot])

    def out_copy(j, slot, m):
        return pltpu.make_async_copy(obuf.at[slot, pl.ds(0, m * rb)],
                                     act_hbm.at[blk(cb[j], m), cols(cs[j])], sout.at[slot])

    def w_copies(e, s_):
        return (pltpu.make_async_copy(wg_hbm.at[e, :, cols(s_)], wraw.at[0], wsem),
                pltpu.make_async_copy(wu_hbm.at[e, :, cols(s_)], wraw.at[1], wsem))

    def load_weights():
        wbf[:, 0:tf] = wraw[0].astype(BF16)
        wbf[:, tf:2 * tf] = wraw[1].astype(BF16)

    def compute(j, islot, oslot, m):
        g = ce[j] * n_split + cs[j]
        lo, hi = _unpack_pairs(_load_token_tiles(xbuf.at[islot], m * rb, nseg))
        h = jnp.concatenate([lo.astype(BF16), hi.astype(BF16)], axis=1)
        gu = jnp.dot(h, wbf[...], preferred_element_type=F32)
        gate = jnp.minimum(gu[:, :tf] + bg_ref[g], SWIGLU_LIMIT)
        up = jnp.clip(gu[:, tf:] + bu_ref[g], -SWIGLU_LIMIT, SWIGLU_LIMIT)
        act = (up + 1.0) * gate * _sigmoid(SWIGLU_ALPHA * gate)
        obuf[oslot, 0:m * rb, :] = act.astype(BF16)

    _chunk_loop(n, nb_ref, ce, cs, cn, in_copy, out_copy, w_copies, load_weights, compute)


def _gmm1_call(groups, xs, wg, wu, bg, bu, rb, tf, nseg):
    n_buf = xs.shape[0] // nseg
    n_e, d, dff = wg.shape
    n_split = dff // tf
    max_chunks = n_split * (n_buf // rb)
    vmem_full = lambda shape: pl.BlockSpec(shape, lambda i, *_: (0,) * len(shape))
    return pl.pallas_call(
        functools.partial(_gmm1_kernel, rb=rb, tf=tf, n_blocks=n_buf // rb, n_split=n_split, nseg=nseg),
        out_shape=_sds((n_buf, dff), BF16),
        grid_spec=pltpu.PrefetchScalarGridSpec(
            num_scalar_prefetch=2,
            grid=(1,),
            in_specs=[pl.BlockSpec(memory_space=pl.ANY),
                      pl.BlockSpec(memory_space=pl.ANY),
                      pl.BlockSpec(memory_space=pl.ANY),
                      vmem_full((n_e * n_split, 1, tf)),
                      vmem_full((n_e * n_split, 1, tf))],
            out_specs=pl.BlockSpec(memory_space=pl.ANY),
            scratch_shapes=[pltpu.VMEM((GROUP_IN_SLOTS, 2 * rb * nseg, LANES), U32),
                            pltpu.VMEM((GROUP_OUT_SLOTS, 2 * rb, tf), BF16),
                            pltpu.VMEM((2, d, tf), F32),
                            pltpu.VMEM((d, 2 * tf), BF16),
                            pltpu.VMEM((rb, tf), BF16),
                            pltpu.SMEM((max_chunks,), I32),
                            pltpu.SMEM((max_chunks,), I32),
                            pltpu.SMEM((max_chunks,), I32),
                            pltpu.SMEM((max_chunks,), I32),
                            pltpu.SemaphoreType.DMA((GROUP_IN_SLOTS,)),
                            pltpu.SemaphoreType.DMA((GROUP_OUT_SLOTS,)),
                            pltpu.SemaphoreType.DMA,
                            pltpu.SemaphoreType.DMA]),
        compiler_params=_params(("arbitrary",)),
        name="gmm1",
    )(*groups, xs, wg, wu, bg.reshape(n_e * n_split, 1, tf), bu.reshape(n_e * n_split, 1, tf))


def _gmm2_kernel(bs_ref, nb_ref, act_hbm, wd_hbm, bd_ref, y_hbm,
                 abuf, ybuf, wraw, wbf, zbuf, ce, cs, cb, cn, sin, sout, wsem, zsem, *, rb, n_blocks, nseg):
    def blk(b, m=1):
        return pl.ds(pl.multiple_of(b * rb, rb), m * rb)

    def tok_blk(b, m=1):
        return pl.ds(pl.multiple_of(b * (rb * nseg), rb * nseg), m * rb * nseg)

    n_e = bs_ref.shape[0]
    _zero_tail(bs_ref[n_e - 1] + nb_ref[n_e - 1], n_blocks, zbuf, lambda c: y_hbm.at[tok_blk(c)], zsem)
    n = _build_chunks(bs_ref, nb_ref, ce, cs, cb, cn, 1)

    def in_copy(j, slot, m):
        return pltpu.make_async_copy(act_hbm.at[blk(cb[j], m)], abuf.at[slot, pl.ds(0, m * rb)],
                                     sin.at[slot])

    def out_copy(j, slot, m):
        return pltpu.make_async_copy(ybuf.at[slot, pl.ds(0, m * rb * nseg)],
                                     y_hbm.at[tok_blk(cb[j], m)], sout.at[slot])

    def w_copies(e, s_):
        return (pltpu.make_async_copy(wd_hbm.at[e], wraw, wsem),)

    def load_weights():
        wbf[...] = wraw[...].astype(BF16)

    def compute(j, islot, oslot, m):
        y = jnp.dot(abuf[islot, 0:m * rb, :], wbf[...], preferred_element_type=F32) + bd_ref[ce[j]]
        _store_token_tiles(ybuf.at[oslot], _pack_pairs(y), m * rb)

    _chunk_loop(n, nb_ref, ce, cs, cn, in_copy, out_copy, w_copies, load_weights, compute)


def _gmm2_call(groups, act, wd, bd, rb, nseg):
    n_buf, dff = act.shape
    n_e, _, d = wd.shape
    assert d == 2 * nseg * LANES
    max_chunks = n_buf // rb
    vmem_full = lambda shape: pl.BlockSpec(shape, lambda i, *_: (0,) * len(shape))
    return pl.pallas_call(
        functools.partial(_gmm2_kernel, rb=rb, n_blocks=n_buf // rb, nseg=nseg),
        out_shape=_sds((n_buf * nseg, LANES), U32),
        grid_spec=pltpu.PrefetchScalarGridSpec(
            num_scalar_prefetch=2,
            grid=(1,),
            in_specs=[pl.BlockSpec(memory_space=pl.ANY),
                      pl.BlockSpec(memory_space=pl.ANY),
                      vmem_full((n_e, 1, d))],
            out_specs=pl.BlockSpec(memory_space=pl.ANY),
            scratch_shapes=[pltpu.VMEM((GROUP_IN_SLOTS, 2 * rb, dff), BF16),
                            pltpu.VMEM((GROUP_OUT_SLOTS, 2 * rb * nseg, LANES), U32),
                            pltpu.VMEM((dff, d), F32),
                            pltpu.VMEM((dff, d), BF16),
                            pltpu.VMEM((rb * nseg, LANES), U32),
                            pltpu.SMEM((max_chunks,), I32),
                            pltpu.SMEM((max_chunks,), I32),
                            pltpu.SMEM((max_chunks,), I32),
                            pltpu.SMEM((max_chunks,), I32),
                            pltpu.SemaphoreType.DMA((GROUP_IN_SLOTS,)),
                            pltpu.SemaphoreType.DMA((GROUP_OUT_SLOTS,)),
                            pltpu.SemaphoreType.DMA,
                            pltpu.SemaphoreType.DMA]),
        compiler_params=_params(("arbitrary",)),
        name="gmm2",
    )(*groups, act, wd, bd.reshape(n_e, 1, d))


def _final_kernel(pos_ref, posn_ref, y_hbm, tw_ref, x1_ref, gt_ref, gfin_ref, o_ref, ybuf, sem, *,
                  tc, n_steps, nseg):
    step = pl.program_id(0) * pl.num_programs(1) + pl.program_id(1)

    def tok(t):
        return pl.ds(pl.multiple_of(t * nseg, nseg), nseg)

    def issue(p_ref, slot):
        def body(t, c):
            for k in range(TOP_K):
                src = y_hbm.at[tok(p_ref[0, 0, t * TOP_K + k])]
                pltpu.make_async_copy(src, ybuf.at[slot, k, tok(t)], sem.at[slot]).start(priority=k % 2)
            return c

        lax.fori_loop(0, tc, body, 0, unroll=2)

    def run(cur):
        @pl.when(step + 1 < n_steps)
        def _():
            issue(posn_ref, 1 - cur)

        for k in range(TOP_K):
            pltpu.make_async_copy(y_hbm.at[pl.ds(0, tc * nseg)], ybuf.at[cur, k], sem.at[cur]).wait()
        tw = tw_ref[0]
        lo, hi = None, None
        for k in range(TOP_K):
            lo_k, hi_k = _unpack_pairs(_load_token_tiles(ybuf.at[cur, k], tc, nseg))
            wk = tw[:, k:k + 1]
            lo = wk * lo_k if lo is None else lo + wk * lo_k
            hi = wk * hi_k if hi is None else hi + wk * hi_k
        moe = jnp.concatenate([lo, hi], axis=1)
        x2 = x1_ref[0] + gt_ref[0] * moe
        o_ref[0] = _rms(x2, gfin_ref[...])

    @pl.when(step == 0)
    def _():
        issue(pos_ref, 0)

    parity = lax.rem(step, 2)

    @pl.when(parity == 0)
    def _():
        run(0)

    @pl.when(parity == 1)
    def _():
        run(1)


def _final_call(y, pos, top_w, x1, gt2, gfin, nseg):
    bsz, n, d = x1.shape
    tc = _pick(n, 256, SUBLANES)
    nt = n // tc
    n_steps = bsz * nt
    pos3 = pos.reshape(n_steps, 1, tc * TOP_K)
    row = lambda b, i: (b, i, 0)
    return pl.pallas_call(
        functools.partial(_final_kernel, tc=tc, n_steps=n_steps, nseg=nseg),
        out_shape=_sds((bsz, n, d), F32),
        grid=(bsz, nt),
        in_specs=[pl.BlockSpec((1, 1, tc * TOP_K), lambda b, i: (b * nt + i, 0, 0), memory_space=pltpu.SMEM),
                  pl.BlockSpec((1, 1, tc * TOP_K), lambda b, i: (jnp.minimum(b * nt + i + 1, n_steps - 1), 0, 0),
                               memory_space=pltpu.SMEM),
                  pl.BlockSpec(memory_space=pl.ANY),
                  pl.BlockSpec((1, tc, LANES), row),
                  pl.BlockSpec((1, tc, d), row),
                  pl.BlockSpec((1, 1, d), lambda b, i: (b, 0, 0)),
                  pl.BlockSpec((1, d), lambda b, i: (0, 0))],
        out_specs=pl.BlockSpec((1, tc, d), row),
        scratch_shapes=[pltpu.VMEM((2, TOP_K, tc * nseg, LANES), U32), pltpu.SemaphoreType.DMA((2,))],
        compiler_params=_params(("arbitrary", "arbitrary")),
        name="final",
    )(pos3, pos3, y, top_w, x1, gt2, gfin)


def _group_tables(counts, rb):
    nblk = (counts + rb - 1) // rb
    bstart = jnp.cumsum(nblk) - nblk
    return bstart * rb, (bstart.astype(I32), nblk.astype(I32))


def kernel(x, c, ctx, c_ctx, w_mod, b_mod, g_mix, w_in, conv_a_w, conv_a_b, lru_w_r, lru_b_r,
           lru_w_i, lru_b_i, lru_lam, conv_b_w, g_out_a, g_out_b, w_out, g_ffn, w_router,
           b_router, w_gate, b_gate, w_up, b_up, w_down, b_down, g_final):
    assert w_mod.shape[0] == 1, "single-layer block"
    bsz, n_lat, d = x.shape
    d_a = conv_a_w.shape[-1]
    d_b = conv_b_w.shape[-1]
    n_e = w_router.shape[-1]
    assert n_lat % GRID_W == 0 and n_e <= LANES and d % (2 * LANES) == 0
    l = 0

    cs = jnp.zeros((SUBLANES, d), F32).at[:bsz].set(c).at[bsz].set(c_ctx)
    mod = _mod_call(cs, w_mod[l], b_mod[l][None])
    sh1, sc1, gt1, sh2, sc2, gt2 = [m[:bsz, None, :] for m in jnp.split(mod, 6, axis=-1)]
    ssh1, ssc1 = [jnp.broadcast_to(m[bsz][None, None, :], (bsz, 1, d))
                  for m in jnp.split(mod, 6, axis=-1)[:2]]

    w_in_bf = w_in[l].astype(BF16)
    g_mix2 = g_mix[l][None]
    cw = conv_a_w[l]
    cb = conv_a_b[l][None]
    wri = jnp.concatenate([lru_w_r[l], lru_w_i[l]], axis=-1).astype(BF16)
    br, bi, lam = lru_b_r[l], lru_b_i[l], lru_lam[l]
    lru_p = lambda dr: (cw, cb, wri[dr], br[dr][None], bi[dr][None], lam[dr][None])

    (s_ax,) = _inproj_call(ctx, g_mix2, ssh1, ssc1, w_in_bf[:, d_a:2 * d_a], d_a, d_b, latent=False)
    zero_state = jnp.zeros((bsz, 1, d_a), F32)
    _, h0f = _lru_call(s_ax, zero_state, *lru_p(0), reverse=False)
    _, h0b = _lru_call(s_ax, zero_state, *lru_p(1), reverse=True)

    ag, ax, bb, p = _inproj_call(x, g_mix2, sh1, sc1, w_in_bf, d_a, d_b, latent=True)
    hf, _ = _lru_call(ax, h0f, *lru_p(0), reverse=False)
    ya, _ = _lru_call(ax, h0b, *lru_p(1), reverse=True, hf=hf, ag=ag)

    wr_bf = jnp.zeros((d, LANES), BF16).at[:, :n_e].set(w_router[l].astype(BF16))
    brt = jnp.full((1, LANES), NEG_BIG, F32).at[0, :n_e].set(b_router[l])
    x1, xp, top_e, top_w = _mixout_call(
        ya, bb, p, conv_b_w[l], g_out_a[l][None], g_out_b[l][None], w_out[l].astype(BF16), x, gt1,
        g_ffn[l][None], sh2, sc2, wr_bf, brt)

    n_tok = bsz * n_lat
    n_rows = n_tok * TOP_K
    rb = MOE_ROW_BLOCK
    n_buf = n_rows + n_e * rb
    rank, cnt = _rank_call(top_e.reshape(n_tok, LANES))
    counts = cnt[0, :n_e].astype(I32)
    starts, groups = _group_tables(counts, rb)
    te = top_e.reshape(n_tok, LANES)[:, :TOP_K]
    onehot = te[:, :, None] == jnp.arange(n_e, dtype=I32)[None, None, :]
    pos = (jnp.sum(jnp.where(onehot, starts[None, None, :], 0), axis=-1) + rank[:, :TOP_K]).astype(I32)
    nseg = (d // 2) // LANES
    xs = _dispatch_call(groups, xp.reshape(n_tok * nseg, LANES), pos, n_buf, rb, nseg)
    d_ff = w_gate.shape[-1]
    tf = _pick(d_ff, 1024)
    act = _gmm1_call(groups, xs, w_gate[l], w_up[l], b_gate[l], b_up[l], rb, tf, nseg)
    y = _gmm2_call(groups, act, w_down[l], b_down[l], rb, nseg)
    return _final_call(y, pos, top_w, x1, gt2, g_final[None], nseg)
```

```python
import functools

import jax
import jax.numpy as jnp
from jax import lax
from jax.experimental import pallas as pl
from jax.experimental.pallas import tpu as pltpu

F32 = jnp.float32
BF16 = jnp.bfloat16
I32 = jnp.int32
U32 = jnp.uint32

GRID_W = 64
TOP_K = 4
LRU_C = 8.0
CONV_A_LEFT = 2
SCAN_SEG = 4
SWIGLU_LIMIT = 7.0
SWIGLU_ALPHA = 1.702
EPS = 1e-6
MOE_ROW_BLOCK = 256

LANES = 128
SUBLANES = 8
VMEM_LIMIT_BYTES = 56 * 1024 * 1024
NEG_BIG = -1e30


def _sds(shape, dtype):
    return jax.ShapeDtypeStruct(shape, dtype)


def _pick(n, pref, mult=LANES):
    if n <= pref:
        return n
    t = (pref // mult) * mult
    while t >= mult:
        if n % t == 0:
            return t
        t -= mult
    return n


def _params(sem):
    return pltpu.CompilerParams(dimension_semantics=sem, vmem_limit_bytes=VMEM_LIMIT_BYTES)


def _sigmoid(x):
    return 1.0 / (1.0 + jnp.exp(-x))


def _rms(x, g):
    ms = jnp.mean(x * x, axis=-1, keepdims=True)
    return (x * lax.rsqrt(ms + EPS)) * g


def _pack_pairs(x):
    w = x.shape[1] // 2
    lo = lax.bitcast_convert_type(x[:, :w].astype(BF16).astype(F32), U32)
    hi = lax.bitcast_convert_type(x[:, w:].astype(BF16).astype(F32), U32)
    return lax.shift_right_logical(lo, jnp.uint32(16)) | (hi & jnp.uint32(0xFFFF0000))


def _unpack_pairs(words):
    lo = lax.bitcast_convert_type(lax.shift_left(words, jnp.uint32(16)), F32)
    hi = lax.bitcast_convert_type(words & jnp.uint32(0xFFFF0000), F32)
    return lo, hi


def _store_token_tiles(ref, words, rows):
    nseg = words.shape[1] // LANES
    if nseg == 1:
        ref[...] = words
        return
    for s in range(nseg):
        ref[pl.ds(s, rows, stride=nseg), :] = words[:, s * LANES:(s + 1) * LANES]


def _load_token_tiles(ref, rows, nseg):
    if nseg == 1:
        return ref[...]
    return jnp.concatenate([ref[pl.ds(s, rows, stride=nseg), :] for s in range(nseg)], axis=1)


def _mod_kernel(c_ref, w_ref, b_ref, o_ref):
    c = c_ref[...]
    s = c * _sigmoid(c)
    o_ref[...] = jnp.dot(s.astype(BF16), w_ref[...].astype(BF16),
                         preferred_element_type=F32) + b_ref[...]


def _mod_call(cs, w, b):
    d, n6 = w.shape
    tn = _pick(n6, 1024)
    return pl.pallas_call(
        _mod_kernel,
        out_shape=_sds((cs.shape[0], n6), F32),
        grid=(n6 // tn,),
        in_specs=[pl.BlockSpec((cs.shape[0], d), lambda j: (0, 0)),
                  pl.BlockSpec((d, tn), lambda j: (0, j)),
                  pl.BlockSpec((1, tn), lambda j: (0, j))],
        out_specs=pl.BlockSpec((cs.shape[0], tn), lambda j: (0, j)),
        compiler_params=_params(("arbitrary",)),
        name="mod",
    )(cs, w, b)


def _inproj_kernel(x_ref, g_ref, sh_ref, sc_ref, w_ref, *out_refs, d_a, d_b, latent):
    x = x_ref[0]
    xn = _rms(x, g_ref[...]) * (1.0 + sc_ref[0]) + sh_ref[0]
    xb = xn.astype(BF16)

    def sec(lo, width):
        return jnp.dot(xb, w_ref[:, lo:lo + width], preferred_element_type=F32)

    if not latent:
        out_refs[0][0] = sec(0, d_a)
        return
    ag_ref, ax_ref, bb_ref, p_ref = out_refs
    ag_ref[0] = sec(0, d_a).astype(BF16)
    ax_ref[0] = sec(d_a, d_a)
    bb_ref[0] = sec(2 * d_a, d_b).astype(BF16)
    p_ref[0] = (sec(2 * d_a + d_b, d_b) * sec(2 * d_a + 2 * d_b, d_b)).astype(BF16)


def _inproj_call(x, g, sh, sc, w_bf, d_a, d_b, latent):
    bsz, n, d = x.shape
    tm = _pick(n, 512, SUBLANES)
    n_w = w_bf.shape[1]
    row = lambda b, i: (b, i, 0)
    if latent:
        out_shape = (_sds((bsz, n, d_a), BF16), _sds((bsz, n, d_a), F32),
                     _sds((bsz, n, d_b), BF16), _sds((bsz, n, d_b), BF16))
        out_specs = (pl.BlockSpec((1, tm, d_a), row), pl.BlockSpec((1, tm, d_a), row),
                     pl.BlockSpec((1, tm, d_b), row), pl.BlockSpec((1, tm, d_b), row))
    else:
        out_shape = (_sds((bsz, n, d_a), F32),)
        out_specs = (pl.BlockSpec((1, tm, d_a), row),)
    return pl.pallas_call(
        functools.partial(_inproj_kernel, d_a=d_a, d_b=d_b, latent=latent),
        out_shape=out_shape,
        grid=(bsz, n // tm),
        in_specs=[pl.BlockSpec((1, tm, d), row),
                  pl.BlockSpec((1, d), lambda b, i: (0, 0)),
                  pl.BlockSpec((1, 1, d), lambda b, i: (b, 0, 0)),
                  pl.BlockSpec((1, 1, d), lambda b, i: (b, 0, 0)),
                  pl.BlockSpec((d, n_w), lambda b, i: (0, 0), pipeline_mode=pl.Buffered(1))],
        out_specs=out_specs,
        compiler_params=_params(("arbitrary", "arbitrary")),
        name="inproj_lat" if latent else "inproj_ctx",
    )(x, g, sh, sc, w_bf)


def _gelu_tanh(x):
    c = 0.7978845608028654
    return x * (0.5 * (1.0 + jnp.tanh(c * (x + 0.044715 * (x * x * x)))))


def _lru_kernel(*refs, reverse, combine, nc, tl, heads, blk):
    if combine:
        (prev_ref, main_ref, next_ref, cw_ref, cb_ref, wri_ref, br_ref, bi_ref, lam_ref, h0_ref,
         hf_ref, ag_ref, out_ref, hlast_ref, ebuf, xc_s, a_s, b_s, carry) = refs
    else:
        (prev_ref, main_ref, next_ref, cw_ref, cb_ref, wri_ref, br_ref, bi_ref, lam_ref, h0_ref,
         out_ref, hlast_ref, ebuf, xc_s, a_s, b_s, carry) = refs
    da = heads * blk
    c = pl.program_id(1)
    cidx = (nc - 1 - c) if reverse else c

    @pl.when(c == 0)
    def _():
        carry[...] = jnp.broadcast_to(h0_ref[0], carry.shape)

    main = main_ref[0]
    zero8 = jnp.zeros((SUBLANES, da), F32)
    ebuf[0:SUBLANES, :] = jnp.where(cidx == 0, zero8, prev_ref[0])
    ebuf[SUBLANES:SUBLANES + tl, :] = main
    ebuf[SUBLANES + tl:2 * SUBLANES + tl, :] = jnp.where(cidx == nc - 1, zero8, next_ref[0])
    cw = cw_ref[...]
    off = SUBLANES - CONV_A_LEFT
    xc_s[...] = (cw[0:1] * ebuf[off:off + tl, :] + cw[1:2] * ebuf[off + 1:off + 1 + tl, :]
                 + cw[2:3] * main + cw[3:4] * ebuf[off + 3:off + 3 + tl, :] + cb_ref[...])

    z = -lam_ref[...]
    sp = jnp.maximum(z, 0.0) + jnp.log1p(jnp.exp(-jnp.abs(z)))
    spb = blk // LANES
    rc = min(tl, 128)
    for r0 in range(0, tl, rc):
        for h in range(heads):
            cs = slice(h * blk, (h + 1) * blk)
            xh = xc_s[r0:r0 + rc, cs]
            zz = jnp.dot(xh.astype(BF16), wri_ref[h], preferred_element_type=F32)
            r = _sigmoid(zz[:, :blk] + br_ref[:, cs])
            i = _sigmoid(zz[:, blk:] + bi_ref[:, cs])
            log_a = (-LRU_C * r) * sp[:, cs]
            a = jnp.exp(log_a)
            v = jnp.tanh(-log_a) * (1.0 + a * a)
            b = jnp.where(v > 0.0, v * lax.rsqrt(v), 0.0) * (i * xh)
            for q in range(spb):
                a_s[h * spb + q, r0:r0 + rc, :] = a[:, q * LANES:(q + 1) * LANES]
                b_s[h * spb + q, r0:r0 + rc, :] = b[:, q * LANES:(q + 1) * LANES]

    n_slab = da // LANES
    sub_rows = SUBLANES * SCAN_SEG
    n_sub = tl // sub_rows
    row = lax.broadcasted_iota(I32, (SUBLANES, LANES), 0)
    ks = list(range(SCAN_SEG - 1, -1, -1)) if reverse else list(range(SCAN_SEG))

    def seg_scan(at, bt):
        for s in (1, 2, 4):
            if reverse:
                keep = row < (SUBLANES - s)
                sh = SUBLANES - s
            else:
                keep = row >= s
                sh = s
            a_sh = jnp.where(keep, pltpu.roll(at, sh, 0), 1.0)
            b_sh = jnp.where(keep, pltpu.roll(bt, sh, 0), 0.0)
            bt = at * b_sh + bt
            at = at * a_sh
        return at, bt

    def body(si, hcs):
        sub = (n_sub - 1 - si) if reverse else si
        base = sub * sub_rows
        out = []
        for slab in range(n_slab):
            hc = hcs[slab]
            rows = [pl.ds(base + k, SUBLANES, stride=SCAN_SEG) for k in range(SCAN_SEG)]
            acc_a, acc_b = {}, {}
            a_run = b_run = None
            for k in ks:
                ak = a_s[slab, rows[k], :]
                bk = b_s[slab, rows[k], :]
                if a_run is None:
                    a_run, b_run = ak, bk
                else:
                    b_run = ak * b_run + bk
                    a_run = ak * a_run
                acc_a[k], acc_b[k] = a_run, b_run
            at, bt = seg_scan(a_run, b_run)
            h_out = at * hc + bt
            if reverse:
                h_in = jnp.where(row < SUBLANES - 1, pltpu.roll(h_out, SUBLANES - 1, 0), hc)
                edge = h_out[0:1, :]
            else:
                h_in = jnp.where(row >= 1, pltpu.roll(h_out, 1, 0), hc)
                edge = h_out[SUBLANES - 1:SUBLANES, :]
            for k in ks:
                b_s[slab, rows[k], :] = acc_a[k] * h_in + acc_b[k]
            out.append(jnp.broadcast_to(edge, (SUBLANES, LANES)))
        return tuple(out)

    hcs = tuple(carry[:, slab * LANES:(slab + 1) * LANES] for slab in range(n_slab))
    hcs = lax.fori_loop(0, n_sub, body, hcs)
    for slab in range(n_slab):
        ls = slice(slab * LANES, (slab + 1) * LANES)
        carry[:, ls] = hcs[slab]
        hlast_ref[0, :, ls] = hcs[slab][0:1, :]
        if combine:
            hsum = hf_ref[0, :, ls] + b_s[slab]
            out_ref[0, :, ls] = (_gelu_tanh(ag_ref[0, :, ls].astype(F32)) * hsum).astype(out_ref.dtype)
        else:
            out_ref[0, :, ls] = b_s[slab]


def _lru_call(ax, h0, cw, cb, wri, br, bi, lam, *, reverse, hf=None, ag=None):
    bsz, n, da = ax.shape
    heads, blk, _ = wri.shape
    tl = _pick(n, 512, SUBLANES * SCAN_SEG)
    assert blk % LANES == 0 and tl % (SUBLANES * SCAN_SEG) == 0
    nc = n // tl
    nb8 = n // SUBLANES
    g8 = tl // SUBLANES
    combine = hf is not None

    def cidx(c):
        return (nc - 1 - c) if reverse else c

    main_map = lambda b, c: (b, cidx(c), 0)
    prev_map = lambda b, c: (b, jnp.maximum(cidx(c) * g8 - 1, 0), 0)
    next_map = lambda b, c: (b, jnp.minimum((cidx(c) + 1) * g8, nb8 - 1), 0)
    const2 = lambda b, c: (0, 0)
    in_specs = [pl.BlockSpec((1, SUBLANES, da), prev_map),
                pl.BlockSpec((1, tl, da), main_map),
                pl.BlockSpec((1, SUBLANES, da), next_map),
                pl.BlockSpec((4, da), const2),
                pl.BlockSpec((1, da), const2),
                pl.BlockSpec((heads, blk, 2 * blk), lambda b, c: (0, 0, 0)),
                pl.BlockSpec((1, da), const2),
                pl.BlockSpec((1, da), const2),
                pl.BlockSpec((1, da), const2),
                pl.BlockSpec((1, 1, da), lambda b, c: (b, 0, 0))]
    args = [ax, ax, ax, cw, cb, wri, br, bi, lam, h0]
    if combine:
        in_specs += [pl.BlockSpec((1, tl, da), main_map), pl.BlockSpec((1, tl, da), main_map)]
        args += [hf, ag]
    out_dtype = BF16 if combine else F32
    return pl.pallas_call(
        functools.partial(_lru_kernel, reverse=reverse, combine=combine, nc=nc, tl=tl,
                          heads=heads, blk=blk),
        out_shape=(_sds((bsz, n, da), out_dtype), _sds((bsz, 1, da), F32)),
        grid=(bsz, nc),
        in_specs=in_specs,
        out_specs=(pl.BlockSpec((1, tl, da), main_map),
                   pl.BlockSpec((1, 1, da), lambda b, c: (b, 0, 0))),
        scratch_shapes=[pltpu.VMEM((tl + 2 * SUBLANES, da), F32),
                        pltpu.VMEM((tl, da), F32),
                        pltpu.VMEM((da // LANES, tl, LANES), F32),
                        pltpu.VMEM((da // LANES, tl, LANES), F32),
                        pltpu.VMEM((SUBLANES, da), F32)],
        compiler_params=_params(("arbitrary", "arbitrary")),
        name=("lru_bwd" if reverse else "lru_fwd") + ("_mix" if combine else ""),
    )(*args)


def _mixout_kernel(ya_ref, bb_ref, p_ref, pu_ref, pd_ref, cbw_ref, ga_ref, gb_ref, wo_ref, x_ref,
                   gt_ref, gf_ref, sh_ref, sc_ref, wr_ref, brt_ref,
                   x1_ref, xp_ref, te_ref, tw_ref, *, tm, d_a, d_b, n_tiles):
    i = pl.program_id(1)
    half = d_b // 2
    z = p_ref[0].astype(F32)
    w = cbw_ref[...]
    zh = z[:, :half]
    col = lax.broadcasted_iota(I32, (tm, half), 0) % GRID_W
    left = jnp.where(col >= 1, pltpu.roll(zh, 1, 0), 0.0)
    right = jnp.where(col <= GRID_W - 2, pltpu.roll(zh, tm - 1, 0), 0.0)
    horiz = w[0:1, :half] * left + w[1:2, :half] * zh + w[2:3, :half] * right
    zv = z[:, half:]
    up_halo = jnp.where(i == 0, 0.0, pu_ref[0].astype(F32))
    dn_halo = jnp.where(i == n_tiles - 1, 0.0, pd_ref[0].astype(F32))
    if tm > GRID_W:
        up = jnp.concatenate([up_halo, zv[:tm - GRID_W]], axis=0)
        dn = jnp.concatenate([zv[GRID_W:], dn_halo], axis=0)
    else:
        up, dn = up_halo, dn_halo
    vert = w[0:1, half:] * up + w[1:2, half:] * zv + w[2:3, half:] * dn
    bb = bb_ref[0].astype(F32)
    yb = jnp.concatenate([bb[:, :half] * horiz, bb[:, half:] * vert], axis=1)
    ya = ya_ref[0].astype(F32)
    ya_n = _rms(ya, ga_ref[...]).astype(BF16)
    yb_n = _rms(yb, gb_ref[...]).astype(BF16)
    mix = (jnp.dot(ya_n, wo_ref[0:d_a, :], preferred_element_type=F32)
           + jnp.dot(yb_n, wo_ref[d_a:d_a + d_b, :], preferred_element_type=F32))
    x1 = x_ref[0] + gt_ref[0] * mix
    x1_ref[0] = x1
    xn = _rms(x1, gf_ref[...]) * (1.0 + sc_ref[0]) + sh_ref[0]
    xb = xn.astype(BF16)
    _store_token_tiles(xp_ref.at[0], _pack_pairs(xn), tm)
    logits = jnp.dot(xb, wr_ref[...], preferred_element_type=F32) + brt_ref[...]
    lane = lax.broadcasted_iota(I32, logits.shape, 1)
    lane_f = lane.astype(F32)
    vals = logits
    tv, te = [], []
    for _ in range(TOP_K):
        m = jnp.max(vals, axis=-1, keepdims=True)
        idx = jnp.min(jnp.where(vals == m, lane_f, float(LANES)), axis=-1, keepdims=True)
        tv.append(m)
        te.append(idx)
        vals = jnp.where(lane_f == idx, -jnp.inf, vals)
    ex = [jnp.exp(v - tv[0]) for v in tv]
    den = ex[0]
    for e in ex[1:]:
        den = den + e
    e_out = jnp.zeros(logits.shape, I32)
    w_out = jnp.zeros(logits.shape, F32)
    for k in range(TOP_K):
        e_out = jnp.where(lane == k, te[k].astype(I32), e_out)
        w_out = jnp.where(lane == k, ex[k] / den, w_out)
    te_ref[0] = e_out
    tw_ref[0] = w_out


def _mixout_call(ya, bb, p, cbw, ga, gb, wo_bf, x, gt1, gf, sh2, sc2, wr_bf, brt):
    bsz, n, d = x.shape
    d_a = ya.shape[-1]
    d_b = bb.shape[-1]
    half = d_b // 2
    tm = _pick(n, 512, GRID_W)
    n_tiles = n // tm
    nseg = (d // 2) // LANES
    rpt = tm // GRID_W
    n_rows = n // GRID_W
    row = lambda b, i: (b, i, 0)
    vec = lambda b, i: (b, 0, 0)
    const2 = lambda b, i: (0, 0)
    return pl.pallas_call(
        functools.partial(_mixout_kernel, tm=tm, d_a=d_a, d_b=d_b, n_tiles=n_tiles),
        out_shape=(_sds((bsz, n, d), F32), _sds((bsz, n * nseg, LANES), U32),
                   _sds((bsz, n, LANES), I32), _sds((bsz, n, LANES), F32)),
        grid=(bsz, n_tiles),
        in_specs=[pl.BlockSpec((1, tm, d_a), row),
                  pl.BlockSpec((1, tm, d_b), row),
                  pl.BlockSpec((1, tm, d_b), row),
                  pl.BlockSpec((1, GRID_W, half), lambda b, i: (b, jnp.maximum(i * rpt - 1, 0), 1)),
                  pl.BlockSpec((1, GRID_W, half), lambda b, i: (b, jnp.minimum((i + 1) * rpt, n_rows - 1), 1)),
                  pl.BlockSpec((3, d_b), const2),
                  pl.BlockSpec((1, d_a), const2),
                  pl.BlockSpec((1, d_b), const2),
                  pl.BlockSpec((d_a + d_b, d), const2),
                  pl.BlockSpec((1, tm, d), row),
                  pl.BlockSpec((1, 1, d), vec),
                  pl.BlockSpec((1, d), const2),
                  pl.BlockSpec((1, 1, d), vec),
                  pl.BlockSpec((1, 1, d), vec),
                  pl.BlockSpec((d, LANES), const2),
                  pl.BlockSpec((1, LANES), const2)],
        out_specs=(pl.BlockSpec((1, tm, d), row), pl.BlockSpec((1, tm * nseg, LANES), row),
                   pl.BlockSpec((1, tm, LANES), row), pl.BlockSpec((1, tm, LANES), row)),
        compiler_params=_params(("arbitrary", "arbitrary")),
        name="mixout",
    )(ya, bb, p, p, p, cbw, ga, gb, wo_bf, x, gt1, gf, sh2, sc2, wr_bf, brt)


def _rank_kernel(e_ref, rank_ref, cnt_ref, carry, *, tt):
    @pl.when(pl.program_id(0) == 0)
    def _():
        carry[...] = jnp.zeros(carry.shape, F32)

    e = e_ref[...]
    lane = lax.broadcasted_iota(I32, (tt, LANES), 1)
    ohs = []
    m = jnp.zeros((tt, LANES), F32)
    for k in range(TOP_K):
        oh = lane == e[:, k:k + 1]
        ohs.append(oh)
        m = m + jnp.where(oh, 1.0, 0.0)
    ri = lax.broadcasted_iota(I32, (tt, tt), 0)
    ci = lax.broadcasted_iota(I32, (tt, tt), 1)
    ltri = jnp.where(ri > ci, 1.0, 0.0).astype(BF16)
    pref = jnp.dot(ltri, m.astype(BF16), preferred_element_type=F32) + carry[0:1, :]
    out = jnp.zeros((tt, LANES), I32)
    for k in range(TOP_K):
        rk = jnp.sum(jnp.where(ohs[k], pref, 0.0), axis=-1, keepdims=True)
        out = jnp.where(lane == k, rk.astype(I32), out)
    rank_ref[...] = out
    tot = carry[0:1, :] + jnp.sum(m, axis=0, keepdims=True)
    carry[...] = jnp.broadcast_to(tot, carry.shape)
    cnt_ref[...] = jnp.broadcast_to(tot, cnt_ref.shape)


def _rank_call(top_e):
    t = top_e.shape[0]
    tt = _pick(t, 512, SUBLANES)
    return pl.pallas_call(
        functools.partial(_rank_kernel, tt=tt),
        out_shape=(_sds((t, LANES), I32), _sds((SUBLANES, LANES), F32)),
        grid=(t // tt,),
        in_specs=[pl.BlockSpec((tt, LANES), lambda i: (i, 0))],
        out_specs=(pl.BlockSpec((tt, LANES), lambda i: (i, 0)),
                   pl.BlockSpec((SUBLANES, LANES), lambda i: (0, 0))),
        scratch_shapes=[pltpu.VMEM((SUBLANES, LANES), F32)],
        compiler_params=_params(("arbitrary",)),
        name="rank",
    )(top_e)


def _zero_tail(first, n_blocks, zbuf, dst_block, sem):
    zbuf[...] = jnp.zeros(zbuf.shape, zbuf.dtype)

    def start(c, carry):
        pltpu.make_async_copy(zbuf, dst_block(c), sem).start()
        return carry

    def wait(c, carry):
        pltpu.make_async_copy(zbuf, dst_block(c), sem).wait()
        return carry

    lax.fori_loop(first, n_blocks, start, 0)
    lax.fori_loop(first, n_blocks, wait, 0)


def _dispatch_kernel(bs_ref, nb_ref, pos_ref, x_ref, o_hbm, zbuf, sem, zsem, *, td, rb, n_e, n_blocks, nseg):
    rbr = rb * nseg

    def blk(b):
        return pl.ds(pl.multiple_of(b * rbr, rbr), rbr)

    def tok(t):
        return pl.ds(pl.multiple_of(t * nseg, nseg), nseg)

    @pl.when(pl.program_id(0) == 0)
    def _():
        _zero_tail(bs_ref[n_e - 1] + nb_ref[n_e - 1], n_blocks, zbuf, lambda c: o_hbm.at[blk(c)], zsem)

        def zero_copy(e):
            return pltpu.make_async_copy(zbuf, o_hbm.at[blk(bs_ref[e] + nb_ref[e] - 1)], zsem)

        for e in range(n_e):
            @pl.when(nb_ref[e] > 0)
            def _():
                zero_copy(e).start()

        for e in range(n_e):
            @pl.when(nb_ref[e] > 0)
            def _():
                zero_copy(e).wait()

    def body(t, c):
        src = x_ref.at[tok(t)]
        for k in range(TOP_K):
            pltpu.make_async_copy(src, o_hbm.at[tok(pos_ref[0, 0, t * TOP_K + k])], sem).start(priority=k % 2)
        return c

    lax.fori_loop(0, td, body, 0, unroll=2)
    for _ in range(TOP_K):
        pltpu.make_async_copy(x_ref, o_hbm.at[pl.ds(0, td * nseg)], sem).wait()


def _dispatch_call(groups, xp, pos, n_buf, rb, nseg):
    bstart, nblk = groups
    n_e = bstart.shape[0]
    t = xp.shape[0] // nseg
    td = _pick(t, 512, SUBLANES)
    pos3 = pos.reshape(t // td, 1, td * TOP_K)
    return pl.pallas_call(
        functools.partial(_dispatch_kernel, td=td, rb=rb, n_e=n_e, n_blocks=n_buf // rb, nseg=nseg),
        out_shape=_sds((n_buf * nseg, LANES), U32),
        grid_spec=pltpu.PrefetchScalarGridSpec(
            num_scalar_prefetch=2,
            grid=(t // td,),
            in_specs=[pl.BlockSpec((1, 1, td * TOP_K), lambda i, bs, nb: (i, 0, 0),
                                   memory_space=pltpu.SMEM),
                      pl.BlockSpec((td * nseg, LANES), lambda i, bs, nb: (i, 0))],
            out_specs=pl.BlockSpec(memory_space=pl.ANY),
            scratch_shapes=[pltpu.VMEM((rb * nseg, LANES), U32), pltpu.SemaphoreType.DMA,
                            pltpu.SemaphoreType.DMA]),
        compiler_params=_params(("arbitrary",)),
        name="dispatch",
    )(bstart, nblk, pos3, xp)


GROUP_IN_SLOTS = 4
GROUP_OUT_SLOTS = 3


def _build_chunks(bs_ref, nb_ref, ce, cs, cb, cn, n_split):
    def per_expert(e, j):
        nb = nb_ref[e]
        b0 = bs_ref[e]

        def per_slab(s_, j):
            def per_pair(c, j):
                ce[j] = e
                cs[j] = s_
                cb[j] = b0 + 2 * c
                cn[j] = jnp.minimum(nb - 2 * c, 2)
                return j + 1

            return lax.fori_loop(0, (nb + 1) // 2, per_pair, j)

        return lax.fori_loop(0, n_split, per_slab, j)

    return lax.fori_loop(0, bs_ref.shape[0], per_expert, jnp.int32(0))


def _chunk_loop(n, nb_ref, ce, cs, cn, in_copy, out_copy, w_copies, load_weights, compute):
    ahead = GROUP_IN_SLOTS - 1

    def by_size(j, fn):
        @pl.when(cn[j] == 2)
        def _():
            fn(2)

        @pl.when(cn[j] == 1)
        def _():
            fn(1)

    @pl.when(n > 0)
    def _():
        for cp in w_copies(ce[0], cs[0]):
            cp.start()
        for j in range(ahead):
            @pl.when(j < n)
            def _():
                by_size(j, lambda m: in_copy(j, j, m).start())

        def body(j, carry):
            jp = jnp.maximum(j - 1, 0)
            first = jnp.logical_or(j == 0, jnp.logical_or(ce[j] != ce[jp], cs[j] != cs[jp]))

            @pl.when(first)
            def _():
                for cp in w_copies(ce[j], cs[j]):
                    cp.wait()

            islot = lax.rem(j, GROUP_IN_SLOTS)
            oslot = lax.rem(j, GROUP_OUT_SLOTS)

            @pl.when(j + ahead < n)
            def _():
                ja = j + ahead
                by_size(ja, lambda m: in_copy(ja, lax.rem(ja, GROUP_IN_SLOTS), m).start())

            by_size(j, lambda m: in_copy(j, islot, m).wait())

            @pl.when(j >= GROUP_OUT_SLOTS)
            def _():
                jo = j - GROUP_OUT_SLOTS
                by_size(jo, lambda m: out_copy(jo, oslot, m).wait())

            @pl.when(first)
            def _():
                def run(m):
                    load_weights()
                    compute(j, islot, oslot, m)
                    out_copy(j, oslot, m).start()

                by_size(j, run)
                jn = j + (nb_ref[ce[j]] + 1) // 2

                @pl.when(jn < n)
                def _():
                    for cp in w_copies(ce[jn], cs[jn]):
                        cp.start()

            @pl.when(jnp.logical_not(first))
            def _():
                def run(m):
                    compute(j, islot, oslot, m)
                    out_copy(j, oslot, m).start()

                by_size(j, run)

            return carry

        lax.fori_loop(0, n, body, 0)

        for k in range(GROUP_OUT_SLOTS):
            @pl.when(n > k)
            def _():
                jl = n - 1 - k
                by_size(jl, lambda m: out_copy(jl, lax.rem(jl, GROUP_OUT_SLOTS), m).wait())


def _gmm1_kernel(bs_ref, nb_ref, xs_hbm, wg_hbm, wu_hbm, bg_ref, bu_ref, act_hbm,
                 xbuf, obuf, wraw, wbf, zbuf, ce, cs, cb, cn, sin, sout, wsem, zsem,
                 *, rb, tf, n_blocks, n_split, nseg):
    def blk(b, m=1):
        return pl.ds(pl.multiple_of(b * rb, rb), m * rb)

    def tok_blk(b, m=1):
        return pl.ds(pl.multiple_of(b * (rb * nseg), rb * nseg), m * rb * nseg)

    def cols(s_):
        return pl.ds(pl.multiple_of(s_ * tf, tf), tf)

    n_e = bs_ref.shape[0]
    for s_ in range(n_split):
        _zero_tail(bs_ref[n_e - 1] + nb_ref[n_e - 1], n_blocks, zbuf,
                   lambda c: act_hbm.at[blk(c), cols(s_)], zsem)
    n = _build_chunks(bs_ref, nb_ref, ce, cs, cb, cn, n_split)

    def in_copy(j, slot, m):
        return pltpu.make_async_copy(xs_hbm.at[tok_blk(cb[j], m)],
                                     xbuf.at[slot, pl.ds(0, m * rb * nseg)], sin.at[slot])

    def out_copy(j, slot, m):
        return pltpu.make_async_copy(obuf.at[slot, pl.ds(0, m * rb)],
                                     act_hbm.at[blk(cb[j], m), cols(cs[j])], sout.at[slot])

    def w_copies(e, s_):
        return (pltpu.make_async_copy(wg_hbm.at[e, :, cols(s_)], wraw.at[0], wsem),
                pltpu.make_async_copy(wu_hbm.at[e, :, cols(s_)], wraw.at[1], wsem))

    def load_weights():
        wbf[:, 0:tf] = wraw[0].astype(BF16)
        wbf[:, tf:2 * tf] = wraw[1].astype(BF16)

    def compute(j, islot, oslot, m):
        g = ce[j] * n_split + cs[j]
        lo, hi = _unpack_pairs(_load_token_tiles(xbuf.at[islot], m * rb, nseg))
        h = jnp.concatenate([lo.astype(BF16), hi.astype(BF16)], axis=1)
        gu = jnp.dot(h, wbf[...], preferred_element_type=F32)
        gate = jnp.minimum(gu[:, :tf] + bg_ref[g], SWIGLU_LIMIT)
        up = jnp.clip(gu[:, tf:] + bu_ref[g], -SWIGLU_LIMIT, SWIGLU_LIMIT)
        act = (up + 1.0) * gate * _sigmoid(SWIGLU_ALPHA * gate)
        obuf[oslot, 0:m * rb, :] = act.astype(BF16)

    _chunk_loop(n, nb_ref, ce, cs, cn, in_copy, out_copy, w_copies, load_weights, compute)


def _gmm1_call(groups, xs, wg, wu, bg, bu, rb, tf, nseg):
    n_buf = xs.shape[0] // nseg
    n_e, d, dff = wg.shape
    n_split = dff // tf
    max_chunks = n_split * (n_buf // rb)
    vmem_full = lambda shape: pl.BlockSpec(shape, lambda i, *_: (0,) * len(shape))
    return pl.pallas_call(
        functools.partial(_gmm1_kernel, rb=rb, tf=tf, n_blocks=n_buf // rb, n_split=n_split, nseg=nseg),
        out_shape=_sds((n_buf, dff), BF16),
        grid_spec=pltpu.PrefetchScalarGridSpec(
            num_scalar_prefetch=2,
            grid=(1,),
            in_specs=[pl.BlockSpec(memory_space=pl.ANY),
                      pl.BlockSpec(memory_space=pl.ANY),
                      pl.BlockSpec(memory_space=pl.ANY),
                      vmem_full((n_e * n_split, 1, tf)),
                      vmem_full((n_e * n_split, 1, tf))],
            out_specs=pl.BlockSpec(memory_space=pl.ANY),
            scratch_shapes=[pltpu.VMEM((GROUP_IN_SLOTS, 2 * rb * nseg, LANES), U32),
                            pltpu.VMEM((GROUP_OUT_SLOTS, 2 * rb, tf), BF16),
                            pltpu.VMEM((2, d, tf), F32),
                            pltpu.VMEM((d, 2 * tf), BF16),
                            pltpu.VMEM((rb, tf), BF16),
                            pltpu.SMEM((max_chunks,), I32),
                            pltpu.SMEM((max_chunks,), I32),
                            pltpu.SMEM((max_chunks,), I32),
                            pltpu.SMEM((max_chunks,), I32),
                            pltpu.SemaphoreType.DMA((GROUP_IN_SLOTS,)),
                            pltpu.SemaphoreType.DMA((GROUP_OUT_SLOTS,)),
                            pltpu.SemaphoreType.DMA,
                            pltpu.SemaphoreType.DMA]),
        compiler_params=_params(("arbitrary",)),
        name="gmm1",
    )(*groups, xs, wg, wu, bg.reshape(n_e * n_split, 1, tf), bu.reshape(n_e * n_split, 1, tf))


def _gmm2_kernel(bs_ref, nb_ref, act_hbm, wd_hbm, bd_ref, y_hbm,
                 abuf, ybuf, wraw, wbf, zbuf, ce, cs, cb, cn, sin, sout, wsem, zsem, *, rb, n_blocks, nseg):
    def blk(b, m=1):
        return pl.ds(pl.multiple_of(b * rb, rb), m * rb)

    def tok_blk(b, m=1):
        return pl.ds(pl.multiple_of(b * (rb * nseg), rb * nseg), m * rb * nseg)

    n_e = bs_ref.shape[0]
    _zero_tail(bs_ref[n_e - 1] + nb_ref[n_e - 1], n_blocks, zbuf, lambda c: y_hbm.at[tok_blk(c)], zsem)
    n = _build_chunks(bs_ref, nb_ref, ce, cs, cb, cn, 1)

    def in_copy(j, slot, m):
        return pltpu.make_async_copy(act_hbm.at[blk(cb[j], m)], abuf.at[slot, pl.ds(0, m * rb)],
                                     sin.at[slot])

    def out_copy(j, slot, m):
        return pltpu.make_async_copy(ybuf.at[slot, pl.ds(0, m * rb * nseg)],
                                     y_hbm.at[tok_blk(cb[j], m)], sout.at[slot])

    def w_copies(e, s_):
        return (pltpu.make_async_copy(wd_hbm.at[e], wraw, wsem),)

    def load_weights():
        wbf[...] = wraw[...].astype(BF16)

    def compute(j, islot, oslot, m):
        y = jnp.dot(abuf[islot, 0:m * rb, :], wbf[...], preferred_element_type=F32) + bd_ref[ce[j]]
        _store_token_tiles(ybuf.at[oslot], _pack_pairs(y), m * rb)

    _chunk_loop(n, nb_ref, ce, cs, cn, in_copy, out_copy, w_copies, load_weights, compute)


def _gmm2_call(groups, act, wd, bd, rb, nseg):
    n_buf, dff = act.shape
    n_e, _, d = wd.shape
    assert d == 2 * nseg * LANES
    max_chunks = n_buf // rb
    vmem_full = lambda shape: pl.BlockSpec(shape, lambda i, *_: (0,) * len(shape))
    return pl.pallas_call(
        functools.partial(_gmm2_kernel, rb=rb, n_blocks=n_buf // rb, nseg=nseg),
        out_shape=_sds((n_buf * nseg, LANES), U32),
        grid_spec=pltpu.PrefetchScalarGridSpec(
            num_scalar_prefetch=2,
            grid=(1,),
            in_specs=[pl.BlockSpec(memory_space=pl.ANY),
                      pl.BlockSpec(memory_space=pl.ANY),
                      vmem_full((n_e, 1, d))],
            out_specs=pl.BlockSpec(memory_space=pl.ANY),
            scratch_shapes=[pltpu.VMEM((GROUP_IN_SLOTS, 2 * rb, dff), BF16),
                            pltpu.VMEM((GROUP_OUT_SLOTS, 2 * rb * nseg, LANES), U32),
                            pltpu.VMEM((dff, d), F32),
                            pltpu.VMEM((dff, d), BF16),
                            pltpu.VMEM((rb * nseg, LANES), U32),
                            pltpu.SMEM((max_chunks,), I32),
                            pltpu.SMEM((max_chunks,), I32),
                            pltpu.SMEM((max_chunks,), I32),
                            pltpu.SMEM((max_chunks,), I32),
                            pltpu.SemaphoreType.DMA((GROUP_IN_SLOTS,)),
                            pltpu.SemaphoreType.DMA((GROUP_OUT_SLOTS,)),
                            pltpu.SemaphoreType.DMA,
                            pltpu.SemaphoreType.DMA]),
        compiler_params=_params(("arbitrary",)),
        name="gmm2",
    )(*groups, act, wd, bd.reshape(n_e, 1, d))


def _final_kernel(pos_ref, posn_ref, y_hbm, tw_ref, x1_ref, gt_ref, gfin_ref, o_ref, ybuf, sem, *,
                  tc, n_steps, nseg):
    step = pl.program_id(0) * pl.num_programs(1) + pl.program_id(1)

    def tok(t):
        return pl.ds(pl.multiple_of(t * nseg, nseg), nseg)

    def issue(p_ref, slot):
        def body(t, c):
            for k in range(TOP_K):
                src = y_hbm.at[tok(p_ref[0, 0, t * TOP_K + k])]
                pltpu.make_async_copy(src, ybuf.at[slot, k, tok(t)], sem.at[slot]).start(priority=k % 2)
            return c

        lax.fori_loop(0, tc, body, 0, unroll=2)

    def run(cur):
        @pl.when(step + 1 < n_steps)
        def _():
            issue(posn_ref, 1 - cur)

        for k in range(TOP_K):
            pltpu.make_async_copy(y_hbm.at[pl.ds(0, tc * nseg)], ybuf.at[cur, k], sem.at[cur]).wait()
        tw = tw_ref[0]
        lo, hi = None, None
        for k in range(TOP_K):
            lo_k, hi_k = _unpack_pairs(_load_token_tiles(ybuf.at[cur, k], tc, nseg))
            wk = tw[:, k:k + 1]
            lo = wk * lo_k if lo is None else lo + wk * lo_k
            hi = wk * hi_k if hi is None else hi + wk * hi_k
        moe = jnp.concatenate([lo, hi], axis=1)
        x2 = x1_ref[0] + gt_ref[0] * moe
        o_ref[0] = _rms(x2, gfin_ref[...])

    @pl.when(step == 0)
    def _():
        issue(pos_ref, 0)

    parity = lax.rem(step, 2)

    @pl.when(parity == 0)
    def _():
        run(0)

    @pl.when(parity == 1)
    def _():
        run(1)


def _final_call(y, pos, top_w, x1, gt2, gfin, nseg):
    bsz, n, d = x1.shape
    tc = _pick(n, 512, SUBLANES)
    nt = n // tc
    n_steps = bsz * nt
    pos3 = pos.reshape(n_steps, 1, tc * TOP_K)
    row = lambda b, i: (b, i, 0)
    return pl.pallas_call(
        functools.partial(_final_kernel, tc=tc, n_steps=n_steps, nseg=nseg),
        out_shape=_sds((bsz, n, d), F32),
        grid=(bsz, nt),
        in_specs=[pl.BlockSpec((1, 1, tc * TOP_K), lambda b, i: (b * nt + i, 0, 0), memory_space=pltpu.SMEM),
                  pl.BlockSpec((1, 1, tc * TOP_K), lambda b, i: (jnp.minimum(b * nt + i + 1, n_steps - 1), 0, 0),
                               memory_space=pltpu.SMEM),
                  pl.BlockSpec(memory_space=pl.ANY),
                  pl.BlockSpec((1, tc, LANES), row),
                  pl.BlockSpec((1, tc, d), row),
                  pl.BlockSpec((1, 1, d), lambda b, i: (b, 0, 0)),
                  pl.BlockSpec((1, d), lambda b, i: (0, 0))],
        out_specs=pl.BlockSpec((1, tc, d), row),
        scratch_shapes=[pltpu.VMEM((2, TOP_K, tc * nseg, LANES), U32), pltpu.SemaphoreType.DMA((2,))],
        compiler_params=_params(("arbitrary", "arbitrary")),
        name="final",
    )(pos3, pos3, y, top_w, x1, gt2, gfin)


def _group_tables(counts, rb):
    nblk = (counts + rb - 1) // rb
    bstart = jnp.cumsum(nblk) - nblk
    return bstart * rb, (bstart.astype(I32), nblk.astype(I32))


def kernel(x, c, ctx, c_ctx, w_mod, b_mod, g_mix, w_in, conv_a_w, conv_a_b, lru_w_r, lru_b_r,
           lru_w_i, lru_b_i, lru_lam, conv_b_w, g_out_a, g_out_b, w_out, g_ffn, w_router,
           b_router, w_gate, b_gate, w_up, b_up, w_down, b_down, g_final):
    assert w_mod.shape[0] == 1, "single-layer block"
    bsz, n_lat, d = x.shape
    d_a = conv_a_w.shape[-1]
    d_b = conv_b_w.shape[-1]
    n_e = w_router.shape[-1]
    assert n_lat % GRID_W == 0 and n_e <= LANES and d % (2 * LANES) == 0
    l = 0

    cs = jnp.zeros((SUBLANES, d), F32).at[:bsz].set(c).at[bsz].set(c_ctx)
    mod = _mod_call(cs, w_mod[l], b_mod[l][None])
    sh1, sc1, gt1, sh2, sc2, gt2 = [m[:bsz, None, :] for m in jnp.split(mod, 6, axis=-1)]
    ssh1, ssc1 = [jnp.broadcast_to(m[bsz][None, None, :], (bsz, 1, d))
                  for m in jnp.split(mod, 6, axis=-1)[:2]]

    w_in_bf = w_in[l].astype(BF16)
    g_mix2 = g_mix[l][None]
    cw = conv_a_w[l]
    cb = conv_a_b[l][None]
    wri = jnp.concatenate([lru_w_r[l], lru_w_i[l]], axis=-1).astype(BF16)
    br, bi, lam = lru_b_r[l], lru_b_i[l], lru_lam[l]
    lru_p = lambda dr: (cw, cb, wri[dr], br[dr][None], bi[dr][None], lam[dr][None])

    (s_ax,) = _inproj_call(ctx, g_mix2, ssh1, ssc1, w_in_bf[:, d_a:2 * d_a], d_a, d_b, latent=False)
    zero_state = jnp.zeros((bsz, 1, d_a), F32)
    _, h0f = _lru_call(s_ax, zero_state, *lru_p(0), reverse=False)
    _, h0b = _lru_call(s_ax, zero_state, *lru_p(1), reverse=True)

    ag, ax, bb, p = _inproj_call(x, g_mix2, sh1, sc1, w_in_bf, d_a, d_b, latent=True)
    hf, _ = _lru_call(ax, h0f, *lru_p(0), reverse=False)
    ya, _ = _lru_call(ax, h0b, *lru_p(1), reverse=True, hf=hf, ag=ag)

    wr_bf = jnp.zeros((d, LANES), BF16).at[:, :n_e].set(w_router[l].astype(BF16))
    brt = jnp.full((1, LANES), NEG_BIG, F32).at[0, :n_e].set(b_router[l])
    x1, xp, top_e, top_w = _mixout_call(
        ya, bb, p, conv_b_w[l], g_out_a[l][None], g_out_b[l][None], w_out[l].astype(BF16), x, gt1,
        g_ffn[l][None], sh2, sc2, wr_bf, brt)

    n_tok = bsz * n_lat
    n_rows = n_tok * TOP_K
    rb = MOE_ROW_BLOCK
    n_buf = n_rows + n_e * rb
    rank, cnt = _rank_call(top_e.reshape(n_tok, LANES))
    counts = cnt[0, :n_e].astype(I32)
    starts, groups = _group_tables(counts, rb)
    te = top_e.reshape(n_tok, LANES)[:, :TOP_K]
    onehot = te[:, :, None] == jnp.arange(n_e, dtype=I32)[None, None, :]
    pos = (jnp.sum(jnp.where(onehot, starts[None, None, :], 0), axis=-1) + rank[:, :TOP_K]).astype(I32)
    nseg = (d // 2) // LANES
    xs = _dispatch_call(groups, xp.reshape(n_tok * nseg, LANES), pos, n_buf, rb, nseg)
    d_ff = w_gate.shape[-1]
    tf = _pick(d_ff, 1024)
    act = _gmm1_call(groups, xs, w_gate[l], w_up[l], b_gate[l], b_up[l], rb, tf, nseg)
    y = _gmm2_call(groups, act, w_down[l], b_down[l], rb, nseg)
    return _final_call(y, pos, top_w, x1, gt2, g_final[None], nseg)
```

```python
import functools

import jax
import jax.numpy as jnp
from jax import lax
from jax.experimental import pallas as pl
from jax.experimental.pallas import tpu as pltpu

F32 = jnp.float32
BF16 = jnp.bfloat16
I32 = jnp.int32
U32 = jnp.uint32

GRID_W = 64
TOP_K = 4
LRU_C = 8.0
CONV_A_LEFT = 2
SCAN_SEG = 4
SWIGLU_LIMIT = 7.0
SWIGLU_ALPHA = 1.702
EPS = 1e-6
MOE_ROW_BLOCK = 256

LANES = 128
SUBLANES = 8
VMEM_LIMIT_BYTES = 56 * 1024 * 1024
NEG_BIG = -1e30


def _sds(shape, dtype):
    return jax.ShapeDtypeStruct(shape, dtype)


def _pick(n, pref, mult=LANES):
    if n <= pref:
        return n
    t = (pref // mult) * mult
    while t >= mult:
        if n % t == 0:
            return t
        t -= mult
    return n


def _params(sem):
    return pltpu.CompilerParams(dimension_semantics=sem, vmem_limit_bytes=VMEM_LIMIT_BYTES)


def _sigmoid(x):
    return 1.0 / (1.0 + jnp.exp(-x))


def _rms(x, g):
    ms = jnp.mean(x * x, axis=-1, keepdims=True)
    return (x * lax.rsqrt(ms + EPS)) * g


def _pack_pairs(x):
    w = x.shape[1] // 2
    lo = lax.bitcast_convert_type(x[:, :w].astype(BF16).astype(F32), U32)
    hi = lax.bitcast_convert_type(x[:, w:].astype(BF16).astype(F32), U32)
    return lax.shift_right_logical(lo, jnp.uint32(16)) | (hi & jnp.uint32(0xFFFF0000))


def _unpack_pairs(words):
    lo = lax.bitcast_convert_type(lax.shift_left(words, jnp.uint32(16)), F32)
    hi = lax.bitcast_convert_type(words & jnp.uint32(0xFFFF0000), F32)
    return lo, hi


def _store_token_tiles(ref, words, rows):
    nseg = words.shape[1] // LANES
    if nseg == 1:
        ref[...] = words
        return
    for s in range(nseg):
        ref[pl.ds(s, rows, stride=nseg), :] = words[:, s * LANES:(s + 1) * LANES]


def _load_token_tiles(ref, rows, nseg):
    if nseg == 1:
        return ref[...]
    return jnp.concatenate([ref[pl.ds(s, rows, stride=nseg), :] for s in range(nseg)], axis=1)


def _mod_kernel(c_ref, w_ref, b_ref, o_ref):
    c = c_ref[...]
    s = c * _sigmoid(c)
    o_ref[...] = jnp.dot(s.astype(BF16), w_ref[...].astype(BF16),
                         preferred_element_type=F32) + b_ref[...]


def _mod_call(cs, w, b):
    d, n6 = w.shape
    tn = _pick(n6, 1024)
    return pl.pallas_call(
        _mod_kernel,
        out_shape=_sds((cs.shape[0], n6), F32),
        grid=(n6 // tn,),
        in_specs=[pl.BlockSpec((cs.shape[0], d), lambda j: (0, 0)),
                  pl.BlockSpec((d, tn), lambda j: (0, j)),
                  pl.BlockSpec((1, tn), lambda j: (0, j))],
        out_specs=pl.BlockSpec((cs.shape[0], tn), lambda j: (0, j)),
        compiler_params=_params(("arbitrary",)),
        name="mod",
    )(cs, w, b)


def _inproj_kernel(x_ref, g_ref, sh_ref, sc_ref, w_ref, *out_refs, d_a, d_b, latent):
    x = x_ref[0]
    xn = _rms(x, g_ref[...]) * (1.0 + sc_ref[0]) + sh_ref[0]
    xb = xn.astype(BF16)

    def sec(lo, width):
        return jnp.dot(xb, w_ref[:, lo:lo + width], preferred_element_type=F32)

    if not latent:
        out_refs[0][0] = sec(0, d_a)
        return
    ag_ref, ax_ref, bb_ref, p_ref = out_refs
    ag_ref[0] = sec(0, d_a).astype(BF16)
    ax_ref[0] = sec(d_a, d_a)
    bb_ref[0] = sec(2 * d_a, d_b).astype(BF16)
    p_ref[0] = (sec(2 * d_a + d_b, d_b) * sec(2 * d_a + 2 * d_b, d_b)).astype(BF16)


def _inproj_call(x, g, sh, sc, w_bf, d_a, d_b, latent):
    bsz, n, d = x.shape
    tm = _pick(n, 512, SUBLANES)
    n_w = w_bf.shape[1]
    row = lambda b, i: (b, i, 0)
    if latent:
        out_shape = (_sds((bsz, n, d_a), BF16), _sds((bsz, n, d_a), F32),
                     _sds((bsz, n, d_b), BF16), _sds((bsz, n, d_b), BF16))
        out_specs = (pl.BlockSpec((1, tm, d_a), row), pl.BlockSpec((1, tm, d_a), row),
                     pl.BlockSpec((1, tm, d_b), row), pl.BlockSpec((1, tm, d_b), row))
    else:
        out_shape = (_sds((bsz, n, d_a), F32),)
        out_specs = (pl.BlockSpec((1, tm, d_a), row),)
    return pl.pallas_call(
        functools.partial(_inproj_kernel, d_a=d_a, d_b=d_b, latent=latent),
        out_shape=out_shape,
        grid=(bsz, n // tm),
        in_specs=[pl.BlockSpec((1, tm, d), row),
                  pl.BlockSpec((1, d), lambda b, i: (0, 0)),
                  pl.BlockSpec((1, 1, d), lambda b, i: (b, 0, 0)),
                  pl.BlockSpec((1, 1, d), lambda b, i: (b, 0, 0)),
                  pl.BlockSpec((d, n_w), lambda b, i: (0, 0), pipeline_mode=pl.Buffered(1))],
        out_specs=out_specs,
        compiler_params=_params(("arbitrary", "arbitrary")),
        name="inproj_lat" if latent else "inproj_ctx",
    )(x, g, sh, sc, w_bf)


def _gelu_tanh(x):
    c = 0.7978845608028654
    return x * (0.5 * (1.0 + jnp.tanh(c * (x + 0.044715 * (x * x * x)))))


def _lru_kernel(*refs, reverse, combine, nc, tl, heads, blk):
    if combine:
        (prev_ref, main_ref, next_ref, cw_ref, cb_ref, wri_ref, br_ref, bi_ref, lam_ref, h0_ref,
         hf_ref, ag_ref, out_ref, hlast_ref, ebuf, xc_s, a_s, b_s, carry) = refs
    else:
        (prev_ref, main_ref, next_ref, cw_ref, cb_ref, wri_ref, br_ref, bi_ref, lam_ref, h0_ref,
         out_ref, hlast_ref, ebuf, xc_s, a_s, b_s, carry) = refs
    da = heads * blk
    c = pl.program_id(1)
    cidx = (nc - 1 - c) if reverse else c

    @pl.when(c == 0)
    def _():
        carry[...] = jnp.broadcast_to(h0_ref[0], carry.shape)

    main = main_ref[0]
    zero8 = jnp.zeros((SUBLANES, da), F32)
    ebuf[0:SUBLANES, :] = jnp.where(cidx == 0, zero8, prev_ref[0])
    ebuf[SUBLANES:SUBLANES + tl, :] = main
    ebuf[SUBLANES + tl:2 * SUBLANES + tl, :] = jnp.where(cidx == nc - 1, zero8, next_ref[0])
    cw = cw_ref[...]
    off = SUBLANES - CONV_A_LEFT
    xc_s[...] = (cw[0:1] * ebuf[off:off + tl, :] + cw[1:2] * ebuf[off + 1:off + 1 + tl, :]
                 + cw[2:3] * main + cw[3:4] * ebuf[off + 3:off + 3 + tl, :] + cb_ref[...])

    z = -lam_ref[...]
    sp = jnp.maximum(z, 0.0) + jnp.log1p(jnp.exp(-jnp.abs(z)))
    spb = blk // LANES
    rc = min(tl, 128)
    for r0 in range(0, tl, rc):
        for h in range(heads):
            cs = slice(h * blk, (h + 1) * blk)
            xh = xc_s[r0:r0 + rc, cs]
            zz = jnp.dot(xh.astype(BF16), wri_ref[h], preferred_element_type=F32)
            r = _sigmoid(zz[:, :blk] + br_ref[:, cs])
            i = _sigmoid(zz[:, blk:] + bi_ref[:, cs])
            log_a = (-LRU_C * r) * sp[:, cs]
            a = jnp.exp(log_a)
            v = jnp.tanh(-log_a) * (1.0 + a * a)
            b = jnp.where(v > 0.0, v * lax.rsqrt(v), 0.0) * (i * xh)
            for q in range(spb):
                a_s[h * spb + q, r0:r0 + rc, :] = a[:, q * LANES:(q + 1) * LANES]
                b_s[h * spb + q, r0:r0 + rc, :] = b[:, q * LANES:(q + 1) * LANES]

    n_slab = da // LANES
    sub_rows = SUBLANES * SCAN_SEG
    n_sub = tl // sub_rows
    row = lax.broadcasted_iota(I32, (SUBLANES, LANES), 0)
    ks = list(range(SCAN_SEG - 1, -1, -1)) if reverse else list(range(SCAN_SEG))

    def seg_scan(at, bt):
        for s in (1, 2, 4):
            if reverse:
                keep = row < (SUBLANES - s)
                sh = SUBLANES - s
            else:
                keep = row >= s
                sh = s
            a_sh = jnp.where(keep, pltpu.roll(at, sh, 0), 1.0)
            b_sh = jnp.where(keep, pltpu.roll(bt, sh, 0), 0.0)
            bt = at * b_sh + bt
            at = at * a_sh
        return at, bt

    def body(si, hcs):
        sub = (n_sub - 1 - si) if reverse else si
        base = sub * sub_rows
        out = []
        for slab in range(n_slab):
            hc = hcs[slab]
            rows = [pl.ds(base + k, SUBLANES, stride=SCAN_SEG) for k in range(SCAN_SEG)]
            acc_a, acc_b = {}, {}
            a_run = b_run = None
            for k in ks:
                ak = a_s[slab, rows[k], :]
                bk = b_s[slab, rows[k], :]
                if a_run is None:
                    a_run, b_run = ak, bk
                else:
                    b_run = ak * b_run + bk
                    a_run = ak * a_run
                acc_a[k], acc_b[k] = a_run, b_run
            at, bt = seg_scan(a_run, b_run)
            h_out = at * hc + bt
            if reverse:
                h_in = jnp.where(row < SUBLANES - 1, pltpu.roll(h_out, SUBLANES - 1, 0), hc)
                edge = h_out[0:1, :]
            else:
                h_in = jnp.where(row >= 1, pltpu.roll(h_out, 1, 0), hc)
                edge = h_out[SUBLANES - 1:SUBLANES, :]
            for k in ks:
                b_s[slab, rows[k], :] = acc_a[k] * h_in + acc_b[k]
            out.append(jnp.broadcast_to(edge, (SUBLANES, LANES)))
        return tuple(out)

    hcs = tuple(carry[:, slab * LANES:(slab + 1) * LANES] for slab in range(n_slab))
    hcs = lax.fori_loop(0, n_sub, body, hcs)
    for slab in range(n_slab):
        ls = slice(slab * LANES, (slab + 1) * LANES)
        carry[:, ls] = hcs[slab]
        hlast_ref[0, :, ls] = hcs[slab][0:1, :]
        if combine:
            hsum = hf_ref[0, :, ls] + b_s[slab]
            out_ref[0, :, ls] = (_gelu_tanh(ag_ref[0, :, ls].astype(F32)) * hsum).astype(out_ref.dtype)
        else:
            out_ref[0, :, ls] = b_s[slab]


def _lru_call(ax, h0, cw, cb, wri, br, bi, lam, *, reverse, hf=None, ag=None):
    bsz, n, da = ax.shape
    heads, blk, _ = wri.shape
    tl = _pick(n, 512, SUBLANES * SCAN_SEG)
    assert blk % LANES == 0 and tl % (SUBLANES * SCAN_SEG) == 0
    nc = n // tl
    nb8 = n // SUBLANES
    g8 = tl // SUBLANES
    combine = hf is not None

    def cidx(c):
        return (nc - 1 - c) if reverse else c

    main_map = lambda b, c: (b, cidx(c), 0)
    prev_map = lambda b, c: (b, jnp.maximum(cidx(c) * g8 - 1, 0), 0)
    next_map = lambda b, c: (b, jnp.minimum((cidx(c) + 1) * g8, nb8 - 1), 0)
    const2 = lambda b, c: (0, 0)
    in_specs = [pl.BlockSpec((1, SUBLANES, da), prev_map),
                pl.BlockSpec((1, tl, da), main_map),
                pl.BlockSpec((1, SUBLANES, da), next_map),
                pl.BlockSpec((4, da), const2),
                pl.BlockSpec((1, da), const2),
                pl.BlockSpec((heads, blk, 2 * blk), lambda b, c: (0, 0, 0)),
                pl.BlockSpec((1, da), const2),
                pl.BlockSpec((1, da), const2),
                pl.BlockSpec((1, da), const2),
                pl.BlockSpec((1, 1, da), lambda b, c: (b, 0, 0))]
    args = [ax, ax, ax, cw, cb, wri, br, bi, lam, h0]
    if combine:
        in_specs += [pl.BlockSpec((1, tl, da), main_map), pl.BlockSpec((1, tl, da), main_map)]
        args += [hf, ag]
    out_dtype = BF16 if combine else F32
    return pl.pallas_call(
        functools.partial(_lru_kernel, reverse=reverse, combine=combine, nc=nc, tl=tl,
                          heads=heads, blk=blk),
        out_shape=(_sds((bsz, n, da), out_dtype), _sds((bsz, 1, da), F32)),
        grid=(bsz, nc),
        in_specs=in_specs,
        out_specs=(pl.BlockSpec((1, tl, da), main_map),
                   pl.BlockSpec((1, 1, da), lambda b, c: (b, 0, 0))),
        scratch_shapes=[pltpu.VMEM((tl + 2 * SUBLANES, da), F32),
                        pltpu.VMEM((tl, da), F32),
                        pltpu.VMEM((da // LANES, tl, LANES), F32),
                        pltpu.VMEM((da // LANES, tl, LANES), F32),
                        pltpu.VMEM((SUBLANES, da), F32)],
        compiler_params=_params(("arbitrary", "arbitrary")),
        name=("lru_bwd" if reverse else "lru_fwd") + ("_mix" if combine else ""),
    )(*args)


def _mixout_kernel(ya_ref, bb_ref, p_ref, pu_ref, pd_ref, cbw_ref, ga_ref, gb_ref, wo_ref, x_ref,
                   gt_ref, gf_ref, sh_ref, sc_ref, wr_ref, brt_ref,
                   x1_ref, xp_ref, te_ref, tw_ref, rk_ref, cnt_ref, carry, *, tm, d_a, d_b, n_tiles):
    i = pl.program_id(1)

    @pl.when(jnp.logical_and(pl.program_id(0) == 0, i == 0))
    def _():
        carry[...] = jnp.zeros(carry.shape, F32)

    half = d_b // 2
    z = p_ref[0].astype(F32)
    w = cbw_ref[...]
    zh = z[:, :half]
    col = lax.broadcasted_iota(I32, (tm, half), 0) % GRID_W
    left = jnp.where(col >= 1, pltpu.roll(zh, 1, 0), 0.0)
    right = jnp.where(col <= GRID_W - 2, pltpu.roll(zh, tm - 1, 0), 0.0)
    horiz = w[0:1, :half] * left + w[1:2, :half] * zh + w[2:3, :half] * right
    zv = z[:, half:]
    up_halo = jnp.where(i == 0, 0.0, pu_ref[0].astype(F32))
    dn_halo = jnp.where(i == n_tiles - 1, 0.0, pd_ref[0].astype(F32))
    if tm > GRID_W:
        up = jnp.concatenate([up_halo, zv[:tm - GRID_W]], axis=0)
        dn = jnp.concatenate([zv[GRID_W:], dn_halo], axis=0)
    else:
        up, dn = up_halo, dn_halo
    vert = w[0:1, half:] * up + w[1:2, half:] * zv + w[2:3, half:] * dn
    bb = bb_ref[0].astype(F32)
    yb = jnp.concatenate([bb[:, :half] * horiz, bb[:, half:] * vert], axis=1)
    ya = ya_ref[0].astype(F32)
    ya_n = _rms(ya, ga_ref[...]).astype(BF16)
    yb_n = _rms(yb, gb_ref[...]).astype(BF16)
    mix = (jnp.dot(ya_n, wo_ref[0:d_a, :], preferred_element_type=F32)
           + jnp.dot(yb_n, wo_ref[d_a:d_a + d_b, :], preferred_element_type=F32))
    x1 = x_ref[0] + gt_ref[0] * mix
    x1_ref[0] = x1
    xn = _rms(x1, gf_ref[...]) * (1.0 + sc_ref[0]) + sh_ref[0]
    xb = xn.astype(BF16)
    _store_token_tiles(xp_ref.at[0], _pack_pairs(xn), tm)
    logits = jnp.dot(xb, wr_ref[...], preferred_element_type=F32) + brt_ref[...]
    lane = lax.broadcasted_iota(I32, logits.shape, 1)
    lane_f = lane.astype(F32)
    vals = logits
    tv, te = [], []
    for _ in range(TOP_K):
        m = jnp.max(vals, axis=-1, keepdims=True)
        idx = jnp.min(jnp.where(vals == m, lane_f, float(LANES)), axis=-1, keepdims=True)
        tv.append(m)
        te.append(idx)
        vals = jnp.where(lane_f == idx, -jnp.inf, vals)
    ex = [jnp.exp(v - tv[0]) for v in tv]
    den = ex[0]
    for e in ex[1:]:
        den = den + e
    e_out = jnp.zeros(logits.shape, I32)
    w_out = jnp.zeros(logits.shape, F32)
    for k in range(TOP_K):
        e_out = jnp.where(lane == k, te[k].astype(I32), e_out)
        w_out = jnp.where(lane == k, ex[k] / den, w_out)
    te_ref[0] = e_out[:, :TOP_K]
    tw_ref[0] = w_out
    ohs = [lane_f == te[k] for k in range(TOP_K)]
    m_oh = jnp.zeros(logits.shape, F32)
    for oh in ohs:
        m_oh = m_oh + jnp.where(oh, 1.0, 0.0)
    ri = lax.broadcasted_iota(I32, (tm, tm), 0)
    ci = lax.broadcasted_iota(I32, (tm, tm), 1)
    ltri = jnp.where(ri > ci, 1.0, 0.0).astype(BF16)
    pref = jnp.dot(ltri, m_oh.astype(BF16), preferred_element_type=F32) + carry[0:1, :]
    r_out = jnp.zeros(logits.shape, I32)
    for k in range(TOP_K):
        rk = jnp.sum(jnp.where(ohs[k], pref, 0.0), axis=-1, keepdims=True)
        r_out = jnp.where(lane == k, rk.astype(I32), r_out)
    rk_ref[0] = r_out[:, :TOP_K]
    tot = carry[0:1, :] + jnp.sum(m_oh, axis=0, keepdims=True)
    carry[...] = jnp.broadcast_to(tot, carry.shape)
    cnt_ref[...] = jnp.broadcast_to(tot, cnt_ref.shape)


def _mixout_call(ya, bb, p, cbw, ga, gb, wo_bf, x, gt1, gf, sh2, sc2, wr_bf, brt):
    bsz, n, d = x.shape
    d_a = ya.shape[-1]
    d_b = bb.shape[-1]
    half = d_b // 2
    tm = _pick(n, 512, GRID_W)
    n_tiles = n // tm
    nseg = (d // 2) // LANES
    rpt = tm // GRID_W
    n_rows = n // GRID_W
    row = lambda b, i: (b, i, 0)
    vec = lambda b, i: (b, 0, 0)
    const2 = lambda b, i: (0, 0)
    return pl.pallas_call(
        functools.partial(_mixout_kernel, tm=tm, d_a=d_a, d_b=d_b, n_tiles=n_tiles),
        out_shape=(_sds((bsz, n, d), F32), _sds((bsz, n * nseg, LANES), U32),
                   _sds((bsz, n, TOP_K), I32), _sds((bsz, n, LANES), F32),
                   _sds((bsz, n, TOP_K), I32), _sds((SUBLANES, LANES), F32)),
        grid=(bsz, n_tiles),
        in_specs=[pl.BlockSpec((1, tm, d_a), row),
                  pl.BlockSpec((1, tm, d_b), row),
                  pl.BlockSpec((1, tm, d_b), row),
                  pl.BlockSpec((1, GRID_W, half), lambda b, i: (b, jnp.maximum(i * rpt - 1, 0), 1)),
                  pl.BlockSpec((1, GRID_W, half), lambda b, i: (b, jnp.minimum((i + 1) * rpt, n_rows - 1), 1)),
                  pl.BlockSpec((3, d_b), const2),
                  pl.BlockSpec((1, d_a), const2),
                  pl.BlockSpec((1, d_b), const2),
                  pl.BlockSpec((d_a + d_b, d), const2),
                  pl.BlockSpec((1, tm, d), row),
                  pl.BlockSpec((1, 1, d), vec),
                  pl.BlockSpec((1, d), const2),
                  pl.BlockSpec((1, 1, d), vec),
                  pl.BlockSpec((1, 1, d), vec),
                  pl.BlockSpec((d, LANES), const2),
                  pl.BlockSpec((1, LANES), const2)],
        out_specs=(pl.BlockSpec((1, tm, d), row), pl.BlockSpec((1, tm * nseg, LANES), row),
                   pl.BlockSpec((1, tm, TOP_K), row), pl.BlockSpec((1, tm, LANES), row),
                   pl.BlockSpec((1, tm, TOP_K), row), pl.BlockSpec((SUBLANES, LANES), const2)),
        scratch_shapes=[pltpu.VMEM((SUBLANES, LANES), F32)],
        compiler_params=_params(("arbitrary", "arbitrary")),
        name="mixout",
    )(ya, bb, p, p, p, cbw, ga, gb, wo_bf, x, gt1, gf, sh2, sc2, wr_bf, brt)


def _zero_tail(first, n_blocks, zbuf, dst_block, sem):
    zbuf[...] = jnp.zeros(zbuf.shape, zbuf.dtype)

    def start(c, carry):
        pltpu.make_async_copy(zbuf, dst_block(c), sem).start()
        return carry

    def wait(c, carry):
        pltpu.make_async_copy(zbuf, dst_block(c), sem).wait()
        return carry

    lax.fori_loop(first, n_blocks, start, 0)
    lax.fori_loop(first, n_blocks, wait, 0)


def _dispatch_kernel(bs_ref, nb_ref, pos_ref, x_ref, o_hbm, zbuf, sem, zsem, *, td, rb, n_e, n_blocks, nseg):
    rbr = rb * nseg

    def blk(b):
        return pl.ds(pl.multiple_of(b * rbr, rbr), rbr)

    def tok(t):
        return pl.ds(pl.multiple_of(t * nseg, nseg), nseg)

    @pl.when(pl.program_id(0) == 0)
    def _():
        _zero_tail(bs_ref[n_e - 1] + nb_ref[n_e - 1], n_blocks, zbuf, lambda c: o_hbm.at[blk(c)], zsem)

        def zero_copy(e):
            return pltpu.make_async_copy(zbuf, o_hbm.at[blk(bs_ref[e] + nb_ref[e] - 1)], zsem)

        for e in range(n_e):
            @pl.when(nb_ref[e] > 0)
            def _():
                zero_copy(e).start()

        for e in range(n_e):
            @pl.when(nb_ref[e] > 0)
            def _():
                zero_copy(e).wait()

    def body(t, c):
        src = x_ref.at[tok(t)]
        for k in range(TOP_K):
            pltpu.make_async_copy(src, o_hbm.at[tok(pos_ref[0, 0, t * TOP_K + k])], sem).start(priority=k % 2)
        return c

    lax.fori_loop(0, td, body, 0, unroll=2)
    for _ in range(TOP_K):
        pltpu.make_async_copy(x_ref, o_hbm.at[pl.ds(0, td * nseg)], sem).wait()


def _dispatch_call(groups, xp, pos, n_buf, rb, nseg):
    bstart, nblk = groups
    n_e = bstart.shape[0]
    t = xp.shape[0] // nseg
    td = _pick(t, 512, SUBLANES)
    pos3 = pos.reshape(t // td, 1, td * TOP_K)
    return pl.pallas_call(
        functools.partial(_dispatch_kernel, td=td, rb=rb, n_e=n_e, n_blocks=n_buf // rb, nseg=nseg),
        out_shape=_sds((n_buf * nseg, LANES), U32),
        grid_spec=pltpu.PrefetchScalarGridSpec(
            num_scalar_prefetch=2,
            grid=(t // td,),
            in_specs=[pl.BlockSpec((1, 1, td * TOP_K), lambda i, bs, nb: (i, 0, 0),
                                   memory_space=pltpu.SMEM),
                      pl.BlockSpec((td * nseg, LANES), lambda i, bs, nb: (i, 0))],
            out_specs=pl.BlockSpec(memory_space=pl.ANY),
            scratch_shapes=[pltpu.VMEM((rb * nseg, LANES), U32), pltpu.SemaphoreType.DMA,
                            pltpu.SemaphoreType.DMA]),
        compiler_params=_params(("arbitrary",)),
        name="dispatch",
    )(bstart, nblk, pos3, xp)


GROUP_IN_SLOTS = 4
GROUP_OUT_SLOTS = 3


def _build_chunks(bs_ref, nb_ref, ce, cs, cb, cn, n_split):
    def per_expert(e, j):
        nb = nb_ref[e]
        b0 = bs_ref[e]

        def per_slab(s_, j):
            def per_pair(c, j):
                ce[j] = e
                cs[j] = s_
                cb[j] = b0 + 2 * c
                cn[j] = jnp.minimum(nb - 2 * c, 2)
                return j + 1

            return lax.fori_loop(0, (nb + 1) // 2, per_pair, j)

        return lax.fori_loop(0, n_split, per_slab, j)

    return lax.fori_loop(0, bs_ref.shape[0], per_expert, jnp.int32(0))


def _chunk_loop(n, nb_ref, ce, cs, cn, in_copy, out_copy, w_copies, load_weights, compute):
    ahead = GROUP_IN_SLOTS - 1

    def by_size(j, fn):
        @pl.when(cn[j] == 2)
        def _():
            fn(2)

        @pl.when(cn[j] == 1)
        def _():
            fn(1)

    @pl.when(n > 0)
    def _():
        for cp in w_copies(ce[0], cs[0]):
            cp.start()
        for j in range(ahead):
            @pl.when(j < n)
            def _():
                by_size(j, lambda m: in_copy(j, j, m).start())

        def body(j, carry):
            jp = jnp.maximum(j - 1, 0)
            first = jnp.logical_or(j == 0, jnp.logical_or(ce[j] != ce[jp], cs[j] != cs[jp]))

            @pl.when(first)
            def _():
                for cp in w_copies(ce[j], cs[j]):
                    cp.wait()

            islot = lax.rem(j, GROUP_IN_SLOTS)
            oslot = lax.rem(j, GROUP_OUT_SLOTS)

            @pl.when(j + ahead < n)
            def _():
                ja = j + ahead
                by_size(ja, lambda m: in_copy(ja, lax.rem(ja, GROUP_IN_SLOTS), m).start())

            by_size(j, lambda m: in_copy(j, islot, m).wait())

            @pl.when(j >= GROUP_OUT_SLOTS)
            def _():
                jo = j - GROUP_OUT_SLOTS
                by_size(jo, lambda m: out_copy(jo, oslot, m).wait())

            @pl.when(first)
            def _():
                def run(m):
                    load_weights()
                    compute(j, islot, oslot, m)
                    out_copy(j, oslot, m).start()

                by_size(j, run)
                jn = j + (nb_ref[ce[j]] + 1) // 2

                @pl.when(jn < n)
                def _():
                    for cp in w_copies(ce[jn], cs[jn]):
                        cp.start()

            @pl.when(jnp.logical_not(first))
            def _():
                def run(m):
                    compute(j, islot, oslot, m)
                    out_copy(j, oslot, m).start()

                by_size(j, run)

            return carry

        lax.fori_loop(0, n, body, 0)

        for k in range(GROUP_OUT_SLOTS):
            @pl.when(n > k)
            def _():
                jl = n - 1 - k
                by_size(jl, lambda m: out_copy(jl, lax.rem(jl, GROUP_OUT_SLOTS), m).wait())


def _gmm1_kernel(bs_ref, nb_ref, xs_hbm, wg_hbm, wu_hbm, bg_ref, bu_ref, act_hbm,
                 xbuf, obuf, wraw, wbf, zbuf, ce, cs, cb, cn, sin, sout, wsem, zsem,
                 *, rb, tf, n_blocks, n_split, nseg):
    def blk(b, m=1):
        return pl.ds(pl.multiple_of(b * rb, rb), m * rb)

    def tok_blk(b, m=1):
        return pl.ds(pl.multiple_of(b * (rb * nseg), rb * nseg), m * rb * nseg)

    def cols(s_):
        return pl.ds(pl.multiple_of(s_ * tf, tf), tf)

    n_e = bs_ref.shape[0]
    for s_ in range(n_split):
        _zero_tail(bs_ref[n_e - 1] + nb_ref[n_e - 1], n_blocks, zbuf,
                   lambda c: act_hbm.at[blk(c), cols(s_)], zsem)
    n = _build_chunks(bs_ref, nb_ref, ce, cs, cb, cn, n_split)

    def in_copy(j, slot, m):
        return pltpu.make_async_copy(xs_hbm.at[tok_blk(cb[j], m)],
                                     xbuf.at[slot, pl.ds(0, m * rb * nseg)], sin.at[slot])

    def out_copy(j, slot, m):
        return pltpu.make_async_copy(obuf.at[slot, pl.ds(0, m * rb)],
                                     act_hbm.at[blk(cb[j], m), cols(cs[j])], sout.at[slot])

    def w_copies(e, s_):
        return (pltpu.make_async_copy(wg_hbm.at[e, :, cols(s_)], wraw.at[0], wsem),
                pltpu.make_async_copy(wu_hbm.at[e, :, cols(s_)], wraw.at[1], wsem))

    def load_weights():
        wbf[:, 0:tf] = wraw[0].astype(BF16)
        wbf[:, tf:2 * tf] = wraw[1].astype(BF16)

    def compute(j, islot, oslot, m):
        g = ce[j] * n_split + cs[j]
        lo, hi = _unpack_pairs(_load_token_tiles(xbuf.at[islot], m * rb, nseg))
        h = jnp.concatenate([lo.astype(BF16), hi.astype(BF16)], axis=1)
        gu = jnp.dot(h, wbf[...], preferred_element_type=F32)
        gate = jnp.minimum(gu[:, :tf] + bg_ref[g], SWIGLU_LIMIT)
        up = jnp.clip(gu[:, tf:] + bu_ref[g], -SWIGLU_LIMIT, SWIGLU_LIMIT)
        act = (up + 1.0) * gate * _sigmoid(SWIGLU_ALPHA * gate)
        obuf[oslot, 0:m * rb, :] = act.astype(BF16)

    _chunk_loop(n, nb_ref, ce, cs, cn, in_copy, out_copy, w_copies, load_weights, compute)


def _gmm1_call(groups, xs, wg, wu, bg, bu, rb, tf, nseg):
    n_buf = xs.shape[0] // nseg
    n_e, d, dff = wg.shape
    n_split = dff // tf
    max_chunks = n_split * (n_buf // rb)
    vmem_full = lambda shape: pl.BlockSpec(shape, lambda i, *_: (0,) * len(shape))
    return pl.pallas_call(
        functools.partial(_gmm1_kernel, rb=rb, tf=tf, n_blocks=n_buf // rb, n_split=n_split, nseg=nseg),
        out_shape=_sds((n_buf, dff), BF16),
        grid_spec=pltpu.PrefetchScalarGridSpec(
            num_scalar_prefetch=2,
            grid=(1,),
            in_specs=[pl.BlockSpec(memory_space=pl.ANY),
                      pl.BlockSpec(memory_space=pl.ANY),
                      pl.BlockSpec(memory_space=pl.ANY),
                      vmem_full((n_e * n_split, 1, tf)),
                      vmem_full((n_e * n_split, 1, tf))],
            out_specs=pl.BlockSpec(memory_space=pl.ANY),
            scratch_shapes=[pltpu.VMEM((GROUP_IN_SLOTS, 2 * rb * nseg, LANES), U32),
                            pltpu.VMEM((GROUP_OUT_SLOTS, 2 * rb, tf), BF16),
                            pltpu.VMEM((2, d, tf), F32),
                            pltpu.VMEM((d, 2 * tf), BF16),
                            pltpu.VMEM((rb, tf), BF16),
                            pltpu.SMEM((max_chunks,), I32),
                            pltpu.SMEM((max_chunks,), I32),
                            pltpu.SMEM((max_chunks,), I32),
                            pltpu.SMEM((max_chunks,), I32),
                            pltpu.SemaphoreType.DMA((GROUP_IN_SLOTS,)),
                            pltpu.SemaphoreType.DMA((GROUP_OUT_SLOTS,)),
                            pltpu.SemaphoreType.DMA,
                            pltpu.SemaphoreType.DMA]),
        compiler_params=_params(("arbitrary",)),
        name="gmm1",
    )(*groups, xs, wg, wu, bg.reshape(n_e * n_split, 1, tf), bu.reshape(n_e * n_split, 1, tf))


def _gmm2_kernel(bs_ref, nb_ref, act_hbm, wd_hbm, bd_ref, y_hbm,
                 abuf, ybuf, wraw, wbf, zbuf, ce, cs, cb, cn, sin, sout, wsem, zsem, *, rb, n_blocks, nseg):
    def blk(b, m=1):
        return pl.ds(pl.multiple_of(b * rb, rb), m * rb)

    def tok_blk(b, m=1):
        return pl.ds(pl.multiple_of(b * (rb * nseg), rb * nseg), m * rb * nseg)

    n_e = bs_ref.shape[0]
    _zero_tail(bs_ref[n_e - 1] + nb_ref[n_e - 1], n_blocks, zbuf, lambda c: y_hbm.at[tok_blk(c)], zsem)
    n = _build_chunks(bs_ref, nb_ref, ce, cs, cb, cn, 1)

    def in_copy(j, slot, m):
        return pltpu.make_async_copy(act_hbm.at[blk(cb[j], m)], abuf.at[slot, pl.ds(0, m * rb)],
                                     sin.at[slot])

    def out_copy(j, slot, m):
        return pltpu.make_async_copy(ybuf.at[slot, pl.ds(0, m * rb * nseg)],
                                     y_hbm.at[tok_blk(cb[j], m)], sout.at[slot])

    def w_copies(e, s_):
        return (pltpu.make_async_copy(wd_hbm.at[e], wraw, wsem),)

    def load_weights():
        wbf[...] = wraw[...].astype(BF16)

    def compute(j, islot, oslot, m):
        y = jnp.dot(abuf[islot, 0:m * rb, :], wbf[...], preferred_element_type=F32) + bd_ref[ce[j]]
        _store_token_tiles(ybuf.at[oslot], _pack_pairs(y), m * rb)

    _chunk_loop(n, nb_ref, ce, cs, cn, in_copy, out_copy, w_copies, load_weights, compute)


def _gmm2_call(groups, act, wd, bd, rb, nseg):
    n_buf, dff = act.shape
    n_e, _, d = wd.shape
    assert d == 2 * nseg * LANES
    max_chunks = n_buf // rb
    vmem_full = lambda shape: pl.BlockSpec(shape, lambda i, *_: (0,) * len(shape))
    return pl.pallas_call(
        functools.partial(_gmm2_kernel, rb=rb, n_blocks=n_buf // rb, nseg=nseg),
        out_shape=_sds((n_buf * nseg, LANES), U32),
        grid_spec=pltpu.PrefetchScalarGridSpec(
            num_scalar_prefetch=2,
            grid=(1,),
            in_specs=[pl.BlockSpec(memory_space=pl.ANY),
                      pl.BlockSpec(memory_space=pl.ANY),
                      vmem_full((n_e, 1, d))],
            out_specs=pl.BlockSpec(memory_space=pl.ANY),
            scratch_shapes=[pltpu.VMEM((GROUP_IN_SLOTS, 2 * rb, dff), BF16),
                            pltpu.VMEM((GROUP_OUT_SLOTS, 2 * rb * nseg, LANES), U32),
                            pltpu.VMEM((dff, d), F32),
                            pltpu.VMEM((dff, d), BF16),
                            pltpu.VMEM((rb * nseg, LANES), U32),
                            pltpu.SMEM((max_chunks,), I32),
                            pltpu.SMEM((max_chunks,), I32),
                            pltpu.SMEM((max_chunks,), I32),
                            pltpu.SMEM((max_chunks,), I32),
                            pltpu.SemaphoreType.DMA((GROUP_IN_SLOTS,)),
                            pltpu.SemaphoreType.DMA((GROUP_OUT_SLOTS,)),
                            pltpu.SemaphoreType.DMA,
                            pltpu.SemaphoreType.DMA]),
        compiler_params=_params(("arbitrary",)),
        name="gmm2",
    )(*groups, act, wd, bd.reshape(n_e, 1, d))


def _final_kernel(pos_ref, posn_ref, y_hbm, tw_ref, x1_ref, gt_ref, gfin_ref, o_ref, ybuf, sem, *,
                  tc, n_steps, nseg):
    step = pl.program_id(0) * pl.num_programs(1) + pl.program_id(1)

    def tok(t):
        return pl.ds(pl.multiple_of(t * nseg, nseg), nseg)

    def issue(p_ref, slot):
        def body(t, c):
            for k in range(TOP_K):
                src = y_hbm.at[tok(p_ref[0, 0, t * TOP_K + k])]
                pltpu.make_async_copy(src, ybuf.at[slot, k, tok(t)], sem.at[slot]).start(priority=k % 2)
            return c

        lax.fori_loop(0, tc, body, 0, unroll=2)

    def run(cur):
        @pl.when(step + 1 < n_steps)
        def _():
            issue(posn_ref, 1 - cur)

        for k in range(TOP_K):
            pltpu.make_async_copy(y_hbm.at[pl.ds(0, tc * nseg)], ybuf.at[cur, k], sem.at[cur]).wait()
        tw = tw_ref[0]
        lo, hi = None, None
        for k in range(TOP_K):
            lo_k, hi_k = _unpack_pairs(_load_token_tiles(ybuf.at[cur, k], tc, nseg))
            wk = tw[:, k:k + 1]
            lo = wk * lo_k if lo is None else lo + wk * lo_k
            hi = wk * hi_k if hi is None else hi + wk * hi_k
        moe = jnp.concatenate([lo, hi], axis=1)
        x2 = x1_ref[0] + gt_ref[0] * moe
        o_ref[0] = _rms(x2, gfin_ref[...])

    @pl.when(step == 0)
    def _():
        issue(pos_ref, 0)

    parity = lax.rem(step, 2)

    @pl.when(parity == 0)
    def _():
        run(0)

    @pl.when(parity == 1)
    def _():
        run(1)


def _final_call(y, pos, top_w, x1, gt2, gfin, nseg):
    bsz, n, d = x1.shape
    tc = _pick(n, 256, SUBLANES)
    nt = n // tc
    n_steps = bsz * nt
    pos3 = pos.reshape(n_steps, 1, tc * TOP_K)
    row = lambda b, i: (b, i, 0)
    return pl.pallas_call(
        functools.partial(_final_kernel, tc=tc, n_steps=n_steps, nseg=nseg),
        out_shape=_sds((bsz, n, d), F32),
        grid=(bsz, nt),
        in_specs=[pl.BlockSpec((1, 1, tc * TOP_K), lambda b, i: (b * nt + i, 0, 0), memory_space=pltpu.SMEM),
                  pl.BlockSpec((1, 1, tc * TOP_K), lambda b, i: (jnp.minimum(b * nt + i + 1, n_steps - 1), 0, 0),
                               memory_space=pltpu.SMEM),
                  pl.BlockSpec(memory_space=pl.ANY),
                  pl.BlockSpec((1, tc, LANES), row),
                  pl.BlockSpec((1, tc, d), row),
                  pl.BlockSpec((1, 1, d), lambda b, i: (b, 0, 0)),
                  pl.BlockSpec((1, d), lambda b, i: (0, 0))],
        out_specs=pl.BlockSpec((1, tc, d), row),
        scratch_shapes=[pltpu.VMEM((2, TOP_K, tc * nseg, LANES), U32), pltpu.SemaphoreType.DMA((2,))],
        compiler_params=_params(("arbitrary", "arbitrary")),
        name="final",
    )(pos3, pos3, y, top_w, x1, gt2, gfin)


def _group_tables(counts, rb):
    nblk = (counts + rb - 1) // rb
    bstart = jnp.cumsum(nblk) - nblk
    return bstart * rb, (bstart.astype(I32), nblk.astype(I32))


def kernel(x, c, ctx, c_ctx, w_mod, b_mod, g_mix, w_in, conv_a_w, conv_a_b, lru_w_r, lru_b_r,
           lru_w_i, lru_b_i, lru_lam, conv_b_w, g_out_a, g_out_b, w_out, g_ffn, w_router,
           b_router, w_gate, b_gate, w_up, b_up, w_down, b_down, g_final):
    assert w_mod.shape[0] == 1, "single-layer block"
    bsz, n_lat, d = x.shape
    d_a = conv_a_w.shape[-1]
    d_b = conv_b_w.shape[-1]
    n_e = w_router.shape[-1]
    assert n_lat % GRID_W == 0 and n_e <= LANES and d % (2 * LANES) == 0
    l = 0

    cs = jnp.zeros((SUBLANES, d), F32).at[:bsz].set(c).at[bsz].set(c_ctx)
    mod = _mod_call(cs, w_mod[l], b_mod[l][None])
    sh1, sc1, gt1, sh2, sc2, gt2 = [m[:bsz, None, :] for m in jnp.split(mod, 6, axis=-1)]
    ssh1, ssc1 = [jnp.broadcast_to(m[bsz][None, None, :], (bsz, 1, d))
                  for m in jnp.split(mod, 6, axis=-1)[:2]]

    w_in_bf = w_in[l].astype(BF16)
    g_mix2 = g_mix[l][None]
    cw = conv_a_w[l]
    cb = conv_a_b[l][None]
    wri = jnp.concatenate([lru_w_r[l], lru_w_i[l]], axis=-1).astype(BF16)
    br, bi, lam = lru_b_r[l], lru_b_i[l], lru_lam[l]
    lru_p = lambda dr: (cw, cb, wri[dr], br[dr][None], bi[dr][None], lam[dr][None])

    (s_ax,) = _inproj_call(ctx, g_mix2, ssh1, ssc1, w_in_bf[:, d_a:2 * d_a], d_a, d_b, latent=False)
    zero_state = jnp.zeros((bsz, 1, d_a), F32)
    _, h0f = _lru_call(s_ax, zero_state, *lru_p(0), reverse=False)
    _, h0b = _lru_call(s_ax, zero_state, *lru_p(1), reverse=True)

    ag, ax, bb, p = _inproj_call(x, g_mix2, sh1, sc1, w_in_bf, d_a, d_b, latent=True)
    hf, _ = _lru_call(ax, h0f, *lru_p(0), reverse=False)
    ya, _ = _lru_call(ax, h0b, *lru_p(1), reverse=True, hf=hf, ag=ag)

    wr_bf = jnp.zeros((d, LANES), BF16).at[:, :n_e].set(w_router[l].astype(BF16))
    brt = jnp.full((1, LANES), NEG_BIG, F32).at[0, :n_e].set(b_router[l])
    x1, xp, top_e, top_w, rank, cnt = _mixout_call(
        ya, bb, p, conv_b_w[l], g_out_a[l][None], g_out_b[l][None], w_out[l].astype(BF16), x, gt1,
        g_ffn[l][None], sh2, sc2, wr_bf, brt)

    n_tok = bsz * n_lat
    n_rows = n_tok * TOP_K
    rb = MOE_ROW_BLOCK
    n_buf = n_rows + n_e * rb
    counts = cnt[0, :n_e].astype(I32)
    starts, groups = _group_tables(counts, rb)
    onehot = top_e.reshape(n_tok, TOP_K, 1) == jnp.arange(n_e, dtype=I32)[None, None, :]
    pos = (jnp.sum(jnp.where(onehot, starts[None, None, :], 0), axis=-1)
           + rank.reshape(n_tok, TOP_K)).astype(I32)
    nseg = (d // 2) // LANES
    xs = _dispatch_call(groups, xp.reshape(n_tok * nseg, LANES), pos, n_buf, rb, nseg)
    d_ff = w_gate.shape[-1]
    tf = _pick(d_ff, 1024)
    act = _gmm1_call(groups, xs, w_gate[l], w_up[l], b_gate[l], b_up[l], rb, tf, nseg)
    y = _gmm2_call(groups, act, w_down[l], b_down[l], rb, nseg)
    return _final_call(y, pos, top_w, x1, gt2, g_final[None], nseg)
```

```python
import functools

import jax
import jax.numpy as jnp
from jax import lax
from jax.experimental import pallas as pl
from jax.experimental.pallas import tpu as pltpu

F32 = jnp.float32
BF16 = jnp.bfloat16
I32 = jnp.int32
U32 = jnp.uint32

GRID_W = 64
TOP_K = 4
LRU_C = 8.0
CONV_A_LEFT = 2
SCAN_SEG = 4
SWIGLU_LIMIT = 7.0
SWIGLU_ALPHA = 1.702
EPS = 1e-6
MOE_ROW_BLOCK = 256

LANES = 128
SUBLANES = 8
VMEM_LIMIT_BYTES = 56 * 1024 * 1024
NEG_BIG = -1e30


def _sds(shape, dtype):
    return jax.ShapeDtypeStruct(shape, dtype)


def _pick(n, pref, mult=LANES):
    if n <= pref:
        return n
    t = (pref // mult) * mult
    while t >= mult:
        if n % t == 0:
            return t
        t -= mult
    return n


def _params(sem):
    return pltpu.CompilerParams(dimension_semantics=sem, vmem_limit_bytes=VMEM_LIMIT_BYTES)


def _sigmoid(x):
    return 1.0 / (1.0 + jnp.exp(-x))


def _rms(x, g):
    ms = jnp.mean(x * x, axis=-1, keepdims=True)
    return (x * lax.rsqrt(ms + EPS)) * g


def _pack_pairs(x):
    w = x.shape[1] // 2
    lo = lax.bitcast_convert_type(x[:, :w].astype(BF16).astype(F32), U32)
    hi = lax.bitcast_convert_type(x[:, w:].astype(BF16).astype(F32), U32)
    return lax.shift_right_logical(lo, jnp.uint32(16)) | (hi & jnp.uint32(0xFFFF0000))


def _unpack_pairs(words):
    lo = lax.bitcast_convert_type(lax.shift_left(words, jnp.uint32(16)), F32)
    hi = lax.bitcast_convert_type(words & jnp.uint32(0xFFFF0000), F32)
    return lo, hi


def _store_token_tiles(ref, words, rows):
    nseg = words.shape[1] // LANES
    if nseg == 1:
        ref[...] = words
        return
    for s in range(nseg):
        ref[pl.ds(s, rows, stride=nseg), :] = words[:, s * LANES:(s + 1) * LANES]


def _load_token_tiles(ref, rows, nseg):
    if nseg == 1:
        return ref[...]
    return jnp.concatenate([ref[pl.ds(s, rows, stride=nseg), :] for s in range(nseg)], axis=1)


def _mod_kernel(c_ref, w_ref, b_ref, o_ref):
    c = c_ref[...]
    s = c * _sigmoid(c)
    o_ref[...] = jnp.dot(s.astype(BF16), w_ref[...].astype(BF16),
                         preferred_element_type=F32) + b_ref[...]


def _mod_call(cs, w, b):
    d, n6 = w.shape
    tn = _pick(n6, 1024)
    return pl.pallas_call(
        _mod_kernel,
        out_shape=_sds((cs.shape[0], n6), F32),
        grid=(n6 // tn,),
        in_specs=[pl.BlockSpec((cs.shape[0], d), lambda j: (0, 0)),
                  pl.BlockSpec((d, tn), lambda j: (0, j)),
                  pl.BlockSpec((1, tn), lambda j: (0, j))],
        out_specs=pl.BlockSpec((cs.shape[0], tn), lambda j: (0, j)),
        compiler_params=_params(("arbitrary",)),
        name="mod",
    )(cs, w, b)


def _inproj_kernel(x_ref, g_ref, sh_ref, sc_ref, w_ref, *out_refs, d_a, d_b, latent):
    x = x_ref[0]
    xn = _rms(x, g_ref[...]) * (1.0 + sc_ref[0]) + sh_ref[0]
    xb = xn.astype(BF16)

    def sec(lo, width):
        return jnp.dot(xb, w_ref[:, lo:lo + width], preferred_element_type=F32)

    if not latent:
        out_refs[0][0] = sec(0, d_a)
        return
    ag_ref, ax_ref, bb_ref, p_ref = out_refs
    ag_ref[0] = sec(0, d_a).astype(BF16)
    ax_ref[0] = sec(d_a, d_a)
    bb_ref[0] = sec(2 * d_a, d_b).astype(BF16)
    p_ref[0] = (sec(2 * d_a + d_b, d_b) * sec(2 * d_a + 2 * d_b, d_b)).astype(BF16)


def _inproj_call(x, g, sh, sc, w_bf, d_a, d_b, latent):
    bsz, n, d = x.shape
    tm = _pick(n, 512, SUBLANES)
    n_w = w_bf.shape[1]
    row = lambda b, i: (b, i, 0)
    if latent:
        out_shape = (_sds((bsz, n, d_a), BF16), _sds((bsz, n, d_a), F32),
                     _sds((bsz, n, d_b), BF16), _sds((bsz, n, d_b), BF16))
        out_specs = (pl.BlockSpec((1, tm, d_a), row), pl.BlockSpec((1, tm, d_a), row),
                     pl.BlockSpec((1, tm, d_b), row), pl.BlockSpec((1, tm, d_b), row))
    else:
        out_shape = (_sds((bsz, n, d_a), F32),)
        out_specs = (pl.BlockSpec((1, tm, d_a), row),)
    return pl.pallas_call(
        functools.partial(_inproj_kernel, d_a=d_a, d_b=d_b, latent=latent),
        out_shape=out_shape,
        grid=(bsz, n // tm),
        in_specs=[pl.BlockSpec((1, tm, d), row),
                  pl.BlockSpec((1, d), lambda b, i: (0, 0)),
                  pl.BlockSpec((1, 1, d), lambda b, i: (b, 0, 0)),
                  pl.BlockSpec((1, 1, d), lambda b, i: (b, 0, 0)),
                  pl.BlockSpec((d, n_w), lambda b, i: (0, 0), pipeline_mode=pl.Buffered(1))],
        out_specs=out_specs,
        compiler_params=_params(("arbitrary", "arbitrary")),
        name="inproj_lat" if latent else "inproj_ctx",
    )(x, g, sh, sc, w_bf)


def _gelu_tanh(x):
    c = 0.7978845608028654
    return x * (0.5 * (1.0 + jnp.tanh(c * (x + 0.044715 * (x * x * x)))))


def _lru_kernel(*refs, reverse, combine, nc, tl, heads, blk):
    if combine:
        (prev_ref, main_ref, next_ref, cw_ref, cb_ref, wri_ref, br_ref, bi_ref, lam_ref, h0_ref,
         hf_ref, ag_ref, out_ref, hlast_ref, ebuf, xc_s, a_s, b_s, carry) = refs
    else:
        (prev_ref, main_ref, next_ref, cw_ref, cb_ref, wri_ref, br_ref, bi_ref, lam_ref, h0_ref,
         out_ref, hlast_ref, ebuf, xc_s, a_s, b_s, carry) = refs
    da = heads * blk
    c = pl.program_id(1)
    cidx = (nc - 1 - c) if reverse else c

    @pl.when(c == 0)
    def _():
        carry[...] = jnp.broadcast_to(h0_ref[0], carry.shape)

    main = main_ref[0]
    zero8 = jnp.zeros((SUBLANES, da), F32)
    ebuf[0:SUBLANES, :] = jnp.where(cidx == 0, zero8, prev_ref[0])
    ebuf[SUBLANES:SUBLANES + tl, :] = main
    ebuf[SUBLANES + tl:2 * SUBLANES + tl, :] = jnp.where(cidx == nc - 1, zero8, next_ref[0])
    cw = cw_ref[...]
    off = SUBLANES - CONV_A_LEFT
    xc_s[...] = (cw[0:1] * ebuf[off:off + tl, :] + cw[1:2] * ebuf[off + 1:off + 1 + tl, :]
                 + cw[2:3] * main + cw[3:4] * ebuf[off + 3:off + 3 + tl, :] + cb_ref[...])

    z = -lam_ref[...]
    sp = jnp.maximum(z, 0.0) + jnp.log1p(jnp.exp(-jnp.abs(z)))
    spb = blk // LANES
    rc = min(tl, 128)
    for r0 in range(0, tl, rc):
        for h in range(heads):
            cs = slice(h * blk, (h + 1) * blk)
            xh = xc_s[r0:r0 + rc, cs]
            zz = jnp.dot(xh.astype(BF16), wri_ref[h], preferred_element_type=F32)
            r = _sigmoid(zz[:, :blk] + br_ref[:, cs])
            i = _sigmoid(zz[:, blk:] + bi_ref[:, cs])
            log_a = (-LRU_C * r) * sp[:, cs]
            a = jnp.exp(log_a)
            v = jnp.tanh(-log_a) * (1.0 + a * a)
            b = jnp.where(v > 0.0, v * lax.rsqrt(v), 0.0) * (i * xh)
            for q in range(spb):
                a_s[h * spb + q, r0:r0 + rc, :] = a[:, q * LANES:(q + 1) * LANES]
                b_s[h * spb + q, r0:r0 + rc, :] = b[:, q * LANES:(q + 1) * LANES]

    n_slab = da // LANES
    sub_rows = SUBLANES * SCAN_SEG
    n_sub = tl // sub_rows
    row = lax.broadcasted_iota(I32, (SUBLANES, LANES), 0)
    ks = list(range(SCAN_SEG - 1, -1, -1)) if reverse else list(range(SCAN_SEG))

    def seg_scan(at, bt):
        for s in (1, 2, 4):
            if reverse:
                keep = row < (SUBLANES - s)
                sh = SUBLANES - s
            else:
                keep = row >= s
                sh = s
            a_sh = jnp.where(keep, pltpu.roll(at, sh, 0), 1.0)
            b_sh = jnp.where(keep, pltpu.roll(bt, sh, 0), 0.0)
            bt = at * b_sh + bt
            at = at * a_sh
        return at, bt

    def body(si, hcs):
        sub = (n_sub - 1 - si) if reverse else si
        base = sub * sub_rows
        out = []
        for slab in range(n_slab):
            hc = hcs[slab]
            rows = [pl.ds(base + k, SUBLANES, stride=SCAN_SEG) for k in range(SCAN_SEG)]
            acc_a, acc_b = {}, {}
            a_run = b_run = None
            for k in ks:
                ak = a_s[slab, rows[k], :]
                bk = b_s[slab, rows[k], :]
                if a_run is None:
                    a_run, b_run = ak, bk
                else:
                    b_run = ak * b_run + bk
                    a_run = ak * a_run
                acc_a[k], acc_b[k] = a_run, b_run
            at, bt = seg_scan(a_run, b_run)
            h_out = at * hc + bt
            if reverse:
                h_in = jnp.where(row < SUBLANES - 1, pltpu.roll(h_out, SUBLANES - 1, 0), hc)
                edge = h_out[0:1, :]
            else:
                h_in = jnp.where(row >= 1, pltpu.roll(h_out, 1, 0), hc)
                edge = h_out[SUBLANES - 1:SUBLANES, :]
            for k in ks:
                b_s[slab, rows[k], :] = acc_a[k] * h_in + acc_b[k]
            out.append(jnp.broadcast_to(edge, (SUBLANES, LANES)))
        return tuple(out)

    hcs = tuple(carry[:, slab * LANES:(slab + 1) * LANES] for slab in range(n_slab))
    hcs = lax.fori_loop(0, n_sub, body, hcs)
    for slab in range(n_slab):
        ls = slice(slab * LANES, (slab + 1) * LANES)
        carry[:, ls] = hcs[slab]
        hlast_ref[0, :, ls] = hcs[slab][0:1, :]
        if combine:
            hsum = hf_ref[0, :, ls] + b_s[slab]
            out_ref[0, :, ls] = (_gelu_tanh(ag_ref[0, :, ls].astype(F32)) * hsum).astype(out_ref.dtype)
        else:
            out_ref[0, :, ls] = b_s[slab]


def _lru_call(ax, h0, cw, cb, wri, br, bi, lam, *, reverse, hf=None, ag=None):
    bsz, n, da = ax.shape
    heads, blk, _ = wri.shape
    tl = _pick(n, 512, SUBLANES * SCAN_SEG)
    assert blk % LANES == 0 and tl % (SUBLANES * SCAN_SEG) == 0
    nc = n // tl
    nb8 = n // SUBLANES
    g8 = tl // SUBLANES
    combine = hf is not None

    def cidx(c):
        return (nc - 1 - c) if reverse else c

    main_map = lambda b, c: (b, cidx(c), 0)
    prev_map = lambda b, c: (b, jnp.maximum(cidx(c) * g8 - 1, 0), 0)
    next_map = lambda b, c: (b, jnp.minimum((cidx(c) + 1) * g8, nb8 - 1), 0)
    const2 = lambda b, c: (0, 0)
    in_specs = [pl.BlockSpec((1, SUBLANES, da), prev_map),
                pl.BlockSpec((1, tl, da), main_map),
                pl.BlockSpec((1, SUBLANES, da), next_map),
                pl.BlockSpec((4, da), const2),
                pl.BlockSpec((1, da), const2),
                pl.BlockSpec((heads, blk, 2 * blk), lambda b, c: (0, 0, 0)),
                pl.BlockSpec((1, da), const2),
                pl.BlockSpec((1, da), const2),
                pl.BlockSpec((1, da), const2),
                pl.BlockSpec((1, 1, da), lambda b, c: (b, 0, 0))]
    args = [ax, ax, ax, cw, cb, wri, br, bi, lam, h0]
    if combine:
        in_specs += [pl.BlockSpec((1, tl, da), main_map), pl.BlockSpec((1, tl, da), main_map)]
        args += [hf, ag]
    out_dtype = BF16 if combine else F32
    return pl.pallas_call(
        functools.partial(_lru_kernel, reverse=reverse, combine=combine, nc=nc, tl=tl,
                          heads=heads, blk=blk),
        out_shape=(_sds((bsz, n, da), out_dtype), _sds((bsz, 1, da), F32)),
        grid=(bsz, nc),
        in_specs=in_specs,
        out_specs=(pl.BlockSpec((1, tl, da), main_map),
                   pl.BlockSpec((1, 1, da), lambda b, c: (b, 0, 0))),
        scratch_shapes=[pltpu.VMEM((tl + 2 * SUBLANES, da), F32),
                        pltpu.VMEM((tl, da), F32),
                        pltpu.VMEM((da // LANES, tl, LANES), F32),
                        pltpu.VMEM((da // LANES, tl, LANES), F32),
                        pltpu.VMEM((SUBLANES, da), F32)],
        compiler_params=_params(("arbitrary", "arbitrary")),
        name=("lru_bwd" if reverse else "lru_fwd") + ("_mix" if combine else ""),
    )(*args)


def _mixout_kernel(ya_ref, bb_ref, p_ref, pu_ref, pd_ref, cbw_ref, ga_ref, gb_ref, wo_ref, x_ref,
                   gt_ref, gf_ref, sh_ref, sc_ref, wr_ref, brt_ref,
                   x1_ref, xp_ref, te_ref, tw_ref, rk_ref, cnt_ref, carry, *, tm, d_a, d_b, n_tiles):
    i = pl.program_id(1)

    @pl.when(jnp.logical_and(pl.program_id(0) == 0, i == 0))
    def _():
        carry[...] = jnp.zeros(carry.shape, F32)

    half = d_b // 2
    z = p_ref[0].astype(F32)
    w = cbw_ref[...]
    zh = z[:, :half]
    col = lax.broadcasted_iota(I32, (tm, half), 0) % GRID_W
    left = jnp.where(col >= 1, pltpu.roll(zh, 1, 0), 0.0)
    right = jnp.where(col <= GRID_W - 2, pltpu.roll(zh, tm - 1, 0), 0.0)
    horiz = w[0:1, :half] * left + w[1:2, :half] * zh + w[2:3, :half] * right
    zv = z[:, half:]
    up_halo = jnp.where(i == 0, 0.0, pu_ref[0].astype(F32))
    dn_halo = jnp.where(i == n_tiles - 1, 0.0, pd_ref[0].astype(F32))
    if tm > GRID_W:
        up = jnp.concatenate([up_halo, zv[:tm - GRID_W]], axis=0)
        dn = jnp.concatenate([zv[GRID_W:], dn_halo], axis=0)
    else:
        up, dn = up_halo, dn_halo
    vert = w[0:1, half:] * up + w[1:2, half:] * zv + w[2:3, half:] * dn
    bb = bb_ref[0].astype(F32)
    yb = jnp.concatenate([bb[:, :half] * horiz, bb[:, half:] * vert], axis=1)
    ya = ya_ref[0].astype(F32)
    ya_n = _rms(ya, ga_ref[...]).astype(BF16)
    yb_n = _rms(yb, gb_ref[...]).astype(BF16)
    mix = (jnp.dot(ya_n, wo_ref[0:d_a, :], preferred_element_type=F32)
           + jnp.dot(yb_n, wo_ref[d_a:d_a + d_b, :], preferred_element_type=F32))
    x1 = x_ref[0] + gt_ref[0] * mix
    x1_ref[0] = x1
    xn = _rms(x1, gf_ref[...]) * (1.0 + sc_ref[0]) + sh_ref[0]
    xb = xn.astype(BF16)
    _store_token_tiles(xp_ref.at[0], _pack_pairs(xn), tm)
    logits = jnp.dot(xb, wr_ref[...], preferred_element_type=F32) + brt_ref[...]
    lane = lax.broadcasted_iota(I32, logits.shape, 1)
    lane_f = lane.astype(F32)
    vals = logits
    tv, te = [], []
    for _ in range(TOP_K):
        m = jnp.max(vals, axis=-1, keepdims=True)
        idx = jnp.min(jnp.where(vals == m, lane_f, float(LANES)), axis=-1, keepdims=True)
        tv.append(m)
        te.append(idx)
        vals = jnp.where(lane_f == idx, -jnp.inf, vals)
    ex = [jnp.exp(v - tv[0]) for v in tv]
    den = ex[0]
    for e in ex[1:]:
        den = den + e
    e_out = jnp.zeros(logits.shape, I32)
    w_out = jnp.zeros(logits.shape, F32)
    for k in range(TOP_K):
        e_out = jnp.where(lane == k, te[k].astype(I32), e_out)
        w_out = jnp.where(lane == k, ex[k] / den, w_out)
    te_ref[0] = e_out.T[:TOP_K, :]
    tw_ref[0] = w_out
    ohs = [lane_f == te[k] for k in range(TOP_K)]
    m_oh = jnp.zeros(logits.shape, F32)
    for oh in ohs:
        m_oh = m_oh + jnp.where(oh, 1.0, 0.0)
    ri = lax.broadcasted_iota(I32, (tm, tm), 0)
    ci = lax.broadcasted_iota(I32, (tm, tm), 1)
    ltri = jnp.where(ri > ci, 1.0, 0.0).astype(BF16)
    pref = jnp.dot(ltri, m_oh.astype(BF16), preferred_element_type=F32) + carry[0:1, :]
    r_out = jnp.zeros(logits.shape, I32)
    for k in range(TOP_K):
        rk = jnp.sum(jnp.where(ohs[k], pref, 0.0), axis=-1, keepdims=True)
        r_out = jnp.where(lane == k, rk.astype(I32), r_out)
    rk_ref[0] = r_out.T[:TOP_K, :]
    tot = carry[0:1, :] + jnp.sum(m_oh, axis=0, keepdims=True)
    carry[...] = jnp.broadcast_to(tot, carry.shape)
    cnt_ref[...] = jnp.broadcast_to(tot, cnt_ref.shape)


def _mixout_call(ya, bb, p, cbw, ga, gb, wo_bf, x, gt1, gf, sh2, sc2, wr_bf, brt):
    bsz, n, d = x.shape
    d_a = ya.shape[-1]
    d_b = bb.shape[-1]
    half = d_b // 2
    tm = _pick(n, 512, GRID_W)
    n_tiles = n // tm
    nseg = (d // 2) // LANES
    rpt = tm // GRID_W
    n_rows = n // GRID_W
    row = lambda b, i: (b, i, 0)
    vec = lambda b, i: (b, 0, 0)
    const2 = lambda b, i: (0, 0)
    return pl.pallas_call(
        functools.partial(_mixout_kernel, tm=tm, d_a=d_a, d_b=d_b, n_tiles=n_tiles),
        out_shape=(_sds((bsz, n, d), F32), _sds((bsz, n * nseg, LANES), U32),
                   _sds((bsz, TOP_K, n), I32), _sds((bsz, n, LANES), F32),
                   _sds((bsz, TOP_K, n), I32), _sds((SUBLANES, LANES), F32)),
        grid=(bsz, n_tiles),
        in_specs=[pl.BlockSpec((1, tm, d_a), row),
                  pl.BlockSpec((1, tm, d_b), row),
                  pl.BlockSpec((1, tm, d_b), row),
                  pl.BlockSpec((1, GRID_W, half), lambda b, i: (b, jnp.maximum(i * rpt - 1, 0), 1)),
                  pl.BlockSpec((1, GRID_W, half), lambda b, i: (b, jnp.minimum((i + 1) * rpt, n_rows - 1), 1)),
                  pl.BlockSpec((3, d_b), const2),
                  pl.BlockSpec((1, d_a), const2),
                  pl.BlockSpec((1, d_b), const2),
                  pl.BlockSpec((d_a + d_b, d), const2),
                  pl.BlockSpec((1, tm, d), row),
                  pl.BlockSpec((1, 1, d), vec),
                  pl.BlockSpec((1, d), const2),
                  pl.BlockSpec((1, 1, d), vec),
                  pl.BlockSpec((1, 1, d), vec),
                  pl.BlockSpec((d, LANES), const2),
                  pl.BlockSpec((1, LANES), const2)],
        out_specs=(pl.BlockSpec((1, tm, d), row), pl.BlockSpec((1, tm * nseg, LANES), row),
                   pl.BlockSpec((1, TOP_K, tm), lambda b, i: (b, 0, i)), pl.BlockSpec((1, tm, LANES), row),
                   pl.BlockSpec((1, TOP_K, tm), lambda b, i: (b, 0, i)), pl.BlockSpec((SUBLANES, LANES), const2)),
        scratch_shapes=[pltpu.VMEM((SUBLANES, LANES), F32)],
        compiler_params=_params(("arbitrary", "arbitrary")),
        name="mixout",
    )(ya, bb, p, p, p, cbw, ga, gb, wo_bf, x, gt1, gf, sh2, sc2, wr_bf, brt)


def _zero_tail(first, n_blocks, zbuf, dst_block, sem):
    zbuf[...] = jnp.zeros(zbuf.shape, zbuf.dtype)

    def start(c, carry):
        pltpu.make_async_copy(zbuf, dst_block(c), sem).start()
        return carry

    def wait(c, carry):
        pltpu.make_async_copy(zbuf, dst_block(c), sem).wait()
        return carry

    lax.fori_loop(first, n_blocks, start, 0)
    lax.fori_loop(first, n_blocks, wait, 0)


def _dispatch_kernel(bs_ref, nb_ref, pos_ref, x_ref, o_hbm, zbuf, sem, zsem, *, td, rb, n_e, n_blocks, nseg):
    rbr = rb * nseg

    def blk(b):
        return pl.ds(pl.multiple_of(b * rbr, rbr), rbr)

    def tok(t):
        return pl.ds(pl.multiple_of(t * nseg, nseg), nseg)

    @pl.when(pl.program_id(0) == 0)
    def _():
        _zero_tail(bs_ref[n_e - 1] + nb_ref[n_e - 1], n_blocks, zbuf, lambda c: o_hbm.at[blk(c)], zsem)

        def zero_copy(e):
            return pltpu.make_async_copy(zbuf, o_hbm.at[blk(bs_ref[e] + nb_ref[e] - 1)], zsem)

        for e in range(n_e):
            @pl.when(nb_ref[e] > 0)
            def _():
                zero_copy(e).start()

        for e in range(n_e):
            @pl.when(nb_ref[e] > 0)
            def _():
                zero_copy(e).wait()

    def body(t, c):
        src = x_ref.at[tok(t)]
        for k in range(TOP_K):
            pltpu.make_async_copy(src, o_hbm.at[tok(pos_ref[0, k, t])], sem).start(priority=k % 2)
        return c

    lax.fori_loop(0, td, body, 0, unroll=2)
    for _ in range(TOP_K):
        pltpu.make_async_copy(x_ref, o_hbm.at[pl.ds(0, td * nseg)], sem).wait()


def _dispatch_call(groups, xp, pos, n_buf, rb, nseg):
    bstart, nblk = groups
    n_e = bstart.shape[0]
    t = xp.shape[0] // nseg
    n = pos.shape[-1]
    td = _pick(n, 512, SUBLANES)
    nt = n // td
    return pl.pallas_call(
        functools.partial(_dispatch_kernel, td=td, rb=rb, n_e=n_e, n_blocks=n_buf // rb, nseg=nseg),
        out_shape=_sds((n_buf * nseg, LANES), U32),
        grid_spec=pltpu.PrefetchScalarGridSpec(
            num_scalar_prefetch=2,
            grid=(t // td,),
            in_specs=[pl.BlockSpec((1, TOP_K, td), lambda i, bs, nb: (i // nt, 0, i % nt),
                                   memory_space=pltpu.SMEM),
                      pl.BlockSpec((td * nseg, LANES), lambda i, bs, nb: (i, 0))],
            out_specs=pl.BlockSpec(memory_space=pl.ANY),
            scratch_shapes=[pltpu.VMEM((rb * nseg, LANES), U32), pltpu.SemaphoreType.DMA,
                            pltpu.SemaphoreType.DMA]),
        compiler_params=_params(("arbitrary",)),
        name="dispatch",
    )(bstart, nblk, pos, xp)


GROUP_IN_SLOTS = 4
GROUP_OUT_SLOTS = 3


def _build_chunks(bs_ref, nb_ref, ce, cs, cb, cn, n_split):
    def per_expert(e, j):
        nb = nb_ref[e]
        b0 = bs_ref[e]

        def per_slab(s_, j):
            def per_pair(c, j):
                ce[j] = e
                cs[j] = s_
                cb[j] = b0 + 2 * c
                cn[j] = jnp.minimum(nb - 2 * c, 2)
                return j + 1

            return lax.fori_loop(0, (nb + 1) // 2, per_pair, j)

        return lax.fori_loop(0, n_split, per_slab, j)

    return lax.fori_loop(0, bs_ref.shape[0], per_expert, jnp.int32(0))


def _chunk_loop(n, nb_ref, ce, cs, cn, in_copy, out_copy, w_copies, load_weights, compute):
    ahead = GROUP_IN_SLOTS - 1

    def by_size(j, fn):
        @pl.when(cn[j] == 2)
        def _():
            fn(2)

        @pl.when(cn[j] == 1)
        def _():
            fn(1)

    @pl.when(n > 0)
    def _():
        for cp in w_copies(ce[0], cs[0]):
            cp.start()
        for j in range(ahead):
            @pl.when(j < n)
            def _():
                by_size(j, lambda m: in_copy(j, j, m).start())

        def body(j, carry):
            jp = jnp.maximum(j - 1, 0)
            first = jnp.logical_or(j == 0, jnp.logical_or(ce[j] != ce[jp], cs[j] != cs[jp]))

            @pl.when(first)
            def _():
                for cp in w_copies(ce[j], cs[j]):
                    cp.wait()

            islot = lax.rem(j, GROUP_IN_SLOTS)
            oslot = lax.rem(j, GROUP_OUT_SLOTS)

            @pl.when(j + ahead < n)
            def _():
                ja = j + ahead
                by_size(ja, lambda m: in_copy(ja, lax.rem(ja, GROUP_IN_SLOTS), m).start())

            by_size(j, lambda m: in_copy(j, islot, m).wait())

            @pl.when(j >= GROUP_OUT_SLOTS)
            def _():
                jo = j - GROUP_OUT_SLOTS
                by_size(jo, lambda m: out_copy(jo, oslot, m).wait())

            @pl.when(first)
            def _():
                def run(m):
                    load_weights()
                    compute(j, islot, oslot, m)
                    out_copy(j, oslot, m).start()

                by_size(j, run)
                jn = j + (nb_ref[ce[j]] + 1) // 2

                @pl.when(jn < n)
                def _():
                    for cp in w_copies(ce[jn], cs[jn]):
                        cp.start()

            @pl.when(jnp.logical_not(first))
            def _():
                def run(m):
                    compute(j, islot, oslot, m)
                    out_copy(j, oslot, m).start()

                by_size(j, run)

            return carry

        lax.fori_loop(0, n, body, 0)

        for k in range(GROUP_OUT_SLOTS):
            @pl.when(n > k)
            def _():
                jl = n - 1 - k
                by_size(jl, lambda m: out_copy(jl, lax.rem(jl, GROUP_OUT_SLOTS), m).wait())


def _gmm1_kernel(bs_ref, nb_ref, xs_hbm, wg_hbm, wu_hbm, bg_ref, bu_ref, act_hbm,
                 xbuf, obuf, wraw, wbf, zbuf, ce, cs, cb, cn, sin, sout, wsem, zsem,
                 *, rb, tf, n_blocks, n_split, nseg):
    def blk(b, m=1):
        return pl.ds(pl.multiple_of(b * rb, rb), m * rb)

    def tok_blk(b, m=1):
        return pl.ds(pl.multiple_of(b * (rb * nseg), rb * nseg), m * rb * nseg)

    def cols(s_):
        return pl.ds(pl.multiple_of(s_ * tf, tf), tf)

    n_e = bs_ref.shape[0]
    for s_ in range(n_split):
        _zero_tail(bs_ref[n_e - 1] + nb_ref[n_e - 1], n_blocks, zbuf,
                   lambda c: act_hbm.at[blk(c), cols(s_)], zsem)
    n = _build_chunks(bs_ref, nb_ref, ce, cs, cb, cn, n_split)

    def in_copy(j, slot, m):
        return pltpu.make_async_copy(xs_hbm.at[tok_blk(cb[j], m)],
                                     xbuf.at[slot, pl.ds(0, m * rb * nseg)], sin.at[slot])

    def out_copy(j, slot, m):
        return pltpu.make_async_copy(obuf.at[slot, pl.ds(0, m * rb)],
                                     act_hbm.at[blk(cb[j], m), cols(cs[j])], sout.at[slot])

    def w_copies(e, s_):
        return (pltpu.make_async_copy(wg_hbm.at[e, :, cols(s_)], wraw.at[0], wsem),
                pltpu.make_async_copy(wu_hbm.at[e, :, cols(s_)], wraw.at[1], wsem))

    def load_weights():
        wbf[:, 0:tf] = wraw[0].astype(BF16)
        wbf[:, tf:2 * tf] = wraw[1].astype(BF16)

    def compute(j, islot, oslot, m):
        g = ce[j] * n_split + cs[j]
        lo, hi = _unpack_pairs(_load_token_tiles(xbuf.at[islot], m * rb, nseg))
        h = jnp.concatenate([lo.astype(BF16), hi.astype(BF16)], axis=1)
        gu = jnp.dot(h, wbf[...], preferred_element_type=F32)
        gate = jnp.minimum(gu[:, :tf] + bg_ref[g], SWIGLU_LIMIT)
        up = jnp.clip(gu[:, tf:] + bu_ref[g], -SWIGLU_LIMIT, SWIGLU_LIMIT)
        act = (up + 1.0) * gate * _sigmoid(SWIGLU_ALPHA * gate)
        obuf[oslot, 0:m * rb, :] = act.astype(BF16)

    _chunk_loop(n, nb_ref, ce, cs, cn, in_copy, out_copy, w_copies, load_weights, compute)


def _gmm1_call(groups, xs, wg, wu, bg, bu, rb, tf, nseg):
    n_buf = xs.shape[0] // nseg
    n_e, d, dff = wg.shape
    n_split = dff // tf
    max_chunks = n_split * (n_buf // rb)
    vmem_full = lambda shape: pl.BlockSpec(shape, lambda i, *_: (0,) * len(shape))
    return pl.pallas_call(
        functools.partial(_gmm1_kernel, rb=rb, tf=tf, n_blocks=n_buf // rb, n_split=n_split, nseg=nseg),
        out_shape=_sds((n_buf, dff), BF16),
        grid_spec=pltpu.PrefetchScalarGridSpec(
            num_scalar_prefetch=2,
            grid=(1,),
            in_specs=[pl.BlockSpec(memory_space=pl.ANY),
                      pl.BlockSpec(memory_space=pl.ANY),
                      pl.BlockSpec(memory_space=pl.ANY),
                      vmem_full((n_e * n_split, 1, tf)),
                      vmem_full((n_e * n_split, 1, tf))],
            out_specs=pl.BlockSpec(memory_space=pl.ANY),
            scratch_shapes=[pltpu.VMEM((GROUP_IN_SLOTS, 2 * rb * nseg, LANES), U32),
                            pltpu.VMEM((GROUP_OUT_SLOTS, 2 * rb, tf), BF16),
                            pltpu.VMEM((2, d, tf), F32),
                            pltpu.VMEM((d, 2 * tf), BF16),
                            pltpu.VMEM((rb, tf), BF16),
                            pltpu.SMEM((max_chunks,), I32),
                            pltpu.SMEM((max_chunks,), I32),
                            pltpu.SMEM((max_chunks,), I32),
                            pltpu.SMEM((max_chunks,), I32),
                            pltpu.SemaphoreType.DMA((GROUP_IN_SLOTS,)),
                            pltpu.SemaphoreType.DMA((GROUP_OUT_SLOTS,)),
                            pltpu.SemaphoreType.DMA,
                            pltpu.SemaphoreType.DMA]),
        compiler_params=_params(("arbitrary",)),
        name="gmm1",
    )(*groups, xs, wg, wu, bg.reshape(n_e * n_split, 1, tf), bu.reshape(n_e * n_split, 1, tf))


def _gmm2_kernel(bs_ref, nb_ref, act_hbm, wd_hbm, bd_ref, y_hbm,
                 abuf, ybuf, wraw, wbf, zbuf, ce, cs, cb, cn, sin, sout, wsem, zsem, *, rb, n_blocks, nseg):
    def blk(b, m=1):
        return pl.ds(pl.multiple_of(b * rb, rb), m * rb)

    def tok_blk(b, m=1):
        return pl.ds(pl.multiple_of(b * (rb * nseg), rb * nseg), m * rb * nseg)

    n_e = bs_ref.shape[0]
    _zero_tail(bs_ref[n_e - 1] + nb_ref[n_e - 1], n_blocks, zbuf, lambda c: y_hbm.at[tok_blk(c)], zsem)
    n = _build_chunks(bs_ref, nb_ref, ce, cs, cb, cn, 1)

    def in_copy(j, slot, m):
        return pltpu.make_async_copy(act_hbm.at[blk(cb[j], m)], abuf.at[slot, pl.ds(0, m * rb)],
                                     sin.at[slot])

    def out_copy(j, slot, m):
        return pltpu.make_async_copy(ybuf.at[slot, pl.ds(0, m * rb * nseg)],
                                     y_hbm.at[tok_blk(cb[j], m)], sout.at[slot])

    def w_copies(e, s_):
        return (pltpu.make_async_copy(wd_hbm.at[e], wraw, wsem),)

    def load_weights():
        wbf[...] = wraw[...].astype(BF16)

    def compute(j, islot, oslot, m):
        y = jnp.dot(abuf[islot, 0:m * rb, :], wbf[...], preferred_element_type=F32) + bd_ref[ce[j]]
        _store_token_tiles(ybuf.at[oslot], _pack_pairs(y), m * rb)

    _chunk_loop(n, nb_ref, ce, cs, cn, in_copy, out_copy, w_copies, load_weights, compute)


def _gmm2_call(groups, act, wd, bd, rb, nseg):
    n_buf, dff = act.shape
    n_e, _, d = wd.shape
    assert d == 2 * nseg * LANES
    max_chunks = n_buf // rb
    vmem_full = lambda shape: pl.BlockSpec(shape, lambda i, *_: (0,) * len(shape))
    return pl.pallas_call(
        functools.partial(_gmm2_kernel, rb=rb, n_blocks=n_buf // rb, nseg=nseg),
        out_shape=_sds((n_buf * nseg, LANES), U32),
        grid_spec=pltpu.PrefetchScalarGridSpec(
            num_scalar_prefetch=2,
            grid=(1,),
            in_specs=[pl.BlockSpec(memory_space=pl.ANY),
                      pl.BlockSpec(memory_space=pl.ANY),
                      vmem_full((n_e, 1, d))],
            out_specs=pl.BlockSpec(memory_space=pl.ANY),
            scratch_shapes=[pltpu.VMEM((GROUP_IN_SLOTS, 2 * rb, dff), BF16),
                            pltpu.VMEM((GROUP_OUT_SLOTS, 2 * rb * nseg, LANES), U32),
                            pltpu.VMEM((dff, d), F32),
                            pltpu.VMEM((dff, d), BF16),
                            pltpu.VMEM((rb * nseg, LANES), U32),
                            pltpu.SMEM((max_chunks,), I32),
                            pltpu.SMEM((max_chunks,), I32),
                            pltpu.SMEM((max_chunks,), I32),
                            pltpu.SMEM((max_chunks,), I32),
                            pltpu.SemaphoreType.DMA((GROUP_IN_SLOTS,)),
                            pltpu.SemaphoreType.DMA((GROUP_OUT_SLOTS,)),
                            pltpu.SemaphoreType.DMA,
                            pltpu.SemaphoreType.DMA]),
        compiler_params=_params(("arbitrary",)),
        name="gmm2",
    )(*groups, act, wd, bd.reshape(n_e, 1, d))


def _final_kernel(pos_ref, posn_ref, y_hbm, tw_ref, x1_ref, gt_ref, gfin_ref, o_ref, ybuf, sem, *,
                  tc, n_steps, nseg):
    step = pl.program_id(0) * pl.num_programs(1) + pl.program_id(1)

    def tok(t):
        return pl.ds(pl.multiple_of(t * nseg, nseg), nseg)

    def issue(p_ref, slot):
        def body(t, c):
            for k in range(TOP_K):
                src = y_hbm.at[tok(p_ref[0, k, t])]
                pltpu.make_async_copy(src, ybuf.at[slot, k, tok(t)], sem.at[slot]).start(priority=k % 2)
            return c

        lax.fori_loop(0, tc, body, 0, unroll=2)

    def run(cur):
        @pl.when(step + 1 < n_steps)
        def _():
            issue(posn_ref, 1 - cur)

        for k in range(TOP_K):
            pltpu.make_async_copy(y_hbm.at[pl.ds(0, tc * nseg)], ybuf.at[cur, k], sem.at[cur]).wait()
        tw = tw_ref[0]
        lo, hi = None, None
        for k in range(TOP_K):
            lo_k, hi_k = _unpack_pairs(_load_token_tiles(ybuf.at[cur, k], tc, nseg))
            wk = tw[:, k:k + 1]
            lo = wk * lo_k if lo is None else lo + wk * lo_k
            hi = wk * hi_k if hi is None else hi + wk * hi_k
        moe = jnp.concatenate([lo, hi], axis=1)
        x2 = x1_ref[0] + gt_ref[0] * moe
        o_ref[0] = _rms(x2, gfin_ref[...])

    @pl.when(step == 0)
    def _():
        issue(pos_ref, 0)

    parity = lax.rem(step, 2)

    @pl.when(parity == 0)
    def _():
        run(0)

    @pl.when(parity == 1)
    def _():
        run(1)


def _final_call(y, pos, top_w, x1, gt2, gfin, nseg):
    bsz, n, d = x1.shape
    tc = _pick(n, 256, SUBLANES)
    nt = n // tc
    n_steps = bsz * nt

    def next_pos(b, i):
        nxt = jnp.minimum(b * nt + i + 1, n_steps - 1)
        return (nxt // nt, 0, nxt % nt)

    row = lambda b, i: (b, i, 0)
    return pl.pallas_call(
        functools.partial(_final_kernel, tc=tc, n_steps=n_steps, nseg=nseg),
        out_shape=_sds((bsz, n, d), F32),
        grid=(bsz, nt),
        in_specs=[pl.BlockSpec((1, TOP_K, tc), lambda b, i: (b, 0, i), memory_space=pltpu.SMEM),
                  pl.BlockSpec((1, TOP_K, tc), next_pos, memory_space=pltpu.SMEM),
                  pl.BlockSpec(memory_space=pl.ANY),
                  pl.BlockSpec((1, tc, LANES), row),
                  pl.BlockSpec((1, tc, d), row),
                  pl.BlockSpec((1, 1, d), lambda b, i: (b, 0, 0)),
                  pl.BlockSpec((1, d), lambda b, i: (0, 0))],
        out_specs=pl.BlockSpec((1, tc, d), row),
        scratch_shapes=[pltpu.VMEM((2, TOP_K, tc * nseg, LANES), U32), pltpu.SemaphoreType.DMA((2,))],
        compiler_params=_params(("arbitrary", "arbitrary")),
        name="final",
    )(pos, pos, y, top_w, x1, gt2, gfin)


def _group_tables(counts, rb):
    nblk = (counts + rb - 1) // rb
    bstart = jnp.cumsum(nblk) - nblk
    return bstart * rb, (bstart.astype(I32), nblk.astype(I32))


def kernel(x, c, ctx, c_ctx, w_mod, b_mod, g_mix, w_in, conv_a_w, conv_a_b, lru_w_r, lru_b_r,
           lru_w_i, lru_b_i, lru_lam, conv_b_w, g_out_a, g_out_b, w_out, g_ffn, w_router,
           b_router, w_gate, b_gate, w_up, b_up, w_down, b_down, g_final):
    assert w_mod.shape[0] == 1, "single-layer block"
    bsz, n_lat, d = x.shape
    d_a = conv_a_w.shape[-1]
    d_b = conv_b_w.shape[-1]
    n_e = w_router.shape[-1]
    assert n_lat % GRID_W == 0 and n_e <= LANES and d % (2 * LANES) == 0
    l = 0

    cs = jnp.zeros((SUBLANES, d), F32).at[:bsz].set(c).at[bsz].set(c_ctx)
    mod = _mod_call(cs, w_mod[l], b_mod[l][None])
    sh1, sc1, gt1, sh2, sc2, gt2 = [m[:bsz, None, :] for m in jnp.split(mod, 6, axis=-1)]
    ssh1, ssc1 = [jnp.broadcast_to(m[bsz][None, None, :], (bsz, 1, d))
                  for m in jnp.split(mod, 6, axis=-1)[:2]]

    w_in_bf = w_in[l].astype(BF16)
    g_mix2 = g_mix[l][None]
    cw = conv_a_w[l]
    cb = conv_a_b[l][None]
    wri = jnp.concatenate([lru_w_r[l], lru_w_i[l]], axis=-1).astype(BF16)
    br, bi, lam = lru_b_r[l], lru_b_i[l], lru_lam[l]
    lru_p = lambda dr: (cw, cb, wri[dr], br[dr][None], bi[dr][None], lam[dr][None])

    (s_ax,) = _inproj_call(ctx, g_mix2, ssh1, ssc1, w_in_bf[:, d_a:2 * d_a], d_a, d_b, latent=False)
    zero_state = jnp.zeros((bsz, 1, d_a), F32)
    _, h0f = _lru_call(s_ax, zero_state, *lru_p(0), reverse=False)
    _, h0b = _lru_call(s_ax, zero_state, *lru_p(1), reverse=True)

    ag, ax, bb, p = _inproj_call(x, g_mix2, sh1, sc1, w_in_bf, d_a, d_b, latent=True)
    hf, _ = _lru_call(ax, h0f, *lru_p(0), reverse=False)
    ya, _ = _lru_call(ax, h0b, *lru_p(1), reverse=True, hf=hf, ag=ag)

    wr_bf = jnp.zeros((d, LANES), BF16).at[:, :n_e].set(w_router[l].astype(BF16))
    brt = jnp.full((1, LANES), NEG_BIG, F32).at[0, :n_e].set(b_router[l])
    x1, xp, top_e, top_w, rank, cnt = _mixout_call(
        ya, bb, p, conv_b_w[l], g_out_a[l][None], g_out_b[l][None], w_out[l].astype(BF16), x, gt1,
        g_ffn[l][None], sh2, sc2, wr_bf, brt)

    n_tok = bsz * n_lat
    n_rows = n_tok * TOP_K
    rb = MOE_ROW_BLOCK
    n_buf = n_rows + n_e * rb
    counts = cnt[0, :n_e].astype(I32)
    starts, groups = _group_tables(counts, rb)
    onehot = top_e[..., None] == jnp.arange(n_e, dtype=I32)
    pos = (jnp.sum(jnp.where(onehot, starts, 0), axis=-1) + rank).astype(I32)
    nseg = (d // 2) // LANES
    xs = _dispatch_call(groups, xp.reshape(n_tok * nseg, LANES), pos, n_buf, rb, nseg)
    d_ff = w_gate.shape[-1]
    tf = _pick(d_ff, 1024)
    act = _gmm1_call(groups, xs, w_gate[l], w_up[l], b_gate[l], b_up[l], rb, tf, nseg)
    y = _gmm2_call(groups, act, w_down[l], b_down[l], rb, nseg)
    return _final_call(y, pos, top_w, x1, gt2, g_final[None], nseg)
```

```python
import functools

import jax
import jax.numpy as jnp
from jax import lax
from jax.experimental import pallas as pl
from jax.experimental.pallas import tpu as pltpu

F32 = jnp.float32
BF16 = jnp.bfloat16
I32 = jnp.int32
U32 = jnp.uint32

GRID_W = 64
TOP_K = 4
LRU_C = 8.0
CONV_A_LEFT = 2
SCAN_SEG = 4
SWIGLU_LIMIT = 7.0
SWIGLU_ALPHA = 1.702
EPS = 1e-6
MOE_ROW_BLOCK = 256

TOKEN_TILE = 512
COMBINE_TILE = 256
GATE_ROWS = 128
MOD_COLS = 1024
GMM1_SLAB = 1024

LANES = 128
SUBLANES = 8
VMEM_LIMIT_BYTES = 56 * 1024 * 1024
NEG_BIG = -1e30


def _sds(shape, dtype):
    return jax.ShapeDtypeStruct(shape, dtype)


def _pick(n, pref, mult=LANES):
    if n <= pref:
        return n
    t = (pref // mult) * mult
    while t >= mult:
        if n % t == 0:
            return t
        t -= mult
    return n


def _params(sem):
    return pltpu.CompilerParams(dimension_semantics=sem, vmem_limit_bytes=VMEM_LIMIT_BYTES)


def _sigmoid(x):
    return 1.0 / (1.0 + jnp.exp(-x))


def _rms(x, g):
    ms = jnp.mean(x * x, axis=-1, keepdims=True)
    return (x * lax.rsqrt(ms + EPS)) * g


def _pack_pairs(x):
    w = x.shape[1] // 2
    lo = lax.bitcast_convert_type(x[:, :w].astype(BF16).astype(F32), U32)
    hi = lax.bitcast_convert_type(x[:, w:].astype(BF16).astype(F32), U32)
    return lax.shift_right_logical(lo, jnp.uint32(16)) | (hi & jnp.uint32(0xFFFF0000))


def _unpack_pairs(words):
    lo = lax.bitcast_convert_type(lax.shift_left(words, jnp.uint32(16)), F32)
    hi = lax.bitcast_convert_type(words & jnp.uint32(0xFFFF0000), F32)
    return lo, hi


def _store_token_tiles(ref, words, rows):
    nseg = words.shape[1] // LANES
    if nseg == 1:
        ref[...] = words
        return
    for s in range(nseg):
        ref[pl.ds(s, rows, stride=nseg), :] = words[:, s * LANES:(s + 1) * LANES]


def _load_token_tiles(ref, rows, nseg):
    if nseg == 1:
        return ref[...]
    return jnp.concatenate([ref[pl.ds(s, rows, stride=nseg), :] for s in range(nseg)], axis=1)


def _mod_kernel(c_ref, w_ref, b_ref, o_ref):
    c = c_ref[...]
    s = c * _sigmoid(c)
    o_ref[...] = jnp.dot(s.astype(BF16), w_ref[...].astype(BF16),
                         preferred_element_type=F32) + b_ref[...]


def _mod_call(cs, w, b):
    d, n6 = w.shape
    tn = _pick(n6, MOD_COLS)
    return pl.pallas_call(
        _mod_kernel,
        out_shape=_sds((cs.shape[0], n6), F32),
        grid=(n6 // tn,),
        in_specs=[pl.BlockSpec((cs.shape[0], d), lambda j: (0, 0)),
                  pl.BlockSpec((d, tn), lambda j: (0, j)),
                  pl.BlockSpec((1, tn), lambda j: (0, j))],
        out_specs=pl.BlockSpec((cs.shape[0], tn), lambda j: (0, j)),
        compiler_params=_params(("arbitrary",)),
        name="mod",
    )(cs, w, b)


def _inproj_kernel(x_ref, g_ref, sh_ref, sc_ref, w_ref, *out_refs, d_a, d_b, latent):
    x = x_ref[0]
    xn = _rms(x, g_ref[...]) * (1.0 + sc_ref[0]) + sh_ref[0]
    xb = xn.astype(BF16)

    def sec(lo, width):
        return jnp.dot(xb, w_ref[:, lo:lo + width], preferred_element_type=F32)

    if not latent:
        out_refs[0][0] = sec(0, d_a)
        return
    ag_ref, ax_ref, bb_ref, p_ref = out_refs
    ag_ref[0] = sec(0, d_a).astype(BF16)
    ax_ref[0] = sec(d_a, d_a)
    bb_ref[0] = sec(2 * d_a, d_b).astype(BF16)
    p_ref[0] = (sec(2 * d_a + d_b, d_b) * sec(2 * d_a + 2 * d_b, d_b)).astype(BF16)


def _inproj_call(x, g, sh, sc, w_bf, d_a, d_b, latent):
    bsz, n, d = x.shape
    tm = _pick(n, TOKEN_TILE, SUBLANES)
    n_w = w_bf.shape[1]
    row = lambda b, i: (b, i, 0)
    if latent:
        out_shape = (_sds((bsz, n, d_a), BF16), _sds((bsz, n, d_a), F32),
                     _sds((bsz, n, d_b), BF16), _sds((bsz, n, d_b), BF16))
        out_specs = (pl.BlockSpec((1, tm, d_a), row), pl.BlockSpec((1, tm, d_a), row),
                     pl.BlockSpec((1, tm, d_b), row), pl.BlockSpec((1, tm, d_b), row))
    else:
        out_shape = (_sds((bsz, n, d_a), F32),)
        out_specs = (pl.BlockSpec((1, tm, d_a), row),)
    return pl.pallas_call(
        functools.partial(_inproj_kernel, d_a=d_a, d_b=d_b, latent=latent),
        out_shape=out_shape,
        grid=(bsz, n // tm),
        in_specs=[pl.BlockSpec((1, tm, d), row),
                  pl.BlockSpec((1, d), lambda b, i: (0, 0)),
                  pl.BlockSpec((1, 1, d), lambda b, i: (b, 0, 0)),
                  pl.BlockSpec((1, 1, d), lambda b, i: (b, 0, 0)),
                  pl.BlockSpec((d, n_w), lambda b, i: (0, 0), pipeline_mode=pl.Buffered(1))],
        out_specs=out_specs,
        compiler_params=_params(("arbitrary", "arbitrary")),
        name="inproj_lat" if latent else "inproj_ctx",
    )(x, g, sh, sc, w_bf)


def _gelu_tanh(x):
    c = 0.7978845608028654
    return x * (0.5 * (1.0 + jnp.tanh(c * (x + 0.044715 * (x * x * x)))))


def _lru_kernel(*refs, reverse, combine, nc, tl, heads, blk):
    if combine:
        (prev_ref, main_ref, next_ref, cw_ref, cb_ref, wri_ref, br_ref, bi_ref, lam_ref, h0_ref,
         hf_ref, ag_ref, out_ref, hlast_ref, ebuf, xc_s, a_s, b_s, carry) = refs
    else:
        (prev_ref, main_ref, next_ref, cw_ref, cb_ref, wri_ref, br_ref, bi_ref, lam_ref, h0_ref,
         out_ref, hlast_ref, ebuf, xc_s, a_s, b_s, carry) = refs
    da = heads * blk
    c = pl.program_id(1)
    cidx = (nc - 1 - c) if reverse else c

    @pl.when(c == 0)
    def _():
        carry[...] = jnp.broadcast_to(h0_ref[0], carry.shape)

    main = main_ref[0]
    zero8 = jnp.zeros((SUBLANES, da), F32)
    ebuf[0:SUBLANES, :] = jnp.where(cidx == 0, zero8, prev_ref[0])
    ebuf[SUBLANES:SUBLANES + tl, :] = main
    ebuf[SUBLANES + tl:2 * SUBLANES + tl, :] = jnp.where(cidx == nc - 1, zero8, next_ref[0])
    cw = cw_ref[...]
    off = SUBLANES - CONV_A_LEFT
    xc_s[...] = (cw[0:1] * ebuf[off:off + tl, :] + cw[1:2] * ebuf[off + 1:off + 1 + tl, :]
                 + cw[2:3] * main + cw[3:4] * ebuf[off + 3:off + 3 + tl, :] + cb_ref[...])

    z = -lam_ref[...]
    sp = jnp.maximum(z, 0.0) + jnp.log1p(jnp.exp(-jnp.abs(z)))
    spb = blk // LANES
    rc = min(tl, GATE_ROWS)
    for r0 in range(0, tl, rc):
        for h in range(heads):
            cs = slice(h * blk, (h + 1) * blk)
            xh = xc_s[r0:r0 + rc, cs]
            zz = jnp.dot(xh.astype(BF16), wri_ref[h], preferred_element_type=F32)
            r = _sigmoid(zz[:, :blk] + br_ref[:, cs])
            i = _sigmoid(zz[:, blk:] + bi_ref[:, cs])
            log_a = (-LRU_C * r) * sp[:, cs]
            a = jnp.exp(log_a)
            v = jnp.tanh(-log_a) * (1.0 + a * a)
            b = jnp.where(v > 0.0, v * lax.rsqrt(v), 0.0) * (i * xh)
            for q in range(spb):
                a_s[h * spb + q, r0:r0 + rc, :] = a[:, q * LANES:(q + 1) * LANES]
                b_s[h * spb + q, r0:r0 + rc, :] = b[:, q * LANES:(q + 1) * LANES]

    n_slab = da // LANES
    sub_rows = SUBLANES * SCAN_SEG
    n_sub = tl // sub_rows
    row = lax.broadcasted_iota(I32, (SUBLANES, LANES), 0)
    ks = list(range(SCAN_SEG - 1, -1, -1)) if reverse else list(range(SCAN_SEG))

    def seg_scan(at, bt):
        for s in (1, 2, 4):
            if reverse:
                keep = row < (SUBLANES - s)
                sh = SUBLANES - s
            else:
                keep = row >= s
                sh = s
            a_sh = jnp.where(keep, pltpu.roll(at, sh, 0), 1.0)
            b_sh = jnp.where(keep, pltpu.roll(bt, sh, 0), 0.0)
            bt = at * b_sh + bt
            at = at * a_sh
        return at, bt

    def body(si, hcs):
        sub = (n_sub - 1 - si) if reverse else si
        base = sub * sub_rows
        out = []
        for slab in range(n_slab):
            hc = hcs[slab]
            rows = [pl.ds(base + k, SUBLANES, stride=SCAN_SEG) for k in range(SCAN_SEG)]
            acc_a, acc_b = {}, {}
            a_run = b_run = None
            for k in ks:
                ak = a_s[slab, rows[k], :]
                bk = b_s[slab, rows[k], :]
                if a_run is None:
                    a_run, b_run = ak, bk
                else:
                    b_run = ak * b_run + bk
                    a_run = ak * a_run
                acc_a[k], acc_b[k] = a_run, b_run
            at, bt = seg_scan(a_run, b_run)
            h_out = at * hc + bt
            if reverse:
                h_in = jnp.where(row < SUBLANES - 1, pltpu.roll(h_out, SUBLANES - 1, 0), hc)
                edge = h_out[0:1, :]
            else:
                h_in = jnp.where(row >= 1, pltpu.roll(h_out, 1, 0), hc)
                edge = h_out[SUBLANES - 1:SUBLANES, :]
            for k in ks:
                b_s[slab, rows[k], :] = acc_a[k] * h_in + acc_b[k]
            out.append(jnp.broadcast_to(edge, (SUBLANES, LANES)))
        return tuple(out)

    hcs = tuple(carry[:, slab * LANES:(slab + 1) * LANES] for slab in range(n_slab))
    hcs = lax.fori_loop(0, n_sub, body, hcs)
    for slab in range(n_slab):
        ls = slice(slab * LANES, (slab + 1) * LANES)
        carry[:, ls] = hcs[slab]
        hlast_ref[0, :, ls] = hcs[slab][0:1, :]
        if combine:
            hsum = hf_ref[0, :, ls] + b_s[slab]
            out_ref[0, :, ls] = (_gelu_tanh(ag_ref[0, :, ls].astype(F32)) * hsum).astype(out_ref.dtype)
        else:
            out_ref[0, :, ls] = b_s[slab]


def _lru_call(ax, h0, cw, cb, wri, br, bi, lam, *, reverse, hf=None, ag=None):
    bsz, n, da = ax.shape
    heads, blk, _ = wri.shape
    tl = _pick(n, TOKEN_TILE, SUBLANES * SCAN_SEG)
    assert blk % LANES == 0 and tl % (SUBLANES * SCAN_SEG) == 0
    nc = n // tl
    nb8 = n // SUBLANES
    g8 = tl // SUBLANES
    combine = hf is not None

    def cidx(c):
        return (nc - 1 - c) if reverse else c

    main_map = lambda b, c: (b, cidx(c), 0)
    prev_map = lambda b, c: (b, jnp.maximum(cidx(c) * g8 - 1, 0), 0)
    next_map = lambda b, c: (b, jnp.minimum((cidx(c) + 1) * g8, nb8 - 1), 0)
    const2 = lambda b, c: (0, 0)
    in_specs = [pl.BlockSpec((1, SUBLANES, da), prev_map),
                pl.BlockSpec((1, tl, da), main_map),
                pl.BlockSpec((1, SUBLANES, da), next_map),
                pl.BlockSpec((4, da), const2),
                pl.BlockSpec((1, da), const2),
                pl.BlockSpec((heads, blk, 2 * blk), lambda b, c: (0, 0, 0)),
                pl.BlockSpec((1, da), const2),
                pl.BlockSpec((1, da), const2),
                pl.BlockSpec((1, da), const2),
                pl.BlockSpec((1, 1, da), lambda b, c: (b, 0, 0))]
    args = [ax, ax, ax, cw, cb, wri, br, bi, lam, h0]
    if combine:
        in_specs += [pl.BlockSpec((1, tl, da), main_map), pl.BlockSpec((1, tl, da), main_map)]
        args += [hf, ag]
    out_dtype = BF16 if combine else F32
    return pl.pallas_call(
        functools.partial(_lru_kernel, reverse=reverse, combine=combine, nc=nc, tl=tl,
                          heads=heads, blk=blk),
        out_shape=(_sds((bsz, n, da), out_dtype), _sds((bsz, 1, da), F32)),
        grid=(bsz, nc),
        in_specs=in_specs,
        out_specs=(pl.BlockSpec((1, tl, da), main_map),
                   pl.BlockSpec((1, 1, da), lambda b, c: (b, 0, 0))),
        scratch_shapes=[pltpu.VMEM((tl + 2 * SUBLANES, da), F32),
                        pltpu.VMEM((tl, da), F32),
                        pltpu.VMEM((da // LANES, tl, LANES), F32),
                        pltpu.VMEM((da // LANES, tl, LANES), F32),
                        pltpu.VMEM((SUBLANES, da), F32)],
        compiler_params=_params(("arbitrary", "arbitrary")),
        name=("lru_bwd" if reverse else "lru_fwd") + ("_mix" if combine else ""),
    )(*args)


def _mixout_kernel(ya_ref, bb_ref, p_ref, pu_ref, pd_ref, cbw_ref, ga_ref, gb_ref, wo_ref, x_ref,
                   gt_ref, gf_ref, sh_ref, sc_ref, wr_ref, brt_ref,
                   x1_ref, xp_ref, te_ref, tw_ref, rk_ref, cnt_ref, x1_s, carry,
                   *, tm, d_a, d_b, n_tiles, n_total):
    s = pl.program_id(0)

    @pl.when(s == 0)
    def _():
        carry[...] = jnp.zeros(carry.shape, F32)
        x1_s[...] = jnp.zeros(x1_s.shape, F32)

    xn = _rms(x1_s[...], gf_ref[...]) * (1.0 + sc_ref[0]) + sh_ref[0]
    xb = xn.astype(BF16)
    _store_token_tiles(xp_ref.at[0], _pack_pairs(xn), tm)
    logits = jnp.dot(xb, wr_ref[...], preferred_element_type=F32) + brt_ref[...]

    i = lax.rem(jnp.minimum(s, n_total - 1), n_tiles)
    half = d_b // 2
    z = p_ref[0].astype(F32)
    w = cbw_ref[...]
    zh = z[:, :half]
    col = lax.broadcasted_iota(I32, (tm, half), 0) % GRID_W
    left = jnp.where(col >= 1, pltpu.roll(zh, 1, 0), 0.0)
    right = jnp.where(col <= GRID_W - 2, pltpu.roll(zh, tm - 1, 0), 0.0)
    horiz = w[0:1, :half] * left + w[1:2, :half] * zh + w[2:3, :half] * right
    zv = z[:, half:]
    up_halo = jnp.where(i == 0, 0.0, pu_ref[0].astype(F32))
    dn_halo = jnp.where(i == n_tiles - 1, 0.0, pd_ref[0].astype(F32))
    if tm > GRID_W:
        up = jnp.concatenate([up_halo, zv[:tm - GRID_W]], axis=0)
        dn = jnp.concatenate([zv[GRID_W:], dn_halo], axis=0)
    else:
        up, dn = up_halo, dn_halo
    vert = w[0:1, half:] * up + w[1:2, half:] * zv + w[2:3, half:] * dn
    bb = bb_ref[0].astype(F32)
    yb = jnp.concatenate([bb[:, :half] * horiz, bb[:, half:] * vert], axis=1)
    ya = ya_ref[0].astype(F32)
    ya_n = _rms(ya, ga_ref[...]).astype(BF16)
    yb_n = _rms(yb, gb_ref[...]).astype(BF16)
    mix = (jnp.dot(ya_n, wo_ref[0:d_a, :], preferred_element_type=F32)
           + jnp.dot(yb_n, wo_ref[d_a:d_a + d_b, :], preferred_element_type=F32))
    x1 = x_ref[0] + gt_ref[0] * mix
    x1_ref[0] = x1
    x1_s[...] = x1

    lane = lax.broadcasted_iota(I32, logits.shape, 1)
    lane_f = lane.astype(F32)
    vals = logits
    tv, te = [], []
    for _ in range(TOP_K):
        m = jnp.max(vals, axis=-1, keepdims=True)
        idx = jnp.min(jnp.where(vals == m, lane_f, float(LANES)), axis=-1, keepdims=True)
        tv.append(m)
        te.append(idx)
        vals = jnp.where(lane_f == idx, -jnp.inf, vals)
    ex = [jnp.exp(v - tv[0]) for v in tv]
    den = ex[0]
    for e in ex[1:]:
        den = den + e
    e_out = jnp.zeros(logits.shape, I32)
    w_out = jnp.zeros(logits.shape, F32)
    for k in range(TOP_K):
        e_out = jnp.where(lane == k, te[k].astype(I32), e_out)
        w_out = jnp.where(lane == k, ex[k] / den, w_out)
    te_ref[0] = e_out.T[:TOP_K, :]
    tw_ref[0] = w_out
    ohs = [lane_f == te[k] for k in range(TOP_K)]
    m_oh = jnp.zeros(logits.shape, F32)
    for oh in ohs:
        m_oh = m_oh + jnp.where(oh, 1.0, 0.0)
    ri = lax.broadcasted_iota(I32, (tm, tm), 0)
    ci = lax.broadcasted_iota(I32, (tm, tm), 1)
    ltri = jnp.where(ri > ci, 1.0, 0.0).astype(BF16)
    pref = jnp.dot(ltri, m_oh.astype(BF16), preferred_element_type=F32) + carry[0:1, :]
    r_out = jnp.zeros(logits.shape, I32)
    for k in range(TOP_K):
        rk = jnp.sum(jnp.where(ohs[k], pref, 0.0), axis=-1, keepdims=True)
        r_out = jnp.where(lane == k, rk.astype(I32), r_out)
    rk_ref[0] = r_out.T[:TOP_K, :]
    tot = carry[0:1, :] + jnp.where(s >= 1, jnp.sum(m_oh, axis=0, keepdims=True), 0.0)
    carry[...] = jnp.broadcast_to(tot, carry.shape)
    cnt_ref[...] = jnp.broadcast_to(tot, cnt_ref.shape)


def _mixout_call(ya, bb, p, cbw, ga, gb, wo_bf, x, gt1, gf, sh2, sc2, wr_bf, brt):
    bsz, n, d = x.shape
    d_a = ya.shape[-1]
    d_b = bb.shape[-1]
    half = d_b // 2
    tm = _pick(n, TOKEN_TILE, GRID_W)
    n_tiles = n // tm
    n_total = bsz * n_tiles
    nseg = (d // 2) // LANES
    rpt = tm // GRID_W
    n_rows = n // GRID_W

    def tile(s, lag):
        a = jnp.clip(s - lag, 0, n_total - 1)
        return a // n_tiles, a % n_tiles

    def row(lag):
        return lambda s: (tile(s, lag)[0], tile(s, lag)[1], 0)

    def vec(lag):
        return lambda s: (tile(s, lag)[0], 0, 0)

    tab1 = lambda s: (tile(s, 1)[0], 0, tile(s, 1)[1])
    const2 = lambda s: (0, 0)
    return pl.pallas_call(
        functools.partial(_mixout_kernel, tm=tm, d_a=d_a, d_b=d_b, n_tiles=n_tiles, n_total=n_total),
        out_shape=(_sds((bsz, n, d), F32), _sds((bsz, n * nseg, LANES), U32),
                   _sds((bsz, TOP_K, n), I32), _sds((bsz, n, LANES), F32),
                   _sds((bsz, TOP_K, n), I32), _sds((SUBLANES, LANES), F32)),
        grid=(n_total + 1,),
        in_specs=[pl.BlockSpec((1, tm, d_a), row(0)),
                  pl.BlockSpec((1, tm, d_b), row(0)),
                  pl.BlockSpec((1, tm, d_b), row(0)),
                  pl.BlockSpec((1, GRID_W, half),
                               lambda s: (tile(s, 0)[0], jnp.maximum(tile(s, 0)[1] * rpt - 1, 0), 1)),
                  pl.BlockSpec((1, GRID_W, half),
                               lambda s: (tile(s, 0)[0], jnp.minimum((tile(s, 0)[1] + 1) * rpt, n_rows - 1), 1)),
                  pl.BlockSpec((3, d_b), const2),
                  pl.BlockSpec((1, d_a), const2),
                  pl.BlockSpec((1, d_b), const2),
                  pl.BlockSpec((d_a + d_b, d), const2),
                  pl.BlockSpec((1, tm, d), row(0)),
                  pl.BlockSpec((1, 1, d), vec(0)),
                  pl.BlockSpec((1, d), const2),
                  pl.BlockSpec((1, 1, d), vec(1)),
                  pl.BlockSpec((1, 1, d), vec(1)),
                  pl.BlockSpec((d, LANES), const2),
                  pl.BlockSpec((1, LANES), const2)],
        out_specs=(pl.BlockSpec((1, tm, d), row(0)), pl.BlockSpec((1, tm * nseg, LANES), row(1)),
                   pl.BlockSpec((1, TOP_K, tm), tab1), pl.BlockSpec((1, tm, LANES), row(1)),
                   pl.BlockSpec((1, TOP_K, tm), tab1), pl.BlockSpec((SUBLANES, LANES), const2)),
        scratch_shapes=[pltpu.VMEM((tm, d), F32), pltpu.VMEM((SUBLANES, LANES), F32)],
        compiler_params=_params(("arbitrary",)),
        name="mixout",
    )(ya, bb, p, p, p, cbw, ga, gb, wo_bf, x, gt1, gf, sh2, sc2, wr_bf, brt)


def _zero_tail(first, n_blocks, zbuf, dst_block, sem):
    zbuf[...] = jnp.zeros(zbuf.shape, zbuf.dtype)

    def start(c, carry):
        pltpu.make_async_copy(zbuf, dst_block(c), sem).start()
        return carry

    def wait(c, carry):
        pltpu.make_async_copy(zbuf, dst_block(c), sem).wait()
        return carry

    lax.fori_loop(first, n_blocks, start, 0)
    lax.fori_loop(first, n_blocks, wait, 0)


def _dispatch_kernel(bs_ref, nb_ref, pos_ref, x_ref, o_hbm, zbuf, sem, zsem, *, td, rb, n_e, n_blocks, nseg):
    rbr = rb * nseg

    def blk(b):
        return pl.ds(pl.multiple_of(b * rbr, rbr), rbr)

    def tok(t):
        return pl.ds(pl.multiple_of(t * nseg, nseg), nseg)

    @pl.when(pl.program_id(0) == 0)
    def _():
        _zero_tail(bs_ref[n_e - 1] + nb_ref[n_e - 1], n_blocks, zbuf, lambda c: o_hbm.at[blk(c)], zsem)

        def zero_copy(e):
            return pltpu.make_async_copy(zbuf, o_hbm.at[blk(bs_ref[e] + nb_ref[e] - 1)], zsem)

        for e in range(n_e):
            @pl.when(nb_ref[e] > 0)
            def _():
                zero_copy(e).start()

        for e in range(n_e):
            @pl.when(nb_ref[e] > 0)
            def _():
                zero_copy(e).wait()

    def body(t, c):
        src = x_ref.at[tok(t)]
        for k in range(TOP_K):
            pltpu.make_async_copy(src, o_hbm.at[tok(pos_ref[0, k, t])], sem).start(priority=k % 2)
        return c

    lax.fori_loop(0, td, body, 0, unroll=2)
    for _ in range(TOP_K):
        pltpu.make_async_copy(x_ref, o_hbm.at[pl.ds(0, td * nseg)], sem).wait()


def _dispatch_call(groups, xp, pos, n_buf, rb, nseg):
    bstart, nblk = groups
    n_e = bstart.shape[0]
    t = xp.shape[0] // nseg
    n = pos.shape[-1]
    td = _pick(n, TOKEN_TILE, SUBLANES)
    nt = n // td
    return pl.pallas_call(
        functools.partial(_dispatch_kernel, td=td, rb=rb, n_e=n_e, n_blocks=n_buf // rb, nseg=nseg),
        out_shape=_sds((n_buf * nseg, LANES), U32),
        grid_spec=pltpu.PrefetchScalarGridSpec(
            num_scalar_prefetch=2,
            grid=(t // td,),
            in_specs=[pl.BlockSpec((1, TOP_K, td), lambda i, bs, nb: (i // nt, 0, i % nt),
                                   memory_space=pltpu.SMEM),
                      pl.BlockSpec((td * nseg, LANES), lambda i, bs, nb: (i, 0))],
            out_specs=pl.BlockSpec(memory_space=pl.ANY),
            scratch_shapes=[pltpu.VMEM((rb * nseg, LANES), U32), pltpu.SemaphoreType.DMA,
                            pltpu.SemaphoreType.DMA]),
        compiler_params=_params(("arbitrary",)),
        name="dispatch",
    )(bstart, nblk, pos, xp)


GROUP_IN_SLOTS = 4
GROUP_OUT_SLOTS = 3


def _build_chunks(bs_ref, nb_ref, ce, cs, cb, cn, n_split):
    def per_expert(e, j):
        nb = nb_ref[e]
        b0 = bs_ref[e]

        def per_slab(s_, j):
            def per_pair(c, j):
                ce[j] = e
                cs[j] = s_
                cb[j] = b0 + 2 * c
                cn[j] = jnp.minimum(nb - 2 * c, 2)
                return j + 1

            return lax.fori_loop(0, (nb + 1) // 2, per_pair, j)

        return lax.fori_loop(0, n_split, per_slab, j)

    return lax.fori_loop(0, bs_ref.shape[0], per_expert, jnp.int32(0))


def _chunk_loop(n, nb_ref, ce, cs, cn, in_copy, out_copy, w_copies, load_weights, compute):
    ahead = GROUP_IN_SLOTS - 1

    def by_size(j, fn):
        @pl.when(cn[j] == 2)
        def _():
            fn(2)

        @pl.when(cn[j] == 1)
        def _():
            fn(1)

    @pl.when(n > 0)
    def _():
        for cp in w_copies(ce[0], cs[0]):
            cp.start()
        for j in range(ahead):
            @pl.when(j < n)
            def _():
                by_size(j, lambda m: in_copy(j, j, m).start())

        def body(j, carry):
            jp = jnp.maximum(j - 1, 0)
            first = jnp.logical_or(j == 0, jnp.logical_or(ce[j] != ce[jp], cs[j] != cs[jp]))

            @pl.when(first)
            def _():
                for cp in w_copies(ce[j], cs[j]):
                    cp.wait()

            islot = lax.rem(j, GROUP_IN_SLOTS)
            oslot = lax.rem(j, GROUP_OUT_SLOTS)

            @pl.when(j + ahead < n)
            def _():
                ja = j + ahead
                by_size(ja, lambda m: in_copy(ja, lax.rem(ja, GROUP_IN_SLOTS), m).start())

            by_size(j, lambda m: in_copy(j, islot, m).wait())

            @pl.when(j >= GROUP_OUT_SLOTS)
            def _():
                jo = j - GROUP_OUT_SLOTS
                by_size(jo, lambda m: out_copy(jo, oslot, m).wait())

            @pl.when(first)
            def _():
                def run(m):
                    load_weights()
                    compute(j, islot, oslot, m)
                    out_copy(j, oslot, m).start()

                by_size(j, run)
                jn = j + (nb_ref[ce[j]] + 1) // 2

                @pl.when(jn < n)
                def _():
                    for cp in w_copies(ce[jn], cs[jn]):
                        cp.start()

            @pl.when(jnp.logical_not(first))
            def _():
                def run(m):
                    compute(j, islot, oslot, m)
                    out_copy(j, oslot, m).start()

                by_size(j, run)

            return carry

        lax.fori_loop(0, n, body, 0)

        for k in range(GROUP_OUT_SLOTS):
            @pl.when(n > k)
            def _():
                jl = n - 1 - k
                by_size(jl, lambda m: out_copy(jl, lax.rem(jl, GROUP_OUT_SLOTS), m).wait())


def _gmm1_kernel(bs_ref, nb_ref, xs_hbm, wg_hbm, wu_hbm, bg_ref, bu_ref, act_hbm,
                 xbuf, obuf, wraw, wbf, zbuf, ce, cs, cb, cn, sin, sout, wsem, zsem,
                 *, rb, tf, n_blocks, n_split, nseg):
    def blk(b, m=1):
        return pl.ds(pl.multiple_of(b * rb, rb), m * rb)

    def tok_blk(b, m=1):
        return pl.ds(pl.multiple_of(b * (rb * nseg), rb * nseg), m * rb * nseg)

    def cols(s_):
        return pl.ds(pl.multiple_of(s_ * tf, tf), tf)

    n_e = bs_ref.shape[0]
    for s_ in range(n_split):
        _zero_tail(bs_ref[n_e - 1] + nb_ref[n_e - 1], n_blocks, zbuf,
                   lambda c: act_hbm.at[blk(c), cols(s_)], zsem)
    n = _build_chunks(bs_ref, nb_ref, ce, cs, cb, cn, n_split)

    def in_copy(j, slot, m):
        return pltpu.make_async_copy(xs_hbm.at[tok_blk(cb[j], m)],
                                     xbuf.at[slot, pl.ds(0, m * rb * nseg)], sin.at[slot])

    def out_copy(j, slot, m):
        return pltpu.make_async_copy(obuf.at[slot, pl.ds(0, m * rb)],
                                     act_hbm.at[blk(cb[j], m), cols(cs[j])], sout.at[slot])

    def w_copies(e, s_):
        return (pltpu.make_async_copy(wg_hbm.at[e, :, cols(s_)], wraw.at[0], wsem),
                pltpu.make_async_copy(wu_hbm.at[e, :, cols(s_)], wraw.at[1], wsem))

    def load_weights():
        wbf[:, 0:tf] = wraw[0].astype(BF16)
        wbf[:, tf:2 * tf] = wraw[1].astype(BF16)

    def compute(j, islot, oslot, m):
        g = ce[j] * n_split + cs[j]
        lo, hi = _unpack_pairs(_load_token_tiles(xbuf.at[islot], m * rb, nseg))
        h = jnp.concatenate([lo.astype(BF16), hi.astype(BF16)], axis=1)
        gu = jnp.dot(h, wbf[...], preferred_element_type=F32)
        gate = jnp.minimum(gu[:, :tf] + bg_ref[g], SWIGLU_LIMIT)
        up = jnp.clip(gu[:, tf:] + bu_ref[g], -SWIGLU_LIMIT, SWIGLU_LIMIT)
        act = (up + 1.0) * gate * _sigmoid(SWIGLU_ALPHA * gate)
        obuf[oslot, 0:m * rb, :] = act.astype(BF16)

    _chunk_loop(n, nb_ref, ce, cs, cn, in_copy, out_copy, w_copies, load_weights, compute)


def _gmm1_call(groups, xs, wg, wu, bg, bu, rb, tf, nseg):
    n_buf = xs.shape[0] // nseg
    n_e, d, dff = wg.shape
    n_split = dff // tf
    max_chunks = n_split * (n_buf // rb)
    vmem_full = lambda shape: pl.BlockSpec(shape, lambda i, *_: (0,) * len(shape))
    return pl.pallas_call(
        functools.partial(_gmm1_kernel, rb=rb, tf=tf, n_blocks=n_buf // rb, n_split=n_split, nseg=nseg),
        out_shape=_sds((n_buf, dff), BF16),
        grid_spec=pltpu.PrefetchScalarGridSpec(
            num_scalar_prefetch=2,
            grid=(1,),
            in_specs=[pl.BlockSpec(memory_space=pl.ANY),
                      pl.BlockSpec(memory_space=pl.ANY),
                      pl.BlockSpec(memory_space=pl.ANY),
                      vmem_full((n_e * n_split, 1, tf)),
                      vmem_full((n_e * n_split, 1, tf))],
            out_specs=pl.BlockSpec(memory_space=pl.ANY),
            scratch_shapes=[pltpu.VMEM((GROUP_IN_SLOTS, 2 * rb * nseg, LANES), U32),
                            pltpu.VMEM((GROUP_OUT_SLOTS, 2 * rb, tf), BF16),
                            pltpu.VMEM((2, d, tf), F32),
                            pltpu.VMEM((d, 2 * tf), BF16),
                            pltpu.VMEM((rb, tf), BF16),
                            pltpu.SMEM((max_chunks,), I32),
                            pltpu.SMEM((max_chunks,), I32),
                            pltpu.SMEM((max_chunks,), I32),
                            pltpu.SMEM((max_chunks,), I32),
                            pltpu.SemaphoreType.DMA((GROUP_IN_SLOTS,)),
                            pltpu.SemaphoreType.DMA((GROUP_OUT_SLOTS,)),
                            pltpu.SemaphoreType.DMA,
                            pltpu.SemaphoreType.DMA]),
        compiler_params=_params(("arbitrary",)),
        name="gmm1",
    )(*groups, xs, wg, wu, bg.reshape(n_e * n_split, 1, tf), bu.reshape(n_e * n_split, 1, tf))


def _gmm2_kernel(bs_ref, nb_ref, act_hbm, wd_hbm, bd_ref, y_hbm,
                 abuf, ybuf, wraw, wbf, zbuf, ce, cs, cb, cn, sin, sout, wsem, zsem, *, rb, n_blocks, nseg):
    def blk(b, m=1):
        return pl.ds(pl.multiple_of(b * rb, rb), m * rb)

    def tok_blk(b, m=1):
        return pl.ds(pl.multiple_of(b * (rb * nseg), rb * nseg), m * rb * nseg)

    n_e = bs_ref.shape[0]
    _zero_tail(bs_ref[n_e - 1] + nb_ref[n_e - 1], n_blocks, zbuf, lambda c: y_hbm.at[tok_blk(c)], zsem)
    n = _build_chunks(bs_ref, nb_ref, ce, cs, cb, cn, 1)

    def in_copy(j, slot, m):
        return pltpu.make_async_copy(act_hbm.at[blk(cb[j], m)], abuf.at[slot, pl.ds(0, m * rb)],
                                     sin.at[slot])

    def out_copy(j, slot, m):
        return pltpu.make_async_copy(ybuf.at[slot, pl.ds(0, m * rb * nseg)],
                                     y_hbm.at[tok_blk(cb[j], m)], sout.at[slot])

    def w_copies(e, s_):
        return (pltpu.make_async_copy(wd_hbm.at[e], wraw, wsem),)

    def load_weights():
        wbf[...] = wraw[...].astype(BF16)

    def compute(j, islot, oslot, m):
        y = jnp.dot(abuf[islot, 0:m * rb, :], wbf[...], preferred_element_type=F32) + bd_ref[ce[j]]
        _store_token_tiles(ybuf.at[oslot], _pack_pairs(y), m * rb)

    _chunk_loop(n, nb_ref, ce, cs, cn, in_copy, out_copy, w_copies, load_weights, compute)


def _gmm2_call(groups, act, wd, bd, rb, nseg):
    n_buf, dff = act.shape
    n_e, _, d = wd.shape
    assert d == 2 * nseg * LANES
    max_chunks = n_buf // rb
    vmem_full = lambda shape: pl.BlockSpec(shape, lambda i, *_: (0,) * len(shape))
    return pl.pallas_call(
        functools.partial(_gmm2_kernel, rb=rb, n_blocks=n_buf // rb, nseg=nseg),
        out_shape=_sds((n_buf * nseg, LANES), U32),
        grid_spec=pltpu.PrefetchScalarGridSpec(
            num_scalar_prefetch=2,
            grid=(1,),
            in_specs=[pl.BlockSpec(memory_space=pl.ANY),
                      pl.BlockSpec(memory_space=pl.ANY),
                      vmem_full((n_e, 1, d))],
            out_specs=pl.BlockSpec(memory_space=pl.ANY),
            scratch_shapes=[pltpu.VMEM((GROUP_IN_SLOTS, 2 * rb, dff), BF16),
                            pltpu.VMEM((GROUP_OUT_SLOTS, 2 * rb * nseg, LANES), U32),
                            pltpu.VMEM((dff, d), F32),
                            pltpu.VMEM((dff, d), BF16),
                            pltpu.VMEM((rb * nseg, LANES), U32),
                            pltpu.SMEM((max_chunks,), I32),
                            pltpu.SMEM((max_chunks,), I32),
                            pltpu.SMEM((max_chunks,), I32),
                            pltpu.SMEM((max_chunks,), I32),
                            pltpu.SemaphoreType.DMA((GROUP_IN_SLOTS,)),
                            pltpu.SemaphoreType.DMA((GROUP_OUT_SLOTS,)),
                            pltpu.SemaphoreType.DMA,
                            pltpu.SemaphoreType.DMA]),
        compiler_params=_params(("arbitrary",)),
        name="gmm2",
    )(*groups, act, wd, bd.reshape(n_e, 1, d))


def _final_kernel(*refs, tc, n_steps, nseg):
    pos_refs, posn_refs = refs[:TOP_K], refs[TOP_K:2 * TOP_K]
    y_hbm, tw_ref, x1_ref, gt_ref, gfin_ref, o_ref, ybuf, sem = refs[2 * TOP_K:]
    step = pl.program_id(0) * pl.num_programs(1) + pl.program_id(1)

    def tok(t):
        return pl.ds(pl.multiple_of(t * nseg, nseg), nseg)

    def issue(p_refs, slot):
        def body(t, c):
            for k in range(TOP_K):
                src = y_hbm.at[tok(p_refs[k][0, 0, 0, t])]
                pltpu.make_async_copy(src, ybuf.at[slot, k, tok(t)], sem.at[slot]).start(priority=k % 2)
            return c

        lax.fori_loop(0, tc, body, 0, unroll=2)

    def run(cur):
        @pl.when(step + 1 < n_steps)
        def _():
            issue(posn_refs, 1 - cur)

        for k in range(TOP_K):
            pltpu.make_async_copy(y_hbm.at[pl.ds(0, tc * nseg)], ybuf.at[cur, k], sem.at[cur]).wait()
        tw = tw_ref[0]
        lo, hi = None, None
        for k in range(TOP_K):
            lo_k, hi_k = _unpack_pairs(_load_token_tiles(ybuf.at[cur, k], tc, nseg))
            wk = tw[:, k:k + 1]
            lo = wk * lo_k if lo is None else lo + wk * lo_k
            hi = wk * hi_k if hi is None else hi + wk * hi_k
        moe = jnp.concatenate([lo, hi], axis=1)
        x2 = x1_ref[0] + gt_ref[0] * moe
        o_ref[0] = _rms(x2, gfin_ref[...])

    @pl.when(step == 0)
    def _():
        issue(pos_refs, 0)

    parity = lax.rem(step, 2)

    @pl.when(parity == 0)
    def _():
        run(0)

    @pl.when(parity == 1)
    def _():
        run(1)


def _final_call(y, pos, top_w, x1, gt2, gfin, nseg):
    bsz, n, d = x1.shape
    tc = _pick(n, COMBINE_TILE, SUBLANES)
    nt = n // tc
    n_steps = bsz * nt

    def cur_pos(k):
        return pl.BlockSpec((1, 1, 1, tc), lambda b, i: (b, k, 0, i), memory_space=pltpu.SMEM)

    def next_pos(k):
        def index(b, i):
            nxt = jnp.minimum(b * nt + i + 1, n_steps - 1)
            return (nxt // nt, k, 0, nxt % nt)

        return pl.BlockSpec((1, 1, 1, tc), index, memory_space=pltpu.SMEM)

    row = lambda b, i: (b, i, 0)
    return pl.pallas_call(
        functools.partial(_final_kernel, tc=tc, n_steps=n_steps, nseg=nseg),
        out_shape=_sds((bsz, n, d), F32),
        grid=(bsz, nt),
        in_specs=[cur_pos(k) for k in range(TOP_K)] + [next_pos(k) for k in range(TOP_K)] + [
                  pl.BlockSpec(memory_space=pl.ANY),
                  pl.BlockSpec((1, tc, LANES), row),
                  pl.BlockSpec((1, tc, d), row),
                  pl.BlockSpec((1, 1, d), lambda b, i: (b, 0, 0)),
                  pl.BlockSpec((1, d), lambda b, i: (0, 0))],
        out_specs=pl.BlockSpec((1, tc, d), row),
        scratch_shapes=[pltpu.VMEM((2, TOP_K, tc * nseg, LANES), U32), pltpu.SemaphoreType.DMA((2,))],
        compiler_params=_params(("arbitrary", "arbitrary")),
        name="final",
    )(*([pos[:, :, None, :]] * (2 * TOP_K)), y, top_w, x1, gt2, gfin)


def _group_tables(counts, rb):
    nblk = (counts + rb - 1) // rb
    bstart = jnp.cumsum(nblk) - nblk
    return bstart * rb, (bstart.astype(I32), nblk.astype(I32))


def kernel(x, c, ctx, c_ctx, w_mod, b_mod, g_mix, w_in, conv_a_w, conv_a_b, lru_w_r, lru_b_r,
           lru_w_i, lru_b_i, lru_lam, conv_b_w, g_out_a, g_out_b, w_out, g_ffn, w_router,
           b_router, w_gate, b_gate, w_up, b_up, w_down, b_down, g_final):
    assert w_mod.shape[0] == 1, "single-layer block"
    bsz, n_lat, d = x.shape
    d_a = conv_a_w.shape[-1]
    d_b = conv_b_w.shape[-1]
    n_e = w_router.shape[-1]
    assert n_lat % GRID_W == 0 and n_e <= LANES and d % (2 * LANES) == 0
    l = 0

    cs = jnp.zeros((SUBLANES, d), F32).at[:bsz].set(c).at[bsz].set(c_ctx)
    mod = _mod_call(cs, w_mod[l], b_mod[l][None])
    sh1, sc1, gt1, sh2, sc2, gt2 = [m[:bsz, None, :] for m in jnp.split(mod, 6, axis=-1)]
    ssh1, ssc1 = [jnp.broadcast_to(m[bsz][None, None, :], (bsz, 1, d))
                  for m in jnp.split(mod, 6, axis=-1)[:2]]

    w_in_bf = w_in[l].astype(BF16)
    g_mix2 = g_mix[l][None]
    cw = conv_a_w[l]
    cb = conv_a_b[l][None]
    wri = jnp.concatenate([lru_w_r[l], lru_w_i[l]], axis=-1).astype(BF16)
    br, bi, lam = lru_b_r[l], lru_b_i[l], lru_lam[l]
    lru_p = lambda dr: (cw, cb, wri[dr], br[dr][None], bi[dr][None], lam[dr][None])

    (s_ax,) = _inproj_call(ctx, g_mix2, ssh1, ssc1, w_in_bf[:, d_a:2 * d_a], d_a, d_b, latent=False)
    zero_state = jnp.zeros((bsz, 1, d_a), F32)
    _, h0f = _lru_call(s_ax, zero_state, *lru_p(0), reverse=False)
    _, h0b = _lru_call(s_ax, zero_state, *lru_p(1), reverse=True)

    ag, ax, bb, p = _inproj_call(x, g_mix2, sh1, sc1, w_in_bf, d_a, d_b, latent=True)
    hf, _ = _lru_call(ax, h0f, *lru_p(0), reverse=False)
    ya, _ = _lru_call(ax, h0b, *lru_p(1), reverse=True, hf=hf, ag=ag)

    wr_bf = jnp.zeros((d, LANES), BF16).at[:, :n_e].set(w_router[l].astype(BF16))
    brt = jnp.full((1, LANES), NEG_BIG, F32).at[0, :n_e].set(b_router[l])
    x1, xp, top_e, top_w, rank, cnt = _mixout_call(
        ya, bb, p, conv_b_w[l], g_out_a[l][None], g_out_b[l][None], w_out[l].astype(BF16), x, gt1,
        g_ffn[l][None], sh2, sc2, wr_bf, brt)

    n_tok = bsz * n_lat
    n_rows = n_tok * TOP_K
    rb = MOE_ROW_BLOCK
    n_buf = n_rows + n_e * rb
    counts = cnt[0, :n_e].astype(I32)
    starts, groups = _group_tables(counts, rb)
    onehot = top_e[..., None] == jnp.arange(n_e, dtype=I32)
    pos = (jnp.sum(jnp.where(onehot, starts, 0), axis=-1) + rank).astype(I32)
    nseg = (d // 2) // LANES
    xs = _dispatch_call(groups, xp.reshape(n_tok * nseg, LANES), pos, n_buf, rb, nseg)
    d_ff = w_gate.shape[-1]
    tf = _pick(d_ff, GMM1_SLAB)
    act = _gmm1_call(groups, xs, w_gate[l], w_up[l], b_gate[l], b_up[l], rb, tf, nseg)
    y = _gmm2_call(groups, act, w_down[l], b_down[l], rb, nseg)
    return _final_call(y, pos, top_w, x1, gt2, g_final[None], nseg)
```

```python
import functools

import jax
import jax.numpy as jnp
from jax import lax
from jax.experimental import pallas as pl
from jax.experimental.pallas import tpu as pltpu

F32 = jnp.float32
BF16 = jnp.bfloat16
I32 = jnp.int32
U32 = jnp.uint32

GRID_W = 64
TOP_K = 4
LRU_C = 8.0
CONV_A_LEFT = 2
SCAN_SEG = 4
SWIGLU_LIMIT = 7.0
SWIGLU_ALPHA = 1.702
EPS = 1e-6
MOE_ROW_BLOCK = 256

TOKEN_TILE = 512
COMBINE_TILE = 256
GATE_ROWS = 128
MOD_COLS = 1024
GMM1_SLAB = 1024

LANES = 128
SUBLANES = 8
VMEM_LIMIT_BYTES = 56 * 1024 * 1024
NEG_BIG = -1e30


def _sds(shape, dtype):
    return jax.ShapeDtypeStruct(shape, dtype)


def _pick(n, pref, mult=LANES):
    if n <= pref:
        return n
    t = (pref // mult) * mult
    while t >= mult:
        if n % t == 0:
            return t
        t -= mult
    return n


def _params(sem):
    return pltpu.CompilerParams(dimension_semantics=sem, vmem_limit_bytes=VMEM_LIMIT_BYTES)


def _sigmoid(x):
    return 1.0 / (1.0 + jnp.exp(-x))


def _rms(x, g):
    ms = jnp.mean(x * x, axis=-1, keepdims=True)
    return (x * lax.rsqrt(ms + EPS)) * g


def _pack_pairs(x):
    w = x.shape[1] // 2
    lo = lax.bitcast_convert_type(x[:, :w].astype(BF16).astype(F32), U32)
    hi = lax.bitcast_convert_type(x[:, w:].astype(BF16).astype(F32), U32)
    return lax.shift_right_logical(lo, jnp.uint32(16)) | (hi & jnp.uint32(0xFFFF0000))


def _unpack_pairs(words):
    lo = lax.bitcast_convert_type(lax.shift_left(words, jnp.uint32(16)), F32)
    hi = lax.bitcast_convert_type(words & jnp.uint32(0xFFFF0000), F32)
    return lo, hi


def _store_token_tiles(ref, words, rows):
    nseg = words.shape[1] // LANES
    if nseg == 1:
        ref[...] = words
        return
    for s in range(nseg):
        ref[pl.ds(s, rows, stride=nseg), :] = words[:, s * LANES:(s + 1) * LANES]


def _load_token_tiles(ref, rows, nseg):
    if nseg == 1:
        return ref[...]
    return jnp.concatenate([ref[pl.ds(s, rows, stride=nseg), :] for s in range(nseg)], axis=1)


def _mod_kernel(c_ref, w_ref, b_ref, o_ref):
    c = c_ref[...]
    s = c * _sigmoid(c)
    o_ref[...] = jnp.dot(s.astype(BF16), w_ref[...].astype(BF16),
                         preferred_element_type=F32) + b_ref[...]


def _mod_call(cs, w, b):
    d, n6 = w.shape
    tn = _pick(n6, MOD_COLS)
    return pl.pallas_call(
        _mod_kernel,
        out_shape=_sds((cs.shape[0], n6), F32),
        grid=(n6 // tn,),
        in_specs=[pl.BlockSpec((cs.shape[0], d), lambda j: (0, 0)),
                  pl.BlockSpec((d, tn), lambda j: (0, j)),
                  pl.BlockSpec((1, tn), lambda j: (0, j))],
        out_specs=pl.BlockSpec((cs.shape[0], tn), lambda j: (0, j)),
        compiler_params=_params(("arbitrary",)),
        name="mod",
    )(cs, w, b)


def _inproj_kernel(x_ref, g_ref, sh_ref, sc_ref, w_ref, *out_refs, d_a, d_b, latent):
    x = x_ref[0]
    xn = _rms(x, g_ref[...]) * (1.0 + sc_ref[0]) + sh_ref[0]
    xb = xn.astype(BF16)

    def sec(lo, width):
        return jnp.dot(xb, w_ref[:, lo:lo + width], preferred_element_type=F32)

    if not latent:
        out_refs[0][0] = sec(0, d_a)
        return
    ag_ref, ax_ref, bb_ref, p_ref = out_refs
    ag_ref[0] = sec(0, d_a).astype(BF16)
    ax_ref[0] = sec(d_a, d_a)
    bb_ref[0] = sec(2 * d_a, d_b).astype(BF16)
    p_ref[0] = (sec(2 * d_a + d_b, d_b) * sec(2 * d_a + 2 * d_b, d_b)).astype(BF16)


def _inproj_call(x, g, sh, sc, w_bf, d_a, d_b, latent):
    bsz, n, d = x.shape
    tm = _pick(n, TOKEN_TILE, SUBLANES)
    n_w = w_bf.shape[1]
    row = lambda b, i: (b, i, 0)
    if latent:
        out_shape = (_sds((bsz, n, d_a), BF16), _sds((bsz, n, d_a), F32),
                     _sds((bsz, n, d_b), BF16), _sds((bsz, n, d_b), BF16))
        out_specs = (pl.BlockSpec((1, tm, d_a), row), pl.BlockSpec((1, tm, d_a), row),
                     pl.BlockSpec((1, tm, d_b), row), pl.BlockSpec((1, tm, d_b), row))
    else:
        out_shape = (_sds((bsz, n, d_a), F32),)
        out_specs = (pl.BlockSpec((1, tm, d_a), row),)
    return pl.pallas_call(
        functools.partial(_inproj_kernel, d_a=d_a, d_b=d_b, latent=latent),
        out_shape=out_shape,
        grid=(bsz, n // tm),
        in_specs=[pl.BlockSpec((1, tm, d), row),
                  pl.BlockSpec((1, d), lambda b, i: (0, 0)),
                  pl.BlockSpec((1, 1, d), lambda b, i: (b, 0, 0)),
                  pl.BlockSpec((1, 1, d), lambda b, i: (b, 0, 0)),
                  pl.BlockSpec((d, n_w), lambda b, i: (0, 0), pipeline_mode=pl.Buffered(1))],
        out_specs=out_specs,
        compiler_params=_params(("arbitrary", "arbitrary")),
        name="inproj_lat" if latent else "inproj_ctx",
    )(x, g, sh, sc, w_bf)


def _gelu_tanh(x):
    c = 0.7978845608028654
    return x * (0.5 * (1.0 + jnp.tanh(c * (x + 0.044715 * (x * x * x)))))


def _lru_kernel(*refs, reverse, combine, emit_conv, conv_given, nc, tl, heads, blk):
    (prev_ref, main_ref, next_ref, cw_ref, cb_ref, wri_ref, br_ref, bi_ref, lam_ref, h0_ref) = refs[:10]
    rest = list(refs[10:])
    if combine:
        hf_ref, ag_ref = rest[:2]
        rest = rest[2:]
    out_ref, hlast_ref = rest[:2]
    rest = rest[2:]
    if emit_conv:
        xco_ref = rest[0]
        rest = rest[1:]
    ebuf, xc_s, a_s, b_s, carry = rest
    da = heads * blk
    c = pl.program_id(1)
    cidx = (nc - 1 - c) if reverse else c

    @pl.when(c == 0)
    def _():
        carry[...] = jnp.broadcast_to(h0_ref[0], carry.shape)

    main = main_ref[0]
    if conv_given:
        xc_s[...] = main
    else:
        zero8 = jnp.zeros((SUBLANES, da), F32)
        ebuf[0:SUBLANES, :] = jnp.where(cidx == 0, zero8, prev_ref[0])
        ebuf[SUBLANES:SUBLANES + tl, :] = main
        ebuf[SUBLANES + tl:2 * SUBLANES + tl, :] = jnp.where(cidx == nc - 1, zero8, next_ref[0])
        cw = cw_ref[...]
        off = SUBLANES - CONV_A_LEFT
        xc_s[...] = (cw[0:1] * ebuf[off:off + tl, :] + cw[1:2] * ebuf[off + 1:off + 1 + tl, :]
                     + cw[2:3] * main + cw[3:4] * ebuf[off + 3:off + 3 + tl, :] + cb_ref[...])
    if emit_conv:
        xco_ref[0] = xc_s[...]

    z = -lam_ref[...]
    sp = jnp.maximum(z, 0.0) + jnp.log1p(jnp.exp(-jnp.abs(z)))
    spb = blk // LANES
    rc = min(tl, GATE_ROWS)
    for r0 in range(0, tl, rc):
        for h in range(heads):
            cs = slice(h * blk, (h + 1) * blk)
            xh = xc_s[r0:r0 + rc, cs]
            zz = jnp.dot(xh.astype(BF16), wri_ref[h], preferred_element_type=F32)
            r = _sigmoid(zz[:, :blk] + br_ref[:, cs])
            i = _sigmoid(zz[:, blk:] + bi_ref[:, cs])
            log_a = (-LRU_C * r) * sp[:, cs]
            a = jnp.exp(log_a)
            v = jnp.tanh(-log_a) * (1.0 + a * a)
            b = jnp.where(v > 0.0, v * lax.rsqrt(v), 0.0) * (i * xh)
            for q in range(spb):
                a_s[h * spb + q, r0:r0 + rc, :] = a[:, q * LANES:(q + 1) * LANES]
                b_s[h * spb + q, r0:r0 + rc, :] = b[:, q * LANES:(q + 1) * LANES]

    n_slab = da // LANES
    sub_rows = SUBLANES * SCAN_SEG
    n_sub = tl // sub_rows
    row = lax.broadcasted_iota(I32, (SUBLANES, LANES), 0)
    ks = list(range(SCAN_SEG - 1, -1, -1)) if reverse else list(range(SCAN_SEG))

    def seg_scan(at, bt):
        for s in (1, 2, 4):
            if reverse:
                keep = row < (SUBLANES - s)
                sh = SUBLANES - s
            else:
                keep = row >= s
                sh = s
            a_sh = jnp.where(keep, pltpu.roll(at, sh, 0), 1.0)
            b_sh = jnp.where(keep, pltpu.roll(bt, sh, 0), 0.0)
            bt = at * b_sh + bt
            at = at * a_sh
        return at, bt

    def body(si, hcs):
        sub = (n_sub - 1 - si) if reverse else si
        base = sub * sub_rows
        out = []
        for slab in range(n_slab):
            hc = hcs[slab]
            rows = [pl.ds(base + k, SUBLANES, stride=SCAN_SEG) for k in range(SCAN_SEG)]
            acc_a, acc_b = {}, {}
            a_run = b_run = None
            for k in ks:
                ak = a_s[slab, rows[k], :]
                bk = b_s[slab, rows[k], :]
                if a_run is None:
                    a_run, b_run = ak, bk
                else:
                    b_run = ak * b_run + bk
                    a_run = ak * a_run
                acc_a[k], acc_b[k] = a_run, b_run
            at, bt = seg_scan(a_run, b_run)
            h_out = at * hc + bt
            if reverse:
                h_in = jnp.where(row < SUBLANES - 1, pltpu.roll(h_out, SUBLANES - 1, 0), hc)
                edge = h_out[0:1, :]
            else:
                h_in = jnp.where(row >= 1, pltpu.roll(h_out, 1, 0), hc)
                edge = h_out[SUBLANES - 1:SUBLANES, :]
            for k in ks:
                b_s[slab, rows[k], :] = acc_a[k] * h_in + acc_b[k]
            out.append(jnp.broadcast_to(edge, (SUBLANES, LANES)))
        return tuple(out)

    hcs = tuple(carry[:, slab * LANES:(slab + 1) * LANES] for slab in range(n_slab))
    hcs = lax.fori_loop(0, n_sub, body, hcs)
    for slab in range(n_slab):
        ls = slice(slab * LANES, (slab + 1) * LANES)
        carry[:, ls] = hcs[slab]
        hlast_ref[0, :, ls] = hcs[slab][0:1, :]
        if combine:
            hsum = hf_ref[0, :, ls] + b_s[slab]
            out_ref[0, :, ls] = (_gelu_tanh(ag_ref[0, :, ls].astype(F32)) * hsum).astype(out_ref.dtype)
        else:
            out_ref[0, :, ls] = b_s[slab]


def _lru_call(ax, h0, cw, cb, wri, br, bi, lam, *, reverse, hf=None, ag=None, emit_conv=False,
              conv_given=False):
    bsz, n, da = ax.shape
    heads, blk, _ = wri.shape
    tl = _pick(n, TOKEN_TILE, SUBLANES * SCAN_SEG)
    assert blk % LANES == 0 and tl % (SUBLANES * SCAN_SEG) == 0
    nc = n // tl
    nb8 = n // SUBLANES
    g8 = tl // SUBLANES
    combine = hf is not None

    def cidx(c):
        return (nc - 1 - c) if reverse else c

    main_map = lambda b, c: (b, cidx(c), 0)
    prev_map = lambda b, c: (b, jnp.maximum(cidx(c) * g8 - 1, 0), 0)
    next_map = lambda b, c: (b, jnp.minimum((cidx(c) + 1) * g8, nb8 - 1), 0)
    const2 = lambda b, c: (0, 0)
    in_specs = [pl.BlockSpec((1, SUBLANES, da), prev_map),
                pl.BlockSpec((1, tl, da), main_map),
                pl.BlockSpec((1, SUBLANES, da), next_map),
                pl.BlockSpec((4, da), const2),
                pl.BlockSpec((1, da), const2),
                pl.BlockSpec((heads, blk, 2 * blk), lambda b, c: (0, 0, 0)),
                pl.BlockSpec((1, da), const2),
                pl.BlockSpec((1, da), const2),
                pl.BlockSpec((1, da), const2),
                pl.BlockSpec((1, 1, da), lambda b, c: (b, 0, 0))]
    args = [ax, ax, ax, cw, cb, wri, br, bi, lam, h0]
    if combine:
        in_specs += [pl.BlockSpec((1, tl, da), main_map), pl.BlockSpec((1, tl, da), main_map)]
        args += [hf, ag]
    out_dtype = BF16 if combine else F32
    out_shape = [_sds((bsz, n, da), out_dtype), _sds((bsz, 1, da), F32)]
    out_specs = [pl.BlockSpec((1, tl, da), main_map), pl.BlockSpec((1, 1, da), lambda b, c: (b, 0, 0))]
    if emit_conv:
        out_shape.append(_sds((bsz, n, da), F32))
        out_specs.append(pl.BlockSpec((1, tl, da), main_map))
    return pl.pallas_call(
        functools.partial(_lru_kernel, reverse=reverse, combine=combine, emit_conv=emit_conv,
                          conv_given=conv_given, nc=nc, tl=tl, heads=heads, blk=blk),
        out_shape=tuple(out_shape),
        grid=(bsz, nc),
        in_specs=in_specs,
        out_specs=tuple(out_specs),
        scratch_shapes=[pltpu.VMEM((tl + 2 * SUBLANES, da), F32),
                        pltpu.VMEM((tl, da), F32),
                        pltpu.VMEM((da // LANES, tl, LANES), F32),
                        pltpu.VMEM((da // LANES, tl, LANES), F32),
                        pltpu.VMEM((SUBLANES, da), F32)],
        compiler_params=_params(("arbitrary", "arbitrary")),
        name=("lru_bwd" if reverse else "lru_fwd") + ("_mix" if combine else ""),
    )(*args)


def _mixout_kernel(ya_ref, bb_ref, p_ref, pu_ref, pd_ref, cbw_ref, ga_ref, gb_ref, wo_ref, x_ref,
                   gt_ref, gf_ref, sh_ref, sc_ref, wr_ref, brt_ref,
                   x1_ref, xp_ref, te_ref, tw_ref, rk_ref, cnt_ref, x1_s, carry,
                   *, tm, d_a, d_b, n_tiles, n_total):
    s = pl.program_id(0)

    @pl.when(s == 0)
    def _():
        carry[...] = jnp.zeros(carry.shape, F32)
        x1_s[...] = jnp.zeros(x1_s.shape, F32)

    xn = _rms(x1_s[...], gf_ref[...]) * (1.0 + sc_ref[0]) + sh_ref[0]
    xb = xn.astype(BF16)
    _store_token_tiles(xp_ref.at[0], _pack_pairs(xn), tm)
    logits = jnp.dot(xb, wr_ref[...], preferred_element_type=F32) + brt_ref[...]

    i = lax.rem(jnp.minimum(s, n_total - 1), n_tiles)
    half = d_b // 2
    z = p_ref[0].astype(F32)
    w = cbw_ref[...]
    zh = z[:, :half]
    col = lax.broadcasted_iota(I32, (tm, half), 0) % GRID_W
    left = jnp.where(col >= 1, pltpu.roll(zh, 1, 0), 0.0)
    right = jnp.where(col <= GRID_W - 2, pltpu.roll(zh, tm - 1, 0), 0.0)
    horiz = w[0:1, :half] * left + w[1:2, :half] * zh + w[2:3, :half] * right
    zv = z[:, half:]
    up_halo = jnp.where(i == 0, 0.0, pu_ref[0].astype(F32))
    dn_halo = jnp.where(i == n_tiles - 1, 0.0, pd_ref[0].astype(F32))
    if tm > GRID_W:
        up = jnp.concatenate([up_halo, zv[:tm - GRID_W]], axis=0)
        dn = jnp.concatenate([zv[GRID_W:], dn_halo], axis=0)
    else:
        up, dn = up_halo, dn_halo
    vert = w[0:1, half:] * up + w[1:2, half:] * zv + w[2:3, half:] * dn
    bb = bb_ref[0].astype(F32)
    yb = jnp.concatenate([bb[:, :half] * horiz, bb[:, half:] * vert], axis=1)
    ya = ya_ref[0].astype(F32)
    ya_n = _rms(ya, ga_ref[...]).astype(BF16)
    yb_n = _rms(yb, gb_ref[...]).astype(BF16)
    mix = (jnp.dot(ya_n, wo_ref[0:d_a, :], preferred_element_type=F32)
           + jnp.dot(yb_n, wo_ref[d_a:d_a + d_b, :], preferred_element_type=F32))
    x1 = x_ref[0] + gt_ref[0] * mix
    x1_ref[0] = x1
    x1_s[...] = x1

    lane = lax.broadcasted_iota(I32, logits.shape, 1)
    lane_f = lane.astype(F32)
    vals = logits
    tv, te = [], []
    for _ in range(TOP_K):
        m = jnp.max(vals, axis=-1, keepdims=True)
        idx = jnp.min(jnp.where(vals == m, lane_f, float(LANES)), axis=-1, keepdims=True)
        tv.append(m)
        te.append(idx)
        vals = jnp.where(lane_f == idx, -jnp.inf, vals)
    ex = [jnp.exp(v - tv[0]) for v in tv]
    den = ex[0]
    for e in ex[1:]:
        den = den + e
    e_out = jnp.zeros(logits.shape, I32)
    w_out = jnp.zeros(logits.shape, F32)
    for k in range(TOP_K):
        e_out = jnp.where(lane == k, te[k].astype(I32), e_out)
        w_out = jnp.where(lane == k, ex[k] / den, w_out)
    te_ref[0] = e_out.T[:TOP_K, :]
    tw_ref[0] = w_out
    ohs = [lane_f == te[k] for k in range(TOP_K)]
    m_oh = jnp.zeros(logits.shape, F32)
    for oh in ohs:
        m_oh = m_oh + jnp.where(oh, 1.0, 0.0)
    ri = lax.broadcasted_iota(I32, (tm, tm), 0)
    ci = lax.broadcasted_iota(I32, (tm, tm), 1)
    ltri = jnp.where(ri > ci, 1.0, 0.0).astype(BF16)
    pref = jnp.dot(ltri, m_oh.astype(BF16), preferred_element_type=F32) + carry[0:1, :]
    r_out = jnp.zeros(logits.shape, I32)
    for k in range(TOP_K):
        rk = jnp.sum(jnp.where(ohs[k], pref, 0.0), axis=-1, keepdims=True)
        r_out = jnp.where(lane == k, rk.astype(I32), r_out)
    rk_ref[0] = r_out.T[:TOP_K, :]
    tot = carry[0:1, :] + jnp.where(s >= 1, jnp.sum(m_oh, axis=0, keepdims=True), 0.0)
    carry[...] = jnp.broadcast_to(tot, carry.shape)
    cnt_ref[...] = jnp.broadcast_to(tot, cnt_ref.shape)


def _mixout_call(ya, bb, p, cbw, ga, gb, wo_bf, x, gt1, gf, sh2, sc2, wr_bf, brt):
    bsz, n, d = x.shape
    d_a = ya.shape[-1]
    d_b = bb.shape[-1]
    half = d_b // 2
    tm = _pick(n, TOKEN_TILE, GRID_W)
    n_tiles = n // tm
    n_total = bsz * n_tiles
    nseg = (d // 2) // LANES
    rpt = tm // GRID_W
    n_rows = n // GRID_W

    def tile(s, lag):
        a = jnp.clip(s - lag, 0, n_total - 1)
        return a // n_tiles, a % n_tiles

    def row(lag):
        return lambda s: (tile(s, lag)[0], tile(s, lag)[1], 0)

    def vec(lag):
        return lambda s: (tile(s, lag)[0], 0, 0)

    tab1 = lambda s: (tile(s, 1)[0], 0, tile(s, 1)[1])
    const2 = lambda s: (0, 0)
    return pl.pallas_call(
        functools.partial(_mixout_kernel, tm=tm, d_a=d_a, d_b=d_b, n_tiles=n_tiles, n_total=n_total),
        out_shape=(_sds((bsz, n, d), F32), _sds((bsz, n * nseg, LANES), U32),
                   _sds((bsz, TOP_K, n), I32), _sds((bsz, n, LANES), F32),
                   _sds((bsz, TOP_K, n), I32), _sds((SUBLANES, LANES), F32)),
        grid=(n_total + 1,),
        in_specs=[pl.BlockSpec((1, tm, d_a), row(0)),
                  pl.BlockSpec((1, tm, d_b), row(0)),
                  pl.BlockSpec((1, tm, d_b), row(0)),
                  pl.BlockSpec((1, GRID_W, half),
                               lambda s: (tile(s, 0)[0], jnp.maximum(tile(s, 0)[1] * rpt - 1, 0), 1)),
                  pl.BlockSpec((1, GRID_W, half),
                               lambda s: (tile(s, 0)[0], jnp.minimum((tile(s, 0)[1] + 1) * rpt, n_rows - 1), 1)),
                  pl.BlockSpec((3, d_b), const2),
                  pl.BlockSpec((1, d_a), const2),
                  pl.BlockSpec((1, d_b), const2),
                  pl.BlockSpec((d_a + d_b, d), const2),
                  pl.BlockSpec((1, tm, d), row(0)),
                  pl.BlockSpec((1, 1, d), vec(0)),
                  pl.BlockSpec((1, d), const2),
                  pl.BlockSpec((1, 1, d), vec(1)),
                  pl.BlockSpec((1, 1, d), vec(1)),
                  pl.BlockSpec((d, LANES), const2),
                  pl.BlockSpec((1, LANES), const2)],
        out_specs=(pl.BlockSpec((1, tm, d), row(0)), pl.BlockSpec((1, tm * nseg, LANES), row(1)),
                   pl.BlockSpec((1, TOP_K, tm), tab1), pl.BlockSpec((1, tm, LANES), row(1)),
                   pl.BlockSpec((1, TOP_K, tm), tab1), pl.BlockSpec((SUBLANES, LANES), const2)),
        scratch_shapes=[pltpu.VMEM((tm, d), F32), pltpu.VMEM((SUBLANES, LANES), F32)],
        compiler_params=_params(("arbitrary",)),
        name="mixout",
    )(ya, bb, p, p, p, cbw, ga, gb, wo_bf, x, gt1, gf, sh2, sc2, wr_bf, brt)


def _zero_tail(first, n_blocks, zbuf, dst_block, sem):
    zbuf[...] = jnp.zeros(zbuf.shape, zbuf.dtype)

    def start(c, carry):
        pltpu.make_async_copy(zbuf, dst_block(c), sem).start()
        return carry

    def wait(c, carry):
        pltpu.make_async_copy(zbuf, dst_block(c), sem).wait()
        return carry

    lax.fori_loop(first, n_blocks, start, 0)
    lax.fori_loop(first, n_blocks, wait, 0)


def _dispatch_kernel(bs_ref, nb_ref, pos_ref, x_ref, o_hbm, zbuf, sem, zsem, *, td, rb, n_e, n_blocks, nseg):
    rbr = rb * nseg

    def blk(b):
        return pl.ds(pl.multiple_of(b * rbr, rbr), rbr)

    def tok(t):
        return pl.ds(pl.multiple_of(t * nseg, nseg), nseg)

    @pl.when(pl.program_id(0) == 0)
    def _():
        _zero_tail(bs_ref[n_e - 1] + nb_ref[n_e - 1], n_blocks, zbuf, lambda c: o_hbm.at[blk(c)], zsem)

        def zero_copy(e):
            return pltpu.make_async_copy(zbuf, o_hbm.at[blk(bs_ref[e] + nb_ref[e] - 1)], zsem)

        for e in range(n_e):
            @pl.when(nb_ref[e] > 0)
            def _():
                zero_copy(e).start()

        for e in range(n_e):
            @pl.when(nb_ref[e] > 0)
            def _():
                zero_copy(e).wait()

    def body(t, c):
        src = x_ref.at[tok(t)]
        for k in range(TOP_K):
            pltpu.make_async_copy(src, o_hbm.at[tok(pos_ref[0, k, t])], sem).start(priority=k % 2)
        return c

    lax.fori_loop(0, td, body, 0, unroll=2)
    for _ in range(TOP_K):
        pltpu.make_async_copy(x_ref, o_hbm.at[pl.ds(0, td * nseg)], sem).wait()


def _dispatch_call(groups, xp, pos, n_buf, rb, nseg):
    bstart, nblk = groups
    n_e = bstart.shape[0]
    t = xp.shape[0] // nseg
    n = pos.shape[-1]
    td = _pick(n, TOKEN_TILE, SUBLANES)
    nt = n // td
    return pl.pallas_call(
        functools.partial(_dispatch_kernel, td=td, rb=rb, n_e=n_e, n_blocks=n_buf // rb, nseg=nseg),
        out_shape=_sds((n_buf * nseg, LANES), U32),
        grid_spec=pltpu.PrefetchScalarGridSpec(
            num_scalar_prefetch=2,
            grid=(t // td,),
            in_specs=[pl.BlockSpec((1, TOP_K, td), lambda i, bs, nb: (i // nt, 0, i % nt),
                                   memory_space=pltpu.SMEM),
                      pl.BlockSpec((td * nseg, LANES), lambda i, bs, nb: (i, 0))],
            out_specs=pl.BlockSpec(memory_space=pl.ANY),
            scratch_shapes=[pltpu.VMEM((rb * nseg, LANES), U32), pltpu.SemaphoreType.DMA,
                            pltpu.SemaphoreType.DMA]),
        compiler_params=_params(("arbitrary",)),
        name="dispatch",
    )(bstart, nblk, pos, xp)


GROUP_IN_SLOTS = 4
GROUP_OUT_SLOTS = 3


def _build_chunks(bs_ref, nb_ref, ce, cs, cb, cn, n_split):
    def per_expert(e, j):
        nb = nb_ref[e]
        b0 = bs_ref[e]

        def per_slab(s_, j):
            def per_pair(c, j):
                ce[j] = e
                cs[j] = s_
                cb[j] = b0 + 2 * c
                cn[j] = jnp.minimum(nb - 2 * c, 2)
                return j + 1

            return lax.fori_loop(0, (nb + 1) // 2, per_pair, j)

        return lax.fori_loop(0, n_split, per_slab, j)

    return lax.fori_loop(0, bs_ref.shape[0], per_expert, jnp.int32(0))


def _chunk_loop(n, nb_ref, ce, cs, cn, in_copy, out_copy, w_copies, load_weights, compute):
    ahead = GROUP_IN_SLOTS - 1

    def by_size(j, fn):
        @pl.when(cn[j] == 2)
        def _():
            fn(2)

        @pl.when(cn[j] == 1)
        def _():
            fn(1)

    @pl.when(n > 0)
    def _():
        for cp in w_copies(ce[0], cs[0]):
            cp.start()
        for j in range(ahead):
            @pl.when(j < n)
            def _():
                by_size(j, lambda m: in_copy(j, j, m).start())

        def body(j, carry):
            jp = jnp.maximum(j - 1, 0)
            first = jnp.logical_or(j == 0, jnp.logical_or(ce[j] != ce[jp], cs[j] != cs[jp]))

            @pl.when(first)
            def _():
                for cp in w_copies(ce[j], cs[j]):
                    cp.wait()

            islot = lax.rem(j, GROUP_IN_SLOTS)
            oslot = lax.rem(j, GROUP_OUT_SLOTS)

            @pl.when(j + ahead < n)
            def _():
                ja = j + ahead
                by_size(ja, lambda m: in_copy(ja, lax.rem(ja, GROUP_IN_SLOTS), m).start())

            by_size(j, lambda m: in_copy(j, islot, m).wait())

            @pl.when(j >= GROUP_OUT_SLOTS)
            def _():
                jo = j - GROUP_OUT_SLOTS
                by_size(jo, lambda m: out_copy(jo, oslot, m).wait())

            @pl.when(first)
            def _():
                def run(m):
                    load_weights()
                    compute(j, islot, oslot, m)
                    out_copy(j, oslot, m).start()

                by_size(j, run)
                jn = j + (nb_ref[ce[j]] + 1) // 2

                @pl.when(jn < n)
                def _():
                    for cp in w_copies(ce[jn], cs[jn]):
                        cp.start()

            @pl.when(jnp.logical_not(first))
            def _():
                def run(m):
                    compute(j, islot, oslot, m)
                    out_copy(j, oslot, m).start()

                by_size(j, run)

            return carry

        lax.fori_loop(0, n, body, 0)

        for k in range(GROUP_OUT_SLOTS):
            @pl.when(n > k)
            def _():
                jl = n - 1 - k
                by_size(jl, lambda m: out_copy(jl, lax.rem(jl, GROUP_OUT_SLOTS), m).wait())


def _gmm1_kernel(bs_ref, nb_ref, xs_hbm, wg_hbm, wu_hbm, bg_ref, bu_ref, act_hbm,
                 xbuf, obuf, wraw, wbf, zbuf, ce, cs, cb, cn, sin, sout, wsem, zsem,
                 *, rb, tf, n_blocks, n_split, nseg):
    def blk(b, m=1):
        return pl.ds(pl.multiple_of(b * rb, rb), m * rb)

    def tok_blk(b, m=1):
        return pl.ds(pl.multiple_of(b * (rb * nseg), rb * nseg), m * rb * nseg)

    def cols(s_):
        return pl.ds(pl.multiple_of(s_ * tf, tf), tf)

    n_e = bs_ref.shape[0]
    for s_ in range(n_split):
        _zero_tail(bs_ref[n_e - 1] + nb_ref[n_e - 1], n_blocks, zbuf,
                   lambda c: act_hbm.at[blk(c), cols(s_)], zsem)
    n = _build_chunks(bs_ref, nb_ref, ce, cs, cb, cn, n_split)

    def in_copy(j, slot, m):
        return pltpu.make_async_copy(xs_hbm.at[tok_blk(cb[j], m)],
                                     xbuf.at[slot, pl.ds(0, m * rb * nseg)], sin.at[slot])

    def out_copy(j, slot, m):
        return pltpu.make_async_copy(obuf.at[slot, pl.ds(0, m * rb)],
                                     act_hbm.at[blk(cb[j], m), cols(cs[j])], sout.at[slot])

    def w_copies(e, s_):
        return (pltpu.make_async_copy(wg_hbm.at[e, :, cols(s_)], wraw.at[0], wsem),
                pltpu.make_async_copy(wu_hbm.at[e, :, cols(s_)], wraw.at[1], wsem))

    def load_weights():
        wbf[:, 0:tf] = wraw[0].astype(BF16)
        wbf[:, tf:2 * tf] = wraw[1].astype(BF16)

    def compute(j, islot, oslot, m):
        g = ce[j] * n_split + cs[j]
        lo, hi = _unpack_pairs(_load_token_tiles(xbuf.at[islot], m * rb, nseg))
        h = jnp.concatenate([lo.astype(BF16), hi.astype(BF16)], axis=1)
        gu = jnp.dot(h, wbf[...], preferred_element_type=F32)
        gate = jnp.minimum(gu[:, :tf] + bg_ref[g], SWIGLU_LIMIT)
        up = jnp.clip(gu[:, tf:] + bu_ref[g], -SWIGLU_LIMIT, SWIGLU_LIMIT)
        act = (up + 1.0) * gate * _sigmoid(SWIGLU_ALPHA * gate)
        obuf[oslot, 0:m * rb, :] = act.astype(BF16)

    _chunk_loop(n, nb_ref, ce, cs, cn, in_copy, out_copy, w_copies, load_weights, compute)


def _gmm1_call(groups, xs, wg, wu, bg, bu, rb, tf, nseg):
    n_buf = xs.shape[0] // nseg
    n_e, d, dff = wg.shape
    n_split = dff // tf
    max_chunks = n_split * (n_buf // rb)
    vmem_full = lambda shape: pl.BlockSpec(shape, lambda i, *_: (0,) * len(shape))
    return pl.pallas_call(
        functools.partial(_gmm1_kernel, rb=rb, tf=tf, n_blocks=n_buf // rb, n_split=n_split, nseg=nseg),
        out_shape=_sds((n_buf, dff), BF16),
        grid_spec=pltpu.PrefetchScalarGridSpec(
            num_scalar_prefetch=2,
            grid=(1,),
            in_specs=[pl.BlockSpec(memory_space=pl.ANY),
                      pl.BlockSpec(memory_space=pl.ANY),
                      pl.BlockSpec(memory_space=pl.ANY),
                      vmem_full((n_e * n_split, 1, tf)),
                      vmem_full((n_e * n_split, 1, tf))],
            out_specs=pl.BlockSpec(memory_space=pl.ANY),
            scratch_shapes=[pltpu.VMEM((GROUP_IN_SLOTS, 2 * rb * nseg, LANES), U32),
                            pltpu.VMEM((GROUP_OUT_SLOTS, 2 * rb, tf), BF16),
                            pltpu.VMEM((2, d, tf), F32),
                            pltpu.VMEM((d, 2 * tf), BF16),
                            pltpu.VMEM((rb, tf), BF16),
                            pltpu.SMEM((max_chunks,), I32),
                            pltpu.SMEM((max_chunks,), I32),
                            pltpu.SMEM((max_chunks,), I32),
                            pltpu.SMEM((max_chunks,), I32),
                            pltpu.SemaphoreType.DMA((GROUP_IN_SLOTS,)),
                            pltpu.SemaphoreType.DMA((GROUP_OUT_SLOTS,)),
                            pltpu.SemaphoreType.DMA,
                            pltpu.SemaphoreType.DMA]),
        compiler_params=_params(("arbitrary",)),
        name="gmm1",
    )(*groups, xs, wg, wu, bg.reshape(n_e * n_split, 1, tf), bu.reshape(n_e * n_split, 1, tf))


def _gmm2_kernel(bs_ref, nb_ref, act_hbm, wd_hbm, bd_ref, y_hbm,
                 abuf, ybuf, wraw, wbf, zbuf, ce, cs, cb, cn, sin, sout, wsem, zsem, *, rb, n_blocks, nseg):
    def blk(b, m=1):
        return pl.ds(pl.multiple_of(b * rb, rb), m * rb)

    def tok_blk(b, m=1):
        return pl.ds(pl.multiple_of(b * (rb * nseg), rb * nseg), m * rb * nseg)

    n_e = bs_ref.shape[0]
    _zero_tail(bs_ref[n_e - 1] + nb_ref[n_e - 1], n_blocks, zbuf, lambda c: y_hbm.at[tok_blk(c)], zsem)
    n = _build_chunks(bs_ref, nb_ref, ce, cs, cb, cn, 1)

    def in_copy(j, slot, m):
        return pltpu.make_async_copy(act_hbm.at[blk(cb[j], m)], abuf.at[slot, pl.ds(0, m * rb)],
                                     sin.at[slot])

    def out_copy(j, slot, m):
        return pltpu.make_async_copy(ybuf.at[slot, pl.ds(0, m * rb * nseg)],
                                     y_hbm.at[tok_blk(cb[j], m)], sout.at[slot])

    def w_copies(e, s_):
        return (pltpu.make_async_copy(wd_hbm.at[e], wraw, wsem),)

    def load_weights():
        wbf[...] = wraw[...].astype(BF16)

    def compute(j, islot, oslot, m):
        y = jnp.dot(abuf[islot, 0:m * rb, :], wbf[...], preferred_element_type=F32) + bd_ref[ce[j]]
        _store_token_tiles(ybuf.at[oslot], _pack_pairs(y), m * rb)

    _chunk_loop(n, nb_ref, ce, cs, cn, in_copy, out_copy, w_copies, load_weights, compute)


def _gmm2_call(groups, act, wd, bd, rb, nseg):
    n_buf, dff = act.shape
    n_e, _, d = wd.shape
    assert d == 2 * nseg * LANES
    max_chunks = n_buf // rb
    vmem_full = lambda shape: pl.BlockSpec(shape, lambda i, *_: (0,) * len(shape))
    return pl.pallas_call(
        functools.partial(_gmm2_kernel, rb=rb, n_blocks=n_buf // rb, nseg=nseg),
        out_shape=_sds((n_buf * nseg, LANES), U32),
        grid_spec=pltpu.PrefetchScalarGridSpec(
            num_scalar_prefetch=2,
            grid=(1,),
            in_specs=[pl.BlockSpec(memory_space=pl.ANY),
                      pl.BlockSpec(memory_space=pl.ANY),
                      vmem_full((n_e, 1, d))],
            out_specs=pl.BlockSpec(memory_space=pl.ANY),
            scratch_shapes=[pltpu.VMEM((GROUP_IN_SLOTS, 2 * rb, dff), BF16),
                            pltpu.VMEM((GROUP_OUT_SLOTS, 2 * rb * nseg, LANES), U32),
                            pltpu.VMEM((dff, d), F32),
                            pltpu.VMEM((dff, d), BF16),
                            pltpu.VMEM((rb * nseg, LANES), U32),
                            pltpu.SMEM((max_chunks,), I32),
                            pltpu.SMEM((max_chunks,), I32),
                            pltpu.SMEM((max_chunks,), I32),
                            pltpu.SMEM((max_chunks,), I32),
                            pltpu.SemaphoreType.DMA((GROUP_IN_SLOTS,)),
                            pltpu.SemaphoreType.DMA((GROUP_OUT_SLOTS,)),
                            pltpu.SemaphoreType.DMA,
                            pltpu.SemaphoreType.DMA]),
        compiler_params=_params(("arbitrary",)),
        name="gmm2",
    )(*groups, act, wd, bd.reshape(n_e, 1, d))


def _final_kernel(*refs, tc, n_steps, nseg):
    pos_refs, posn_refs = refs[:TOP_K], refs[TOP_K:2 * TOP_K]
    y_hbm, tw_ref, x1_ref, gt_ref, gfin_ref, o_ref, ybuf, sem = refs[2 * TOP_K:]
    step = pl.program_id(0) * pl.num_programs(1) + pl.program_id(1)

    def tok(t):
        return pl.ds(pl.multiple_of(t * nseg, nseg), nseg)

    def issue(p_refs, slot):
        def body(t, c):
            for k in range(TOP_K):
                src = y_hbm.at[tok(p_refs[k][0, 0, 0, t])]
                pltpu.make_async_copy(src, ybuf.at[slot, k, tok(t)], sem.at[slot]).start(priority=k % 2)
            return c

        lax.fori_loop(0, tc, body, 0, unroll=2)

    def run(cur):
        @pl.when(step + 1 < n_steps)
        def _():
            issue(posn_refs, 1 - cur)

        for k in range(TOP_K):
            pltpu.make_async_copy(y_hbm.at[pl.ds(0, tc * nseg)], ybuf.at[cur, k], sem.at[cur]).wait()
        tw = tw_ref[0]
        lo, hi = None, None
        for k in range(TOP_K):
            lo_k, hi_k = _unpack_pairs(_load_token_tiles(ybuf.at[cur, k], tc, nseg))
            wk = tw[:, k:k + 1]
            lo = wk * lo_k if lo is None else lo + wk * lo_k
            hi = wk * hi_k if hi is None else hi + wk * hi_k
        moe = jnp.concatenate([lo, hi], axis=1)
        x2 = x1_ref[0] + gt_ref[0] * moe
        o_ref[0] = _rms(x2, gfin_ref[...])

    @pl.when(step == 0)
    def _():
        issue(pos_refs, 0)

    parity = lax.rem(step, 2)

    @pl.when(parity == 0)
    def _():
        run(0)

    @pl.when(parity == 1)
    def _():
        run(1)


def _final_call(y, pos, top_w, x1, gt2, gfin, nseg):
    bsz, n, d = x1.shape
    tc = _pick(n, COMBINE_TILE, SUBLANES)
    nt = n // tc
    n_steps = bsz * nt

    def cur_pos(k):
        return pl.BlockSpec((1, 1, 1, tc), lambda b, i: (b, k, 0, i), memory_space=pltpu.SMEM)

    def next_pos(k):
        def index(b, i):
            nxt = jnp.minimum(b * nt + i + 1, n_steps - 1)
            return (nxt // nt, k, 0, nxt % nt)

        return pl.BlockSpec((1, 1, 1, tc), index, memory_space=pltpu.SMEM)

    row = lambda b, i: (b, i, 0)
    return pl.pallas_call(
        functools.partial(_final_kernel, tc=tc, n_steps=n_steps, nseg=nseg),
        out_shape=_sds((bsz, n, d), F32),
        grid=(bsz, nt),
        in_specs=[cur_pos(k) for k in range(TOP_K)] + [next_pos(k) for k in range(TOP_K)] + [
                  pl.BlockSpec(memory_space=pl.ANY),
                  pl.BlockSpec((1, tc, LANES), row),
                  pl.BlockSpec((1, tc, d), row),
                  pl.BlockSpec((1, 1, d), lambda b, i: (b, 0, 0)),
                  pl.BlockSpec((1, d), lambda b, i: (0, 0))],
        out_specs=pl.BlockSpec((1, tc, d), row),
        scratch_shapes=[pltpu.VMEM((2, TOP_K, tc * nseg, LANES), U32), pltpu.SemaphoreType.DMA((2,))],
        compiler_params=_params(("arbitrary", "arbitrary")),
        name="final",
    )(*([pos[:, :, None, :]] * (2 * TOP_K)), y, top_w, x1, gt2, gfin)


def _group_tables(counts, rb):
    nblk = (counts + rb - 1) // rb
    bstart = jnp.cumsum(nblk) - nblk
    return bstart * rb, (bstart.astype(I32), nblk.astype(I32))


def kernel(x, c, ctx, c_ctx, w_mod, b_mod, g_mix, w_in, conv_a_w, conv_a_b, lru_w_r, lru_b_r,
           lru_w_i, lru_b_i, lru_lam, conv_b_w, g_out_a, g_out_b, w_out, g_ffn, w_router,
           b_router, w_gate, b_gate, w_up, b_up, w_down, b_down, g_final):
    assert w_mod.shape[0] == 1, "single-layer block"
    bsz, n_lat, d = x.shape
    d_a = conv_a_w.shape[-1]
    d_b = conv_b_w.shape[-1]
    n_e = w_router.shape[-1]
    assert n_lat % GRID_W == 0 and n_e <= LANES and d % (2 * LANES) == 0
    l = 0

    cs = jnp.zeros((SUBLANES, d), F32).at[:bsz].set(c).at[bsz].set(c_ctx)
    mod = _mod_call(cs, w_mod[l], b_mod[l][None])
    sh1, sc1, gt1, sh2, sc2, gt2 = [m[:bsz, None, :] for m in jnp.split(mod, 6, axis=-1)]
    ssh1, ssc1 = [jnp.broadcast_to(m[bsz][None, None, :], (bsz, 1, d))
                  for m in jnp.split(mod, 6, axis=-1)[:2]]

    w_in_bf = w_in[l].astype(BF16)
    g_mix2 = g_mix[l][None]
    cw = conv_a_w[l]
    cb = conv_a_b[l][None]
    wri = jnp.concatenate([lru_w_r[l], lru_w_i[l]], axis=-1).astype(BF16)
    br, bi, lam = lru_b_r[l], lru_b_i[l], lru_lam[l]
    lru_p = lambda dr: (cw, cb, wri[dr], br[dr][None], bi[dr][None], lam[dr][None])

    (s_ax,) = _inproj_call(ctx, g_mix2, ssh1, ssc1, w_in_bf[:, d_a:2 * d_a], d_a, d_b, latent=False)
    zero_state = jnp.zeros((bsz, 1, d_a), F32)
    _, h0f, s_xc = _lru_call(s_ax, zero_state, *lru_p(0), reverse=False, emit_conv=True)
    _, h0b = _lru_call(s_xc, zero_state, *lru_p(1), reverse=True, conv_given=True)

    ag, ax, bb, p = _inproj_call(x, g_mix2, sh1, sc1, w_in_bf, d_a, d_b, latent=True)
    hf, _, xc = _lru_call(ax, h0f, *lru_p(0), reverse=False, emit_conv=True)
    ya, _ = _lru_call(xc, h0b, *lru_p(1), reverse=True, hf=hf, ag=ag, conv_given=True)

    wr_bf = jnp.zeros((d, LANES), BF16).at[:, :n_e].set(w_router[l].astype(BF16))
    brt = jnp.full((1, LANES), NEG_BIG, F32).at[0, :n_e].set(b_router[l])
    x1, xp, top_e, top_w, rank, cnt = _mixout_call(
        ya, bb, p, conv_b_w[l], g_out_a[l][None], g_out_b[l][None], w_out[l].astype(BF16), x, gt1,
        g_ffn[l][None], sh2, sc2, wr_bf, brt)

    n_tok = bsz * n_lat
    n_rows = n_tok * TOP_K
    rb = MOE_ROW_BLOCK
    n_buf = n_rows + n_e * rb
    counts = cnt[0, :n_e].astype(I32)
    starts, groups = _group_tables(counts, rb)
    onehot = top_e[..., None] == jnp.arange(n_e, dtype=I32)
    pos = (jnp.sum(jnp.where(onehot, starts, 0), axis=-1) + rank).astype(I32)
    nseg = (d // 2) // LANES
    xs = _dispatch_call(groups, xp.reshape(n_tok * nseg, LANES), pos, n_buf, rb, nseg)
    d_ff = w_gate.shape[-1]
    tf = _pick(d_ff, GMM1_SLAB)
    act = _gmm1_call(groups, xs, w_gate[l], w_up[l], b_gate[l], b_up[l], rb, tf, nseg)
    y = _gmm2_call(groups, act, w_down[l], b_down[l], rb, nseg)
    return _final_call(y, pos, top_w, x1, gt2, g_final[None], nseg)
```

```python
import functools

import jax
import jax.numpy as jnp
from jax import lax
from jax.experimental import pallas as pl
from jax.experimental.pallas import tpu as pltpu

F32 = jnp.float32
BF16 = jnp.bfloat16
I32 = jnp.int32
U32 = jnp.uint32

GRID_W = 64
TOP_K = 4
LRU_C = 8.0
CONV_A_LEFT = 2
SCAN_SEG = 4
SWIGLU_LIMIT = 7.0
SWIGLU_ALPHA = 1.702
EPS = 1e-6
MOE_ROW_BLOCK = 256

TOKEN_TILE = 512
COMBINE_TILE = 256
GATE_ROWS = 128
MOD_COLS = 1024
GMM1_SLAB = 1024

LANES = 128
SUBLANES = 8
VMEM_LIMIT_BYTES = 56 * 1024 * 1024
NEG_BIG = -1e30


def _sds(shape, dtype):
    return jax.ShapeDtypeStruct(shape, dtype)


def _pick(n, pref, mult=LANES):
    if n <= pref:
        return n
    t = (pref // mult) * mult
    while t >= mult:
        if n % t == 0:
            return t
        t -= mult
    return n


def _params(sem):
    return pltpu.CompilerParams(dimension_semantics=sem, vmem_limit_bytes=VMEM_LIMIT_BYTES)


def _sigmoid(x):
    return 1.0 / (1.0 + jnp.exp(-x))


def _rms(x, g):
    ms = jnp.mean(x * x, axis=-1, keepdims=True)
    return (x * lax.rsqrt(ms + EPS)) * g


def _pack_pairs(x):
    w = x.shape[1] // 2
    lo = lax.bitcast_convert_type(x[:, :w].astype(BF16).astype(F32), U32)
    hi = lax.bitcast_convert_type(x[:, w:].astype(BF16).astype(F32), U32)
    return lax.shift_right_logical(lo, jnp.uint32(16)) | (hi & jnp.uint32(0xFFFF0000))


def _unpack_pairs(words):
    lo = lax.bitcast_convert_type(lax.shift_left(words, jnp.uint32(16)), F32)
    hi = lax.bitcast_convert_type(words & jnp.uint32(0xFFFF0000), F32)
    return lo, hi


def _store_token_tiles(ref, words, rows):
    nseg = words.shape[1] // LANES
    if nseg == 1:
        ref[...] = words
        return
    for s in range(nseg):
        ref[pl.ds(s, rows, stride=nseg), :] = words[:, s * LANES:(s + 1) * LANES]


def _load_token_tiles(ref, rows, nseg):
    if nseg == 1:
        return ref[...]
    return jnp.concatenate([ref[pl.ds(s, rows, stride=nseg), :] for s in range(nseg)], axis=1)


def _mod_kernel(c_ref, w_ref, b_ref, o_ref):
    c = c_ref[...]
    s = c * _sigmoid(c)
    o_ref[...] = jnp.dot(s.astype(BF16), w_ref[...].astype(BF16),
                         preferred_element_type=F32) + b_ref[...]


def _mod_call(cs, w, b):
    d, n6 = w.shape
    tn = _pick(n6, MOD_COLS)
    return pl.pallas_call(
        _mod_kernel,
        out_shape=_sds((cs.shape[0], n6), F32),
        grid=(n6 // tn,),
        in_specs=[pl.BlockSpec((cs.shape[0], d), lambda j: (0, 0)),
                  pl.BlockSpec((d, tn), lambda j: (0, j)),
                  pl.BlockSpec((1, tn), lambda j: (0, j))],
        out_specs=pl.BlockSpec((cs.shape[0], tn), lambda j: (0, j)),
        compiler_params=_params(("arbitrary",)),
        name="mod",
    )(cs, w, b)


def _inproj_kernel(x_ref, g_ref, sh_ref, sc_ref, w_ref, *out_refs, d_a, d_b, latent):
    x = x_ref[0]
    xn = _rms(x, g_ref[...]) * (1.0 + sc_ref[0]) + sh_ref[0]
    xb = xn.astype(BF16)

    def sec(lo, width):
        return jnp.dot(xb, w_ref[:, lo:lo + width], preferred_element_type=F32)

    if not latent:
        out_refs[0][0] = sec(0, d_a)
        return
    ag_ref, ax_ref, bb_ref, p_ref = out_refs
    ag_ref[0] = sec(0, d_a).astype(BF16)
    ax_ref[0] = sec(d_a, d_a)
    bb_ref[0] = sec(2 * d_a, d_b).astype(BF16)
    p_ref[0] = (sec(2 * d_a + d_b, d_b) * sec(2 * d_a + 2 * d_b, d_b)).astype(BF16)


def _inproj_call(x, g, sh, sc, w_bf, d_a, d_b, latent):
    bsz, n, d = x.shape
    tm = _pick(n, TOKEN_TILE, SUBLANES)
    n_w = w_bf.shape[1]
    row = lambda b, i: (b, i, 0)
    if latent:
        out_shape = (_sds((bsz, n, d_a), BF16), _sds((bsz, n, d_a), F32),
                     _sds((bsz, n, d_b), BF16), _sds((bsz, n, d_b), BF16))
        out_specs = (pl.BlockSpec((1, tm, d_a), row), pl.BlockSpec((1, tm, d_a), row),
                     pl.BlockSpec((1, tm, d_b), row), pl.BlockSpec((1, tm, d_b), row))
    else:
        out_shape = (_sds((bsz, n, d_a), F32),)
        out_specs = (pl.BlockSpec((1, tm, d_a), row),)
    return pl.pallas_call(
        functools.partial(_inproj_kernel, d_a=d_a, d_b=d_b, latent=latent),
        out_shape=out_shape,
        grid=(bsz, n // tm),
        in_specs=[pl.BlockSpec((1, tm, d), row),
                  pl.BlockSpec((1, d), lambda b, i: (0, 0)),
                  pl.BlockSpec((1, 1, d), lambda b, i: (b, 0, 0)),
                  pl.BlockSpec((1, 1, d), lambda b, i: (b, 0, 0)),
                  pl.BlockSpec((d, n_w), lambda b, i: (0, 0), pipeline_mode=pl.Buffered(1))],
        out_specs=out_specs,
        compiler_params=_params(("arbitrary", "arbitrary")),
        name="inproj_lat" if latent else "inproj_ctx",
    )(x, g, sh, sc, w_bf)


def _gelu_tanh(x):
    c = 0.7978845608028654
    return x * (0.5 * (1.0 + jnp.tanh(c * (x + 0.044715 * (x * x * x)))))


def _lru_kernel(*refs, reverse, combine, emit_conv, conv_given, nc, tl, heads, blk):
    (prev_ref, main_ref, next_ref, cw_ref, cb_ref, wri_ref, br_ref, bi_ref, lam_ref, h0_ref) = refs[:10]
    rest = list(refs[10:])
    if combine:
        hf_ref, ag_ref = rest[:2]
        rest = rest[2:]
    out_ref, hlast_ref = rest[:2]
    rest = rest[2:]
    if emit_conv:
        xco_ref = rest[0]
        rest = rest[1:]
    ebuf, xc_s, a_s, b_s, carry = rest
    da = heads * blk
    c = pl.program_id(1)
    cidx = (nc - 1 - c) if reverse else c

    @pl.when(c == 0)
    def _():
        carry[...] = jnp.broadcast_to(h0_ref[0], carry.shape)

    main = main_ref[0]
    if conv_given:
        xc_s[...] = main
    else:
        zero8 = jnp.zeros((SUBLANES, da), F32)
        ebuf[0:SUBLANES, :] = jnp.where(cidx == 0, zero8, prev_ref[0])
        ebuf[SUBLANES:SUBLANES + tl, :] = main
        ebuf[SUBLANES + tl:2 * SUBLANES + tl, :] = jnp.where(cidx == nc - 1, zero8, next_ref[0])
        cw = cw_ref[...]
        off = SUBLANES - CONV_A_LEFT
        xc_s[...] = (cw[0:1] * ebuf[off:off + tl, :] + cw[1:2] * ebuf[off + 1:off + 1 + tl, :]
                     + cw[2:3] * main + cw[3:4] * ebuf[off + 3:off + 3 + tl, :] + cb_ref[...])
    if emit_conv:
        xco_ref[0] = xc_s[...]

    z = -lam_ref[...]
    sp = jnp.maximum(z, 0.0) + jnp.log1p(jnp.exp(-jnp.abs(z)))
    spb = blk // LANES
    rc = min(tl, GATE_ROWS)
    for r0 in range(0, tl, rc):
        for h in range(heads):
            cs = slice(h * blk, (h + 1) * blk)
            xh = xc_s[r0:r0 + rc, cs]
            zz = jnp.dot(xh.astype(BF16), wri_ref[h], preferred_element_type=F32)
            r = 0.5 * jnp.tanh(0.5 * (zz[:, :blk] + br_ref[:, cs])) + 0.5
            i = 0.5 * jnp.tanh(0.5 * (zz[:, blk:] + bi_ref[:, cs])) + 0.5
            log_a = (-LRU_C * r) * sp[:, cs]
            a = jnp.exp(log_a)
            v = jnp.tanh(-log_a) * (1.0 + a * a)
            b = jnp.where(v > 0.0, v * lax.rsqrt(v), 0.0) * (i * xh)
            for q in range(spb):
                a_s[h * spb + q, r0:r0 + rc, :] = a[:, q * LANES:(q + 1) * LANES]
                b_s[h * spb + q, r0:r0 + rc, :] = b[:, q * LANES:(q + 1) * LANES]

    n_slab = da // LANES
    sub_rows = SUBLANES * SCAN_SEG
    n_sub = tl // sub_rows
    row = lax.broadcasted_iota(I32, (SUBLANES, LANES), 0)
    ks = list(range(SCAN_SEG - 1, -1, -1)) if reverse else list(range(SCAN_SEG))

    def seg_scan(at, bt):
        for s in (1, 2, 4):
            if reverse:
                keep = row < (SUBLANES - s)
                sh = SUBLANES - s
            else:
                keep = row >= s
                sh = s
            a_sh = jnp.where(keep, pltpu.roll(at, sh, 0), 1.0)
            b_sh = jnp.where(keep, pltpu.roll(bt, sh, 0), 0.0)
            bt = at * b_sh + bt
            at = at * a_sh
        return at, bt

    def body(si, hcs):
        sub = (n_sub - 1 - si) if reverse else si
        base = sub * sub_rows
        out = []
        for slab in range(n_slab):
            hc = hcs[slab]
            rows = [pl.ds(base + k, SUBLANES, stride=SCAN_SEG) for k in range(SCAN_SEG)]
            acc_a, acc_b = {}, {}
            a_run = b_run = None
            for k in ks:
                ak = a_s[slab, rows[k], :]
                bk = b_s[slab, rows[k], :]
                if a_run is None:
                    a_run, b_run = ak, bk
                else:
                    b_run = ak * b_run + bk
                    a_run = ak * a_run
                acc_a[k], acc_b[k] = a_run, b_run
            at, bt = seg_scan(a_run, b_run)
            h_out = at * hc + bt
            if reverse:
                h_in = jnp.where(row < SUBLANES - 1, pltpu.roll(h_out, SUBLANES - 1, 0), hc)
                edge = h_out[0:1, :]
            else:
                h_in = jnp.where(row >= 1, pltpu.roll(h_out, 1, 0), hc)
                edge = h_out[SUBLANES - 1:SUBLANES, :]
            for k in ks:
                b_s[slab, rows[k], :] = acc_a[k] * h_in + acc_b[k]
            out.append(jnp.broadcast_to(edge, (SUBLANES, LANES)))
        return tuple(out)

    hcs = tuple(carry[:, slab * LANES:(slab + 1) * LANES] for slab in range(n_slab))
    hcs = lax.fori_loop(0, n_sub, body, hcs)
    for slab in range(n_slab):
        ls = slice(slab * LANES, (slab + 1) * LANES)
        carry[:, ls] = hcs[slab]
        hlast_ref[0, :, ls] = hcs[slab][0:1, :]
        if combine:
            hsum = hf_ref[0, :, ls] + b_s[slab]
            out_ref[0, :, ls] = (_gelu_tanh(ag_ref[0, :, ls].astype(F32)) * hsum).astype(out_ref.dtype)
        else:
            out_ref[0, :, ls] = b_s[slab]


def _lru_call(ax, h0, cw, cb, wri, br, bi, lam, *, reverse, hf=None, ag=None, emit_conv=False,
              conv_given=False):
    bsz, n, da = ax.shape
    heads, blk, _ = wri.shape
    tl = _pick(n, TOKEN_TILE, SUBLANES * SCAN_SEG)
    assert blk % LANES == 0 and tl % (SUBLANES * SCAN_SEG) == 0
    nc = n // tl
    nb8 = n // SUBLANES
    g8 = tl // SUBLANES
    combine = hf is not None

    def cidx(c):
        return (nc - 1 - c) if reverse else c

    main_map = lambda b, c: (b, cidx(c), 0)
    prev_map = lambda b, c: (b, jnp.maximum(cidx(c) * g8 - 1, 0), 0)
    next_map = lambda b, c: (b, jnp.minimum((cidx(c) + 1) * g8, nb8 - 1), 0)
    const2 = lambda b, c: (0, 0)
    in_specs = [pl.BlockSpec((1, SUBLANES, da), prev_map),
                pl.BlockSpec((1, tl, da), main_map),
                pl.BlockSpec((1, SUBLANES, da), next_map),
                pl.BlockSpec((4, da), const2),
                pl.BlockSpec((1, da), const2),
                pl.BlockSpec((heads, blk, 2 * blk), lambda b, c: (0, 0, 0)),
                pl.BlockSpec((1, da), const2),
                pl.BlockSpec((1, da), const2),
                pl.BlockSpec((1, da), const2),
                pl.BlockSpec((1, 1, da), lambda b, c: (b, 0, 0))]
    args = [ax, ax, ax, cw, cb, wri, br, bi, lam, h0]
    if combine:
        in_specs += [pl.BlockSpec((1, tl, da), main_map), pl.BlockSpec((1, tl, da), main_map)]
        args += [hf, ag]
    out_dtype = BF16 if combine else F32
    out_shape = [_sds((bsz, n, da), out_dtype), _sds((bsz, 1, da), F32)]
    out_specs = [pl.BlockSpec((1, tl, da), main_map), pl.BlockSpec((1, 1, da), lambda b, c: (b, 0, 0))]
    if emit_conv:
        out_shape.append(_sds((bsz, n, da), F32))
        out_specs.append(pl.BlockSpec((1, tl, da), main_map))
    return pl.pallas_call(
        functools.partial(_lru_kernel, reverse=reverse, combine=combine, emit_conv=emit_conv,
                          conv_given=conv_given, nc=nc, tl=tl, heads=heads, blk=blk),
        out_shape=tuple(out_shape),
        grid=(bsz, nc),
        in_specs=in_specs,
        out_specs=tuple(out_specs),
        scratch_shapes=[pltpu.VMEM((tl + 2 * SUBLANES, da), F32),
                        pltpu.VMEM((tl, da), F32),
                        pltpu.VMEM((da // LANES, tl, LANES), F32),
                        pltpu.VMEM((da // LANES, tl, LANES), F32),
                        pltpu.VMEM((SUBLANES, da), F32)],
        compiler_params=_params(("arbitrary", "arbitrary")),
        name=("lru_bwd" if reverse else "lru_fwd") + ("_mix" if combine else ""),
    )(*args)


def _mixout_kernel(ya_ref, bb_ref, p_ref, pu_ref, pd_ref, cbw_ref, ga_ref, gb_ref, wo_ref, x_ref,
                   gt_ref, gf_ref, sh_ref, sc_ref, wr_ref, brt_ref,
                   x1_ref, xp_ref, te_ref, tw_ref, rk_ref, cnt_ref, x1_s, carry,
                   *, tm, d_a, d_b, n_tiles, n_total):
    s = pl.program_id(0)

    @pl.when(s == 0)
    def _():
        carry[...] = jnp.zeros(carry.shape, F32)
        x1_s[...] = jnp.zeros(x1_s.shape, F32)

    xn = _rms(x1_s[...], gf_ref[...]) * (1.0 + sc_ref[0]) + sh_ref[0]
    xb = xn.astype(BF16)
    _store_token_tiles(xp_ref.at[0], _pack_pairs(xn), tm)
    logits = jnp.dot(xb, wr_ref[...], preferred_element_type=F32) + brt_ref[...]

    i = lax.rem(jnp.minimum(s, n_total - 1), n_tiles)
    half = d_b // 2
    z = p_ref[0].astype(F32)
    w = cbw_ref[...]
    zh = z[:, :half]
    col = lax.broadcasted_iota(I32, (tm, half), 0) % GRID_W
    left = jnp.where(col >= 1, pltpu.roll(zh, 1, 0), 0.0)
    right = jnp.where(col <= GRID_W - 2, pltpu.roll(zh, tm - 1, 0), 0.0)
    horiz = w[0:1, :half] * left + w[1:2, :half] * zh + w[2:3, :half] * right
    zv = z[:, half:]
    up_halo = jnp.where(i == 0, 0.0, pu_ref[0].astype(F32))
    dn_halo = jnp.where(i == n_tiles - 1, 0.0, pd_ref[0].astype(F32))
    if tm > GRID_W:
        up = jnp.concatenate([up_halo, zv[:tm - GRID_W]], axis=0)
        dn = jnp.concatenate([zv[GRID_W:], dn_halo], axis=0)
    else:
        up, dn = up_halo, dn_halo
    vert = w[0:1, half:] * up + w[1:2, half:] * zv + w[2:3, half:] * dn
    bb = bb_ref[0].astype(F32)
    yb = jnp.concatenate([bb[:, :half] * horiz, bb[:, half:] * vert], axis=1)
    ya = ya_ref[0].astype(F32)
    ya_n = _rms(ya, ga_ref[...]).astype(BF16)
    yb_n = _rms(yb, gb_ref[...]).astype(BF16)
    mix = (jnp.dot(ya_n, wo_ref[0:d_a, :], preferred_element_type=F32)
           + jnp.dot(yb_n, wo_ref[d_a:d_a + d_b, :], preferred_element_type=F32))
    x1 = x_ref[0] + gt_ref[0] * mix
    x1_ref[0] = x1
    x1_s[...] = x1

    lane = lax.broadcasted_iota(I32, logits.shape, 1)
    lane_f = lane.astype(F32)
    vals = logits
    tv, te = [], []
    for _ in range(TOP_K):
        m = jnp.max(vals, axis=-1, keepdims=True)
        idx = jnp.min(jnp.where(vals == m, lane_f, float(LANES)), axis=-1, keepdims=True)
        tv.append(m)
        te.append(idx)
        vals = jnp.where(lane_f == idx, -jnp.inf, vals)
    ex = [jnp.exp(v - tv[0]) for v in tv]
    den = ex[0]
    for e in ex[1:]:
        den = den + e
    e_out = jnp.zeros(logits.shape, I32)
    w_out = jnp.zeros(logits.shape, F32)
    for k in range(TOP_K):
        e_out = jnp.where(lane == k, te[k].astype(I32), e_out)
        w_out = jnp.where(lane == k, ex[k] / den, w_out)
    te_ref[0] = e_out.T[:TOP_K, :]
    tw_ref[0] = w_out
    ohs = [lane_f == te[k] for k in range(TOP_K)]
    m_oh = jnp.zeros(logits.shape, F32)
    for oh in ohs:
        m_oh = m_oh + jnp.where(oh, 1.0, 0.0)
    ri = lax.broadcasted_iota(I32, (tm, tm), 0)
    ci = lax.broadcasted_iota(I32, (tm, tm), 1)
    ltri = jnp.where(ri > ci, 1.0, 0.0).astype(BF16)
    pref = jnp.dot(ltri, m_oh.astype(BF16), preferred_element_type=F32) + carry[0:1, :]
    r_out = jnp.zeros(logits.shape, I32)
    for k in range(TOP_K):
        rk = jnp.sum(jnp.where(ohs[k], pref, 0.0), axis=-1, keepdims=True)
        r_out = jnp.where(lane == k, rk.astype(I32), r_out)
    rk_ref[0] = r_out.T[:TOP_K, :]
    tot = carry[0:1, :] + jnp.where(s >= 1, jnp.sum(m_oh, axis=0, keepdims=True), 0.0)
    carry[...] = jnp.broadcast_to(tot, carry.shape)
    cnt_ref[...] = jnp.broadcast_to(tot, cnt_ref.shape)


def _mixout_call(ya, bb, p, cbw, ga, gb, wo_bf, x, gt1, gf, sh2, sc2, wr_bf, brt):
    bsz, n, d = x.shape
    d_a = ya.shape[-1]
    d_b = bb.shape[-1]
    half = d_b // 2
    tm = _pick(n, TOKEN_TILE, GRID_W)
    n_tiles = n // tm
    n_total = bsz * n_tiles
    nseg = (d // 2) // LANES
    rpt = tm // GRID_W
    n_rows = n // GRID_W

    def tile(s, lag):
        a = jnp.clip(s - lag, 0, n_total - 1)
        return a // n_tiles, a % n_tiles

    def row(lag):
        return lambda s: (tile(s, lag)[0], tile(s, lag)[1], 0)

    def vec(lag):
        return lambda s: (tile(s, lag)[0], 0, 0)

    tab1 = lambda s: (tile(s, 1)[0], 0, tile(s, 1)[1])
    const2 = lambda s: (0, 0)
    return pl.pallas_call(
        functools.partial(_mixout_kernel, tm=tm, d_a=d_a, d_b=d_b, n_tiles=n_tiles, n_total=n_total),
        out_shape=(_sds((bsz, n, d), F32), _sds((bsz, n * nseg, LANES), U32),
                   _sds((bsz, TOP_K, n), I32), _sds((bsz, n, LANES), F32),
                   _sds((bsz, TOP_K, n), I32), _sds((SUBLANES, LANES), F32)),
        grid=(n_total + 1,),
        in_specs=[pl.BlockSpec((1, tm, d_a), row(0)),
                  pl.BlockSpec((1, tm, d_b), row(0)),
                  pl.BlockSpec((1, tm, d_b), row(0)),
                  pl.BlockSpec((1, GRID_W, half),
                               lambda s: (tile(s, 0)[0], jnp.maximum(tile(s, 0)[1] * rpt - 1, 0), 1)),
                  pl.BlockSpec((1, GRID_W, half),
                               lambda s: (tile(s, 0)[0], jnp.minimum((tile(s, 0)[1] + 1) * rpt, n_rows - 1), 1)),
                  pl.BlockSpec((3, d_b), const2),
                  pl.BlockSpec((1, d_a), const2),
                  pl.BlockSpec((1, d_b), const2),
                  pl.BlockSpec((d_a + d_b, d), const2),
                  pl.BlockSpec((1, tm, d), row(0)),
                  pl.BlockSpec((1, 1, d), vec(0)),
                  pl.BlockSpec((1, d), const2),
                  pl.BlockSpec((1, 1, d), vec(1)),
                  pl.BlockSpec((1, 1, d), vec(1)),
                  pl.BlockSpec((d, LANES), const2),
                  pl.BlockSpec((1, LANES), const2)],
        out_specs=(pl.BlockSpec((1, tm, d), row(0)), pl.BlockSpec((1, tm * nseg, LANES), row(1)),
                   pl.BlockSpec((1, TOP_K, tm), tab1), pl.BlockSpec((1, tm, LANES), row(1)),
                   pl.BlockSpec((1, TOP_K, tm), tab1), pl.BlockSpec((SUBLANES, LANES), const2)),
        scratch_shapes=[pltpu.VMEM((tm, d), F32), pltpu.VMEM((SUBLANES, LANES), F32)],
        compiler_params=_params(("arbitrary",)),
        name="mixout",
    )(ya, bb, p, p, p, cbw, ga, gb, wo_bf, x, gt1, gf, sh2, sc2, wr_bf, brt)


def _zero_tail(first, n_blocks, zbuf, dst_block, sem):
    zbuf[...] = jnp.zeros(zbuf.shape, zbuf.dtype)

    def start(c, carry):
        pltpu.make_async_copy(zbuf, dst_block(c), sem).start()
        return carry

    def wait(c, carry):
        pltpu.make_async_copy(zbuf, dst_block(c), sem).wait()
        return carry

    lax.fori_loop(first, n_blocks, start, 0)
    lax.fori_loop(first, n_blocks, wait, 0)


def _dispatch_kernel(bs_ref, nb_ref, pos_ref, x_ref, o_hbm, zbuf, sem, zsem, *, td, rb, n_e, n_blocks, nseg):
    rbr = rb * nseg

    def blk(b):
        return pl.ds(pl.multiple_of(b * rbr, rbr), rbr)

    def tok(t):
        return pl.ds(pl.multiple_of(t * nseg, nseg), nseg)

    @pl.when(pl.program_id(0) == 0)
    def _():
        _zero_tail(bs_ref[n_e - 1] + nb_ref[n_e - 1], n_blocks, zbuf, lambda c: o_hbm.at[blk(c)], zsem)

        def zero_copy(e):
            return pltpu.make_async_copy(zbuf, o_hbm.at[blk(bs_ref[e] + nb_ref[e] - 1)], zsem)

        for e in range(n_e):
            @pl.when(nb_ref[e] > 0)
            def _():
                zero_copy(e).start()

        for e in range(n_e):
            @pl.when(nb_ref[e] > 0)
            def _():
                zero_copy(e).wait()

    def body(t, c):
        src = x_ref.at[tok(t)]
        for k in range(TOP_K):
            pltpu.make_async_copy(src, o_hbm.at[tok(pos_ref[0, k, t])], sem).start(priority=k % 2)
        return c

    lax.fori_loop(0, td, body, 0, unroll=2)
    for _ in range(TOP_K):
        pltpu.make_async_copy(x_ref, o_hbm.at[pl.ds(0, td * nseg)], sem).wait()


def _dispatch_call(groups, xp, pos, n_buf, rb, nseg):
    bstart, nblk = groups
    n_e = bstart.shape[0]
    t = xp.shape[0] // nseg
    n = pos.shape[-1]
    td = _pick(n, TOKEN_TILE, SUBLANES)
    nt = n // td
    return pl.pallas_call(
        functools.partial(_dispatch_kernel, td=td, rb=rb, n_e=n_e, n_blocks=n_buf // rb, nseg=nseg),
        out_shape=_sds((n_buf * nseg, LANES), U32),
        grid_spec=pltpu.PrefetchScalarGridSpec(
            num_scalar_prefetch=2,
            grid=(t // td,),
            in_specs=[pl.BlockSpec((1, TOP_K, td), lambda i, bs, nb: (i // nt, 0, i % nt),
                                   memory_space=pltpu.SMEM),
                      pl.BlockSpec((td * nseg, LANES), lambda i, bs, nb: (i, 0))],
            out_specs=pl.BlockSpec(memory_space=pl.ANY),
            scratch_shapes=[pltpu.VMEM((rb * nseg, LANES), U32), pltpu.SemaphoreType.DMA,
                            pltpu.SemaphoreType.DMA]),
        compiler_params=_params(("arbitrary",)),
        name="dispatch",
    )(bstart, nblk, pos, xp)


GROUP_IN_SLOTS = 4
GROUP_OUT_SLOTS = 3


def _build_chunks(bs_ref, nb_ref, ce, cs, cb, cn, n_split):
    def per_expert(e, j):
        nb = nb_ref[e]
        b0 = bs_ref[e]

        def per_slab(s_, j):
            def per_pair(c, j):
                ce[j] = e
                cs[j] = s_
                cb[j] = b0 + 2 * c
                cn[j] = jnp.minimum(nb - 2 * c, 2)
                return j + 1

            return lax.fori_loop(0, (nb + 1) // 2, per_pair, j)

        return lax.fori_loop(0, n_split, per_slab, j)

    return lax.fori_loop(0, bs_ref.shape[0], per_expert, jnp.int32(0))


def _chunk_loop(n, nb_ref, ce, cs, cn, in_copy, out_copy, w_copies, load_weights, compute):
    ahead = GROUP_IN_SLOTS - 1

    def by_size(j, fn):
        @pl.when(cn[j] == 2)
        def _():
            fn(2)

        @pl.when(cn[j] == 1)
        def _():
            fn(1)

    @pl.when(n > 0)
    def _():
        for cp in w_copies(ce[0], cs[0]):
            cp.start()
        for j in range(ahead):
            @pl.when(j < n)
            def _():
                by_size(j, lambda m: in_copy(j, j, m).start())

        def body(j, carry):
            jp = jnp.maximum(j - 1, 0)
            first = jnp.logical_or(j == 0, jnp.logical_or(ce[j] != ce[jp], cs[j] != cs[jp]))

            @pl.when(first)
            def _():
                for cp in w_copies(ce[j], cs[j]):
                    cp.wait()

            islot = lax.rem(j, GROUP_IN_SLOTS)
            oslot = lax.rem(j, GROUP_OUT_SLOTS)

            @pl.when(j + ahead < n)
            def _():
                ja = j + ahead
                by_size(ja, lambda m: in_copy(ja, lax.rem(ja, GROUP_IN_SLOTS), m).start())

            by_size(j, lambda m: in_copy(j, islot, m).wait())

            @pl.when(j >= GROUP_OUT_SLOTS)
            def _():
                jo = j - GROUP_OUT_SLOTS
                by_size(jo, lambda m: out_copy(jo, oslot, m).wait())

            @pl.when(first)
            def _():
                def run(m):
                    load_weights()
                    compute(j, islot, oslot, m)
                    out_copy(j, oslot, m).start()

                by_size(j, run)
                jn = j + (nb_ref[ce[j]] + 1) // 2

                @pl.when(jn < n)
                def _():
                    for cp in w_copies(ce[jn], cs[jn]):
                        cp.start()

            @pl.when(jnp.logical_not(first))
            def _():
                def run(m):
                    compute(j, islot, oslot, m)
                    out_copy(j, oslot, m).start()

                by_size(j, run)

            return carry

        lax.fori_loop(0, n, body, 0)

        for k in range(GROUP_OUT_SLOTS):
            @pl.when(n > k)
            def _():
                jl = n - 1 - k
                by_size(jl, lambda m: out_copy(jl, lax.rem(jl, GROUP_OUT_SLOTS), m).wait())


def _gmm1_kernel(bs_ref, nb_ref, xs_hbm, wg_hbm, wu_hbm, bg_ref, bu_ref, act_hbm,
                 xbuf, obuf, wraw, wbf, zbuf, ce, cs, cb, cn, sin, sout, wsem, zsem,
                 *, rb, tf, n_blocks, n_split, nseg):
    def blk(b, m=1):
        return pl.ds(pl.multiple_of(b * rb, rb), m * rb)

    def tok_blk(b, m=1):
        return pl.ds(pl.multiple_of(b * (rb * nseg), rb * nseg), m * rb * nseg)

    def cols(s_):
        return pl.ds(pl.multiple_of(s_ * tf, tf), tf)

    n_e = bs_ref.shape[0]
    for s_ in range(n_split):
        _zero_tail(bs_ref[n_e - 1] + nb_ref[n_e - 1], n_blocks, zbuf,
                   lambda c: act_hbm.at[blk(c), cols(s_)], zsem)
    n = _build_chunks(bs_ref, nb_ref, ce, cs, cb, cn, n_split)

    def in_copy(j, slot, m):
        return pltpu.make_async_copy(xs_hbm.at[tok_blk(cb[j], m)],
                                     xbuf.at[slot, pl.ds(0, m * rb * nseg)], sin.at[slot])

    def out_copy(j, slot, m):
        return pltpu.make_async_copy(obuf.at[slot, pl.ds(0, m * rb)],
                                     act_hbm.at[blk(cb[j], m), cols(cs[j])], sout.at[slot])

    def w_copies(e, s_):
        return (pltpu.make_async_copy(wg_hbm.at[e, :, cols(s_)], wraw.at[0], wsem),
                pltpu.make_async_copy(wu_hbm.at[e, :, cols(s_)], wraw.at[1], wsem))

    def load_weights():
        wbf[:, 0:tf] = wraw[0].astype(BF16)
        wbf[:, tf:2 * tf] = wraw[1].astype(BF16)

    def compute(j, islot, oslot, m):
        g = ce[j] * n_split + cs[j]
        lo, hi = _unpack_pairs(_load_token_tiles(xbuf.at[islot], m * rb, nseg))
        h = jnp.concatenate([lo.astype(BF16), hi.astype(BF16)], axis=1)
        gu = jnp.dot(h, wbf[...], preferred_element_type=F32)
        gate = jnp.minimum(gu[:, :tf] + bg_ref[g], SWIGLU_LIMIT)
        up = jnp.clip(gu[:, tf:] + bu_ref[g], -SWIGLU_LIMIT, SWIGLU_LIMIT)
        act = (up + 1.0) * gate * _sigmoid(SWIGLU_ALPHA * gate)
        obuf[oslot, 0:m * rb, :] = act.astype(BF16)

    _chunk_loop(n, nb_ref, ce, cs, cn, in_copy, out_copy, w_copies, load_weights, compute)


def _gmm1_call(groups, xs, wg, wu, bg, bu, rb, tf, nseg):
    n_buf = xs.shape[0] // nseg
    n_e, d, dff = wg.shape
    n_split = dff // tf
    max_chunks = n_split * (n_buf // rb)
    vmem_full = lambda shape: pl.BlockSpec(shape, lambda i, *_: (0,) * len(shape))
    return pl.pallas_call(
        functools.partial(_gmm1_kernel, rb=rb, tf=tf, n_blocks=n_buf // rb, n_split=n_split, nseg=nseg),
        out_shape=_sds((n_buf, dff), BF16),
        grid_spec=pltpu.PrefetchScalarGridSpec(
            num_scalar_prefetch=2,
            grid=(1,),
            in_specs=[pl.BlockSpec(memory_space=pl.ANY),
                      pl.BlockSpec(memory_space=pl.ANY),
                      pl.BlockSpec(memory_space=pl.ANY),
                      vmem_full((n_e * n_split, 1, tf)),
                      vmem_full((n_e * n_split, 1, tf))],
            out_specs=pl.BlockSpec(memory_space=pl.ANY),
            scratch_shapes=[pltpu.VMEM((GROUP_IN_SLOTS, 2 * rb * nseg, LANES), U32),
                            pltpu.VMEM((GROUP_OUT_SLOTS, 2 * rb, tf), BF16),
                            pltpu.VMEM((2, d, tf), F32),
                            pltpu.VMEM((d, 2 * tf), BF16),
                            pltpu.VMEM((rb, tf), BF16),
                            pltpu.SMEM((max_chunks,), I32),
                            pltpu.SMEM((max_chunks,), I32),
                            pltpu.SMEM((max_chunks,), I32),
                            pltpu.SMEM((max_chunks,), I32),
                            pltpu.SemaphoreType.DMA((GROUP_IN_SLOTS,)),
                            pltpu.SemaphoreType.DMA((GROUP_OUT_SLOTS,)),
                            pltpu.SemaphoreType.DMA,
                            pltpu.SemaphoreType.DMA]),
        compiler_params=_params(("arbitrary",)),
        name="gmm1",
    )(*groups, xs, wg, wu, bg.reshape(n_e * n_split, 1, tf), bu.reshape(n_e * n_split, 1, tf))


def _gmm2_kernel(bs_ref, nb_ref, act_hbm, wd_hbm, bd_ref, y_hbm,
                 abuf, ybuf, wraw, wbf, zbuf, ce, cs, cb, cn, sin, sout, wsem, zsem, *, rb, n_blocks, nseg):
    def blk(b, m=1):
        return pl.ds(pl.multiple_of(b * rb, rb), m * rb)

    def tok_blk(b, m=1):
        return pl.ds(pl.multiple_of(b * (rb * nseg), rb * nseg), m * rb * nseg)

    n_e = bs_ref.shape[0]
    _zero_tail(bs_ref[n_e - 1] + nb_ref[n_e - 1], n_blocks, zbuf, lambda c: y_hbm.at[tok_blk(c)], zsem)
    n = _build_chunks(bs_ref, nb_ref, ce, cs, cb, cn, 1)

    def in_copy(j, slot, m):
        return pltpu.make_async_copy(act_hbm.at[blk(cb[j], m)], abuf.at[slot, pl.ds(0, m * rb)],
                                     sin.at[slot])

    def out_copy(j, slot, m):
        return pltpu.make_async_copy(ybuf.at[slot, pl.ds(0, m * rb * nseg)],
                                     y_hbm.at[tok_blk(cb[j], m)], sout.at[slot])

    def w_copies(e, s_):
        return (pltpu.make_async_copy(wd_hbm.at[e], wraw, wsem),)

    def load_weights():
        wbf[...] = wraw[...].astype(BF16)

    def compute(j, islot, oslot, m):
        y = jnp.dot(abuf[islot, 0:m * rb, :], wbf[...], preferred_element_type=F32) + bd_ref[ce[j]]
        _store_token_tiles(ybuf.at[oslot], _pack_pairs(y), m * rb)

    _chunk_loop(n, nb_ref, ce, cs, cn, in_copy, out_copy, w_copies, load_weights, compute)


def _gmm2_call(groups, act, wd, bd, rb, nseg):
    n_buf, dff = act.shape
    n_e, _, d = wd.shape
    assert d == 2 * nseg * LANES
    max_chunks = n_buf // rb
    vmem_full = lambda shape: pl.BlockSpec(shape, lambda i, *_: (0,) * len(shape))
    return pl.pallas_call(
        functools.partial(_gmm2_kernel, rb=rb, n_blocks=n_buf // rb, nseg=nseg),
        out_shape=_sds((n_buf * nseg, LANES), U32),
        grid_spec=pltpu.PrefetchScalarGridSpec(
            num_scalar_prefetch=2,
            grid=(1,),
            in_specs=[pl.BlockSpec(memory_space=pl.ANY),
                      pl.BlockSpec(memory_space=pl.ANY),
                      vmem_full((n_e, 1, d))],
            out_specs=pl.BlockSpec(memory_space=pl.ANY),
            scratch_shapes=[pltpu.VMEM((GROUP_IN_SLOTS, 2 * rb, dff), BF16),
                            pltpu.VMEM((GROUP_OUT_SLOTS, 2 * rb * nseg, LANES), U32),
                            pltpu.VMEM((dff, d), F32),
                            pltpu.VMEM((dff, d), BF16),
                            pltpu.VMEM((rb * nseg, LANES), U32),
                            pltpu.SMEM((max_chunks,), I32),
                            pltpu.SMEM((max_chunks,), I32),
                            pltpu.SMEM((max_chunks,), I32),
                            pltpu.SMEM((max_chunks,), I32),
                            pltpu.SemaphoreType.DMA((GROUP_IN_SLOTS,)),
                            pltpu.SemaphoreType.DMA((GROUP_OUT_SLOTS,)),
                            pltpu.SemaphoreType.DMA,
                            pltpu.SemaphoreType.DMA]),
        compiler_params=_params(("arbitrary",)),
        name="gmm2",
    )(*groups, act, wd, bd.reshape(n_e, 1, d))


def _final_kernel(*refs, tc, n_steps, nseg):
    pos_refs, posn_refs = refs[:TOP_K], refs[TOP_K:2 * TOP_K]
    y_hbm, tw_ref, x1_ref, gt_ref, gfin_ref, o_ref, ybuf, sem = refs[2 * TOP_K:]
    step = pl.program_id(0) * pl.num_programs(1) + pl.program_id(1)

    def tok(t):
        return pl.ds(pl.multiple_of(t * nseg, nseg), nseg)

    def issue(p_refs, slot):
        def body(t, c):
            for k in range(TOP_K):
                src = y_hbm.at[tok(p_refs[k][0, 0, 0, t])]
                pltpu.make_async_copy(src, ybuf.at[slot, k, tok(t)], sem.at[slot]).start(priority=k % 2)
            return c

        lax.fori_loop(0, tc, body, 0, unroll=2)

    def run(cur):
        @pl.when(step + 1 < n_steps)
        def _():
            issue(posn_refs, 1 - cur)

        for k in range(TOP_K):
            pltpu.make_async_copy(y_hbm.at[pl.ds(0, tc * nseg)], ybuf.at[cur, k], sem.at[cur]).wait()
        tw = tw_ref[0]
        lo, hi = None, None
        for k in range(TOP_K):
            lo_k, hi_k = _unpack_pairs(_load_token_tiles(ybuf.at[cur, k], tc, nseg))
            wk = tw[:, k:k + 1]
            lo = wk * lo_k if lo is None else lo + wk * lo_k
            hi = wk * hi_k if hi is None else hi + wk * hi_k
        moe = jnp.concatenate([lo, hi], axis=1)
        x2 = x1_ref[0] + gt_ref[0] * moe
        o_ref[0] = _rms(x2, gfin_ref[...])

    @pl.when(step == 0)
    def _():
        issue(pos_refs, 0)

    parity = lax.rem(step, 2)

    @pl.when(parity == 0)
    def _():
        run(0)

    @pl.when(parity == 1)
    def _():
        run(1)


def _final_call(y, pos, top_w, x1, gt2, gfin, nseg):
    bsz, n, d = x1.shape
    tc = _pick(n, COMBINE_TILE, SUBLANES)
    nt = n // tc
    n_steps = bsz * nt

    def cur_pos(k):
        return pl.BlockSpec((1, 1, 1, tc), lambda b, i: (b, k, 0, i), memory_space=pltpu.SMEM)

    def next_pos(k):
        def index(b, i):
            nxt = jnp.minimum(b * nt + i + 1, n_steps - 1)
            return (nxt // nt, k, 0, nxt % nt)

        return pl.BlockSpec((1, 1, 1, tc), index, memory_space=pltpu.SMEM)

    row = lambda b, i: (b, i, 0)
    return pl.pallas_call(
        functools.partial(_final_kernel, tc=tc, n_steps=n_steps, nseg=nseg),
        out_shape=_sds((bsz, n, d), F32),
        grid=(bsz, nt),
        in_specs=[cur_pos(k) for k in range(TOP_K)] + [next_pos(k) for k in range(TOP_K)] + [
                  pl.BlockSpec(memory_space=pl.ANY),
                  pl.BlockSpec((1, tc, LANES), row),
                  pl.BlockSpec((1, tc, d), row),
                  pl.BlockSpec((1, 1, d), lambda b, i: (b, 0, 0)),
                  pl.BlockSpec((1, d), lambda b, i: (0, 0))],
        out_specs=pl.BlockSpec((1, tc, d), row),
        scratch_shapes=[pltpu.VMEM((2, TOP_K, tc * nseg, LANES), U32), pltpu.SemaphoreType.DMA((2,))],
        compiler_params=_params(("arbitrary", "arbitrary")),
        name="final",
    )(*([pos[:, :, None, :]] * (2 * TOP_K)), y, top_w, x1, gt2, gfin)


def _group_tables(counts, rb):
    nblk = (counts + rb - 1) // rb
    bstart = jnp.cumsum(nblk) - nblk
    return bstart * rb, (bstart.astype(I32), nblk.astype(I32))


def kernel(x, c, ctx, c_ctx, w_mod, b_mod, g_mix, w_in, conv_a_w, conv_a_b, lru_w_r, lru_b_r,
           lru_w_i, lru_b_i, lru_lam, conv_b_w, g_out_a, g_out_b, w_out, g_ffn, w_router,
           b_router, w_gate, b_gate, w_up, b_up, w_down, b_down, g_final):
    assert w_mod.shape[0] == 1, "single-layer block"
    bsz, n_lat, d = x.shape
    d_a = conv_a_w.shape[-1]
    d_b = conv_b_w.shape[-1]
    n_e = w_router.shape[-1]
    assert n_lat % GRID_W == 0 and n_e <= LANES and d % (2 * LANES) == 0
    l = 0

    cs = jnp.zeros((SUBLANES, d), F32).at[:bsz].set(c).at[bsz].set(c_ctx)
    mod = _mod_call(cs, w_mod[l], b_mod[l][None])
    sh1, sc1, gt1, sh2, sc2, gt2 = [m[:bsz, None, :] for m in jnp.split(mod, 6, axis=-1)]
    ssh1, ssc1 = [jnp.broadcast_to(m[bsz][None, None, :], (bsz, 1, d))
                  for m in jnp.split(mod, 6, axis=-1)[:2]]

    w_in_bf = w_in[l].astype(BF16)
    g_mix2 = g_mix[l][None]
    cw = conv_a_w[l]
    cb = conv_a_b[l][None]
    wri = jnp.concatenate([lru_w_r[l], lru_w_i[l]], axis=-1).astype(BF16)
    br, bi, lam = lru_b_r[l], lru_b_i[l], lru_lam[l]
    lru_p = lambda dr: (cw, cb, wri[dr], br[dr][None], bi[dr][None], lam[dr][None])

    (s_ax,) = _inproj_call(ctx, g_mix2, ssh1, ssc1, w_in_bf[:, d_a:2 * d_a], d_a, d_b, latent=False)
    zero_state = jnp.zeros((bsz, 1, d_a), F32)
    _, h0f, s_xc = _lru_call(s_ax, zero_state, *lru_p(0), reverse=False, emit_conv=True)
    _, h0b = _lru_call(s_xc, zero_state, *lru_p(1), reverse=True, conv_given=True)

    ag, ax, bb, p = _inproj_call(x, g_mix2, sh1, sc1, w_in_bf, d_a, d_b, latent=True)
    hf, _, xc = _lru_call(ax, h0f, *lru_p(0), reverse=False, emit_conv=True)
    ya, _ = _lru_call(xc, h0b, *lru_p(1), reverse=True, hf=hf, ag=ag, conv_given=True)

    wr_bf = jnp.zeros((d, LANES), BF16).at[:, :n_e].set(w_router[l].astype(BF16))
    brt = jnp.full((1, LANES), NEG_BIG, F32).at[0, :n_e].set(b_router[l])
    x1, xp, top_e, top_w, rank, cnt = _mixout_call(
        ya, bb, p, conv_b_w[l], g_out_a[l][None], g_out_b[l][None], w_out[l].astype(BF16), x, gt1,
        g_ffn[l][None], sh2, sc2, wr_bf, brt)

    n_tok = bsz * n_lat
    n_rows = n_tok * TOP_K
    rb = MOE_ROW_BLOCK
    n_buf = n_rows + n_e * rb
    counts = cnt[0, :n_e].astype(I32)
    starts, groups = _group_tables(counts, rb)
    onehot = top_e[..., None] == jnp.arange(n_e, dtype=I32)
    pos = (jnp.sum(jnp.where(onehot, starts, 0), axis=-1) + rank).astype(I32)
    nseg = (d // 2) // LANES
    xs = _dispatch_call(groups, xp.reshape(n_tok * nseg, LANES), pos, n_buf, rb, nseg)
    d_ff = w_gate.shape[-1]
    tf = _pick(d_ff, GMM1_SLAB)
    act = _gmm1_call(groups, xs, w_gate[l], w_up[l], b_gate[l], b_up[l], rb, tf, nseg)
    y = _gmm2_call(groups, act, w_down[l], b_down[l], rb, nseg)
    return _final_call(y, pos, top_w, x1, gt2, g_final[None], nseg)
```
